```python
import jax, jax.numpy as jnp
from jax import lax
import numpy as np

D_MODEL = 2048
BATCH = 4
SEQ = 2048
DEPTH = 4
DEC_BATCH = 32
DEC_SEQ = 1
PAST_LEN = 16384
PAGE_SIZE = 128

N_MIXERS = 2
N_SWA_LAYERS = (DEPTH + 1) // 2
N_RET_LAYERS = DEPTH // 2
HEAD_DIM = 128
MIX_WIDTH = D_MODEL
MEM_LEN = 256
MEM_HEADS = 4
MEM_DIM = MEM_HEADS * HEAD_DIM
SELF_WIDTH = MIX_WIDTH - MEM_DIM
SWA_Q_HEADS = SELF_WIDTH // HEAD_DIM
SWA_KV_HEADS = SWA_Q_HEADS // 3
SWA_GROUP = SWA_Q_HEADS // SWA_KV_HEADS
WINDOW = 128
ROPE_THETA = 500000.0
ROPE_DIM = HEAD_DIM // 4
RET_DK = 128
RET_DV = 2 * RET_DK
RET_HEADS = SELF_WIDTH // RET_DV
RET_CHUNK = 128
RET_ROT_BASE = 10000.0
D_FF = ((8 * D_MODEL // 3 + 255) // 256) * 256
LN_EPS = 1e-5
HEAD_NORM_EPS = 1e-6
ALPHA = (2.0 * DEPTH) ** 0.25
BETA = (8.0 * DEPTH) ** -0.25
NEG_INF = -1e30

SWA_Q = SWA_Q_HEADS * HEAD_DIM
SWA_KV = SWA_KV_HEADS * HEAD_DIM
SWA_IN_WIDTH = SWA_Q + 2 * SWA_KV + MEM_DIM
SWA_OUT_WIDTH = SWA_Q + MEM_DIM
RET_QK = RET_HEADS * RET_DK
RET_V = RET_HEADS * RET_DV
RET_IN_WIDTH = 2 * RET_QK + 2 * RET_V + MEM_DIM
RET_OUT_WIDTH = RET_V + MEM_DIM

kernel_name = 'hybrid_swa_retention_memory_step'


def layer_norm(x, g, b):
    xf = x.astype(jnp.float32)
    mu = jnp.mean(xf, axis=-1, keepdims=True)
    var = jnp.mean(jnp.square(xf - mu), axis=-1, keepdims=True)
    y = (xf - mu) * lax.rsqrt(var + LN_EPS) * g.astype(jnp.float32) + b.astype(jnp.float32)
    return y.astype(x.dtype)


def swiglu(x, w_gu, w_down):
    g, u = jnp.split(x @ w_gu, 2, axis=-1)
    return (jax.nn.silu(g) * u) @ w_down


def partial_rope(x, pos):
    half = ROPE_DIM // 2
    inv = ROPE_THETA ** (-jnp.arange(half, dtype=jnp.float32) / half)
    ang = pos.astype(jnp.float32)[:, None] * inv[None, :]
    cos = jnp.cos(ang)[None, :, None, :]
    sin = jnp.sin(ang)[None, :, None, :]
    xr = x[..., :ROPE_DIM].astype(jnp.float32)
    x1, x2 = xr[..., :half], xr[..., half:]
    rot = jnp.concatenate([x1 * cos - x2 * sin, x2 * cos + x1 * sin], axis=-1).astype(x.dtype)
    return jnp.concatenate([rot, x[..., ROPE_DIM:]], axis=-1)


def retention_rotate(x, pos):
    half = RET_DK // 2
    angle = RET_ROT_BASE ** (-jnp.linspace(0.0, 1.0, half, dtype=jnp.float32))
    ang = pos.astype(jnp.float32)[:, None] * angle[None, :]
    cos = jnp.cos(ang)[None, :, None, :]
    sin = jnp.sin(ang)[None, :, None, :]
    xf = x.astype(jnp.float32).reshape(x.shape[:-1] + (half, 2))
    x1, x2 = xf[..., 0], xf[..., 1]
    out = jnp.stack([x1 * cos - x2 * sin, x2 * cos + x1 * sin], axis=-1).reshape(x.shape)
    return out.astype(x.dtype)


def sink_softmax(s, sink):
    m = jnp.maximum(jnp.max(s, axis=-1, keepdims=True), sink)
    p = jnp.exp(s - m)
    return p / (jnp.sum(p, axis=-1, keepdims=True) + jnp.exp(sink - m))


def swa_prompt(q, k, v, sinks):
    B, L = q.shape[:2]
    nb = L // WINDOW
    qb = q.reshape(B, nb, WINDOW, SWA_KV_HEADS, SWA_GROUP, HEAD_DIM)

    def band(t):
        tp = jnp.concatenate([jnp.zeros_like(t[:, :WINDOW]), t], axis=1)
        tp = tp.reshape(B, nb + 1, WINDOW, SWA_KV_HEADS, HEAD_DIM)
        return jnp.concatenate([tp[:, :-1], tp[:, 1:]], axis=2)

    kb, vb = band(k), band(v)
    s = jnp.einsum('bnqkgd,bnmkd->bnkgqm', qb, kb,
                   preferred_element_type=jnp.float32) * HEAD_DIM ** -0.5
    rel = jnp.arange(WINDOW)[:, None] + WINDOW - jnp.arange(2 * WINDOW)[None, :]
    band_ok = (rel >= 0) & (rel <= WINDOW)
    real_key = (jnp.arange(nb)[:, None, None] > 0) | (jnp.arange(2 * WINDOW)[None, None, :] >= WINDOW)
    mask = band_ok[None] & real_key
    s = jnp.where(mask[None, :, None, None], s, NEG_INF)
    sink = sinks.astype(jnp.float32).reshape(SWA_KV_HEADS, SWA_GROUP)[None, None, :, :, None, None]
    p = sink_softmax(s, sink)
    o = jnp.einsum('bnkgqm,bnmkd->bnqkgd', p.astype(vb.dtype), vb)
    return o.reshape(B, L, SWA_Q_HEADS, HEAD_DIM)


def swa_step(q, k, v, k_buf, v_buf, sinks):
    B, S = q.shape[:2]
    wb = k_buf.shape[1]
    kk = jnp.concatenate([k_buf.astype(k.dtype), k], axis=1)
    vv = jnp.concatenate([v_buf.astype(v.dtype), v], axis=1)
    qg = q.reshape(B, S, SWA_KV_HEADS, SWA_GROUP, HEAD_DIM)
    s = jnp.einsum('bqkgd,bmkd->bkgqm', qg, kk,
                   preferred_element_type=jnp.float32) * HEAD_DIM ** -0.5
    rel = jnp.arange(S)[:, None] + wb - jnp.arange(wb + S)[None, :]
    mask = (rel >= 0) & (rel <= WINDOW)
    s = jnp.where(mask, s, NEG_INF)
    sink = sinks.astype(jnp.float32).reshape(SWA_KV_HEADS, SWA_GROUP)[None, :, :, None, None]
    p = sink_softmax(s, sink)
    o = jnp.einsum('bkgqm,bmkd->bqkgd', p.astype(vv.dtype), vv)
    return o.reshape(B, S, SWA_Q_HEADS, HEAD_DIM), kk[:, S:], vv[:, S:]


def mem_attend(q, mk, mv):
    s = jnp.einsum('blhd,bmhd->bhlm', q, mk.astype(q.dtype),
                   preferred_element_type=jnp.float32) * HEAD_DIM ** -0.5
    p = jax.nn.softmax(s, axis=-1)
    return jnp.einsum('bhlm,bmhd->blhd', p.astype(q.dtype), mv.astype(q.dtype))


def retention_chunkwise(q, k, v, s0, chunk):
    B, L, H, dk = q.shape
    dv = v.shape[-1]
    nc = L // chunk
    log_g = jnp.log1p(-jnp.exp2(-5.0 - jnp.arange(H, dtype=jnp.float32)))
    n = jnp.arange(chunk, dtype=jnp.float32)
    rel = n[:, None] - n[None, :]
    decay = jnp.where(rel >= 0, jnp.exp(jnp.maximum(rel, 0.0) * log_g[:, None, None]), 0.0)
    q_dec = jnp.exp((n + 1.0) * log_g[:, None])
    k_dec = jnp.exp((chunk - 1.0 - n) * log_g[:, None])
    c_dec = jnp.exp(chunk * log_g)

    def to_chunks(t):
        return t.astype(jnp.float32).reshape(B, nc, chunk, H, t.shape[-1]).transpose(1, 0, 3, 2, 4)

    def step(state, qkv):
        qc, kc, vc = qkv
        inner = jnp.einsum('bhcd,bhmd->bhcm', qc, kc) * decay
        o = (jnp.einsum('bhcm,bhme->bhce', inner, vc)
             + jnp.einsum('bhcd,bhde->bhce', qc * q_dec[:, :, None], state))
        state = c_dec[:, None, None] * state + jnp.einsum('bhmd,bhme->bhde', kc * k_dec[:, :, None], vc)
        return state, o

    s_fin, o = lax.scan(step, s0.astype(jnp.float32), (to_chunks(q), to_chunks(k), to_chunks(v)))
    return o.transpose(1, 0, 3, 2, 4).reshape(B, L, H, dv), s_fin


def swa_mixer(h, pos, k_buf, v_buf, mk, mv, w_in, w_out, sinks):
    B, L, _ = h.shape
    q, k, v, qm = jnp.split(h @ w_in, [SWA_Q, SWA_Q + SWA_KV, SWA_Q + 2 * SWA_KV], axis=-1)
    q = partial_rope(q.reshape(B, L, SWA_Q_HEADS, HEAD_DIM), pos)
    k = partial_rope(k.reshape(B, L, SWA_KV_HEADS, HEAD_DIM), pos)
    v = v.reshape(B, L, SWA_KV_HEADS, HEAD_DIM)
    if k_buf is None:
        o = swa_prompt(q, k, v, sinks)
        wb = min(WINDOW, L)
        nk, nv = k[:, L - wb:], v[:, L - wb:]
    else:
        o, nk, nv = swa_step(q, k, v, k_buf, v_buf, sinks)
    om = mem_attend(qm.reshape(B, L, MEM_HEADS, HEAD_DIM), mk, mv)
    y = jnp.concatenate([o.reshape(B, L, SWA_Q), om.reshape(B, L, MEM_DIM)], axis=-1) @ w_out
    return y, nk, nv


def retention_mixer(h, pos, s0, mk, mv, w_in, w_out, chunk):
    B, L, _ = h.shape
    q, k, v, g, qm = jnp.split(
        h @ w_in, [RET_QK, 2 * RET_QK, 2 * RET_QK + RET_V, 2 * RET_QK + 2 * RET_V], axis=-1)
    q = retention_rotate(q.reshape(B, L, RET_HEADS, RET_DK), pos)
    k = retention_rotate(k.reshape(B, L, RET_HEADS, RET_DK), pos) * RET_DK ** -0.5
    o, s_new = retention_chunkwise(q, k, v.reshape(B, L, RET_HEADS, RET_DV), s0, chunk)
    mu = jnp.mean(o, axis=-1, keepdims=True)
    var = jnp.mean(jnp.square(o - mu), axis=-1, keepdims=True)
    o = ((o - mu) * lax.rsqrt(var + HEAD_NORM_EPS)).reshape(B, L, RET_V)
    o = (jax.nn.silu(g.astype(jnp.float32)) * o).astype(h.dtype)
    om = mem_attend(qm.reshape(B, L, MEM_HEADS, HEAD_DIM), mk, mv)
    y = jnp.concatenate([o, om.reshape(B, L, MEM_DIM)], axis=-1) @ w_out
    return y, s_new


def run_trunk(x, pos, swa_k, swa_v, ret_s, mem_k, mem_v, ln_g, ln_b, ffn_w_gu, ffn_w_down,
              swa_w_in, swa_w_out, swa_sinks, ret_w_in, ret_w_out):
    B, L, _ = x.shape
    prompt = ret_s is None
    new_k, new_v, new_s = [], [], []
    for i in range(DEPTH):
        j = i // N_MIXERS
        x = layer_norm(ALPHA * x + 0.5 * swiglu(x, ffn_w_gu[i, 0], ffn_w_down[i, 0]), ln_g[i, 0], ln_b[i, 0])
        if i % N_MIXERS == 0:
            y, nk, nv = swa_mixer(x, pos, None if prompt else swa_k[j], None if prompt else swa_v[j],
                                  mem_k[i], mem_v[i], swa_w_in[j], swa_w_out[j], swa_sinks[j])
            new_k.append(nk)
            new_v.append(nv)
        else:
            s0 = jnp.zeros((B, RET_HEADS, RET_DK, RET_DV), jnp.float32) if prompt else ret_s[j]
            chunk = min(RET_CHUNK, L) if prompt else L
            y, ns = retention_mixer(x, pos, s0, mem_k[i], mem_v[i], ret_w_in[j], ret_w_out[j], chunk)
            new_s.append(ns)
        x = layer_norm(ALPHA * x + y, ln_g[i, 1], ln_b[i, 1])
        x = layer_norm(ALPHA * x + 0.5 * swiglu(x, ffn_w_gu[i, 1], ffn_w_down[i, 1]), ln_g[i, 2], ln_b[i, 2])
    return x, jnp.stack(new_k), jnp.stack(new_v), jnp.stack(new_s)


def setup_inputs(seed: int = 0) -> dict:
    key = jax.random.key(seed)
    ks = jax.random.split(key, 20)
    f32 = jnp.float32

    def nrm(k, shape, scale=1.0):
        return jax.random.normal(k, shape, f32) * scale

    wbuf = min(WINDOW, PAST_LEN)
    return {
        'x_prompt': nrm(ks[0], (BATCH, SEQ, D_MODEL)),
        'x_sample': nrm(ks[1], (DEC_BATCH, DEC_SEQ, D_MODEL)),
        'cache_swa_k': nrm(ks[2], (N_SWA_LAYERS, DEC_BATCH, wbuf, SWA_KV_HEADS, HEAD_DIM)),
        'cache_swa_v': nrm(ks[3], (N_SWA_LAYERS, DEC_BATCH, wbuf, SWA_KV_HEADS, HEAD_DIM)),
        'state_ret': nrm(ks[4], (N_RET_LAYERS, DEC_BATCH, RET_HEADS, RET_DK, RET_DV), 0.5),
        'cache_mem_k': nrm(ks[5], (DEPTH, DEC_BATCH, MEM_LEN, MEM_HEADS, HEAD_DIM)),
        'cache_mem_v': nrm(ks[6], (DEPTH, DEC_BATCH, MEM_LEN, MEM_HEADS, HEAD_DIM)),
        'mem_prompt': nrm(ks[7], (BATCH, MEM_LEN, D_MODEL)),
        'ln_g': 1.0 + nrm(ks[8], (DEPTH, 3, D_MODEL), 0.05),
        'ln_b': nrm(ks[9], (DEPTH, 3, D_MODEL), 0.02),
        'ffn_w_gu': nrm(ks[10], (DEPTH, 2, D_MODEL, 2 * D_FF), D_MODEL ** -0.5),
        'ffn_w_down': nrm(ks[11], (DEPTH, 2, D_FF, D_MODEL), BETA * D_FF ** -0.5),
        'w_mem_kv': nrm(ks[12], (DEPTH, D_MODEL, 2 * MEM_DIM), D_MODEL ** -0.5),
        'swa_w_in': nrm(ks[13], (N_SWA_LAYERS, D_MODEL, SWA_IN_WIDTH), D_MODEL ** -0.5),
        'swa_w_out': nrm(ks[14], (N_SWA_LAYERS, SWA_OUT_WIDTH, D_MODEL), BETA * SWA_OUT_WIDTH ** -0.5),
        'swa_sinks': nrm(ks[15], (N_SWA_LAYERS, SWA_Q_HEADS), 0.5),
        'ret_w_in': nrm(ks[16], (N_RET_LAYERS, D_MODEL, RET_IN_WIDTH), D_MODEL ** -0.5),
        'ret_w_out': nrm(ks[17], (N_RET_LAYERS, RET_OUT_WIDTH, D_MODEL), BETA * RET_OUT_WIDTH ** -0.5),
    }


def reference(x_prompt, x_sample, cache_swa_k, cache_swa_v, state_ret, cache_mem_k, cache_mem_v,
              mem_prompt, ln_g, ln_b, ffn_w_gu, ffn_w_down, w_mem_kv, swa_w_in, swa_w_out,
              swa_sinks, ret_w_in, ret_w_out):
    bp = mem_prompt.shape[0]
    mem_kv = jnp.einsum('bmd,lde->lbme', mem_prompt, w_mem_kv)
    mem_k_prompt, mem_v_prompt = jnp.split(mem_kv, 2, axis=-1)
    mem_k_prompt = mem_k_prompt.reshape(DEPTH, bp, MEM_LEN, MEM_HEADS, HEAD_DIM)
    mem_v_prompt = mem_v_prompt.reshape(DEPTH, bp, MEM_LEN, MEM_HEADS, HEAD_DIM)

    pos_prompt = jnp.arange(x_prompt.shape[1], dtype=jnp.int32)
    pos_sample = PAST_LEN + jnp.arange(x_sample.shape[1], dtype=jnp.int32)

    y_prompt, swa_k_prompt, swa_v_prompt, ret_state_prompt = run_trunk(
        x_prompt, pos_prompt, None, None, None, mem_k_prompt, mem_v_prompt,
        ln_g, ln_b, ffn_w_gu, ffn_w_down, swa_w_in, swa_w_out, swa_sinks, ret_w_in, ret_w_out)
    y_sample, swa_k_sample, swa_v_sample, ret_state_sample = run_trunk(
        x_sample, pos_sample, cache_swa_k, cache_swa_v, state_ret, cache_mem_k, cache_mem_v,
        ln_g, ln_b, ffn_w_gu, ffn_w_down, swa_w_in, swa_w_out, swa_sinks, ret_w_in, ret_w_out)
    return (y_prompt, y_sample, swa_k_prompt, swa_v_prompt, swa_k_sample, swa_v_sample,
            ret_state_prompt, ret_state_sample, mem_k_prompt, mem_v_prompt)
```

```python
import functools

import jax
import jax.numpy as jnp
from jax import lax
from jax.experimental import pallas as pl
from jax.experimental.pallas import tpu as pltpu

F32 = jnp.float32
BF16 = jnp.bfloat16

D_MODEL = 2048
DEPTH = 4
PAST_LEN = 16384
HEAD_DIM = 128
MEM_LEN = 256
MEM_HEADS = 4
MEM_DIM = MEM_HEADS * HEAD_DIM
SELF_WIDTH = D_MODEL - MEM_DIM
SWA_Q_HEADS = SELF_WIDTH // HEAD_DIM
SWA_KV_HEADS = SWA_Q_HEADS // 3
SWA_GROUP = SWA_Q_HEADS // SWA_KV_HEADS
WINDOW = 128
ROPE_THETA = 500000.0
ROPE_DIM = HEAD_DIM // 4
RET_DK = 128
RET_DV = 2 * RET_DK
RET_HEADS = SELF_WIDTH // RET_DV
RET_CHUNK = 128
RET_ROT_BASE = 10000.0
D_FF = ((8 * D_MODEL // 3 + 255) // 256) * 256
LN_EPS = 1e-5
HEAD_NORM_EPS = 1e-6
ALPHA = (2.0 * DEPTH) ** 0.25
NEG_INF = -1e30
ATT_SCALE = HEAD_DIM ** -0.5
RET_K_SCALE = RET_DK ** -0.5

SWA_Q = SWA_Q_HEADS * HEAD_DIM
SWA_KV = SWA_KV_HEADS * HEAD_DIM
SWA_IN_WIDTH = SWA_Q + 2 * SWA_KV + MEM_DIM
RET_QK = RET_HEADS * RET_DK
RET_V = RET_HEADS * RET_DV
RET_IN_WIDTH = 2 * RET_QK + 2 * RET_V + MEM_DIM

VMEM_LIMIT_BYTES = 58 * 1024 * 1024
LANES = 128
PROJ_TILE_N = 512
OUT_TILE_K = 512


def _row_tile(m):
    return 1024 if m % 1024 == 0 else m


def _params(*sem):
    return pltpu.CompilerParams(dimension_semantics=sem, vmem_limit_bytes=VMEM_LIMIT_BYTES)


def _layer_norm_rows(z, g, b):
    mu = jnp.mean(z, axis=-1, keepdims=True)
    zc = z - mu
    var = jnp.mean(zc * zc, axis=-1, keepdims=True)
    return zc * lax.rsqrt(var + LN_EPS) * g + b


def _bf16_round(v):
    return v.astype(BF16).astype(F32)


def _ffn_kernel(x_ref, wg_ref, wu_ref, wd_ref, g_ref, b_ref, o_ref, xb_ref, wgu_ref, *, nk, tk):
    k = pl.program_id(1)

    @pl.when(k == 0)
    def _():
        xb_ref[...] = x_ref[...].astype(BF16)
        o_ref[...] = jnp.zeros_like(o_ref)

    wgu_ref[:, :tk] = wg_ref[...].astype(BF16)
    wgu_ref[:, tk:] = wu_ref[...].astype(BF16)
    gu = jnp.dot(xb_ref[...], wgu_ref[...], preferred_element_type=F32)
    h = (jax.nn.silu(gu[:, :tk]) * gu[:, tk:]).astype(BF16)
    o_ref[...] += jnp.dot(h, wd_ref[...].astype(BF16), preferred_element_type=F32)

    @pl.when(k == nk - 1)
    def _():
        z = ALPHA * x_ref[...] + 0.5 * o_ref[...]
        o_ref[...] = _layer_norm_rows(z, g_ref[...], b_ref[...])


def _ffn(x, w_gu, w_down, layer, slot, g, b):
    m = x.shape[0]
    tm = _row_tile(m)
    tk = 256 if tm >= 1024 else 512
    nk = D_FF // tk
    return pl.pallas_call(
        functools.partial(_ffn_kernel, nk=nk, tk=tk),
        grid=(m // tm, nk),
        in_specs=[
            pl.BlockSpec((tm, D_MODEL), lambda i, k: (i, 0), pipeline_mode=pl.Buffered(1)),
            pl.BlockSpec((None, None, D_MODEL, tk), lambda i, k: (layer, slot, 0, k)),
            pl.BlockSpec((None, None, D_MODEL, tk), lambda i, k: (layer, slot, 0, nk + k)),
            pl.BlockSpec((None, None, tk, D_MODEL), lambda i, k: (layer, slot, k, 0)),
            pl.BlockSpec((1, D_MODEL), lambda i, k: (0, 0)),
            pl.BlockSpec((1, D_MODEL), lambda i, k: (0, 0)),
        ],
        out_specs=pl.BlockSpec((tm, D_MODEL), lambda i, k: (i, 0)),
        out_shape=jax.ShapeDtypeStruct((m, D_MODEL), F32),
        scratch_shapes=[pltpu.VMEM((tm, D_MODEL), BF16), pltpu.VMEM((D_MODEL, 2 * tk), BF16)],
        compiler_params=_params("parallel", "arbitrary"),
        name="ffn_ln",
    )(x, w_gu, w_gu, w_down, g, b)


def _proj_kernel(x_ref, w_ref, *rest, n_rot, shifts):
    if n_rot:
        c_ref, a_ref, b_ref, o_ref, xb_ref = rest
    else:
        o_ref, xb_ref = rest
    j = pl.program_id(1)

    @pl.when(j == 0)
    def _():
        xb_ref[...] = x_ref[...].astype(BF16)

    y = jnp.dot(xb_ref[...], w_ref[...].astype(BF16), preferred_element_type=F32)
    if not n_rot:
        o_ref[...] = y
        return

    @pl.when(j < n_rot)
    def _():
        c, a, b = c_ref[...], a_ref[...], b_ref[...]
        for h in range(y.shape[1] // LANES):
            yh = y[:, h * LANES:(h + 1) * LANES]
            o_ref[:, h * LANES:(h + 1) * LANES] = (
                yh * c + pltpu.roll(yh, shifts[0], 1) * a + pltpu.roll(yh, shifts[1], 1) * b)

    @pl.when(j >= n_rot)
    def _():
        o_ref[...] = y


def _proj(x, w, layer, n_rot=0, tables=None, shifts=None, seq=None):
    m = x.shape[0]
    n = w.shape[-1]
    tm = _row_tile(m)
    tn = PROJ_TILE_N
    in_specs = [
        pl.BlockSpec((tm, D_MODEL), lambda i, j: (i, 0), pipeline_mode=pl.Buffered(1)),
        pl.BlockSpec((None, D_MODEL, tn), lambda i, j: (layer, 0, j)),
    ]
    args = [x, w]
    if n_rot:
        tiles_per_seq = seq // tm
        tab = pl.BlockSpec((tm, LANES), lambda i, j: (i % tiles_per_seq, 0))
        in_specs += [tab, tab, tab]
        args += list(tables)
    return pl.pallas_call(
        functools.partial(_proj_kernel, n_rot=n_rot, shifts=shifts),
        grid=(m // tm, n // tn),
        in_specs=in_specs,
        out_specs=pl.BlockSpec((tm, tn), lambda i, j: (i, j)),
        out_shape=jax.ShapeDtypeStruct((m, n), F32),
        scratch_shapes=[pltpu.VMEM((tm, D_MODEL), BF16)],
        compiler_params=_params("parallel", "arbitrary"),
        name="proj",
    )(*args)


def _out_ln_kernel(att_ref, w_ref, x_ref, g_ref, b_ref, o_ref, *, nk):
    k = pl.program_id(1)

    @pl.when(k == 0)
    def _():
        o_ref[...] = jnp.zeros_like(o_ref)

    o_ref[...] += jnp.dot(att_ref[...].astype(BF16), w_ref[...].astype(BF16),
                          preferred_element_type=F32)

    @pl.when(k == nk - 1)
    def _():
        z = ALPHA * x_ref[...] + o_ref[...]
        o_ref[...] = _layer_norm_rows(z, g_ref[...], b_ref[...])


def _out_ln(att, w_out, layer, x, g, b):
    m = x.shape[0]
    tm = _row_tile(m)
    tk = OUT_TILE_K
    nk = D_MODEL // tk
    return pl.pallas_call(
        functools.partial(_out_ln_kernel, nk=nk),
        grid=(m // tm, nk),
        in_specs=[
            pl.BlockSpec((tm, tk), lambda i, k: (i, k)),
            pl.BlockSpec((None, tk, D_MODEL), lambda i, k: (layer, k, 0)),
            pl.BlockSpec((tm, D_MODEL), lambda i, k: (i, 0), pipeline_mode=pl.Buffered(1)),
            pl.BlockSpec((1, D_MODEL), lambda i, k: (0, 0)),
            pl.BlockSpec((1, D_MODEL), lambda i, k: (0, 0)),
        ],
        out_specs=pl.BlockSpec((tm, D_MODEL), lambda i, k: (i, 0)),
        out_shape=jax.ShapeDtypeStruct((m, D_MODEL), F32),
        compiler_params=_params("parallel", "arbitrary"),
        name="out_ln",
    )(att, w_out, x, g, b)


def _dot_nt(a, b):
    return lax.dot_general(a, b, (((1,), (1,)), ((), ())), preferred_element_type=F32)


def _mem_attend_block(qm_ref, mk_ref, mv_ref, o_ref, col0):
    for h in range(MEM_HEADS):
        sl = slice(h * HEAD_DIM, (h + 1) * HEAD_DIM)
        q = qm_ref[0, :, sl].astype(BF16)
        s = _dot_nt(q, mk_ref[0, :, sl].astype(BF16)) * ATT_SCALE
        p = jnp.exp(s - jnp.max(s, axis=-1, keepdims=True))
        p = p * (1.0 / jnp.sum(p, axis=-1, keepdims=True))
        o = jnp.dot(p.astype(BF16), mv_ref[0, :, sl].astype(BF16), preferred_element_type=F32)
        o_ref[0, :, col0 + h * HEAD_DIM:col0 + (h + 1) * HEAD_DIM] = o.astype(o_ref.dtype)


def _swa_prompt_kernel(sink_ref, q_ref, kc_ref, kp_ref, vc_ref, vp_ref, qm_ref, mk_ref, mv_ref, o_ref):
    n = pl.program_id(1)
    qi = lax.broadcasted_iota(jnp.int32, (WINDOW, WINDOW), 0)
    kj = lax.broadcasted_iota(jnp.int32, (WINDOW, WINDOW), 1)
    prev_ok = kj >= qi + jnp.where(n > 0, 0, WINDOW)
    cur_ok = kj <= qi
    for h in range(SWA_KV_HEADS):
        sl = slice(h * HEAD_DIM, (h + 1) * HEAD_DIM)
        kp = kp_ref[0, :, sl].astype(BF16)
        kc = kc_ref[0, :, sl].astype(BF16)
        vp = vp_ref[0, :, sl].astype(BF16)
        vc = vc_ref[0, :, sl].astype(BF16)
        for g in range(SWA_GROUP):
            hq = h * SWA_GROUP + g
            q = q_ref[0, :, hq * HEAD_DIM:(hq + 1) * HEAD_DIM].astype(BF16)
            sp = jnp.where(prev_ok, _dot_nt(q, kp) * ATT_SCALE, NEG_INF)
            sc = jnp.where(cur_ok, _dot_nt(q, kc) * ATT_SCALE, NEG_INF)
            sink = sink_ref[hq]
            m = jnp.maximum(jnp.maximum(jnp.max(sp, axis=-1, keepdims=True),
                                        jnp.max(sc, axis=-1, keepdims=True)), sink)
            pp = jnp.exp(sp - m)
            pc = jnp.exp(sc - m)
            den = (jnp.sum(pp, axis=-1, keepdims=True) + jnp.sum(pc, axis=-1, keepdims=True)
                   + jnp.exp(sink - m))
            inv = 1.0 / den
            o = (jnp.dot((pp * inv).astype(BF16), vp, preferred_element_type=F32)
                 + jnp.dot((pc * inv).astype(BF16), vc, preferred_element_type=F32))
            o_ref[0, :, hq * HEAD_DIM:(hq + 1) * HEAD_DIM] = o.astype(o_ref.dtype)
    _mem_attend_block(qm_ref, mk_ref, mv_ref, o_ref, SWA_Q)


def _swa_prompt(qkv, mem_k, mem_v, sinks):
    bsz, seq, _ = qkv.shape
    nb = seq // WINDOW
    kcol = SWA_Q // SWA_KV
    vcol = kcol + 1
    mcol = vcol + 1
    blk = lambda w, f: pl.BlockSpec((1, WINDOW, w), f)
    prev = lambda n: jnp.maximum(n - 1, 0)
    return pl.pallas_call(
        _swa_prompt_kernel,
        grid=(bsz, nb),
        in_specs=[
            pl.BlockSpec(memory_space=pltpu.SMEM),
            blk(SWA_Q, lambda b, n: (b, n, 0)),
            blk(SWA_KV, lambda b, n: (b, n, kcol)),
            blk(SWA_KV, lambda b, n: (b, prev(n), kcol)),
            blk(SWA_KV, lambda b, n: (b, n, vcol)),
            blk(SWA_KV, lambda b, n: (b, prev(n), vcol)),
            blk(MEM_DIM, lambda b, n: (b, n, mcol)),
            pl.BlockSpec((1, MEM_LEN, MEM_DIM), lambda b, n: (b, 0, 0)),
            pl.BlockSpec((1, MEM_LEN, MEM_DIM), lambda b, n: (b, 0, 0)),
        ],
        out_specs=blk(D_MODEL, lambda b, n: (b, n, 0)),
        out_shape=jax.ShapeDtypeStruct((bsz, seq, D_MODEL), BF16),
        compiler_params=_params("parallel", "arbitrary"),
        name="swa_prompt",
    )(sinks, qkv, qkv, qkv, qkv, qkv, qkv, mem_k, mem_v)


def _head_norm_gate(o, gate):
    mu = jnp.mean(o, axis=-1, keepdims=True)
    oc = o - mu
    var = jnp.mean(oc * oc, axis=-1, keepdims=True)
    return jax.nn.silu(gate) * (oc * lax.rsqrt(var + HEAD_NORM_EPS))


def _ret_prompt_kernel(cdec_ref, q_ref, k_ref, v_ref, g_ref, qm_ref, mk_ref, mv_ref,
                       decay_ref, qdec_ref, kdec_ref, o_ref, s_out_ref, state_ref, *, nc):
    c = pl.program_id(1)

    @pl.when(c == 0)
    def _():
        state_ref[...] = jnp.zeros_like(state_ref)

    for h in range(RET_HEADS):
        ksl = slice(h * RET_DK, (h + 1) * RET_DK)
        vsl = slice(h * RET_DV, (h + 1) * RET_DV)
        qc = q_ref[0, :, ksl]
        kc = k_ref[0, :, ksl] * RET_K_SCALE
        vb = v_ref[0, :, vsl].astype(BF16)
        st = state_ref[h]
        inner = _dot_nt(qc.astype(BF16), kc.astype(BF16)) * decay_ref[h]
        o = (jnp.dot(inner.astype(BF16), vb, preferred_element_type=F32)
             + jnp.dot((qc * qdec_ref[h]).astype(BF16), st.astype(BF16), preferred_element_type=F32))
        kd = (kc * kdec_ref[h]).astype(BF16)
        state_ref[h] = cdec_ref[h] * st + lax.dot_general(
            kd, vb, (((0,), (0,)), ((), ())), preferred_element_type=F32)
        o_ref[0, :, vsl] = _head_norm_gate(o, g_ref[0, :, vsl]).astype(o_ref.dtype)
    _mem_attend_block(qm_ref, mk_ref, mv_ref, o_ref, RET_V)

    @pl.when(c == nc - 1)
    def _():
        s_out_ref[0] = state_ref[...]


def _ret_prompt(qkvg, mem_k, mem_v, tables):
    bsz, seq, _ = qkvg.shape
    nc = seq // RET_CHUNK
    decay, qdec, kdec, cdec = tables
    blk = lambda w, f: pl.BlockSpec((1, RET_CHUNK, w), f)
    tab = pl.BlockSpec((RET_HEADS, RET_CHUNK, RET_CHUNK), lambda b, c: (0, 0, 0))
    return pl.pallas_call(
        functools.partial(_ret_prompt_kernel, nc=nc),
        grid=(bsz, nc),
        in_specs=[
            pl.BlockSpec(memory_space=pltpu.SMEM),
            blk(RET_QK, lambda b, c: (b, c, 0)),
            blk(RET_QK, lambda b, c: (b, c, 1)),
            blk(RET_V, lambda b, c: (b, c, 1)),
            blk(RET_V, lambda b, c: (b, c, 2)),
            blk(MEM_DIM, lambda b, c: (b, c, (2 * RET_QK + 2 * RET_V) // MEM_DIM)),
            pl.BlockSpec((1, MEM_LEN, MEM_DIM), lambda b, c: (b, 0, 0)),
            pl.BlockSpec((1, MEM_LEN, MEM_DIM), lambda b, c: (b, 0, 0)),
            tab, tab, tab,
        ],
        out_specs=[
            blk(D_MODEL, lambda b, c: (b, c, 0)),
            pl.BlockSpec((1, RET_HEADS, RET_DK, RET_DV), lambda b, c: (b, 0, 0, 0)),
        ],
        out_shape=[
            jax.ShapeDtypeStruct((bsz, seq, D_MODEL), BF16),
            jax.ShapeDtypeStruct((bsz, RET_HEADS, RET_DK, RET_DV), F32),
        ],
        scratch_shapes=[pltpu.VMEM((RET_HEADS, RET_DK, RET_DV), F32)],
        compiler_params=_params("parallel", "arbitrary"),
        name="ret_prompt",
    )(cdec, qkvg, qkvg, qkvg, qkvg, qkvg, mem_k, mem_v, decay, qdec, kdec)


def _mem_attend_row(rows_ref, row0, mk_ref, mv_ref, o_ref, out_row0):
    for h in range(MEM_HEADS):
        sl = slice(h * HEAD_DIM, (h + 1) * HEAD_DIM)
        q = rows_ref[0, row0 + h:row0 + h + 1, :]
        s = jnp.sum(mk_ref[0, :, sl] * q, axis=1, keepdims=True) * ATT_SCALE
        p = jnp.exp(s - jnp.max(s, axis=0, keepdims=True))
        p = p * (1.0 / jnp.sum(p, axis=0, keepdims=True))
        o = jnp.sum(p * mv_ref[0, :, sl], axis=0, keepdims=True)
        o_ref[0, out_row0 + h:out_row0 + h + 1, :] = o


def _swa_step_kernel(sink_ref, rows_ref, kbuf_ref, vbuf_ref, mk_ref, mv_ref, o_ref, nk_ref, nv_ref):
    krow0 = SWA_Q_HEADS
    vrow0 = krow0 + SWA_KV_HEADS
    mrow0 = vrow0 + SWA_KV_HEADS
    wb = kbuf_ref.shape[1]
    for h in range(SWA_KV_HEADS):
        sl = slice(h * HEAD_DIM, (h + 1) * HEAD_DIM)
        k_new = rows_ref[0, krow0 + h:krow0 + h + 1, :]
        v_new = rows_ref[0, vrow0 + h:vrow0 + h + 1, :]
        kb = kbuf_ref[0, :, sl]
        vb = vbuf_ref[0, :, sl]
        for g in range(SWA_GROUP):
            hq = h * SWA_GROUP + g
            q = rows_ref[0, hq:hq + 1, :]
            s_buf = jnp.sum(kb * q, axis=1, keepdims=True) * ATT_SCALE
            s_new = jnp.sum(k_new * q, axis=1, keepdims=True) * ATT_SCALE
            sink = sink_ref[hq]
            m = jnp.maximum(jnp.maximum(jnp.max(s_buf, axis=0, keepdims=True), s_new), sink)
            p_buf = jnp.exp(s_buf - m)
            p_new = jnp.exp(s_new - m)
            inv = 1.0 / (jnp.sum(p_buf, axis=0, keepdims=True) + p_new + jnp.exp(sink - m))
            o = jnp.sum((p_buf * inv) * vb, axis=0, keepdims=True) + (p_new * inv) * v_new
            o_ref[0, hq:hq + 1, :] = o
        nk_ref[0, wb - 1:wb, sl] = k_new
        nv_ref[0, wb - 1:wb, sl] = v_new
    nk_ref[0, 0:wb - 1, :] = kbuf_ref[0, 1:wb, :]
    nv_ref[0, 0:wb - 1, :] = vbuf_ref[0, 1:wb, :]
    _mem_attend_row(rows_ref, mrow0, mk_ref, mv_ref, o_ref, SWA_Q_HEADS)


def _swa_step(rows, cache_k, cache_v, j, mem_k, mem_v, i, sinks):
    bsz = rows.shape[0]
    wb = cache_k.shape[2]
    cache = pl.BlockSpec((None, 1, wb, SWA_KV), lambda b: (j, b, 0, 0))
    mem = pl.BlockSpec((None, 1, MEM_LEN, MEM_DIM), lambda b: (i, b, 0, 0))
    new = pl.BlockSpec((1, wb, SWA_KV), lambda b: (b, 0, 0))
    nrows = D_MODEL // HEAD_DIM
    return pl.pallas_call(
        _swa_step_kernel,
        grid=(bsz,),
        in_specs=[
            pl.BlockSpec(memory_space=pltpu.SMEM),
            pl.BlockSpec((1,) + rows.shape[1:], lambda b: (b, 0, 0)),
            cache, cache, mem, mem,
        ],
        out_specs=[pl.BlockSpec((1, nrows, HEAD_DIM), lambda b: (b, 0, 0)), new, new],
        out_shape=[
            jax.ShapeDtypeStruct((bsz, nrows, HEAD_DIM), F32),
            jax.ShapeDtypeStruct((bsz, wb, SWA_KV), F32),
            jax.ShapeDtypeStruct((bsz, wb, SWA_KV), F32),
        ],
        compiler_params=_params("parallel"),
        name="swa_step",
    )(sinks, rows, cache_k, cache_v, mem_k, mem_v)


def _ret_step_kernel(dec_ref, rows_ref, cols_ref, s_ref, mk_ref, mv_ref, o_ref, s_out_ref):
    krow0 = RET_HEADS
    vrow0 = 2 * RET_HEADS
    grow0 = vrow0 + 2 * RET_HEADS
    mrow0 = grow0 + 2 * RET_HEADS
    for h in range(RET_HEADS):
        q_row = rows_ref[0, h:h + 1, :]
        k_row = rows_ref[0, krow0 + h:krow0 + h + 1, :] * RET_K_SCALE
        q_col = cols_ref[0, :, h:h + 1]
        k_col = cols_ref[0, :, krow0 + h:krow0 + h + 1] * RET_K_SCALE
        inner = jnp.sum(q_row * k_row, axis=1, keepdims=True) * dec_ref[0, h]
        qd = q_col * dec_ref[1, h]
        kd = k_col * dec_ref[2, h]
        halves = []
        for t in range(2):
            lsl = slice(t * HEAD_DIM, (t + 1) * HEAD_DIM)
            v = rows_ref[0, vrow0 + 2 * h + t:vrow0 + 2 * h + t + 1, :]
            st = s_ref[0, h, :, lsl]
            halves.append(inner * v + jnp.sum(qd * st, axis=0, keepdims=True))
            s_out_ref[0, h, :, lsl] = dec_ref[3, h] * st + kd * v
        mu = (jnp.sum(halves[0], axis=1, keepdims=True)
              + jnp.sum(halves[1], axis=1, keepdims=True)) * (1.0 / RET_DV)
        cen = [o - mu for o in halves]
        var = (jnp.sum(cen[0] * cen[0], axis=1, keepdims=True)
               + jnp.sum(cen[1] * cen[1], axis=1, keepdims=True)) * (1.0 / RET_DV)
        rstd = lax.rsqrt(var + HEAD_NORM_EPS)
        for t in range(2):
            gate = rows_ref[0, grow0 + 2 * h + t:grow0 + 2 * h + t + 1, :]
            o_ref[0, 2 * h + t:2 * h + t + 1, :] = jax.nn.silu(gate) * (cen[t] * rstd)
    _mem_attend_row(rows_ref, mrow0, mk_ref, mv_ref, o_ref, 2 * RET_HEADS)


def _ret_step(rows, cols, state, j, mem_k, mem_v, i, dec):
    bsz = rows.shape[0]
    st_in = pl.BlockSpec((None, 1, RET_HEADS, RET_DK, RET_DV), lambda b: (j, b, 0, 0, 0))
    st_out = pl.BlockSpec((1, RET_HEADS, RET_DK, RET_DV), lambda b: (b, 0, 0, 0))
    mem = pl.BlockSpec((None, 1, MEM_LEN, MEM_DIM), lambda b: (i, b, 0, 0))
    nrows = D_MODEL // HEAD_DIM
    return pl.pallas_call(
        _ret_step_kernel,
        grid=(bsz,),
        in_specs=[
            pl.BlockSpec(memory_space=pltpu.SMEM),
            pl.BlockSpec((1,) + rows.shape[1:], lambda b: (b, 0, 0)),
            pl.BlockSpec((1,) + cols.shape[1:], lambda b: (b, 0, 0)),
            st_in, mem, mem,
        ],
        out_specs=[pl.BlockSpec((1, nrows, HEAD_DIM), lambda b: (b, 0, 0)), st_out],
        out_shape=[
            jax.ShapeDtypeStruct((bsz, nrows, HEAD_DIM), F32),
            jax.ShapeDtypeStruct((bsz, RET_HEADS, RET_DK, RET_DV), F32),
        ],
        compiler_params=_params("parallel"),
        name="ret_step",
    )(dec, rows, cols, state, mem_k, mem_v)


def _rope_tables(pos):
    half = ROPE_DIM // 2
    inv = ROPE_THETA ** (-jnp.arange(half, dtype=F32) / half)
    ang = pos.astype(F32)[:, None] * inv[None, :]
    cos, sin = jnp.cos(ang), jnp.sin(ang)
    n = pos.shape[0]
    rest = HEAD_DIM - ROPE_DIM
    c = jnp.concatenate([cos, cos, jnp.ones((n, rest), F32)], axis=-1)
    a = jnp.concatenate([-sin, jnp.zeros((n, HEAD_DIM - half), F32)], axis=-1)
    b = jnp.concatenate([jnp.zeros((n, half), F32), sin, jnp.zeros((n, rest), F32)], axis=-1)
    return (c, a, b), (HEAD_DIM - half, half)


def _ret_rot_tables(pos):
    half = RET_DK // 2
    angle = RET_ROT_BASE ** (-jnp.linspace(0.0, 1.0, half, dtype=F32))
    ang = pos.astype(F32)[:, None] * angle[None, :]
    cos, sin = jnp.cos(ang), jnp.sin(ang)
    n = pos.shape[0]
    zero = jnp.zeros_like(sin)
    c = jnp.stack([cos, cos], axis=-1).reshape(n, RET_DK)
    a = jnp.stack([-sin, zero], axis=-1).reshape(n, RET_DK)
    b = jnp.stack([zero, sin], axis=-1).reshape(n, RET_DK)
    return (c, a, b), (RET_DK - 1, 1)


def _ret_decay(chunk):
    log_g = jnp.log1p(-jnp.exp2(-5.0 - jnp.arange(RET_HEADS, dtype=F32)))
    n = jnp.arange(chunk, dtype=F32)
    rel = n[:, None] - n[None, :]
    decay = jnp.where(rel >= 0, jnp.exp(jnp.maximum(rel, 0.0) * log_g[:, None, None]), 0.0)
    q_dec = jnp.exp((n + 1.0) * log_g[:, None])
    k_dec = jnp.exp((chunk - 1.0 - n) * log_g[:, None])
    c_dec = jnp.exp(chunk * log_g)
    return decay, q_dec, k_dec, c_dec


def _trunk(x3, pos, swa_k, swa_v, ret_s, mem_k, mem_v, ln_g, ln_b, ffn_w_gu, ffn_w_down,
           swa_w_in, swa_w_out, swa_sinks, ret_w_in, ret_w_out):
    bsz, seq, _ = x3.shape
    prompt = ret_s is None
    m = bsz * seq
    x = x3.reshape(m, D_MODEL)
    rows_per_tile = _row_tile(m)
    tab_rows = seq if prompt else rows_per_tile
    tab_pos = pos if prompt else jnp.broadcast_to(pos, (tab_rows,))
    rope_tabs, rope_shifts = _rope_tables(tab_pos)
    rot_tabs, rot_shifts = _ret_rot_tables(tab_pos)
    if prompt:
        decay, q_dec, k_dec, c_dec = _ret_decay(RET_CHUNK)
        ret_tabs = (decay,
                    jnp.broadcast_to(q_dec[:, :, None], decay.shape),
                    jnp.broadcast_to(k_dec[:, :, None], decay.shape),
                    c_dec)
    else:
        decay, q_dec, k_dec, c_dec = _ret_decay(seq)
        step_dec = jnp.stack([decay[:, 0, 0], q_dec[:, 0], k_dec[:, 0], c_dec])
    mem_k = mem_k.reshape(DEPTH, bsz, MEM_LEN, MEM_DIM)
    mem_v = mem_v.reshape(DEPTH, bsz, MEM_LEN, MEM_DIM)
    if not prompt:
        wb = swa_k.shape[2]
        swa_k = swa_k.reshape(swa_k.shape[0], bsz, wb, SWA_KV)
        swa_v = swa_v.reshape(swa_v.shape[0], bsz, wb, SWA_KV)

    def ln(i, s):
        return ln_g[i, s].reshape(1, D_MODEL), ln_b[i, s].reshape(1, D_MODEL)

    new_k, new_v, new_s = [], [], []
    for i in range(DEPTH):
        j = i // 2
        x = _ffn(x, ffn_w_gu, ffn_w_down, i, 0, *ln(i, 0))
        if i % 2 == 0:
            qkv = _proj(x, swa_w_in, j, n_rot=(SWA_Q + SWA_KV) // PROJ_TILE_N, tables=rope_tabs,
                        shifts=rope_shifts, seq=tab_rows)
            if prompt:
                qkv3 = qkv.reshape(bsz, seq, SWA_IN_WIDTH)
                att = _swa_prompt(qkv3, mem_k[i], mem_v[i], swa_sinks[j])
                wlen = min(WINDOW, seq)
                new_k.append(qkv3[:, seq - wlen:, SWA_Q:SWA_Q + SWA_KV]
                             .reshape(bsz, wlen, SWA_KV_HEADS, HEAD_DIM))
                new_v.append(qkv3[:, seq - wlen:, SWA_Q + SWA_KV:SWA_Q + 2 * SWA_KV]
                             .reshape(bsz, wlen, SWA_KV_HEADS, HEAD_DIM))
            else:
                rows = qkv.reshape(bsz, SWA_IN_WIDTH // HEAD_DIM, HEAD_DIM)
                att, nk, nv = _swa_step(rows, swa_k, swa_v, j, mem_k, mem_v, i, swa_sinks[j])
                new_k.append(nk.reshape(bsz, wb, SWA_KV_HEADS, HEAD_DIM))
                new_v.append(nv.reshape(bsz, wb, SWA_KV_HEADS, HEAD_DIM))
            w_out = swa_w_out
        else:
            qkvg = _proj(x, ret_w_in, j, n_rot=(2 * RET_QK) // PROJ_TILE_N, tables=rot_tabs,
                         shifts=rot_shifts, seq=tab_rows)
            if prompt:
                att, ns = _ret_prompt(qkvg.reshape(bsz, seq, RET_IN_WIDTH), mem_k[i], mem_v[i], ret_tabs)
            else:
                rows = qkvg.reshape(bsz, RET_IN_WIDTH // HEAD_DIM, HEAD_DIM)
                cols = jnp.swapaxes(rows, 1, 2)
                att, ns = _ret_step(rows, cols, ret_s, j, mem_k, mem_v, i, step_dec)
            new_s.append(ns)
            w_out = ret_w_out
        x = _out_ln(att.reshape(m, D_MODEL), w_out, j, x, *ln(i, 1))
        x = _ffn(x, ffn_w_gu, ffn_w_down, i, 1, *ln(i, 2))
    return x.reshape(bsz, seq, D_MODEL), jnp.stack(new_k), jnp.stack(new_v), jnp.stack(new_s)


def kernel(x_prompt, x_sample, cache_swa_k, cache_swa_v, state_ret, cache_mem_k, cache_mem_v,
           mem_prompt, ln_g, ln_b, ffn_w_gu, ffn_w_down, w_mem_kv, swa_w_in, swa_w_out,
           swa_sinks, ret_w_in, ret_w_out):
    bp = mem_prompt.shape[0]
    mem2 = mem_prompt.reshape(bp * MEM_LEN, D_MODEL)
    mem_kv = jnp.stack([_proj(mem2, w_mem_kv, l) for l in range(DEPTH)])
    mem_kv = mem_kv.reshape(DEPTH, bp, MEM_LEN, 2 * MEM_DIM)
    mem_k_prompt = mem_kv[..., :MEM_DIM].reshape(DEPTH, bp, MEM_LEN, MEM_HEADS, HEAD_DIM)
    mem_v_prompt = mem_kv[..., MEM_DIM:].reshape(DEPTH, bp, MEM_LEN, MEM_HEADS, HEAD_DIM)

    pos_prompt = jnp.arange(x_prompt.shape[1], dtype=jnp.int32)
    pos_sample = PAST_LEN + jnp.arange(x_sample.shape[1], dtype=jnp.int32)
    weights = (ln_g, ln_b, ffn_w_gu, ffn_w_down, swa_w_in, swa_w_out, swa_sinks, ret_w_in, ret_w_out)

    y_prompt, swa_k_prompt, swa_v_prompt, ret_state_prompt = _trunk(
        x_prompt, pos_prompt, None, None, None, mem_k_prompt, mem_v_prompt, *weights)
    y_sample, swa_k_sample, swa_v_sample, ret_state_sample = _trunk(
        x_sample, pos_sample, cache_swa_k, cache_swa_v, state_ret, cache_mem_k, cache_mem_v, *weights)
    return (y_prompt, y_sample, swa_k_prompt, swa_v_prompt, swa_k_sample, swa_v_sample,
            ret_state_prompt, ret_state_sample, mem_k_prompt, mem_v_prompt)
```

```python
import functools

import jax
import jax.numpy as jnp
from jax import lax
from jax.experimental import pallas as pl
from jax.experimental.pallas import tpu as pltpu

F32 = jnp.float32
BF16 = jnp.bfloat16

D_MODEL = 2048
DEPTH = 4
PAST_LEN = 16384
HEAD_DIM = 128
MEM_LEN = 256
MEM_HEADS = 4
MEM_DIM = MEM_HEADS * HEAD_DIM
SELF_WIDTH = D_MODEL - MEM_DIM
SWA_Q_HEADS = SELF_WIDTH // HEAD_DIM
SWA_KV_HEADS = SWA_Q_HEADS // 3
SWA_GROUP = SWA_Q_HEADS // SWA_KV_HEADS
WINDOW = 128
ROPE_THETA = 500000.0
ROPE_DIM = HEAD_DIM // 4
RET_DK = 128
RET_DV = 2 * RET_DK
RET_HEADS = SELF_WIDTH // RET_DV
RET_CHUNK = 128
RET_ROT_BASE = 10000.0
D_FF = ((8 * D_MODEL // 3 + 255) // 256) * 256
LN_EPS = 1e-5
HEAD_NORM_EPS = 1e-6
ALPHA = (2.0 * DEPTH) ** 0.25
NEG_INF = -1e30
ATT_SCALE = HEAD_DIM ** -0.5
RET_K_SCALE = RET_DK ** -0.5

SWA_Q = SWA_Q_HEADS * HEAD_DIM
SWA_KV = SWA_KV_HEADS * HEAD_DIM
SWA_IN_WIDTH = SWA_Q + 2 * SWA_KV + MEM_DIM
RET_QK = RET_HEADS * RET_DK
RET_V = RET_HEADS * RET_DV
RET_IN_WIDTH = 2 * RET_QK + 2 * RET_V + MEM_DIM

VMEM_LIMIT_BYTES = 58 * 1024 * 1024
LANES = 128
PROJ_TILE_N = 512
OUT_TILE_K = 512
ROT_ROW_CHUNK = 256


def _row_tile(m):
    return 1024 if m % 1024 == 0 else m


def _params(*sem):
    return pltpu.CompilerParams(dimension_semantics=sem, vmem_limit_bytes=VMEM_LIMIT_BYTES)


def _layer_norm_rows(z, g, b):
    mu = jnp.mean(z, axis=-1, keepdims=True)
    zc = z - mu
    var = jnp.mean(zc * zc, axis=-1, keepdims=True)
    return zc * lax.rsqrt(var + LN_EPS) * g + b


def _bf16_round(v):
    return v.astype(BF16).astype(F32)


def _ffn_kernel(x_ref, wg_ref, wu_ref, wd_ref, g_ref, b_ref, o_ref, xb_ref, wgu_ref, *, nk, tk):
    k = pl.program_id(1)

    @pl.when(k == 0)
    def _():
        xb_ref[...] = x_ref[...].astype(BF16)
        o_ref[...] = jnp.zeros_like(o_ref)

    wgu_ref[:, :tk] = wg_ref[...].astype(BF16)
    wgu_ref[:, tk:] = wu_ref[...].astype(BF16)
    gu = jnp.dot(xb_ref[...], wgu_ref[...], preferred_element_type=F32)
    h = (jax.nn.silu(gu[:, :tk]) * gu[:, tk:]).astype(BF16)
    o_ref[...] += jnp.dot(h, wd_ref[...].astype(BF16), preferred_element_type=F32)

    @pl.when(k == nk - 1)
    def _():
        z = ALPHA * x_ref[...] + 0.5 * o_ref[...]
        o_ref[...] = _layer_norm_rows(z, g_ref[...], b_ref[...])


def _ffn(x, w_gu, w_down, layer, slot, g, b):
    m = x.shape[0]
    tm = _row_tile(m)
    tk = 256 if tm >= 1024 else 512
    nk = D_FF // tk
    return pl.pallas_call(
        functools.partial(_ffn_kernel, nk=nk, tk=tk),
        grid=(m // tm, nk),
        in_specs=[
            pl.BlockSpec((tm, D_MODEL), lambda i, k: (i, 0), pipeline_mode=pl.Buffered(1)),
            pl.BlockSpec((None, None, D_MODEL, tk), lambda i, k: (layer, slot, 0, k)),
            pl.BlockSpec((None, None, D_MODEL, tk), lambda i, k: (layer, slot, 0, nk + k)),
            pl.BlockSpec((None, None, tk, D_MODEL), lambda i, k: (layer, slot, k, 0)),
            pl.BlockSpec((1, D_MODEL), lambda i, k: (0, 0)),
            pl.BlockSpec((1, D_MODEL), lambda i, k: (0, 0)),
        ],
        out_specs=pl.BlockSpec((tm, D_MODEL), lambda i, k: (i, 0)),
        out_shape=jax.ShapeDtypeStruct((m, D_MODEL), F32),
        scratch_shapes=[pltpu.VMEM((tm, D_MODEL), BF16), pltpu.VMEM((D_MODEL, 2 * tk), BF16)],
        compiler_params=_params("parallel", "arbitrary"),
        name="ffn_ln",
    )(x, w_gu, w_gu, w_down, g, b)


def _proj_kernel(x_ref, w_ref, *rest, n_rot, shifts):
    if n_rot:
        c_ref, a_ref, b_ref, o_ref, xb_ref = rest
    else:
        o_ref, xb_ref = rest
    j = pl.program_id(1)

    @pl.when(j == 0)
    def _():
        xb_ref[...] = x_ref[...].astype(BF16)

    wb = w_ref[...].astype(BF16)
    if not n_rot:
        o_ref[...] = jnp.dot(xb_ref[...], wb, preferred_element_type=F32)
        return

    tm, tn = o_ref.shape
    rc = min(tm, ROT_ROW_CHUNK)

    @pl.when(j < n_rot)
    def _():
        for r0 in range(0, tm, rc):
            rows = slice(r0, r0 + rc)
            y = jnp.dot(xb_ref[rows, :], wb, preferred_element_type=F32)
            c, a, b = c_ref[rows, :], a_ref[rows, :], b_ref[rows, :]
            for h in range(tn // LANES):
                yh = y[:, h * LANES:(h + 1) * LANES]
                o_ref[rows, h * LANES:(h + 1) * LANES] = (
                    yh * c + pltpu.roll(yh, shifts[0], 1) * a + pltpu.roll(yh, shifts[1], 1) * b)

    @pl.when(j >= n_rot)
    def _():
        o_ref[...] = jnp.dot(xb_ref[...], wb, preferred_element_type=F32)


def _proj(x, w, layer, n_rot=0, tables=None, shifts=None, seq=None):
    m = x.shape[0]
    n = w.shape[-1]
    tm = _row_tile(m)
    tn = PROJ_TILE_N
    in_specs = [
        pl.BlockSpec((tm, D_MODEL), lambda i, j: (i, 0), pipeline_mode=pl.Buffered(1)),
        pl.BlockSpec((None, D_MODEL, tn), lambda i, j: (layer, 0, j)),
    ]
    args = [x, w]
    if n_rot:
        tiles_per_seq = seq // tm
        tab = pl.BlockSpec((tm, LANES), lambda i, j: (i % tiles_per_seq, 0))
        in_specs += [tab, tab, tab]
        args += list(tables)
    return pl.pallas_call(
        functools.partial(_proj_kernel, n_rot=n_rot, shifts=shifts),
        grid=(m // tm, n // tn),
        in_specs=in_specs,
        out_specs=pl.BlockSpec((tm, tn), lambda i, j: (i, j)),
        out_shape=jax.ShapeDtypeStruct((m, n), F32),
        scratch_shapes=[pltpu.VMEM((tm, D_MODEL), BF16)],
        compiler_params=_params("parallel", "arbitrary"),
        name="proj",
    )(*args)


def _out_ln_kernel(att_ref, w_ref, x_ref, g_ref, b_ref, o_ref, *, nk):
    k = pl.program_id(1)

    @pl.when(k == 0)
    def _():
        o_ref[...] = jnp.zeros_like(o_ref)

    o_ref[...] += jnp.dot(att_ref[...].astype(BF16), w_ref[...].astype(BF16),
                          preferred_element_type=F32)

    @pl.when(k == nk - 1)
    def _():
        z = ALPHA * x_ref[...] + o_ref[...]
        o_ref[...] = _layer_norm_rows(z, g_ref[...], b_ref[...])


def _out_ln(att, w_out, layer, x, g, b):
    m = x.shape[0]
    tm = _row_tile(m)
    tk = OUT_TILE_K
    nk = D_MODEL // tk
    return pl.pallas_call(
        functools.partial(_out_ln_kernel, nk=nk),
        grid=(m // tm, nk),
        in_specs=[
            pl.BlockSpec((tm, tk), lambda i, k: (i, k)),
            pl.BlockSpec((None, tk, D_MODEL), lambda i, k: (layer, k, 0)),
            pl.BlockSpec((tm, D_MODEL), lambda i, k: (i, 0), pipeline_mode=pl.Buffered(1)),
            pl.BlockSpec((1, D_MODEL), lambda i, k: (0, 0)),
            pl.BlockSpec((1, D_MODEL), lambda i, k: (0, 0)),
        ],
        out_specs=pl.BlockSpec((tm, D_MODEL), lambda i, k: (i, 0)),
        out_shape=jax.ShapeDtypeStruct((m, D_MODEL), F32),
        compiler_params=_params("parallel", "arbitrary"),
        name="out_ln",
    )(att, w_out, x, g, b)


def _dot_nt(a, b):
    return lax.dot_general(a, b, (((1,), (1,)), ((), ())), preferred_element_type=F32)


def _mem_attend_block(qm_ref, mk_ref, mv_ref, o_ref, col0):
    for h in range(MEM_HEADS):
        sl = slice(h * HEAD_DIM, (h + 1) * HEAD_DIM)
        q = qm_ref[0, :, sl].astype(BF16)
        s = _dot_nt(q, mk_ref[0, :, sl].astype(BF16)) * ATT_SCALE
        p = jnp.exp(s - jnp.max(s, axis=-1, keepdims=True))
        p = p * (1.0 / jnp.sum(p, axis=-1, keepdims=True))
        o = jnp.dot(p.astype(BF16), mv_ref[0, :, sl].astype(BF16), preferred_element_type=F32)
        o_ref[0, :, col0 + h * HEAD_DIM:col0 + (h + 1) * HEAD_DIM] = o.astype(o_ref.dtype)


def _swa_prompt_kernel(sink_ref, q_ref, kc_ref, kp_ref, vc_ref, vp_ref, qm_ref, mk_ref, mv_ref, o_ref,
                       s_ref, p_ref):
    n = pl.program_id(1)
    nq = SWA_Q_HEADS
    for h in range(SWA_KV_HEADS):
        sl = slice(h * HEAD_DIM, (h + 1) * HEAD_DIM)
        q3 = jnp.concatenate(
            [q_ref[0, :, (h * SWA_GROUP + g) * HEAD_DIM:(h * SWA_GROUP + g + 1) * HEAD_DIM]
             for g in range(SWA_GROUP)], axis=0).astype(BF16)
        k2 = jnp.concatenate([kp_ref[0, :, sl], kc_ref[0, :, sl]], axis=0).astype(BF16)
        s = _dot_nt(q3, k2) * ATT_SCALE
        s_ref[h * SWA_GROUP:(h + 1) * SWA_GROUP] = s.reshape(SWA_GROUP, WINDOW, 2 * WINDOW)
        qm = qm_ref[0, :, sl].astype(BF16)
        s_ref[nq + h] = _dot_nt(qm, mk_ref[0, :, sl].astype(BF16)) * ATT_SCALE

    qi = lax.broadcasted_iota(jnp.int32, (WINDOW, 2 * WINDOW), 0)
    kj = lax.broadcasted_iota(jnp.int32, (WINDOW, 2 * WINDOW), 1)
    first_key = jnp.where(n > 0, qi, WINDOW)
    ok = (kj >= first_key) & (kj <= qi + WINDOW)
    s = jnp.where(ok[None], s_ref[0:nq], NEG_INF)
    sink = sink_ref[...]
    m = jnp.maximum(jnp.max(s, axis=-1, keepdims=True), sink)
    p = jnp.exp(s - m)
    den = jnp.sum(p, axis=-1, keepdims=True) + jnp.exp(sink - m)
    p_ref[0:nq] = (p * (1.0 / den)).astype(BF16)

    s = s_ref[nq:nq + MEM_HEADS]
    p = jnp.exp(s - jnp.max(s, axis=-1, keepdims=True))
    p_ref[nq:nq + MEM_HEADS] = (p * (1.0 / jnp.sum(p, axis=-1, keepdims=True))).astype(BF16)

    for h in range(SWA_KV_HEADS):
        sl = slice(h * HEAD_DIM, (h + 1) * HEAD_DIM)
        v2 = jnp.concatenate([vp_ref[0, :, sl], vc_ref[0, :, sl]], axis=0).astype(BF16)
        p3 = p_ref[h * SWA_GROUP:(h + 1) * SWA_GROUP].reshape(SWA_GROUP * WINDOW, 2 * WINDOW)
        o = jnp.dot(p3, v2, preferred_element_type=F32)
        for g in range(SWA_GROUP):
            hq = h * SWA_GROUP + g
            o_ref[0, :, hq * HEAD_DIM:(hq + 1) * HEAD_DIM] = (
                o[g * WINDOW:(g + 1) * WINDOW].astype(o_ref.dtype))
        om = jnp.dot(p_ref[nq + h], mv_ref[0, :, sl].astype(BF16), preferred_element_type=F32)
        o_ref[0, :, SWA_Q + h * HEAD_DIM:SWA_Q + (h + 1) * HEAD_DIM] = om.astype(o_ref.dtype)


def _swa_prompt(qkv, mem_k, mem_v, sinks):
    bsz, seq, _ = qkv.shape
    nb = seq // WINDOW
    kcol = SWA_Q // SWA_KV
    vcol = kcol + 1
    mcol = vcol + 1
    blk = lambda w, f: pl.BlockSpec((1, WINDOW, w), f)
    prev = lambda n: jnp.maximum(n - 1, 0)
    nheads = SWA_Q_HEADS + MEM_HEADS
    assert MEM_LEN == 2 * WINDOW
    return pl.pallas_call(
        _swa_prompt_kernel,
        grid=(bsz, nb),
        in_specs=[
            pl.BlockSpec((SWA_Q_HEADS, 1, 1), lambda b, n: (0, 0, 0)),
            blk(SWA_Q, lambda b, n: (b, n, 0)),
            blk(SWA_KV, lambda b, n: (b, n, kcol)),
            blk(SWA_KV, lambda b, n: (b, prev(n), kcol)),
            blk(SWA_KV, lambda b, n: (b, n, vcol)),
            blk(SWA_KV, lambda b, n: (b, prev(n), vcol)),
            blk(MEM_DIM, lambda b, n: (b, n, mcol)),
            pl.BlockSpec((1, MEM_LEN, MEM_DIM), lambda b, n: (b, 0, 0)),
            pl.BlockSpec((1, MEM_LEN, MEM_DIM), lambda b, n: (b, 0, 0)),
        ],
        out_specs=blk(D_MODEL, lambda b, n: (b, n, 0)),
        out_shape=jax.ShapeDtypeStruct((bsz, seq, D_MODEL), BF16),
        scratch_shapes=[pltpu.VMEM((nheads, WINDOW, 2 * WINDOW), F32),
                        pltpu.VMEM((nheads, WINDOW, 2 * WINDOW), BF16)],
        compiler_params=_params("parallel", "arbitrary"),
        name="swa_prompt",
    )(sinks.reshape(SWA_Q_HEADS, 1, 1), qkv, qkv, qkv, qkv, qkv, qkv, mem_k, mem_v)


def _head_norm_gate(o, gate):
    mu = jnp.mean(o, axis=-1, keepdims=True)
    oc = o - mu
    var = jnp.mean(oc * oc, axis=-1, keepdims=True)
    return jax.nn.silu(gate) * (oc * lax.rsqrt(var + HEAD_NORM_EPS))


def _ret_prompt_kernel(cdec_ref, q_ref, k_ref, v_ref, g_ref, qm_ref, mk_ref, mv_ref,
                       decay_ref, qdec_ref, kdec_ref, o_ref, s_out_ref, state_ref, *, nc):
    c = pl.program_id(1)

    @pl.when(c == 0)
    def _():
        state_ref[...] = jnp.zeros_like(state_ref)

    for h in range(RET_HEADS):
        ksl = slice(h * RET_DK, (h + 1) * RET_DK)
        vsl = slice(h * RET_DV, (h + 1) * RET_DV)
        qc = q_ref[0, :, ksl]
        kc = k_ref[0, :, ksl] * RET_K_SCALE
        vb = v_ref[0, :, vsl].astype(BF16)
        st = state_ref[h]
        inner = _dot_nt(qc.astype(BF16), kc.astype(BF16)) * decay_ref[h]
        o = (jnp.dot(inner.astype(BF16), vb, preferred_element_type=F32)
             + jnp.dot((qc * qdec_ref[h]).astype(BF16), st.astype(BF16), preferred_element_type=F32))
        kd = (kc * kdec_ref[h]).astype(BF16)
        state_ref[h] = cdec_ref[h] * st + lax.dot_general(
            kd, vb, (((0,), (0,)), ((), ())), preferred_element_type=F32)
        o_ref[0, :, vsl] = _head_norm_gate(o, g_ref[0, :, vsl]).astype(o_ref.dtype)
    _mem_attend_block(qm_ref, mk_ref, mv_ref, o_ref, RET_V)

    @pl.when(c == nc - 1)
    def _():
        s_out_ref[0] = state_ref[...]


def _ret_prompt(qkvg, mem_k, mem_v, tables):
    bsz, seq, _ = qkvg.shape
    nc = seq // RET_CHUNK
    decay, qdec, kdec, cdec = tables
    blk = lambda w, f: pl.BlockSpec((1, RET_CHUNK, w), f)
    tab = pl.BlockSpec((RET_HEADS, RET_CHUNK, RET_CHUNK), lambda b, c: (0, 0, 0))
    return pl.pallas_call(
        functools.partial(_ret_prompt_kernel, nc=nc),
        grid=(bsz, nc),
        in_specs=[
            pl.BlockSpec(memory_space=pltpu.SMEM),
            blk(RET_QK, lambda b, c: (b, c, 0)),
            blk(RET_QK, lambda b, c: (b, c, 1)),
            blk(RET_V, lambda b, c: (b, c, 1)),
            blk(RET_V, lambda b, c: (b, c, 2)),
            blk(MEM_DIM, lambda b, c: (b, c, (2 * RET_QK + 2 * RET_V) // MEM_DIM)),
            pl.BlockSpec((1, MEM_LEN, MEM_DIM), lambda b, c: (b, 0, 0)),
            pl.BlockSpec((1, MEM_LEN, MEM_DIM), lambda b, c: (b, 0, 0)),
            tab, tab, tab,
        ],
        out_specs=[
            blk(D_MODEL, lambda b, c: (b, c, 0)),
            pl.BlockSpec((1, RET_HEADS, RET_DK, RET_DV), lambda b, c: (b, 0, 0, 0)),
        ],
        out_shape=[
            jax.ShapeDtypeStruct((bsz, seq, D_MODEL), BF16),
            jax.ShapeDtypeStruct((bsz, RET_HEADS, RET_DK, RET_DV), F32),
        ],
        scratch_shapes=[pltpu.VMEM((RET_HEADS, RET_DK, RET_DV), F32)],
        compiler_params=_params("parallel", "arbitrary"),
        name="ret_prompt",
    )(cdec, qkvg, qkvg, qkvg, qkvg, qkvg, mem_k, mem_v, decay, qdec, kdec)


def _mem_attend_row(rows_ref, row0, mk_ref, mv_ref, o_ref, out_row0):
    for h in range(MEM_HEADS):
        sl = slice(h * HEAD_DIM, (h + 1) * HEAD_DIM)
        q = rows_ref[0, row0 + h:row0 + h + 1, :]
        s = jnp.sum(mk_ref[0, :, sl] * q, axis=1, keepdims=True) * ATT_SCALE
        p = jnp.exp(s - jnp.max(s, axis=0, keepdims=True))
        p = p * (1.0 / jnp.sum(p, axis=0, keepdims=True))
        o = jnp.sum(p * mv_ref[0, :, sl], axis=0, keepdims=True)
        o_ref[0, out_row0 + h:out_row0 + h + 1, :] = o


def _swa_step_kernel(sink_ref, rows_ref, kbuf_ref, vbuf_ref, mk_ref, mv_ref, o_ref, nk_ref, nv_ref):
    krow0 = SWA_Q_HEADS
    vrow0 = krow0 + SWA_KV_HEADS
    mrow0 = vrow0 + SWA_KV_HEADS
    wb = kbuf_ref.shape[1]
    for h in range(SWA_KV_HEADS):
        sl = slice(h * HEAD_DIM, (h + 1) * HEAD_DIM)
        k_new = rows_ref[0, krow0 + h:krow0 + h + 1, :]
        v_new = rows_ref[0, vrow0 + h:vrow0 + h + 1, :]
        kb = kbuf_ref[0, :, sl]
        vb = vbuf_ref[0, :, sl]
        for g in range(SWA_GROUP):
            hq = h * SWA_GROUP + g
            q = rows_ref[0, hq:hq + 1, :]
            s_buf = jnp.sum(kb * q, axis=1, keepdims=True) * ATT_SCALE
            s_new = jnp.sum(k_new * q, axis=1, keepdims=True) * ATT_SCALE
            sink = sink_ref[hq]
            m = jnp.maximum(jnp.maximum(jnp.max(s_buf, axis=0, keepdims=True), s_new), sink)
            p_buf = jnp.exp(s_buf - m)
            p_new = jnp.exp(s_new - m)
            inv = 1.0 / (jnp.sum(p_buf, axis=0, keepdims=True) + p_new + jnp.exp(sink - m))
            o = jnp.sum((p_buf * inv) * vb, axis=0, keepdims=True) + (p_new * inv) * v_new
            o_ref[0, hq:hq + 1, :] = o
        nk_ref[0, wb - 1:wb, sl] = k_new
        nv_ref[0, wb - 1:wb, sl] = v_new
    nk_ref[0, 0:wb - 1, :] = kbuf_ref[0, 1:wb, :]
    nv_ref[0, 0:wb - 1, :] = vbuf_ref[0, 1:wb, :]
    _mem_attend_row(rows_ref, mrow0, mk_ref, mv_ref, o_ref, SWA_Q_HEADS)


def _swa_step(rows, cache_k, cache_v, j, mem_k, mem_v, i, sinks):
    bsz = rows.shape[0]
    wb = cache_k.shape[2]
    cache = pl.BlockSpec((None, 1, wb, SWA_KV), lambda b: (j, b, 0, 0))
    mem = pl.BlockSpec((None, 1, MEM_LEN, MEM_DIM), lambda b: (i, b, 0, 0))
    new = pl.BlockSpec((1, wb, SWA_KV), lambda b: (b, 0, 0))
    nrows = D_MODEL // HEAD_DIM
    return pl.pallas_call(
        _swa_step_kernel,
        grid=(bsz,),
        in_specs=[
            pl.BlockSpec(memory_space=pltpu.SMEM),
            pl.BlockSpec((1,) + rows.shape[1:], lambda b: (b, 0, 0)),
            cache, cache, mem, mem,
        ],
        out_specs=[pl.BlockSpec((1, nrows, HEAD_DIM), lambda b: (b, 0, 0)), new, new],
        out_shape=[
            jax.ShapeDtypeStruct((bsz, nrows, HEAD_DIM), F32),
            jax.ShapeDtypeStruct((bsz, wb, SWA_KV), F32),
            jax.ShapeDtypeStruct((bsz, wb, SWA_KV), F32),
        ],
        compiler_params=_params("parallel"),
        name="swa_step",
    )(sinks, rows, cache_k, cache_v, mem_k, mem_v)


def _ret_step_kernel(dec_ref, rows_ref, cols_ref, s_ref, mk_ref, mv_ref, o_ref, s_out_ref):
    krow0 = RET_HEADS
    vrow0 = 2 * RET_HEADS
    grow0 = vrow0 + 2 * RET_HEADS
    mrow0 = grow0 + 2 * RET_HEADS
    for h in range(RET_HEADS):
        q_row = rows_ref[0, h:h + 1, :]
        k_row = rows_ref[0, krow0 + h:krow0 + h + 1, :] * RET_K_SCALE
        q_col = cols_ref[0, :, h:h + 1]
        k_col = cols_ref[0, :, krow0 + h:krow0 + h + 1] * RET_K_SCALE
        inner = jnp.sum(q_row * k_row, axis=1, keepdims=True) * dec_ref[0, h]
        qd = q_col * dec_ref[1, h]
        kd = k_col * dec_ref[2, h]
        halves = []
        for t in range(2):
            lsl = slice(t * HEAD_DIM, (t + 1) * HEAD_DIM)
            v = rows_ref[0, vrow0 + 2 * h + t:vrow0 + 2 * h + t + 1, :]
            st = s_ref[0, h, :, lsl]
            halves.append(inner * v + jnp.sum(qd * st, axis=0, keepdims=True))
            s_out_ref[0, h, :, lsl] = dec_ref[3, h] * st + kd * v
        mu = (jnp.sum(halves[0], axis=1, keepdims=True)
              + jnp.sum(halves[1], axis=1, keepdims=True)) * (1.0 / RET_DV)
        cen = [o - mu for o in halves]
        var = (jnp.sum(cen[0] * cen[0], axis=1, keepdims=True)
               + jnp.sum(cen[1] * cen[1], axis=1, keepdims=True)) * (1.0 / RET_DV)
        rstd = lax.rsqrt(var + HEAD_NORM_EPS)
        for t in range(2):
            gate = rows_ref[0, grow0 + 2 * h + t:grow0 + 2 * h + t + 1, :]
            o_ref[0, 2 * h + t:2 * h + t + 1, :] = jax.nn.silu(gate) * (cen[t] * rstd)
    _mem_attend_row(rows_ref, mrow0, mk_ref, mv_ref, o_ref, 2 * RET_HEADS)


def _ret_step(rows, cols, state, j, mem_k, mem_v, i, dec):
    bsz = rows.shape[0]
    st_in = pl.BlockSpec((None, 1, RET_HEADS, RET_DK, RET_DV), lambda b: (j, b, 0, 0, 0))
    st_out = pl.BlockSpec((1, RET_HEADS, RET_DK, RET_DV), lambda b: (b, 0, 0, 0))
    mem = pl.BlockSpec((None, 1, MEM_LEN, MEM_DIM), lambda b: (i, b, 0, 0))
    nrows = D_MODEL // HEAD_DIM
    return pl.pallas_call(
        _ret_step_kernel,
        grid=(bsz,),
        in_specs=[
            pl.BlockSpec(memory_space=pltpu.SMEM),
            pl.BlockSpec((1,) + rows.shape[1:], lambda b: (b, 0, 0)),
            pl.BlockSpec((1,) + cols.shape[1:], lambda b: (b, 0, 0)),
            st_in, mem, mem,
        ],
        out_specs=[pl.BlockSpec((1, nrows, HEAD_DIM), lambda b: (b, 0, 0)), st_out],
        out_shape=[
            jax.ShapeDtypeStruct((bsz, nrows, HEAD_DIM), F32),
            jax.ShapeDtypeStruct((bsz, RET_HEADS, RET_DK, RET_DV), F32),
        ],
        compiler_params=_params("parallel"),
        name="ret_step",
    )(dec, rows, cols, state, mem_k, mem_v)


def _rope_tables(pos):
    half = ROPE_DIM // 2
    inv = ROPE_THETA ** (-jnp.arange(half, dtype=F32) / half)
    ang = pos.astype(F32)[:, None] * inv[None, :]
    cos, sin = jnp.cos(ang), jnp.sin(ang)
    n = pos.shape[0]
    rest = HEAD_DIM - ROPE_DIM
    c = jnp.concatenate([cos, cos, jnp.ones((n, rest), F32)], axis=-1)
    a = jnp.concatenate([-sin, jnp.zeros((n, HEAD_DIM - half), F32)], axis=-1)
    b = jnp.concatenate([jnp.zeros((n, half), F32), sin, jnp.zeros((n, rest), F32)], axis=-1)
    return (c, a, b), (HEAD_DIM - half, half)


def _ret_rot_tables(pos):
    half = RET_DK // 2
    angle = RET_ROT_BASE ** (-jnp.linspace(0.0, 1.0, half, dtype=F32))
    ang = pos.astype(F32)[:, None] * angle[None, :]
    cos, sin = jnp.cos(ang), jnp.sin(ang)
    n = pos.shape[0]
    zero = jnp.zeros_like(sin)
    c = jnp.stack([cos, cos], axis=-1).reshape(n, RET_DK)
    a = jnp.stack([-sin, zero], axis=-1).reshape(n, RET_DK)
    b = jnp.stack([zero, sin], axis=-1).reshape(n, RET_DK)
    return (c, a, b), (RET_DK - 1, 1)


def _ret_decay(chunk):
    log_g = jnp.log1p(-jnp.exp2(-5.0 - jnp.arange(RET_HEADS, dtype=F32)))
    n = jnp.arange(chunk, dtype=F32)
    rel = n[:, None] - n[None, :]
    decay = jnp.where(rel >= 0, jnp.exp(jnp.maximum(rel, 0.0) * log_g[:, None, None]), 0.0)
    q_dec = jnp.exp((n + 1.0) * log_g[:, None])
    k_dec = jnp.exp((chunk - 1.0 - n) * log_g[:, None])
    c_dec = jnp.exp(chunk * log_g)
    return decay, q_dec, k_dec, c_dec


def _trunk(x3, pos, swa_k, swa_v, ret_s, mem_k, mem_v, ln_g, ln_b, ffn_w_gu, ffn_w_down,
           swa_w_in, swa_w_out, swa_sinks, ret_w_in, ret_w_out):
    bsz, seq, _ = x3.shape
    prompt = ret_s is None
    m = bsz * seq
    x = x3.reshape(m, D_MODEL)
    rows_per_tile = _row_tile(m)
    tab_rows = seq if prompt else rows_per_tile
    tab_pos = pos if prompt else jnp.broadcast_to(pos, (tab_rows,))
    rope_tabs, rope_shifts = _rope_tables(tab_pos)
    rot_tabs, rot_shifts = _ret_rot_tables(tab_pos)
    if prompt:
        decay, q_dec, k_dec, c_dec = _ret_decay(RET_CHUNK)
        ret_tabs = (decay,
                    jnp.broadcast_to(q_dec[:, :, None], decay.shape),
                    jnp.broadcast_to(k_dec[:, :, None], decay.shape),
                    c_dec)
    else:
        decay, q_dec, k_dec, c_dec = _ret_decay(seq)
        step_dec = jnp.stack([decay[:, 0, 0], q_dec[:, 0], k_dec[:, 0], c_dec])
    mem_k = mem_k.reshape(DEPTH, bsz, MEM_LEN, MEM_DIM)
    mem_v = mem_v.reshape(DEPTH, bsz, MEM_LEN, MEM_DIM)
    if not prompt:
        wb = swa_k.shape[2]
        swa_k = swa_k.reshape(swa_k.shape[0], bsz, wb, SWA_KV)
        swa_v = swa_v.reshape(swa_v.shape[0], bsz, wb, SWA_KV)

    def ln(i, s):
        return ln_g[i, s].reshape(1, D_MODEL), ln_b[i, s].reshape(1, D_MODEL)

    new_k, new_v, new_s = [], [], []
    for i in range(DEPTH):
        j = i // 2
        x = _ffn(x, ffn_w_gu, ffn_w_down, i, 0, *ln(i, 0))
        if i % 2 == 0:
            qkv = _proj(x, swa_w_in, j, n_rot=(SWA_Q + SWA_KV) // PROJ_TILE_N, tables=rope_tabs,
                        shifts=rope_shifts, seq=tab_rows)
            if prompt:
                qkv3 = qkv.reshape(bsz, seq, SWA_IN_WIDTH)
                att = _swa_prompt(qkv3, mem_k[i], mem_v[i], swa_sinks[j])
                wlen = min(WINDOW, seq)
                new_k.append(qkv3[:, seq - wlen:, SWA_Q:SWA_Q + SWA_KV]
                             .reshape(bsz, wlen, SWA_KV_HEADS, HEAD_DIM))
                new_v.append(qkv3[:, seq - wlen:, SWA_Q + SWA_KV:SWA_Q + 2 * SWA_KV]
                             .reshape(bsz, wlen, SWA_KV_HEADS, HEAD_DIM))
            else:
                rows = qkv.reshape(bsz, SWA_IN_WIDTH // HEAD_DIM, HEAD_DIM)
                att, nk, nv = _swa_step(rows, swa_k, swa_v, j, mem_k, mem_v, i, swa_sinks[j])
                new_k.append(nk.reshape(bsz, wb, SWA_KV_HEADS, HEAD_DIM))
                new_v.append(nv.reshape(bsz, wb, SWA_KV_HEADS, HEAD_DIM))
            w_out = swa_w_out
        else:
            qkvg = _proj(x, ret_w_in, j, n_rot=(2 * RET_QK) // PROJ_TILE_N, tables=rot_tabs,
                         shifts=rot_shifts, seq=tab_rows)
            if prompt:
                att, ns = _ret_prompt(qkvg.reshape(bsz, seq, RET_IN_WIDTH), mem_k[i], mem_v[i], ret_tabs)
            else:
                rows = qkvg.reshape(bsz, RET_IN_WIDTH // HEAD_DIM, HEAD_DIM)
                cols = jnp.swapaxes(rows, 1, 2)
                att, ns = _ret_step(rows, cols, ret_s, j, mem_k, mem_v, i, step_dec)
            new_s.append(ns)
            w_out = ret_w_out
        x = _out_ln(att.reshape(m, D_MODEL), w_out, j, x, *ln(i, 1))
        x = _ffn(x, ffn_w_gu, ffn_w_down, i, 1, *ln(i, 2))
    return x.reshape(bsz, seq, D_MODEL), jnp.stack(new_k), jnp.stack(new_v), jnp.stack(new_s)


def kernel(x_prompt, x_sample, cache_swa_k, cache_swa_v, state_ret, cache_mem_k, cache_mem_v,
           mem_prompt, ln_g, ln_b, ffn_w_gu, ffn_w_down, w_mem_kv, swa_w_in, swa_w_out,
           swa_sinks, ret_w_in, ret_w_out):
    bp = mem_prompt.shape[0]
    mem2 = mem_prompt.reshape(bp * MEM_LEN, D_MODEL)
    mem_kv = jnp.stack([_proj(mem2, w_mem_kv, l) for l in range(DEPTH)])
    mem_kv = mem_kv.reshape(DEPTH, bp, MEM_LEN, 2 * MEM_DIM)
    mem_k_prompt = mem_kv[..., :MEM_DIM].reshape(DEPTH, bp, MEM_LEN, MEM_HEADS, HEAD_DIM)
    mem_v_prompt = mem_kv[..., MEM_DIM:].reshape(DEPTH, bp, MEM_LEN, MEM_HEADS, HEAD_DIM)

    pos_prompt = jnp.arange(x_prompt.shape[1], dtype=jnp.int32)
    pos_sample = PAST_LEN + jnp.arange(x_sample.shape[1], dtype=jnp.int32)
    weights = (ln_g, ln_b, ffn_w_gu, ffn_w_down, swa_w_in, swa_w_out, swa_sinks, ret_w_in, ret_w_out)

    y_prompt, swa_k_prompt, swa_v_prompt, ret_state_prompt = _trunk(
        x_prompt, pos_prompt, None, None, None, mem_k_prompt, mem_v_prompt, *weights)
    y_sample, swa_k_sample, swa_v_sample, ret_state_sample = _trunk(
        x_sample, pos_sample, cache_swa_k, cache_swa_v, state_ret, cache_mem_k, cache_mem_v, *weights)
    return (y_prompt, y_sample, swa_k_prompt, swa_v_prompt, swa_k_sample, swa_v_sample,
            ret_state_prompt, ret_state_sample, mem_k_prompt, mem_v_prompt)
```

```python
import functools

import jax
import jax.numpy as jnp
from jax import lax
from jax.experimental import pallas as pl
from jax.experimental.pallas import tpu as pltpu

F32 = jnp.float32
BF16 = jnp.bfloat16

D_MODEL = 2048
DEPTH = 4
PAST_LEN = 16384
HEAD_DIM = 128
MEM_LEN = 256
MEM_HEADS = 4
MEM_DIM = MEM_HEADS * HEAD_DIM
SELF_WIDTH = D_MODEL - MEM_DIM
SWA_Q_HEADS = SELF_WIDTH // HEAD_DIM
SWA_KV_HEADS = SWA_Q_HEADS // 3
SWA_GROUP = SWA_Q_HEADS // SWA_KV_HEADS
WINDOW = 128
ROPE_THETA = 500000.0
ROPE_DIM = HEAD_DIM // 4
RET_DK = 128
RET_DV = 2 * RET_DK
RET_HEADS = SELF_WIDTH // RET_DV
RET_CHUNK = 128
RET_ROT_BASE = 10000.0
D_FF = ((8 * D_MODEL // 3 + 255) // 256) * 256
LN_EPS = 1e-5
HEAD_NORM_EPS = 1e-6
ALPHA = (2.0 * DEPTH) ** 0.25
NEG_INF = -1e30
ATT_SCALE = HEAD_DIM ** -0.5
RET_K_SCALE = RET_DK ** -0.5

SWA_Q = SWA_Q_HEADS * HEAD_DIM
SWA_KV = SWA_KV_HEADS * HEAD_DIM
SWA_IN_WIDTH = SWA_Q + 2 * SWA_KV + MEM_DIM
RET_QK = RET_HEADS * RET_DK
RET_V = RET_HEADS * RET_DV
RET_IN_WIDTH = 2 * RET_QK + 2 * RET_V + MEM_DIM

VMEM_LIMIT_BYTES = 58 * 1024 * 1024
LANES = 128
PROJ_TILE_N = 512
OUT_TILE_N = 512
ROT_ROW_CHUNK = 256


def _row_tile(m):
    return 1024 if m % 1024 == 0 else m


def _params(*sem):
    return pltpu.CompilerParams(dimension_semantics=sem, vmem_limit_bytes=VMEM_LIMIT_BYTES)


def _layer_norm_rows(z, g, b):
    mu = jnp.mean(z, axis=-1, keepdims=True)
    zc = z - mu
    var = jnp.mean(zc * zc, axis=-1, keepdims=True)
    return zc * lax.rsqrt(var + LN_EPS) * g + b


def _ffn_kernel(x_ref, wg_ref, wu_ref, wd_ref, g_ref, b_ref, o_ref, xb_ref, wgu_ref, *, nk, tk):
    k = pl.program_id(1)

    @pl.when(k == 0)
    def _():
        xb_ref[...] = x_ref[...].astype(BF16)
        o_ref[...] = jnp.zeros_like(o_ref)

    wgu_ref[:, :tk] = wg_ref[...].astype(BF16)
    wgu_ref[:, tk:] = wu_ref[...].astype(BF16)
    gu = jnp.dot(xb_ref[...], wgu_ref[...], preferred_element_type=F32)
    h = (jax.nn.silu(gu[:, :tk]) * gu[:, tk:]).astype(BF16)
    o_ref[...] += jnp.dot(h, wd_ref[...].astype(BF16), preferred_element_type=F32)

    @pl.when(k == nk - 1)
    def _():
        z = ALPHA * x_ref[...] + 0.5 * o_ref[...]
        o_ref[...] = _layer_norm_rows(z, g_ref[...], b_ref[...])


def _ffn(x, w_gu, w_down, layer, slot, g, b):
    m = x.shape[0]
    tm = _row_tile(m)
    tk = 256 if tm >= 1024 else 512
    nk = D_FF // tk
    return pl.pallas_call(
        functools.partial(_ffn_kernel, nk=nk, tk=tk),
        grid=(m // tm, nk),
        in_specs=[
            pl.BlockSpec((tm, D_MODEL), lambda i, k: (i, 0), pipeline_mode=pl.Buffered(1)),
            pl.BlockSpec((None, None, D_MODEL, tk), lambda i, k: (layer, slot, 0, k)),
            pl.BlockSpec((None, None, D_MODEL, tk), lambda i, k: (layer, slot, 0, nk + k)),
            pl.BlockSpec((None, None, tk, D_MODEL), lambda i, k: (layer, slot, k, 0)),
            pl.BlockSpec((1, D_MODEL), lambda i, k: (0, 0)),
            pl.BlockSpec((1, D_MODEL), lambda i, k: (0, 0)),
        ],
        out_specs=pl.BlockSpec((tm, D_MODEL), lambda i, k: (i, 0)),
        out_shape=jax.ShapeDtypeStruct((m, D_MODEL), F32),
        scratch_shapes=[pltpu.VMEM((tm, D_MODEL), BF16), pltpu.VMEM((D_MODEL, 2 * tk), BF16)],
        compiler_params=_params("parallel", "arbitrary"),
        name="ffn_ln",
    )(x, w_gu, w_gu, w_down, g, b)


def _proj_kernel(x_ref, w_ref, *rest, n_rot, shifts):
    if n_rot:
        c_ref, a_ref, b_ref, o_ref, xb_ref = rest
    else:
        o_ref, xb_ref = rest
    j = pl.program_id(1)

    @pl.when(j == 0)
    def _():
        xb_ref[...] = x_ref[...].astype(BF16)

    wb = w_ref[...].astype(BF16)
    if not n_rot:
        o_ref[...] = jnp.dot(xb_ref[...], wb, preferred_element_type=F32)
        return

    tm, tn = o_ref.shape
    rc = min(tm, ROT_ROW_CHUNK)

    @pl.when(j < n_rot)
    def _():
        for r0 in range(0, tm, rc):
            rows = slice(r0, r0 + rc)
            y = jnp.dot(xb_ref[rows, :], wb, preferred_element_type=F32)
            c, a, b = c_ref[rows, :], a_ref[rows, :], b_ref[rows, :]
            for h in range(tn // LANES):
                yh = y[:, h * LANES:(h + 1) * LANES]
                o_ref[rows, h * LANES:(h + 1) * LANES] = (
                    yh * c + pltpu.roll(yh, shifts[0], 1) * a + pltpu.roll(yh, shifts[1], 1) * b)

    @pl.when(j >= n_rot)
    def _():
        o_ref[...] = jnp.dot(xb_ref[...], wb, preferred_element_type=F32)


def _proj(x, w, layer, n_rot=0, tables=None, shifts=None, seq=None):
    m = x.shape[0]
    n = w.shape[-1]
    tm = _row_tile(m)
    tn = PROJ_TILE_N
    in_specs = [
        pl.BlockSpec((tm, D_MODEL), lambda i, j: (i, 0)),
        pl.BlockSpec((None, D_MODEL, tn), lambda i, j: (layer, 0, j)),
    ]
    args = [x, w]
    if n_rot:
        tiles_per_seq = seq // tm
        tab = pl.BlockSpec((tm, LANES), lambda i, j: (i % tiles_per_seq, 0))
        in_specs += [tab, tab, tab]
        args += list(tables)
    return pl.pallas_call(
        functools.partial(_proj_kernel, n_rot=n_rot, shifts=shifts),
        grid=(m // tm, n // tn),
        in_specs=in_specs,
        out_specs=pl.BlockSpec((tm, tn), lambda i, j: (i, j)),
        out_shape=jax.ShapeDtypeStruct((m, n), F32),
        scratch_shapes=[pltpu.VMEM((tm, D_MODEL), BF16)],
        compiler_params=_params("parallel", "arbitrary"),
        name="proj",
    )(*args)


def _out_ln_kernel(att_ref, w_ref, x_ref, g_ref, b_ref, o_ref, y_ref, *, nn, tn):
    n = pl.program_id(1)
    y_ref[n] = jnp.dot(att_ref[...].astype(BF16), w_ref[...].astype(BF16),
                       preferred_element_type=F32)

    @pl.when(n == nn - 1)
    def _():
        cols = [slice(c * tn, (c + 1) * tn) for c in range(nn)]
        total = None
        for c in range(nn):
            z = ALPHA * x_ref[:, cols[c]] + y_ref[c]
            y_ref[c] = z
            part = jnp.sum(z, axis=-1, keepdims=True)
            total = part if total is None else total + part
        mu = total * (1.0 / D_MODEL)
        total = None
        for c in range(nn):
            zc = y_ref[c] - mu
            part = jnp.sum(zc * zc, axis=-1, keepdims=True)
            total = part if total is None else total + part
        rstd = lax.rsqrt(total * (1.0 / D_MODEL) + LN_EPS)
        for c in range(nn):
            o_ref[:, cols[c]] = (y_ref[c] - mu) * rstd * g_ref[:, cols[c]] + b_ref[:, cols[c]]


def _out_ln(att, w_out, layer, x, g, b):
    m = x.shape[0]
    tm = _row_tile(m)
    tn = OUT_TILE_N
    nn = D_MODEL // tn
    return pl.pallas_call(
        functools.partial(_out_ln_kernel, nn=nn, tn=tn),
        grid=(m // tm, nn),
        in_specs=[
            pl.BlockSpec((tm, D_MODEL), lambda i, n: (i, 0), pipeline_mode=pl.Buffered(1)),
            pl.BlockSpec((None, D_MODEL, tn), lambda i, n: (layer, 0, n)),
            pl.BlockSpec((tm, D_MODEL), lambda i, n: (i, 0), pipeline_mode=pl.Buffered(1)),
            pl.BlockSpec((1, D_MODEL), lambda i, n: (0, 0)),
            pl.BlockSpec((1, D_MODEL), lambda i, n: (0, 0)),
        ],
        out_specs=pl.BlockSpec((tm, D_MODEL), lambda i, n: (i, 0)),
        out_shape=jax.ShapeDtypeStruct((m, D_MODEL), F32),
        scratch_shapes=[pltpu.VMEM((nn, tm, tn), F32)],
        compiler_params=_params("parallel", "arbitrary"),
        name="out_ln",
    )(att, w_out, x, g, b)


def _dot_nt(a, b):
    return lax.dot_general(a, b, (((1,), (1,)), ((), ())), preferred_element_type=F32)


def _mem_attend_block(qm_ref, mk_ref, mv_ref, o_ref, col0):
    for h in range(MEM_HEADS):
        sl = slice(h * HEAD_DIM, (h + 1) * HEAD_DIM)
        q = qm_ref[0, :, sl].astype(BF16)
        s = _dot_nt(q, mk_ref[0, :, sl].astype(BF16)) * ATT_SCALE
        p = jnp.exp(s - jnp.max(s, axis=-1, keepdims=True))
        p = p * (1.0 / jnp.sum(p, axis=-1, keepdims=True))
        o = jnp.dot(p.astype(BF16), mv_ref[0, :, sl].astype(BF16), preferred_element_type=F32)
        o_ref[0, :, col0 + h * HEAD_DIM:col0 + (h + 1) * HEAD_DIM] = o.astype(o_ref.dtype)


def _swa_prompt_kernel(sink_ref, q_ref, kc_ref, kp_ref, vc_ref, vp_ref, qm_ref, mk_ref, mv_ref, o_ref,
                       s_ref, p_ref):
    n = pl.program_id(1)
    nq = SWA_Q_HEADS
    for h in range(SWA_KV_HEADS):
        sl = slice(h * HEAD_DIM, (h + 1) * HEAD_DIM)
        q3 = jnp.concatenate(
            [q_ref[0, :, (h * SWA_GROUP + g) * HEAD_DIM:(h * SWA_GROUP + g + 1) * HEAD_DIM]
             for g in range(SWA_GROUP)], axis=0).astype(BF16)
        k2 = jnp.concatenate([kp_ref[0, :, sl], kc_ref[0, :, sl]], axis=0).astype(BF16)
        s = _dot_nt(q3, k2) * ATT_SCALE
        s_ref[h * SWA_GROUP:(h + 1) * SWA_GROUP] = s.reshape(SWA_GROUP, WINDOW, 2 * WINDOW)
        qm = qm_ref[0, :, sl].astype(BF16)
        s_ref[nq + h] = _dot_nt(qm, mk_ref[0, :, sl].astype(BF16)) * ATT_SCALE

    qi = lax.broadcasted_iota(jnp.int32, (WINDOW, 2 * WINDOW), 0)
    kj = lax.broadcasted_iota(jnp.int32, (WINDOW, 2 * WINDOW), 1)
    first_key = jnp.where(n > 0, qi, WINDOW)
    ok = (kj >= first_key) & (kj <= qi + WINDOW)
    s = jnp.where(ok[None], s_ref[0:nq], NEG_INF)
    sink = sink_ref[...]
    m = jnp.maximum(jnp.max(s, axis=-1, keepdims=True), sink)
    p = jnp.exp(s - m)
    den = jnp.sum(p, axis=-1, keepdims=True) + jnp.exp(sink - m)
    p_ref[0:nq] = (p * (1.0 / den)).astype(BF16)

    s = s_ref[nq:nq + MEM_HEADS]
    p = jnp.exp(s - jnp.max(s, axis=-1, keepdims=True))
    p_ref[nq:nq + MEM_HEADS] = (p * (1.0 / jnp.sum(p, axis=-1, keepdims=True))).astype(BF16)

    for h in range(SWA_KV_HEADS):
        sl = slice(h * HEAD_DIM, (h + 1) * HEAD_DIM)
        v2 = jnp.concatenate([vp_ref[0, :, sl], vc_ref[0, :, sl]], axis=0).astype(BF16)
        p3 = p_ref[h * SWA_GROUP:(h + 1) * SWA_GROUP].reshape(SWA_GROUP * WINDOW, 2 * WINDOW)
        o = jnp.dot(p3, v2, preferred_element_type=F32)
        for g in range(SWA_GROUP):
            hq = h * SWA_GROUP + g
            o_ref[0, :, hq * HEAD_DIM:(hq + 1) * HEAD_DIM] = (
                o[g * WINDOW:(g + 1) * WINDOW].astype(o_ref.dtype))
        om = jnp.dot(p_ref[nq + h], mv_ref[0, :, sl].astype(BF16), preferred_element_type=F32)
        o_ref[0, :, SWA_Q + h * HEAD_DIM:SWA_Q + (h + 1) * HEAD_DIM] = om.astype(o_ref.dtype)


def _mem_specs(layer):
    shape = (None, 1, MEM_LEN, MEM_DIM)
    return (pl.BlockSpec(shape, lambda b, n: (layer, b, 0, 0)),
            pl.BlockSpec(shape, lambda b, n: (layer, b, 0, 1)))


def _swa_prompt(qkv, mem_kv, layer, sinks):
    bsz, seq, _ = qkv.shape
    nb = seq // WINDOW
    kcol = SWA_Q // SWA_KV
    vcol = kcol + 1
    mcol = vcol + 1
    blk = lambda w, f: pl.BlockSpec((1, WINDOW, w), f)
    prev = lambda n: jnp.maximum(n - 1, 0)
    nheads = SWA_Q_HEADS + MEM_HEADS
    assert MEM_LEN == 2 * WINDOW
    return pl.pallas_call(
        _swa_prompt_kernel,
        grid=(bsz, nb),
        in_specs=[
            pl.BlockSpec((SWA_Q_HEADS, 1, 1), lambda b, n: (0, 0, 0)),
            blk(SWA_Q, lambda b, n: (b, n, 0)),
            blk(SWA_KV, lambda b, n: (b, n, kcol)),
            blk(SWA_KV, lambda b, n: (b, prev(n), kcol)),
            blk(SWA_KV, lambda b, n: (b, n, vcol)),
            blk(SWA_KV, lambda b, n: (b, prev(n), vcol)),
            blk(MEM_DIM, lambda b, n: (b, n, mcol)),
            *_mem_specs(layer),
        ],
        out_specs=blk(D_MODEL, lambda b, n: (b, n, 0)),
        out_shape=jax.ShapeDtypeStruct((bsz, seq, D_MODEL), BF16),
        scratch_shapes=[pltpu.VMEM((nheads, WINDOW, 2 * WINDOW), F32),
                        pltpu.VMEM((nheads, WINDOW, 2 * WINDOW), BF16)],
        compiler_params=_params("parallel", "arbitrary"),
        name="swa_prompt",
    )(sinks.reshape(SWA_Q_HEADS, 1, 1), qkv, qkv, qkv, qkv, qkv, qkv, mem_kv, mem_kv)


def _head_norm_gate(o, gate):
    mu = jnp.mean(o, axis=-1, keepdims=True)
    oc = o - mu
    var = jnp.mean(oc * oc, axis=-1, keepdims=True)
    return jax.nn.silu(gate) * (oc * lax.rsqrt(var + HEAD_NORM_EPS))


def _ret_prompt_kernel(cdec_ref, q_ref, k_ref, v_ref, g_ref, qm_ref, mk_ref, mv_ref,
                       decay_ref, qdec_ref, kdec_ref, o_ref, s_out_ref, state_ref, *, nc):
    c = pl.program_id(1)

    @pl.when(c == 0)
    def _():
        state_ref[...] = jnp.zeros_like(state_ref)

    for h in range(RET_HEADS):
        ksl = slice(h * RET_DK, (h + 1) * RET_DK)
        vsl = slice(h * RET_DV, (h + 1) * RET_DV)
        qc = q_ref[0, :, ksl]
        kc = k_ref[0, :, ksl] * RET_K_SCALE
        vb = v_ref[0, :, vsl].astype(BF16)
        st = state_ref[h]
        inner = _dot_nt(qc.astype(BF16), kc.astype(BF16)) * decay_ref[h]
        o = (jnp.dot(inner.astype(BF16), vb, preferred_element_type=F32)
             + jnp.dot((qc * qdec_ref[h]).astype(BF16), st.astype(BF16), preferred_element_type=F32))
        kd = (kc * kdec_ref[h]).astype(BF16)
        state_ref[h] = cdec_ref[h] * st + lax.dot_general(
            kd, vb, (((0,), (0,)), ((), ())), preferred_element_type=F32)
        o_ref[0, :, vsl] = _head_norm_gate(o, g_ref[0, :, vsl]).astype(o_ref.dtype)
    _mem_attend_block(qm_ref, mk_ref, mv_ref, o_ref, RET_V)

    @pl.when(c == nc - 1)
    def _():
        s_out_ref[0] = state_ref[...]


def _ret_prompt(qkvg, mem_kv, layer, tables):
    bsz, seq, _ = qkvg.shape
    nc = seq // RET_CHUNK
    decay, qdec, kdec, cdec = tables
    blk = lambda w, f: pl.BlockSpec((1, RET_CHUNK, w), f)
    tab = pl.BlockSpec((RET_HEADS, RET_CHUNK, RET_CHUNK), lambda b, c: (0, 0, 0))
    return pl.pallas_call(
        functools.partial(_ret_prompt_kernel, nc=nc),
        grid=(bsz, nc),
        in_specs=[
            pl.BlockSpec(memory_space=pltpu.SMEM),
            blk(RET_QK, lambda b, c: (b, c, 0)),
            blk(RET_QK, lambda b, c: (b, c, 1)),
            blk(RET_V, lambda b, c: (b, c, 1)),
            blk(RET_V, lambda b, c: (b, c, 2)),
            blk(MEM_DIM, lambda b, c: (b, c, (2 * RET_QK + 2 * RET_V) // MEM_DIM)),
            *_mem_specs(layer),
            tab, tab, tab,
        ],
        out_specs=[
            blk(D_MODEL, lambda b, c: (b, c, 0)),
            pl.BlockSpec((1, RET_HEADS, RET_DK, RET_DV), lambda b, c: (b, 0, 0, 0)),
        ],
        out_shape=[
            jax.ShapeDtypeStruct((bsz, seq, D_MODEL), BF16),
            jax.ShapeDtypeStruct((bsz, RET_HEADS, RET_DK, RET_DV), F32),
        ],
        scratch_shapes=[pltpu.VMEM((RET_HEADS, RET_DK, RET_DV), F32)],
        compiler_params=_params("parallel", "arbitrary"),
        name="ret_prompt",
    )(cdec, qkvg, qkvg, qkvg, qkvg, qkvg, mem_kv, mem_kv, decay, qdec, kdec)


def _mem_attend_row(q, mk_ref, mv_ref):
    s = jnp.sum(mk_ref[0] * q[None], axis=-1, keepdims=True) * ATT_SCALE
    p = jnp.exp(s - jnp.max(s, axis=0, keepdims=True))
    p = p * (1.0 / jnp.sum(p, axis=0, keepdims=True))
    return jnp.sum(p * mv_ref[0], axis=0)


def _swa_step_kernel(sink_ref, rows_ref, kbuf_ref, vbuf_ref, mk_ref, mv_ref, o_ref, nk_ref, nv_ref):
    krow0 = SWA_Q_HEADS
    vrow0 = krow0 + SWA_KV_HEADS
    mrow0 = vrow0 + SWA_KV_HEADS
    wb = kbuf_ref.shape[1]
    kb = kbuf_ref[0]
    vb = vbuf_ref[0]
    k_new = rows_ref[0, krow0:krow0 + SWA_KV_HEADS, :]
    v_new = rows_ref[0, vrow0:vrow0 + SWA_KV_HEADS, :]
    for g in range(SWA_GROUP):
        group_rows = pl.ds(g, SWA_KV_HEADS, stride=SWA_GROUP)
        q = rows_ref[0, group_rows, :]
        s_buf = jnp.sum(kb * q[None], axis=-1, keepdims=True) * ATT_SCALE
        s_new = jnp.sum(k_new * q, axis=-1, keepdims=True) * ATT_SCALE
        sink = sink_ref[g]
        m = jnp.maximum(jnp.maximum(jnp.max(s_buf, axis=0), s_new), sink)
        p_buf = jnp.exp(s_buf - m[None])
        p_new = jnp.exp(s_new - m)
        inv = 1.0 / (jnp.sum(p_buf, axis=0) + p_new + jnp.exp(sink - m))
        o_ref[0, group_rows, :] = jnp.sum((p_buf * inv[None]) * vb, axis=0) + (p_new * inv) * v_new
    nk_ref[0, 0:wb - 1] = kbuf_ref[0, 1:wb]
    nv_ref[0, 0:wb - 1] = vbuf_ref[0, 1:wb]
    nk_ref[0, wb - 1] = k_new
    nv_ref[0, wb - 1] = v_new
    o_ref[0, SWA_Q_HEADS:SWA_Q_HEADS + MEM_HEADS, :] = _mem_attend_row(
        rows_ref[0, mrow0:mrow0 + MEM_HEADS, :], mk_ref, mv_ref)


def _swa_step(rows, cache_k, cache_v, j, mem_k, mem_v, i, sinks):
    bsz = rows.shape[0]
    wb = cache_k.shape[2]
    cache = pl.BlockSpec((None, 1, wb, SWA_KV_HEADS, HEAD_DIM), lambda b: (j, b, 0, 0, 0))
    mem = pl.BlockSpec((None, 1, MEM_LEN, MEM_HEADS, HEAD_DIM), lambda b: (i, b, 0, 0, 0))
    new = pl.BlockSpec((1, wb, SWA_KV_HEADS, HEAD_DIM), lambda b: (b, 0, 0, 0))
    nrows = D_MODEL // HEAD_DIM
    sink_gk = sinks.reshape(SWA_KV_HEADS, SWA_GROUP).T.reshape(SWA_GROUP, SWA_KV_HEADS, 1)
    return pl.pallas_call(
        _swa_step_kernel,
        grid=(bsz,),
        in_specs=[
            pl.BlockSpec(sink_gk.shape, lambda b: (0, 0, 0)),
            pl.BlockSpec((1,) + rows.shape[1:], lambda b: (b, 0, 0)),
            cache, cache, mem, mem,
        ],
        out_specs=[pl.BlockSpec((1, nrows, HEAD_DIM), lambda b: (b, 0, 0)), new, new],
        out_shape=[
            jax.ShapeDtypeStruct((bsz, nrows, HEAD_DIM), F32),
            jax.ShapeDtypeStruct((bsz, wb, SWA_KV_HEADS, HEAD_DIM), F32),
            jax.ShapeDtypeStruct((bsz, wb, SWA_KV_HEADS, HEAD_DIM), F32),
        ],
        compiler_params=_params("parallel"),
        name="swa_step",
    )(sink_gk, rows, cache_k, cache_v, mem_k, mem_v)


def _ret_step_kernel(dec_ref, rows_ref, cols_ref, s_ref, mk_ref, mv_ref, o_ref, s_out_ref):
    krow0 = RET_HEADS
    vrow0 = 2 * RET_HEADS
    grow0 = vrow0 + 2 * RET_HEADS
    mrow0 = grow0 + 2 * RET_HEADS
    for h in range(RET_HEADS):
        q_row = rows_ref[0, h:h + 1, :]
        k_row = rows_ref[0, krow0 + h:krow0 + h + 1, :] * RET_K_SCALE
        q_col = cols_ref[0, :, h:h + 1]
        k_col = cols_ref[0, :, krow0 + h:krow0 + h + 1] * RET_K_SCALE
        inner = jnp.sum(q_row * k_row, axis=1, keepdims=True) * dec_ref[0, h]
        qd = q_col * dec_ref[1, h]
        kd = k_col * dec_ref[2, h]
        halves = []
        for t in range(2):
            lsl = slice(t * HEAD_DIM, (t + 1) * HEAD_DIM)
            v = rows_ref[0, vrow0 + 2 * h + t:vrow0 + 2 * h + t + 1, :]
            st = s_ref[0, h, :, lsl]
            halves.append(inner * v + jnp.sum(qd * st, axis=0, keepdims=True))
            s_out_ref[0, h, :, lsl] = dec_ref[3, h] * st + kd * v
        mu = (jnp.sum(halves[0], axis=1, keepdims=True)
              + jnp.sum(halves[1], axis=1, keepdims=True)) * (1.0 / RET_DV)
        cen = [o - mu for o in halves]
        var = (jnp.sum(cen[0] * cen[0], axis=1, keepdims=True)
               + jnp.sum(cen[1] * cen[1], axis=1, keepdims=True)) * (1.0 / RET_DV)
        rstd = lax.rsqrt(var + HEAD_NORM_EPS)
        for t in range(2):
            gate = rows_ref[0, grow0 + 2 * h + t:grow0 + 2 * h + t + 1, :]
            o_ref[0, 2 * h + t:2 * h + t + 1, :] = jax.nn.silu(gate) * (cen[t] * rstd)
    o_ref[0, 2 * RET_HEADS:2 * RET_HEADS + MEM_HEADS, :] = _mem_attend_row(
        rows_ref[0, mrow0:mrow0 + MEM_HEADS, :], mk_ref, mv_ref)


def _ret_step(rows, cols, state, j, mem_k, mem_v, i, dec):
    bsz = rows.shape[0]
    st_in = pl.BlockSpec((None, 1, RET_HEADS, RET_DK, RET_DV), lambda b: (j, b, 0, 0, 0))
    st_out = pl.BlockSpec((1, RET_HEADS, RET_DK, RET_DV), lambda b: (b, 0, 0, 0))
    mem = pl.BlockSpec((None, 1, MEM_LEN, MEM_HEADS, HEAD_DIM), lambda b: (i, b, 0, 0, 0))
    nrows = D_MODEL // HEAD_DIM
    return pl.pallas_call(
        _ret_step_kernel,
        grid=(bsz,),
        in_specs=[
            pl.BlockSpec(memory_space=pltpu.SMEM),
            pl.BlockSpec((1,) + rows.shape[1:], lambda b: (b, 0, 0)),
            pl.BlockSpec((1,) + cols.shape[1:], lambda b: (b, 0, 0)),
            st_in, mem, mem,
        ],
        out_specs=[pl.BlockSpec((1, nrows, HEAD_DIM), lambda b: (b, 0, 0)), st_out],
        out_shape=[
            jax.ShapeDtypeStruct((bsz, nrows, HEAD_DIM), F32),
            jax.ShapeDtypeStruct((bsz, RET_HEADS, RET_DK, RET_DV), F32),
        ],
        compiler_params=_params("parallel"),
        name="ret_step",
    )(dec, rows, cols, state, mem_k, mem_v)


def _rope_tables(pos):
    half = ROPE_DIM // 2
    inv = ROPE_THETA ** (-jnp.arange(half, dtype=F32) / half)
    ang = pos.astype(F32)[:, None] * inv[None, :]
    cos, sin = jnp.cos(ang), jnp.sin(ang)
    n = pos.shape[0]
    rest = HEAD_DIM - ROPE_DIM
    c = jnp.concatenate([cos, cos, jnp.ones((n, rest), F32)], axis=-1)
    a = jnp.concatenate([-sin, jnp.zeros((n, HEAD_DIM - half), F32)], axis=-1)
    b = jnp.concatenate([jnp.zeros((n, half), F32), sin, jnp.zeros((n, rest), F32)], axis=-1)
    return (c, a, b), (HEAD_DIM - half, half)


def _ret_rot_tables(pos):
    half = RET_DK // 2
    angle = RET_ROT_BASE ** (-jnp.linspace(0.0, 1.0, half, dtype=F32))
    ang = pos.astype(F32)[:, None] * angle[None, :]
    cos, sin = jnp.cos(ang), jnp.sin(ang)
    n = pos.shape[0]
    zero = jnp.zeros_like(sin)
    c = jnp.stack([cos, cos], axis=-1).reshape(n, RET_DK)
    a = jnp.stack([-sin, zero], axis=-1).reshape(n, RET_DK)
    b = jnp.stack([zero, sin], axis=-1).reshape(n, RET_DK)
    return (c, a, b), (RET_DK - 1, 1)


def _ret_decay(chunk):
    log_g = jnp.log1p(-jnp.exp2(-5.0 - jnp.arange(RET_HEADS, dtype=F32)))
    n = jnp.arange(chunk, dtype=F32)
    rel = n[:, None] - n[None, :]
    decay = jnp.where(rel >= 0, jnp.exp(jnp.maximum(rel, 0.0) * log_g[:, None, None]), 0.0)
    q_dec = jnp.exp((n + 1.0) * log_g[:, None])
    k_dec = jnp.exp((chunk - 1.0 - n) * log_g[:, None])
    c_dec = jnp.exp(chunk * log_g)
    return decay, q_dec, k_dec, c_dec


def _trunk(x3, pos, swa_k, swa_v, ret_s, mem_k, mem_v, ln_g, ln_b, ffn_w_gu, ffn_w_down,
           swa_w_in, swa_w_out, swa_sinks, ret_w_in, ret_w_out):
    bsz, seq, _ = x3.shape
    prompt = ret_s is None
    m = bsz * seq
    x = x3.reshape(m, D_MODEL)
    rows_per_tile = _row_tile(m)
    tab_rows = seq if prompt else rows_per_tile
    tab_pos = pos if prompt else jnp.broadcast_to(pos, (tab_rows,))
    rope_tabs, rope_shifts = _rope_tables(tab_pos)
    rot_tabs, rot_shifts = _ret_rot_tables(tab_pos)
    if prompt:
        decay, q_dec, k_dec, c_dec = _ret_decay(RET_CHUNK)
        ret_tabs = (decay,
                    jnp.broadcast_to(q_dec[:, :, None], decay.shape),
                    jnp.broadcast_to(k_dec[:, :, None], decay.shape),
                    c_dec)
    else:
        decay, q_dec, k_dec, c_dec = _ret_decay(seq)
        step_dec = jnp.stack([decay[:, 0, 0], q_dec[:, 0], k_dec[:, 0], c_dec])

    def ln(i, s):
        return ln_g[i, s].reshape(1, D_MODEL), ln_b[i, s].reshape(1, D_MODEL)

    new_k, new_v, new_s = [], [], []
    for i in range(DEPTH):
        j = i // 2
        x = _ffn(x, ffn_w_gu, ffn_w_down, i, 0, *ln(i, 0))
        if i % 2 == 0:
            qkv = _proj(x, swa_w_in, j, n_rot=(SWA_Q + SWA_KV) // PROJ_TILE_N, tables=rope_tabs,
                        shifts=rope_shifts, seq=tab_rows)
            if prompt:
                qkv3 = qkv.reshape(bsz, seq, SWA_IN_WIDTH)
                att = _swa_prompt(qkv3, mem_k, i, swa_sinks[j])
                wlen = min(WINDOW, seq)
                new_k.append(qkv3[:, seq - wlen:, SWA_Q:SWA_Q + SWA_KV]
                             .reshape(bsz, wlen, SWA_KV_HEADS, HEAD_DIM))
                new_v.append(qkv3[:, seq - wlen:, SWA_Q + SWA_KV:SWA_Q + 2 * SWA_KV]
                             .reshape(bsz, wlen, SWA_KV_HEADS, HEAD_DIM))
            else:
                rows = qkv.reshape(bsz, SWA_IN_WIDTH // HEAD_DIM, HEAD_DIM)
                att, nk, nv = _swa_step(rows, swa_k, swa_v, j, mem_k, mem_v, i, swa_sinks[j])
                new_k.append(nk)
                new_v.append(nv)
            w_out = swa_w_out
        else:
            qkvg = _proj(x, ret_w_in, j, n_rot=(2 * RET_QK) // PROJ_TILE_N, tables=rot_tabs,
                         shifts=rot_shifts, seq=tab_rows)
            if prompt:
                att, ns = _ret_prompt(qkvg.reshape(bsz, seq, RET_IN_WIDTH), mem_k, i, ret_tabs)
            else:
                rows = qkvg.reshape(bsz, RET_IN_WIDTH // HEAD_DIM, HEAD_DIM)
                cols = jnp.swapaxes(rows, 1, 2)
                att, ns = _ret_step(rows, cols, ret_s, j, mem_k, mem_v, i, step_dec)
            new_s.append(ns)
            w_out = ret_w_out
        x = _out_ln(att.reshape(m, D_MODEL), w_out, j, x, *ln(i, 1))
        x = _ffn(x, ffn_w_gu, ffn_w_down, i, 1, *ln(i, 2))
    return x.reshape(bsz, seq, D_MODEL), jnp.stack(new_k), jnp.stack(new_v), jnp.stack(new_s)


def kernel(x_prompt, x_sample, cache_swa_k, cache_swa_v, state_ret, cache_mem_k, cache_mem_v,
           mem_prompt, ln_g, ln_b, ffn_w_gu, ffn_w_down, w_mem_kv, swa_w_in, swa_w_out,
           swa_sinks, ret_w_in, ret_w_out):
    bp = mem_prompt.shape[0]
    mem2 = mem_prompt.reshape(bp * MEM_LEN, D_MODEL)
    mem_kv = jnp.stack([_proj(mem2, w_mem_kv, l) for l in range(DEPTH)])
    mem_kv = mem_kv.reshape(DEPTH, bp, MEM_LEN, 2 * MEM_DIM)
    mem_k_prompt = mem_kv[..., :MEM_DIM].reshape(DEPTH, bp, MEM_LEN, MEM_HEADS, HEAD_DIM)
    mem_v_prompt = mem_kv[..., MEM_DIM:].reshape(DEPTH, bp, MEM_LEN, MEM_HEADS, HEAD_DIM)

    pos_prompt = jnp.arange(x_prompt.shape[1], dtype=jnp.int32)
    pos_sample = PAST_LEN + jnp.arange(x_sample.shape[1], dtype=jnp.int32)
    weights = (ln_g, ln_b, ffn_w_gu, ffn_w_down, swa_w_in, swa_w_out, swa_sinks, ret_w_in, ret_w_out)

    y_prompt, swa_k_prompt, swa_v_prompt, ret_state_prompt = _trunk(
        x_prompt, pos_prompt, None, None, None, mem_kv, None, *weights)
    y_sample, swa_k_sample, swa_v_sample, ret_state_sample = _trunk(
        x_sample, pos_sample, cache_swa_k, cache_swa_v, state_ret, cache_mem_k, cache_mem_v, *weights)
    return (y_prompt, y_sample, swa_k_prompt, swa_v_prompt, swa_k_sample, swa_v_sample,
            ret_state_prompt, ret_state_sample, mem_k_prompt, mem_v_prompt)
```

```python
import functools

import jax
import jax.numpy as jnp
from jax import lax
from jax.experimental import pallas as pl
from jax.experimental.pallas import tpu as pltpu

F32 = jnp.float32
BF16 = jnp.bfloat16

D_MODEL = 2048
DEPTH = 4
PAST_LEN = 16384
HEAD_DIM = 128
MEM_LEN = 256
MEM_HEADS = 4
MEM_DIM = MEM_HEADS * HEAD_DIM
SELF_WIDTH = D_MODEL - MEM_DIM
SWA_Q_HEADS = SELF_WIDTH // HEAD_DIM
SWA_KV_HEADS = SWA_Q_HEADS // 3
SWA_GROUP = SWA_Q_HEADS // SWA_KV_HEADS
WINDOW = 128
ROPE_THETA = 500000.0
ROPE_DIM = HEAD_DIM // 4
RET_DK = 128
RET_DV = 2 * RET_DK
RET_HEADS = SELF_WIDTH // RET_DV
RET_CHUNK = 128
RET_ROT_BASE = 10000.0
D_FF = ((8 * D_MODEL // 3 + 255) // 256) * 256
LN_EPS = 1e-5
HEAD_NORM_EPS = 1e-6
ALPHA = (2.0 * DEPTH) ** 0.25
NEG_INF = -1e30
ATT_SCALE = HEAD_DIM ** -0.5
RET_K_SCALE = RET_DK ** -0.5

SWA_Q = SWA_Q_HEADS * HEAD_DIM
SWA_KV = SWA_KV_HEADS * HEAD_DIM
SWA_IN_WIDTH = SWA_Q + 2 * SWA_KV + MEM_DIM
RET_QK = RET_HEADS * RET_DK
RET_V = RET_HEADS * RET_DV
RET_IN_WIDTH = 2 * RET_QK + 2 * RET_V + MEM_DIM

VMEM_LIMIT_BYTES = 58 * 1024 * 1024
LANES = 128
FFN_TILE_K = 512
FFN_TILE_K_SMALL_M = 1408
PROJ_TILE_N = 512
OUT_TILE_N = 512
ROT_ROW_CHUNK = 256


def _row_tile(m):
    return 1024 if m % 1024 == 0 else m


def _params(*sem):
    return pltpu.CompilerParams(dimension_semantics=sem, vmem_limit_bytes=VMEM_LIMIT_BYTES)


def _layer_norm_rows(z, g, b):
    mu = jnp.mean(z, axis=-1, keepdims=True)
    zc = z - mu
    var = jnp.mean(zc * zc, axis=-1, keepdims=True)
    return zc * lax.rsqrt(var + LN_EPS) * g + b


def _ffn_kernel(x_ref, wg_ref, wu_ref, wd_ref, g_ref, b_ref, o_ref, xb_ref, *, nk):
    k = pl.program_id(1)

    @pl.when(k == 0)
    def _():
        xb_ref[...] = x_ref[...].astype(BF16)
        o_ref[...] = jnp.zeros_like(o_ref)

    gate = jnp.dot(xb_ref[...], wg_ref[...], preferred_element_type=F32)
    up = jnp.dot(xb_ref[...], wu_ref[...], preferred_element_type=F32)
    h = (jax.nn.silu(gate) * up).astype(BF16)
    o_ref[...] += jnp.dot(h, wd_ref[...], preferred_element_type=F32)

    @pl.when(k == nk - 1)
    def _():
        z = ALPHA * x_ref[...] + 0.5 * o_ref[...]
        o_ref[...] = _layer_norm_rows(z, g_ref[...], b_ref[...])


def _ffn(x, w_gu, w_down, layer, slot, g, b):
    m = x.shape[0]
    tm = _row_tile(m)
    tk = FFN_TILE_K if tm >= 1024 else FFN_TILE_K_SMALL_M
    nk = D_FF // tk
    return pl.pallas_call(
        functools.partial(_ffn_kernel, nk=nk),
        grid=(m // tm, nk),
        in_specs=[
            pl.BlockSpec((tm, D_MODEL), lambda i, k: (i, 0), pipeline_mode=pl.Buffered(1)),
            pl.BlockSpec((None, None, D_MODEL, tk), lambda i, k: (layer, slot, 0, k)),
            pl.BlockSpec((None, None, D_MODEL, tk), lambda i, k: (layer, slot, 0, nk + k)),
            pl.BlockSpec((None, None, tk, D_MODEL), lambda i, k: (layer, slot, k, 0)),
            pl.BlockSpec((1, D_MODEL), lambda i, k: (0, 0)),
            pl.BlockSpec((1, D_MODEL), lambda i, k: (0, 0)),
        ],
        out_specs=pl.BlockSpec((tm, D_MODEL), lambda i, k: (i, 0)),
        out_shape=jax.ShapeDtypeStruct((m, D_MODEL), F32),
        scratch_shapes=[pltpu.VMEM((tm, D_MODEL), BF16)],
        compiler_params=_params("parallel", "arbitrary"),
        name="ffn_ln",
    )(x, w_gu, w_gu, w_down, g, b)


def _proj_kernel(x_ref, w_ref, *rest, n_rot, shifts):
    if n_rot:
        c_ref, a_ref, b_ref, o_ref, xb_ref = rest
    else:
        o_ref, xb_ref = rest
    j = pl.program_id(1)

    @pl.when(j == 0)
    def _():
        xb_ref[...] = x_ref[...].astype(BF16)

    wb = w_ref[...].astype(BF16)
    if not n_rot:
        o_ref[...] = jnp.dot(xb_ref[...], wb, preferred_element_type=F32)
        return

    tm, tn = o_ref.shape
    rc = min(tm, ROT_ROW_CHUNK)

    @pl.when(j < n_rot)
    def _():
        for r0 in range(0, tm, rc):
            rows = slice(r0, r0 + rc)
            y = jnp.dot(xb_ref[rows, :], wb, preferred_element_type=F32)
            c, a, b = c_ref[rows, :], a_ref[rows, :], b_ref[rows, :]
            for h in range(tn // LANES):
                yh = y[:, h * LANES:(h + 1) * LANES]
                o_ref[rows, h * LANES:(h + 1) * LANES] = (
                    yh * c + pltpu.roll(yh, shifts[0], 1) * a + pltpu.roll(yh, shifts[1], 1) * b)

    @pl.when(j >= n_rot)
    def _():
        o_ref[...] = jnp.dot(xb_ref[...], wb, preferred_element_type=F32)


def _proj(x, w, layer, n_rot=0, tables=None, shifts=None, seq=None):
    m = x.shape[0]
    n = w.shape[-1]
    tm = _row_tile(m)
    tn = PROJ_TILE_N
    in_specs = [
        pl.BlockSpec((tm, D_MODEL), lambda i, j: (i, 0)),
        pl.BlockSpec((None, D_MODEL, tn), lambda i, j: (layer, 0, j)),
    ]
    args = [x, w]
    if n_rot:
        tiles_per_seq = seq // tm
        tab = pl.BlockSpec((tm, LANES), lambda i, j: (i % tiles_per_seq, 0))
        in_specs += [tab, tab, tab]
        args += list(tables)
    return pl.pallas_call(
        functools.partial(_proj_kernel, n_rot=n_rot, shifts=shifts),
        grid=(m // tm, n // tn),
        in_specs=in_specs,
        out_specs=pl.BlockSpec((tm, tn), lambda i, j: (i, j)),
        out_shape=jax.ShapeDtypeStruct((m, n), F32),
        scratch_shapes=[pltpu.VMEM((tm, D_MODEL), BF16)],
        compiler_params=_params("parallel", "arbitrary"),
        name="proj",
    )(*args)


def _out_ln_kernel(att_ref, w_ref, x_ref, g_ref, b_ref, o_ref, y_ref, *, nn, tn):
    n = pl.program_id(1)
    y_ref[n] = jnp.dot(att_ref[...].astype(BF16), w_ref[...].astype(BF16),
                       preferred_element_type=F32)

    @pl.when(n == nn - 1)
    def _():
        cols = [slice(c * tn, (c + 1) * tn) for c in range(nn)]
        total = None
        for c in range(nn):
            z = ALPHA * x_ref[:, cols[c]] + y_ref[c]
            y_ref[c] = z
            part = jnp.sum(z, axis=-1, keepdims=True)
            total = part if total is None else total + part
        mu = total * (1.0 / D_MODEL)
        total = None
        for c in range(nn):
            zc = y_ref[c] - mu
            part = jnp.sum(zc * zc, axis=-1, keepdims=True)
            total = part if total is None else total + part
        rstd = lax.rsqrt(total * (1.0 / D_MODEL) + LN_EPS)
        for c in range(nn):
            o_ref[:, cols[c]] = (y_ref[c] - mu) * rstd * g_ref[:, cols[c]] + b_ref[:, cols[c]]


def _out_ln(att, w_out, layer, x, g, b):
    m = x.shape[0]
    tm = _row_tile(m)
    tn = OUT_TILE_N
    nn = D_MODEL // tn
    return pl.pallas_call(
        functools.partial(_out_ln_kernel, nn=nn, tn=tn),
        grid=(m // tm, nn),
        in_specs=[
            pl.BlockSpec((tm, D_MODEL), lambda i, n: (i, 0), pipeline_mode=pl.Buffered(1)),
            pl.BlockSpec((None, D_MODEL, tn), lambda i, n: (layer, 0, n)),
            pl.BlockSpec((tm, D_MODEL), lambda i, n: (i, 0), pipeline_mode=pl.Buffered(1)),
            pl.BlockSpec((1, D_MODEL), lambda i, n: (0, 0)),
            pl.BlockSpec((1, D_MODEL), lambda i, n: (0, 0)),
        ],
        out_specs=pl.BlockSpec((tm, D_MODEL), lambda i, n: (i, 0)),
        out_shape=jax.ShapeDtypeStruct((m, D_MODEL), F32),
        scratch_shapes=[pltpu.VMEM((nn, tm, tn), F32)],
        compiler_params=_params("parallel", "arbitrary"),
        name="out_ln",
    )(att, w_out, x, g, b)


def _dot_nt(a, b):
    return lax.dot_general(a, b, (((1,), (1,)), ((), ())), preferred_element_type=F32)


def _mem_attend_block(qm_ref, mk_ref, mv_ref, o_ref, col0):
    for h in range(MEM_HEADS):
        sl = slice(h * HEAD_DIM, (h + 1) * HEAD_DIM)
        q = qm_ref[0, :, sl].astype(BF16)
        s = _dot_nt(q, mk_ref[0, :, sl].astype(BF16)) * ATT_SCALE
        p = jnp.exp(s - jnp.max(s, axis=-1, keepdims=True))
        p = p * (1.0 / jnp.sum(p, axis=-1, keepdims=True))
        o = jnp.dot(p.astype(BF16), mv_ref[0, :, sl].astype(BF16), preferred_element_type=F32)
        o_ref[0, :, col0 + h * HEAD_DIM:col0 + (h + 1) * HEAD_DIM] = o.astype(o_ref.dtype)


def _swa_prompt_kernel(sink_ref, q_ref, kc_ref, kp_ref, vc_ref, vp_ref, qm_ref, mk_ref, mv_ref, o_ref,
                       s_ref, p_ref):
    n = pl.program_id(1)
    nq = SWA_Q_HEADS
    for h in range(SWA_KV_HEADS):
        sl = slice(h * HEAD_DIM, (h + 1) * HEAD_DIM)
        q3 = jnp.concatenate(
            [q_ref[0, :, (h * SWA_GROUP + g) * HEAD_DIM:(h * SWA_GROUP + g + 1) * HEAD_DIM]
             for g in range(SWA_GROUP)], axis=0).astype(BF16)
        k2 = jnp.concatenate([kp_ref[0, :, sl], kc_ref[0, :, sl]], axis=0).astype(BF16)
        s = _dot_nt(q3, k2) * ATT_SCALE
        s_ref[h * SWA_GROUP:(h + 1) * SWA_GROUP] = s.reshape(SWA_GROUP, WINDOW, 2 * WINDOW)
        qm = qm_ref[0, :, sl].astype(BF16)
        s_ref[nq + h] = _dot_nt(qm, mk_ref[0, :, sl].astype(BF16)) * ATT_SCALE

    qi = lax.broadcasted_iota(jnp.int32, (WINDOW, 2 * WINDOW), 0)
    kj = lax.broadcasted_iota(jnp.int32, (WINDOW, 2 * WINDOW), 1)
    first_key = jnp.where(n > 0, qi, WINDOW)
    ok = (kj >= first_key) & (kj <= qi + WINDOW)
    s = jnp.where(ok[None], s_ref[0:nq], NEG_INF)
    sink = sink_ref[...]
    m = jnp.maximum(jnp.max(s, axis=-1, keepdims=True), sink)
    p = jnp.exp(s - m)
    den = jnp.sum(p, axis=-1, keepdims=True) + jnp.exp(sink - m)
    p_ref[0:nq] = (p * (1.0 / den)).astype(BF16)

    s = s_ref[nq:nq + MEM_HEADS]
    p = jnp.exp(s - jnp.max(s, axis=-1, keepdims=True))
    p_ref[nq:nq + MEM_HEADS] = (p * (1.0 / jnp.sum(p, axis=-1, keepdims=True))).astype(BF16)

    for h in range(SWA_KV_HEADS):
        sl = slice(h * HEAD_DIM, (h + 1) * HEAD_DIM)
        v2 = jnp.concatenate([vp_ref[0, :, sl], vc_ref[0, :, sl]], axis=0).astype(BF16)
        p3 = p_ref[h * SWA_GROUP:(h + 1) * SWA_GROUP].reshape(SWA_GROUP * WINDOW, 2 * WINDOW)
        o = jnp.dot(p3, v2, preferred_element_type=F32)
        for g in range(SWA_GROUP):
            hq = h * SWA_GROUP + g
            o_ref[0, :, hq * HEAD_DIM:(hq + 1) * HEAD_DIM] = (
                o[g * WINDOW:(g + 1) * WINDOW].astype(o_ref.dtype))
        om = jnp.dot(p_ref[nq + h], mv_ref[0, :, sl].astype(BF16), preferred_element_type=F32)
        o_ref[0, :, SWA_Q + h * HEAD_DIM:SWA_Q + (h + 1) * HEAD_DIM] = om.astype(o_ref.dtype)


def _mem_specs(layer):
    shape = (None, 1, MEM_LEN, MEM_DIM)
    return (pl.BlockSpec(shape, lambda b, n: (layer, b, 0, 0)),
            pl.BlockSpec(shape, lambda b, n: (layer, b, 0, 1)))


def _swa_prompt(qkv, mem_kv, layer, sinks):
    bsz, seq, _ = qkv.shape
    nb = seq // WINDOW
    kcol = SWA_Q // SWA_KV
    vcol = kcol + 1
    mcol = vcol + 1
    blk = lambda w, f: pl.BlockSpec((1, WINDOW, w), f)
    prev = lambda n: jnp.maximum(n - 1, 0)
    nheads = SWA_Q_HEADS + MEM_HEADS
    assert MEM_LEN == 2 * WINDOW
    return pl.pallas_call(
        _swa_prompt_kernel,
        grid=(bsz, nb),
        in_specs=[
            pl.BlockSpec((SWA_Q_HEADS, 1, 1), lambda b, n: (0, 0, 0)),
            blk(SWA_Q, lambda b, n: (b, n, 0)),
            blk(SWA_KV, lambda b, n: (b, n, kcol)),
            blk(SWA_KV, lambda b, n: (b, prev(n), kcol)),
            blk(SWA_KV, lambda b, n: (b, n, vcol)),
            blk(SWA_KV, lambda b, n: (b, prev(n), vcol)),
            blk(MEM_DIM, lambda b, n: (b, n, mcol)),
            *_mem_specs(layer),
        ],
        out_specs=blk(D_MODEL, lambda b, n: (b, n, 0)),
        out_shape=jax.ShapeDtypeStruct((bsz, seq, D_MODEL), BF16),
        scratch_shapes=[pltpu.VMEM((nheads, WINDOW, 2 * WINDOW), F32),
                        pltpu.VMEM((nheads, WINDOW, 2 * WINDOW), BF16)],
        compiler_params=_params("parallel", "arbitrary"),
        name="swa_prompt",
    )(sinks.reshape(SWA_Q_HEADS, 1, 1), qkv, qkv, qkv, qkv, qkv, qkv, mem_kv, mem_kv)


def _head_norm_gate(o, gate):
    mu = jnp.mean(o, axis=-1, keepdims=True)
    oc = o - mu
    var = jnp.mean(oc * oc, axis=-1, keepdims=True)
    return jax.nn.silu(gate) * (oc * lax.rsqrt(var + HEAD_NORM_EPS))


def _ret_prompt_kernel(cdec_ref, q_ref, k_ref, v_ref, g_ref, qm_ref, mk_ref, mv_ref,
                       decay_ref, qdec_ref, kdec_ref, o_ref, s_out_ref, state_ref, *, nc):
    c = pl.program_id(1)

    @pl.when(c == 0)
    def _():
        state_ref[...] = jnp.zeros_like(state_ref)

    for h in range(RET_HEADS):
        ksl = slice(h * RET_DK, (h + 1) * RET_DK)
        vsl = slice(h * RET_DV, (h + 1) * RET_DV)
        qc = q_ref[0, :, ksl]
        kc = k_ref[0, :, ksl] * RET_K_SCALE
        vb = v_ref[0, :, vsl].astype(BF16)
        st = state_ref[h]
        inner = _dot_nt(qc.astype(BF16), kc.astype(BF16)) * decay_ref[h]
        o = (jnp.dot(inner.astype(BF16), vb, preferred_element_type=F32)
             + jnp.dot((qc * qdec_ref[h]).astype(BF16), st.astype(BF16), preferred_element_type=F32))
        kd = (kc * kdec_ref[h]).astype(BF16)
        state_ref[h] = cdec_ref[h] * st + lax.dot_general(
            kd, vb, (((0,), (0,)), ((), ())), preferred_element_type=F32)
        o_ref[0, :, vsl] = _head_norm_gate(o, g_ref[0, :, vsl]).astype(o_ref.dtype)
    _mem_attend_block(qm_ref, mk_ref, mv_ref, o_ref, RET_V)

    @pl.when(c == nc - 1)
    def _():
        s_out_ref[0] = state_ref[...]


def _ret_prompt(qkvg, mem_kv, layer, tables):
    bsz, seq, _ = qkvg.shape
    nc = seq // RET_CHUNK
    decay, qdec, kdec, cdec = tables
    blk = lambda w, f: pl.BlockSpec((1, RET_CHUNK, w), f)
    tab = pl.BlockSpec((RET_HEADS, RET_CHUNK, RET_CHUNK), lambda b, c: (0, 0, 0))
    return pl.pallas_call(
        functools.partial(_ret_prompt_kernel, nc=nc),
        grid=(bsz, nc),
        in_specs=[
            pl.BlockSpec(memory_space=pltpu.SMEM),
            blk(RET_QK, lambda b, c: (b, c, 0)),
            blk(RET_QK, lambda b, c: (b, c, 1)),
            blk(RET_V, lambda b, c: (b, c, 1)),
            blk(RET_V, lambda b, c: (b, c, 2)),
            blk(MEM_DIM, lambda b, c: (b, c, (2 * RET_QK + 2 * RET_V) // MEM_DIM)),
            *_mem_specs(layer),
            tab, tab, tab,
        ],
        out_specs=[
            blk(D_MODEL, lambda b, c: (b, c, 0)),
            pl.BlockSpec((1, RET_HEADS, RET_DK, RET_DV), lambda b, c: (b, 0, 0, 0)),
        ],
        out_shape=[
            jax.ShapeDtypeStruct((bsz, seq, D_MODEL), BF16),
            jax.ShapeDtypeStruct((bsz, RET_HEADS, RET_DK, RET_DV), F32),
        ],
        scratch_shapes=[pltpu.VMEM((RET_HEADS, RET_DK, RET_DV), F32)],
        compiler_params=_params("parallel", "arbitrary"),
        name="ret_prompt",
    )(cdec, qkvg, qkvg, qkvg, qkvg, qkvg, mem_kv, mem_kv, decay, qdec, kdec)


def _mem_attend_row(q, mk_ref, mv_ref):
    s = jnp.sum(mk_ref[0] * q[None], axis=-1, keepdims=True) * ATT_SCALE
    p = jnp.exp(s - jnp.max(s, axis=0, keepdims=True))
    p = p * (1.0 / jnp.sum(p, axis=0, keepdims=True))
    return jnp.sum(p * mv_ref[0], axis=0)


def _swa_step_kernel(sink_ref, rows_ref, kbuf_ref, vbuf_ref, mk_ref, mv_ref, o_ref, nk_ref, nv_ref):
    krow0 = SWA_Q_HEADS
    vrow0 = krow0 + SWA_KV_HEADS
    mrow0 = vrow0 + SWA_KV_HEADS
    wb = kbuf_ref.shape[1]
    kb = kbuf_ref[0]
    vb = vbuf_ref[0]
    k_new = rows_ref[0, krow0:krow0 + SWA_KV_HEADS, :]
    v_new = rows_ref[0, vrow0:vrow0 + SWA_KV_HEADS, :]
    for g in range(SWA_GROUP):
        group_rows = pl.ds(g, SWA_KV_HEADS, stride=SWA_GROUP)
        q = rows_ref[0, group_rows, :]
        s_buf = jnp.sum(kb * q[None], axis=-1, keepdims=True) * ATT_SCALE
        s_new = jnp.sum(k_new * q, axis=-1, keepdims=True) * ATT_SCALE
        sink = sink_ref[g]
        m = jnp.maximum(jnp.maximum(jnp.max(s_buf, axis=0), s_new), sink)
        p_buf = jnp.exp(s_buf - m[None])
        p_new = jnp.exp(s_new - m)
        inv = 1.0 / (jnp.sum(p_buf, axis=0) + p_new + jnp.exp(sink - m))
        o_ref[0, group_rows, :] = jnp.sum((p_buf * inv[None]) * vb, axis=0) + (p_new * inv) * v_new
    nk_ref[0, 0:wb - 1] = kbuf_ref[0, 1:wb]
    nv_ref[0, 0:wb - 1] = vbuf_ref[0, 1:wb]
    nk_ref[0, wb - 1] = k_new
    nv_ref[0, wb - 1] = v_new
    o_ref[0, SWA_Q_HEADS:SWA_Q_HEADS + MEM_HEADS, :] = _mem_attend_row(
        rows_ref[0, mrow0:mrow0 + MEM_HEADS, :], mk_ref, mv_ref)


def _swa_step(rows, cache_k, cache_v, j, mem_k, mem_v, i, sinks):
    bsz = rows.shape[0]
    wb = cache_k.shape[2]
    cache = pl.BlockSpec((None, 1, wb, SWA_KV_HEADS, HEAD_DIM), lambda b: (j, b, 0, 0, 0))
    mem = pl.BlockSpec((None, 1, MEM_LEN, MEM_HEADS, HEAD_DIM), lambda b: (i, b, 0, 0, 0))
    new = pl.BlockSpec((1, wb, SWA_KV_HEADS, HEAD_DIM), lambda b: (b, 0, 0, 0))
    nrows = D_MODEL // HEAD_DIM
    sink_gk = sinks.reshape(SWA_KV_HEADS, SWA_GROUP).T.reshape(SWA_GROUP, SWA_KV_HEADS, 1)
    return pl.pallas_call(
        _swa_step_kernel,
        grid=(bsz,),
        in_specs=[
            pl.BlockSpec(sink_gk.shape, lambda b: (0, 0, 0)),
            pl.BlockSpec((1,) + rows.shape[1:], lambda b: (b, 0, 0)),
            cache, cache, mem, mem,
        ],
        out_specs=[pl.BlockSpec((1, nrows, HEAD_DIM), lambda b: (b, 0, 0)), new, new],
        out_shape=[
            jax.ShapeDtypeStruct((bsz, nrows, HEAD_DIM), F32),
            jax.ShapeDtypeStruct((bsz, wb, SWA_KV_HEADS, HEAD_DIM), F32),
            jax.ShapeDtypeStruct((bsz, wb, SWA_KV_HEADS, HEAD_DIM), F32),
        ],
        compiler_params=_params("parallel"),
        name="swa_step",
    )(sink_gk, rows, cache_k, cache_v, mem_k, mem_v)


def _ret_step_kernel(dec_ref, rows_ref, cols_ref, s_ref, mk_ref, mv_ref, o_ref, s_out_ref):
    krow0 = RET_HEADS
    vrow0 = 2 * RET_HEADS
    grow0 = vrow0 + 2 * RET_HEADS
    mrow0 = grow0 + 2 * RET_HEADS
    for h in range(RET_HEADS):
        q_row = rows_ref[0, h:h + 1, :]
        k_row = rows_ref[0, krow0 + h:krow0 + h + 1, :] * RET_K_SCALE
        q_col = cols_ref[0, :, h:h + 1]
        k_col = cols_ref[0, :, krow0 + h:krow0 + h + 1] * RET_K_SCALE
        inner = jnp.sum(q_row * k_row, axis=1, keepdims=True) * dec_ref[0, h]
        qd = q_col * dec_ref[1, h]
        kd = k_col * dec_ref[2, h]
        halves = []
        for t in range(2):
            lsl = slice(t * HEAD_DIM, (t + 1) * HEAD_DIM)
            v = rows_ref[0, vrow0 + 2 * h + t:vrow0 + 2 * h + t + 1, :]
            st = s_ref[0, h, :, lsl]
            halves.append(inner * v + jnp.sum(qd * st, axis=0, keepdims=True))
            s_out_ref[0, h, :, lsl] = dec_ref[3, h] * st + kd * v
        mu = (jnp.sum(halves[0], axis=1, keepdims=True)
              + jnp.sum(halves[1], axis=1, keepdims=True)) * (1.0 / RET_DV)
        cen = [o - mu for o in halves]
        var = (jnp.sum(cen[0] * cen[0], axis=1, keepdims=True)
               + jnp.sum(cen[1] * cen[1], axis=1, keepdims=True)) * (1.0 / RET_DV)
        rstd = lax.rsqrt(var + HEAD_NORM_EPS)
        for t in range(2):
            gate = rows_ref[0, grow0 + 2 * h + t:grow0 + 2 * h + t + 1, :]
            o_ref[0, 2 * h + t:2 * h + t + 1, :] = jax.nn.silu(gate) * (cen[t] * rstd)
    o_ref[0, 2 * RET_HEADS:2 * RET_HEADS + MEM_HEADS, :] = _mem_attend_row(
        rows_ref[0, mrow0:mrow0 + MEM_HEADS, :], mk_ref, mv_ref)


def _ret_step(rows, cols, state, j, mem_k, mem_v, i, dec):
    bsz = rows.shape[0]
    st_in = pl.BlockSpec((None, 1, RET_HEADS, RET_DK, RET_DV), lambda b: (j, b, 0, 0, 0))
    st_out = pl.BlockSpec((1, RET_HEADS, RET_DK, RET_DV), lambda b: (b, 0, 0, 0))
    mem = pl.BlockSpec((None, 1, MEM_LEN, MEM_HEADS, HEAD_DIM), lambda b: (i, b, 0, 0, 0))
    nrows = D_MODEL // HEAD_DIM
    return pl.pallas_call(
        _ret_step_kernel,
        grid=(bsz,),
        in_specs=[
            pl.BlockSpec(memory_space=pltpu.SMEM),
            pl.BlockSpec((1,) + rows.shape[1:], lambda b: (b, 0, 0)),
            pl.BlockSpec((1,) + cols.shape[1:], lambda b: (b, 0, 0)),
            st_in, mem, mem,
        ],
        out_specs=[pl.BlockSpec((1, nrows, HEAD_DIM), lambda b: (b, 0, 0)), st_out],
        out_shape=[
            jax.ShapeDtypeStruct((bsz, nrows, HEAD_DIM), F32),
            jax.ShapeDtypeStruct((bsz, RET_HEADS, RET_DK, RET_DV), F32),
        ],
        compiler_params=_params("parallel"),
        name="ret_step",
    )(dec, rows, cols, state, mem_k, mem_v)


def _rope_tables(pos):
    half = ROPE_DIM // 2
    inv = ROPE_THETA ** (-jnp.arange(half, dtype=F32) / half)
    ang = pos.astype(F32)[:, None] * inv[None, :]
    cos, sin = jnp.cos(ang), jnp.sin(ang)
    n = pos.shape[0]
    rest = HEAD_DIM - ROPE_DIM
    c = jnp.concatenate([cos, cos, jnp.ones((n, rest), F32)], axis=-1)
    a = jnp.concatenate([-sin, jnp.zeros((n, HEAD_DIM - half), F32)], axis=-1)
    b = jnp.concatenate([jnp.zeros((n, half), F32), sin, jnp.zeros((n, rest), F32)], axis=-1)
    return (c, a, b), (HEAD_DIM - half, half)


def _ret_rot_tables(pos):
    half = RET_DK // 2
    angle = RET_ROT_BASE ** (-jnp.linspace(0.0, 1.0, half, dtype=F32))
    ang = pos.astype(F32)[:, None] * angle[None, :]
    cos, sin = jnp.cos(ang), jnp.sin(ang)
    n = pos.shape[0]
    zero = jnp.zeros_like(sin)
    c = jnp.stack([cos, cos], axis=-1).reshape(n, RET_DK)
    a = jnp.stack([-sin, zero], axis=-1).reshape(n, RET_DK)
    b = jnp.stack([zero, sin], axis=-1).reshape(n, RET_DK)
    return (c, a, b), (RET_DK - 1, 1)


def _ret_decay(chunk):
    log_g = jnp.log1p(-jnp.exp2(-5.0 - jnp.arange(RET_HEADS, dtype=F32)))
    n = jnp.arange(chunk, dtype=F32)
    rel = n[:, None] - n[None, :]
    decay = jnp.where(rel >= 0, jnp.exp(jnp.maximum(rel, 0.0) * log_g[:, None, None]), 0.0)
    q_dec = jnp.exp((n + 1.0) * log_g[:, None])
    k_dec = jnp.exp((chunk - 1.0 - n) * log_g[:, None])
    c_dec = jnp.exp(chunk * log_g)
    return decay, q_dec, k_dec, c_dec


def _trunk(x3, pos, swa_k, swa_v, ret_s, mem_k, mem_v, ln_g, ln_b, ffn_w_gu, ffn_w_down,
           swa_w_in, swa_w_out, swa_sinks, ret_w_in, ret_w_out):
    bsz, seq, _ = x3.shape
    prompt = ret_s is None
    m = bsz * seq
    x = x3.reshape(m, D_MODEL)
    rows_per_tile = _row_tile(m)
    tab_rows = seq if prompt else rows_per_tile
    tab_pos = pos if prompt else jnp.broadcast_to(pos, (tab_rows,))
    rope_tabs, rope_shifts = _rope_tables(tab_pos)
    rot_tabs, rot_shifts = _ret_rot_tables(tab_pos)
    if prompt:
        decay, q_dec, k_dec, c_dec = _ret_decay(RET_CHUNK)
        ret_tabs = (decay,
                    jnp.broadcast_to(q_dec[:, :, None], decay.shape),
                    jnp.broadcast_to(k_dec[:, :, None], decay.shape),
                    c_dec)
    else:
        decay, q_dec, k_dec, c_dec = _ret_decay(seq)
        step_dec = jnp.stack([decay[:, 0, 0], q_dec[:, 0], k_dec[:, 0], c_dec])

    def ln(i, s):
        return ln_g[i, s].reshape(1, D_MODEL), ln_b[i, s].reshape(1, D_MODEL)

    new_k, new_v, new_s = [], [], []
    for i in range(DEPTH):
        j = i // 2
        x = _ffn(x, ffn_w_gu, ffn_w_down, i, 0, *ln(i, 0))
        if i % 2 == 0:
            qkv = _proj(x, swa_w_in, j, n_rot=(SWA_Q + SWA_KV) // PROJ_TILE_N, tables=rope_tabs,
                        shifts=rope_shifts, seq=tab_rows)
            if prompt:
                qkv3 = qkv.reshape(bsz, seq, SWA_IN_WIDTH)
                att = _swa_prompt(qkv3, mem_k, i, swa_sinks[j])
                wlen = min(WINDOW, seq)
                new_k.append(qkv3[:, seq - wlen:, SWA_Q:SWA_Q + SWA_KV]
                             .reshape(bsz, wlen, SWA_KV_HEADS, HEAD_DIM))
                new_v.append(qkv3[:, seq - wlen:, SWA_Q + SWA_KV:SWA_Q + 2 * SWA_KV]
                             .reshape(bsz, wlen, SWA_KV_HEADS, HEAD_DIM))
            else:
                rows = qkv.reshape(bsz, SWA_IN_WIDTH // HEAD_DIM, HEAD_DIM)
                att, nk, nv = _swa_step(rows, swa_k, swa_v, j, mem_k, mem_v, i, swa_sinks[j])
                new_k.append(nk)
                new_v.append(nv)
            w_out = swa_w_out
        else:
            qkvg = _proj(x, ret_w_in, j, n_rot=(2 * RET_QK) // PROJ_TILE_N, tables=rot_tabs,
                         shifts=rot_shifts, seq=tab_rows)
            if prompt:
                att, ns = _ret_prompt(qkvg.reshape(bsz, seq, RET_IN_WIDTH), mem_k, i, ret_tabs)
            else:
                rows = qkvg.reshape(bsz, RET_IN_WIDTH // HEAD_DIM, HEAD_DIM)
                cols = jnp.swapaxes(rows, 1, 2)
                att, ns = _ret_step(rows, cols, ret_s, j, mem_k, mem_v, i, step_dec)
            new_s.append(ns)
            w_out = ret_w_out
        x = _out_ln(att.reshape(m, D_MODEL), w_out, j, x, *ln(i, 1))
        x = _ffn(x, ffn_w_gu, ffn_w_down, i, 1, *ln(i, 2))
    return x.reshape(bsz, seq, D_MODEL), jnp.stack(new_k), jnp.stack(new_v), jnp.stack(new_s)


def kernel(x_prompt, x_sample, cache_swa_k, cache_swa_v, state_ret, cache_mem_k, cache_mem_v,
           mem_prompt, ln_g, ln_b, ffn_w_gu, ffn_w_down, w_mem_kv, swa_w_in, swa_w_out,
           swa_sinks, ret_w_in, ret_w_out):
    bp = mem_prompt.shape[0]
    mem2 = mem_prompt.reshape(bp * MEM_LEN, D_MODEL)
    mem_kv = jnp.stack([_proj(mem2, w_mem_kv, l) for l in range(DEPTH)])
    mem_kv = mem_kv.reshape(DEPTH, bp, MEM_LEN, 2 * MEM_DIM)
    mem_k_prompt = mem_kv[..., :MEM_DIM].reshape(DEPTH, bp, MEM_LEN, MEM_HEADS, HEAD_DIM)
    mem_v_prompt = mem_kv[..., MEM_DIM:].reshape(DEPTH, bp, MEM_LEN, MEM_HEADS, HEAD_DIM)

    pos_prompt = jnp.arange(x_prompt.shape[1], dtype=jnp.int32)
    pos_sample = PAST_LEN + jnp.arange(x_sample.shape[1], dtype=jnp.int32)
    weights = (ln_g, ln_b, ffn_w_gu.astype(BF16), ffn_w_down.astype(BF16), swa_w_in, swa_w_out,
               swa_sinks, ret_w_in, ret_w_out)

    y_prompt, swa_k_prompt, swa_v_prompt, ret_state_prompt = _trunk(
        x_prompt, pos_prompt, None, None, None, mem_kv, None, *weights)
    y_sample, swa_k_sample, swa_v_sample, ret_state_sample = _trunk(
        x_sample, pos_sample, cache_swa_k, cache_swa_v, state_ret, cache_mem_k, cache_mem_v, *weights)
    return (y_prompt, y_sample, swa_k_prompt, swa_v_prompt, swa_k_sample, swa_v_sample,
            ret_state_prompt, ret_state_sample, mem_k_prompt, mem_v_prompt)
```

```python
import functools

import jax
import jax.numpy as jnp
from jax import lax
from jax.experimental import pallas as pl
from jax.experimental.pallas import tpu as pltpu

F32 = jnp.float32
BF16 = jnp.bfloat16

D_MODEL = 2048
DEPTH = 4
PAST_LEN = 16384
HEAD_DIM = 128
MEM_LEN = 256
MEM_HEADS = 4
MEM_DIM = MEM_HEADS * HEAD_DIM
SELF_WIDTH = D_MODEL - MEM_DIM
SWA_Q_HEADS = SELF_WIDTH // HEAD_DIM
SWA_KV_HEADS = SWA_Q_HEADS // 3
SWA_GROUP = SWA_Q_HEADS // SWA_KV_HEADS
WINDOW = 128
ROPE_THETA = 500000.0
ROPE_DIM = HEAD_DIM // 4
RET_DK = 128
RET_DV = 2 * RET_DK
RET_HEADS = SELF_WIDTH // RET_DV
RET_CHUNK = 128
RET_ROT_BASE = 10000.0
D_FF = ((8 * D_MODEL // 3 + 255) // 256) * 256
LN_EPS = 1e-5
HEAD_NORM_EPS = 1e-6
ALPHA = (2.0 * DEPTH) ** 0.25
NEG_INF = -1e30
ATT_SCALE = HEAD_DIM ** -0.5
RET_K_SCALE = RET_DK ** -0.5

SWA_Q = SWA_Q_HEADS * HEAD_DIM
SWA_KV = SWA_KV_HEADS * HEAD_DIM
SWA_IN_WIDTH = SWA_Q + 2 * SWA_KV + MEM_DIM
RET_QK = RET_HEADS * RET_DK
RET_V = RET_HEADS * RET_DV
RET_IN_WIDTH = 2 * RET_QK + 2 * RET_V + MEM_DIM

VMEM_LIMIT_BYTES = 58 * 1024 * 1024
LANES = 128
FFN_TILE_K = 256
FFN_TILE_K_SMALL_M = 512
ATT_BLOCKS_PER_STEP = 2
PROJ_TILE_N = 512
OUT_TILE_N = 512
ROT_ROW_CHUNK = 256


def _row_tile(m):
    return 1024 if m % 1024 == 0 else m


def _params(*sem):
    return pltpu.CompilerParams(dimension_semantics=sem, vmem_limit_bytes=VMEM_LIMIT_BYTES)


def _layer_norm_rows(z, g, b):
    mu = jnp.mean(z, axis=-1, keepdims=True)
    zc = z - mu
    var = jnp.mean(zc * zc, axis=-1, keepdims=True)
    return zc * lax.rsqrt(var + LN_EPS) * g + b


def _ffn_kernel(x_ref, wg_ref, wu_ref, wd_ref, g_ref, b_ref, o_ref, xb_ref, wgu_ref, *, nk, tk):
    k = pl.program_id(1)

    @pl.when(k == 0)
    def _():
        xb_ref[...] = x_ref[...].astype(BF16)
        o_ref[...] = jnp.zeros_like(o_ref)

    wgu_ref[:, :tk] = wg_ref[...].astype(BF16)
    wgu_ref[:, tk:] = wu_ref[...].astype(BF16)
    gu = jnp.dot(xb_ref[...], wgu_ref[...], preferred_element_type=F32)
    h = (jax.nn.silu(gu[:, :tk]) * gu[:, tk:]).astype(BF16)
    o_ref[...] += jnp.dot(h, wd_ref[...].astype(BF16), preferred_element_type=F32)

    @pl.when(k == nk - 1)
    def _():
        z = ALPHA * x_ref[...] + 0.5 * o_ref[...]
        o_ref[...] = _layer_norm_rows(z, g_ref[...], b_ref[...])


def _ffn(x, w_gu, w_down, layer, slot, g, b):
    m = x.shape[0]
    tm = _row_tile(m)
    tk = FFN_TILE_K if tm >= 1024 else FFN_TILE_K_SMALL_M
    nk = D_FF // tk
    return pl.pallas_call(
        functools.partial(_ffn_kernel, nk=nk, tk=tk),
        grid=(m // tm, nk),
        in_specs=[
            pl.BlockSpec((tm, D_MODEL), lambda i, k: (i, 0), pipeline_mode=pl.Buffered(1)),
            pl.BlockSpec((None, None, D_MODEL, tk), lambda i, k: (layer, slot, 0, k)),
            pl.BlockSpec((None, None, D_MODEL, tk), lambda i, k: (layer, slot, 0, nk + k)),
            pl.BlockSpec((None, None, tk, D_MODEL), lambda i, k: (layer, slot, k, 0)),
            pl.BlockSpec((1, D_MODEL), lambda i, k: (0, 0)),
            pl.BlockSpec((1, D_MODEL), lambda i, k: (0, 0)),
        ],
        out_specs=pl.BlockSpec((tm, D_MODEL), lambda i, k: (i, 0)),
        out_shape=jax.ShapeDtypeStruct((m, D_MODEL), F32),
        scratch_shapes=[pltpu.VMEM((tm, D_MODEL), BF16), pltpu.VMEM((D_MODEL, 2 * tk), BF16)],
        compiler_params=_params("parallel", "arbitrary"),
        name="ffn_ln",
    )(x, w_gu, w_gu, w_down, g, b)


def _proj_kernel(x_ref, w_ref, *rest, n_rot, shifts):
    if n_rot:
        c_ref, a_ref, b_ref, o_ref, xb_ref = rest
    else:
        o_ref, xb_ref = rest
    j = pl.program_id(1)

    @pl.when(j == 0)
    def _():
        xb_ref[...] = x_ref[...].astype(BF16)

    wb = w_ref[...].astype(BF16)
    if not n_rot:
        o_ref[...] = jnp.dot(xb_ref[...], wb, preferred_element_type=F32)
        return

    tm, tn = o_ref.shape
    rc = min(tm, ROT_ROW_CHUNK)

    @pl.when(j < n_rot)
    def _():
        for r0 in range(0, tm, rc):
            rows = slice(r0, r0 + rc)
            y = jnp.dot(xb_ref[rows, :], wb, preferred_element_type=F32)
            c, a, b = c_ref[rows, :], a_ref[rows, :], b_ref[rows, :]
            for h in range(tn // LANES):
                yh = y[:, h * LANES:(h + 1) * LANES]
                o_ref[rows, h * LANES:(h + 1) * LANES] = (
                    yh * c + pltpu.roll(yh, shifts[0], 1) * a + pltpu.roll(yh, shifts[1], 1) * b)

    @pl.when(j >= n_rot)
    def _():
        o_ref[...] = jnp.dot(xb_ref[...], wb, preferred_element_type=F32)


def _proj(x, w, layer, n_rot=0, tables=None, shifts=None, seq=None):
    m = x.shape[0]
    n = w.shape[-1]
    tm = _row_tile(m)
    tn = PROJ_TILE_N
    in_specs = [
        pl.BlockSpec((tm, D_MODEL), lambda i, j: (i, 0)),
        pl.BlockSpec((None, D_MODEL, tn), lambda i, j: (layer, 0, j)),
    ]
    args = [x, w]
    if n_rot:
        tiles_per_seq = seq // tm
        tab = pl.BlockSpec((tm, LANES), lambda i, j: (i % tiles_per_seq, 0))
        in_specs += [tab, tab, tab]
        args += list(tables)
    return pl.pallas_call(
        functools.partial(_proj_kernel, n_rot=n_rot, shifts=shifts),
        grid=(m // tm, n // tn),
        in_specs=in_specs,
        out_specs=pl.BlockSpec((tm, tn), lambda i, j: (i, j)),
        out_shape=jax.ShapeDtypeStruct((m, n), F32),
        scratch_shapes=[pltpu.VMEM((tm, D_MODEL), BF16)],
        compiler_params=_params("parallel", "arbitrary"),
        name="proj",
    )(*args)


def _out_ln_kernel(att_ref, w_ref, x_ref, g_ref, b_ref, o_ref, y_ref, *, nn, tn):
    n = pl.program_id(1)
    y_ref[n] = jnp.dot(att_ref[...].astype(BF16), w_ref[...].astype(BF16),
                       preferred_element_type=F32)

    @pl.when(n == nn - 1)
    def _():
        cols = [slice(c * tn, (c + 1) * tn) for c in range(nn)]
        total = None
        for c in range(nn):
            z = ALPHA * x_ref[:, cols[c]] + y_ref[c]
            y_ref[c] = z
            part = jnp.sum(z, axis=-1, keepdims=True)
            total = part if total is None else total + part
        mu = total * (1.0 / D_MODEL)
        total = None
        for c in range(nn):
            zc = y_ref[c] - mu
            part = jnp.sum(zc * zc, axis=-1, keepdims=True)
            total = part if total is None else total + part
        rstd = lax.rsqrt(total * (1.0 / D_MODEL) + LN_EPS)
        for c in range(nn):
            o_ref[:, cols[c]] = (y_ref[c] - mu) * rstd * g_ref[:, cols[c]] + b_ref[:, cols[c]]


def _out_ln(att, w_out, layer, x, g, b):
    m = x.shape[0]
    tm = _row_tile(m)
    tn = OUT_TILE_N
    nn = D_MODEL // tn
    return pl.pallas_call(
        functools.partial(_out_ln_kernel, nn=nn, tn=tn),
        grid=(m // tm, nn),
        in_specs=[
            pl.BlockSpec((tm, D_MODEL), lambda i, n: (i, 0), pipeline_mode=pl.Buffered(1)),
            pl.BlockSpec((None, D_MODEL, tn), lambda i, n: (layer, 0, n)),
            pl.BlockSpec((tm, D_MODEL), lambda i, n: (i, 0), pipeline_mode=pl.Buffered(1)),
            pl.BlockSpec((1, D_MODEL), lambda i, n: (0, 0)),
            pl.BlockSpec((1, D_MODEL), lambda i, n: (0, 0)),
        ],
        out_specs=pl.BlockSpec((tm, D_MODEL), lambda i, n: (i, 0)),
        out_shape=jax.ShapeDtypeStruct((m, D_MODEL), F32),
        scratch_shapes=[pltpu.VMEM((nn, tm, tn), F32)],
        compiler_params=_params("parallel", "arbitrary"),
        name="out_ln",
    )(att, w_out, x, g, b)


def _dot_nt(a, b):
    return lax.dot_general(a, b, (((1,), (1,)), ((), ())), preferred_element_type=F32)


def _out_proj_ln(first, att_ref, w_ref, wb_ref, x_ref, g_ref, b_ref, o_ref):
    @pl.when(first)
    def _():
        wb_ref[...] = w_ref[...].astype(BF16)

    y = jnp.dot(att_ref[...], wb_ref[...], preferred_element_type=F32)
    o_ref[...] = _layer_norm_rows(ALPHA * x_ref[...] + y, g_ref[...], b_ref[...])


def _mem_attend_block(qm_ref, mk_ref, mv_ref, att_ref, col0):
    for h in range(MEM_HEADS):
        sl = slice(h * HEAD_DIM, (h + 1) * HEAD_DIM)
        q = qm_ref[:, sl].astype(BF16)
        s = _dot_nt(q, mk_ref[0, :, sl].astype(BF16)) * ATT_SCALE
        p = jnp.exp(s - jnp.max(s, axis=-1, keepdims=True))
        p = p * (1.0 / jnp.sum(p, axis=-1, keepdims=True))
        o = jnp.dot(p.astype(BF16), mv_ref[0, :, sl].astype(BF16), preferred_element_type=F32)
        att_ref[:, col0 + h * HEAD_DIM:col0 + (h + 1) * HEAD_DIM] = o.astype(att_ref.dtype)


def _swa_prompt_kernel(sink_ref, q_ref, kc_ref, kp_ref, vc_ref, vp_ref, qm_ref, mk_ref, mv_ref,
                       w_ref, x_ref, g_ref, b_ref, o_ref, s_ref, p_ref, att_ref, wb_ref, *, nblk):
    n = pl.program_id(1)
    nq = SWA_Q_HEADS
    w2 = 2 * WINDOW

    def prev_cur(cur_ref, prev_ref, i, sl):
        rows = slice(i * WINDOW, (i + 1) * WINDOW)
        before = prev_ref[:, sl] if i == 0 else cur_ref[(i - 1) * WINDOW:i * WINDOW, sl]
        return jnp.concatenate([before, cur_ref[rows, sl]], axis=0).astype(BF16)

    for i in range(nblk):
        rows = slice(i * WINDOW, (i + 1) * WINDOW)
        for h in range(SWA_KV_HEADS):
            sl = slice(h * HEAD_DIM, (h + 1) * HEAD_DIM)
            q3 = jnp.concatenate(
                [q_ref[rows, (h * SWA_GROUP + g) * HEAD_DIM:(h * SWA_GROUP + g + 1) * HEAD_DIM]
                 for g in range(SWA_GROUP)], axis=0).astype(BF16)
            s = _dot_nt(q3, prev_cur(kc_ref, kp_ref, i, sl)) * ATT_SCALE
            s_ref[i, h * SWA_GROUP:(h + 1) * SWA_GROUP] = s.reshape(SWA_GROUP, WINDOW, w2)
    for h in range(MEM_HEADS):
        sl = slice(h * HEAD_DIM, (h + 1) * HEAD_DIM)
        sm = _dot_nt(qm_ref[:, sl].astype(BF16), mk_ref[0, :, sl].astype(BF16)) * ATT_SCALE
        s_ref[:, nq + h] = sm.reshape(nblk, WINDOW, w2)

    qi = lax.broadcasted_iota(jnp.int32, (WINDOW, w2), 0)
    kj = lax.broadcasted_iota(jnp.int32, (WINDOW, w2), 1)
    sink = sink_ref[...]
    for i in range(nblk):
        first_key = jnp.where(n > 0, qi, WINDOW) if i == 0 else qi
        ok = (kj >= first_key) & (kj <= qi + WINDOW)
        s = jnp.where(ok[None], s_ref[i, 0:nq], NEG_INF)
        m = jnp.maximum(jnp.max(s, axis=-1, keepdims=True), sink)
        p = jnp.exp(s - m)
        den = jnp.sum(p, axis=-1, keepdims=True) + jnp.exp(sink - m)
        p_ref[i, 0:nq] = (p * (1.0 / den)).astype(BF16)
    s = s_ref[:, nq:nq + MEM_HEADS]
    p = jnp.exp(s - jnp.max(s, axis=-1, keepdims=True))
    p_ref[:, nq:nq + MEM_HEADS] = (p * (1.0 / jnp.sum(p, axis=-1, keepdims=True))).astype(BF16)

    for i in range(nblk):
        rows = slice(i * WINDOW, (i + 1) * WINDOW)
        for h in range(SWA_KV_HEADS):
            sl = slice(h * HEAD_DIM, (h + 1) * HEAD_DIM)
            p3 = p_ref[i, h * SWA_GROUP:(h + 1) * SWA_GROUP].reshape(SWA_GROUP * WINDOW, w2)
            o = jnp.dot(p3, prev_cur(vc_ref, vp_ref, i, sl), preferred_element_type=F32)
            for g in range(SWA_GROUP):
                hq = h * SWA_GROUP + g
                att_ref[rows, hq * HEAD_DIM:(hq + 1) * HEAD_DIM] = (
                    o[g * WINDOW:(g + 1) * WINDOW].astype(BF16))
    for h in range(MEM_HEADS):
        sl = slice(h * HEAD_DIM, (h + 1) * HEAD_DIM)
        pm = p_ref[:, nq + h].reshape(nblk * WINDOW, w2)
        om = jnp.dot(pm, mv_ref[0, :, sl].astype(BF16), preferred_element_type=F32)
        att_ref[:, SWA_Q + h * HEAD_DIM:SWA_Q + (h + 1) * HEAD_DIM] = om.astype(BF16)

    first = jnp.logical_and(pl.program_id(0) == 0, n == 0)
    _out_proj_ln(first, att_ref, w_ref, wb_ref, x_ref, g_ref, b_ref, o_ref)


def _mixer_tail_specs(layer, row_of, rows):
    mem = (None, 1, MEM_LEN, MEM_DIM)
    vec = pl.BlockSpec((1, D_MODEL), lambda b, n: (0, 0))
    mixer = layer // 2
    return [
        pl.BlockSpec(mem, lambda b, n: (layer, b, 0, 0)),
        pl.BlockSpec(mem, lambda b, n: (layer, b, 0, 1)),
        pl.BlockSpec((None, D_MODEL, D_MODEL), lambda b, n: (mixer, 0, 0), pipeline_mode=pl.Buffered(1)),
        pl.BlockSpec((rows, D_MODEL), lambda b, n: (row_of(b, n), 0)),
        vec, vec,
    ]


def _swa_prompt(qkv, bsz, seq, mem_kv, layer, sinks, w_out, x, g, b):
    nblk = ATT_BLOCKS_PER_STEP
    rows = nblk * WINDOW
    steps = seq // rows
    blocks_per_seq = seq // WINDOW
    kcol = SWA_Q // SWA_KV
    vcol = kcol + 1
    mcol = vcol + 1
    row_of = lambda b, n: b * steps + n
    prev_of = lambda b, n: b * blocks_per_seq + jnp.maximum(n * nblk - 1, 0)
    cur = lambda w, col: pl.BlockSpec((rows, w), lambda b, n: (row_of(b, n), col))
    prev = lambda col: pl.BlockSpec((WINDOW, SWA_KV), lambda b, n: (prev_of(b, n), col))
    nheads = SWA_Q_HEADS + MEM_HEADS
    assert MEM_LEN == 2 * WINDOW
    in_specs = [
        pl.BlockSpec((SWA_Q_HEADS, 1, 1), lambda b, n: (0, 0, 0)),
        cur(SWA_Q, 0), cur(SWA_KV, kcol), prev(kcol), cur(SWA_KV, vcol), prev(vcol), cur(MEM_DIM, mcol),
    ] + _mixer_tail_specs(layer, row_of, rows)
    return pl.pallas_call(
        functools.partial(_swa_prompt_kernel, nblk=nblk),
        grid=(bsz, steps),
        in_specs=in_specs,
        out_specs=pl.BlockSpec((rows, D_MODEL), lambda b, n: (row_of(b, n), 0)),
        out_shape=jax.ShapeDtypeStruct(x.shape, F32),
        scratch_shapes=[pltpu.VMEM((nblk, nheads, WINDOW, 2 * WINDOW), F32),
                        pltpu.VMEM((nblk, nheads, WINDOW, 2 * WINDOW), BF16),
                        pltpu.VMEM((rows, D_MODEL), BF16),
                        pltpu.VMEM((D_MODEL, D_MODEL), BF16)],
        input_output_aliases={len(in_specs) - 3: 0},
        compiler_params=_params("arbitrary", "arbitrary"),
        name="swa_prompt",
    )(sinks.reshape(SWA_Q_HEADS, 1, 1), qkv, qkv, qkv, qkv, qkv, qkv, mem_kv, mem_kv, w_out, x, g, b)


def _head_norm_gate(o, gate):
    mu = jnp.mean(o, axis=-1, keepdims=True)
    oc = o - mu
    var = jnp.mean(oc * oc, axis=-1, keepdims=True)
    return jax.nn.silu(gate) * (oc * lax.rsqrt(var + HEAD_NORM_EPS))


def _ret_prompt_kernel(cdec_ref, decay_ref, qdec_ref, kdec_ref, q_ref, k_ref, v_ref, gate_ref, qm_ref,
                       mk_ref, mv_ref, w_ref, x_ref, g_ref, b_ref, o_ref, s_out_ref,
                       state_ref, att_ref, wb_ref, *, steps, nblk):
    c = pl.program_id(1)

    @pl.when(c == 0)
    def _():
        state_ref[...] = jnp.zeros_like(state_ref)

    for i in range(nblk):
        rows = slice(i * RET_CHUNK, (i + 1) * RET_CHUNK)
        for h in range(RET_HEADS):
            ksl = slice(h * RET_DK, (h + 1) * RET_DK)
            vsl = slice(h * RET_DV, (h + 1) * RET_DV)
            qc = q_ref[rows, ksl]
            kc = k_ref[rows, ksl] * RET_K_SCALE
            vb = v_ref[rows, vsl].astype(BF16)
            st = state_ref[h]
            inner = _dot_nt(qc.astype(BF16), kc.astype(BF16)) * decay_ref[h]
            o = (jnp.dot(inner.astype(BF16), vb, preferred_element_type=F32)
                 + jnp.dot((qc * qdec_ref[h]).astype(BF16), st.astype(BF16),
                           preferred_element_type=F32))
            kd = (kc * kdec_ref[h]).astype(BF16)
            state_ref[h] = cdec_ref[h] * st + lax.dot_general(
                kd, vb, (((0,), (0,)), ((), ())), preferred_element_type=F32)
            att_ref[rows, vsl] = _head_norm_gate(o, gate_ref[rows, vsl]).astype(BF16)
    _mem_attend_block(qm_ref, mk_ref, mv_ref, att_ref, RET_V)

    @pl.when(c == steps - 1)
    def _():
        s_out_ref[0] = state_ref[...]

    first = jnp.logical_and(pl.program_id(0) == 0, c == 0)
    _out_proj_ln(first, att_ref, w_ref, wb_ref, x_ref, g_ref, b_ref, o_ref)


def _ret_prompt(qkvg, bsz, seq, mem_kv, layer, tables, w_out, x, g, b):
    nblk = ATT_BLOCKS_PER_STEP
    rows = nblk * RET_CHUNK
    steps = seq // rows
    decay, qdec, kdec, cdec = tables
    row_of = lambda b, n: b * steps + n
    cur = lambda w, col: pl.BlockSpec((rows, w), lambda b, n: (row_of(b, n), col))
    tab = pl.BlockSpec((RET_HEADS, RET_CHUNK, RET_CHUNK), lambda b, n: (0, 0, 0))
    in_specs = [
        pl.BlockSpec(memory_space=pltpu.SMEM), tab, tab, tab,
        cur(RET_QK, 0), cur(RET_QK, 1), cur(RET_V, 1), cur(RET_V, 2),
        cur(MEM_DIM, (2 * RET_QK + 2 * RET_V) // MEM_DIM),
    ] + _mixer_tail_specs(layer, row_of, rows)
    return pl.pallas_call(
        functools.partial(_ret_prompt_kernel, steps=steps, nblk=nblk),
        grid=(bsz, steps),
        in_specs=in_specs,
        out_specs=[
            pl.BlockSpec((rows, D_MODEL), lambda b, n: (row_of(b, n), 0)),
            pl.BlockSpec((1, RET_HEADS, RET_DK, RET_DV), lambda b, n: (b, 0, 0, 0)),
        ],
        out_shape=[
            jax.ShapeDtypeStruct(x.shape, F32),
            jax.ShapeDtypeStruct((bsz, RET_HEADS, RET_DK, RET_DV), F32),
        ],
        scratch_shapes=[pltpu.VMEM((RET_HEADS, RET_DK, RET_DV), F32),
                        pltpu.VMEM((rows, D_MODEL), BF16),
                        pltpu.VMEM((D_MODEL, D_MODEL), BF16)],
        input_output_aliases={len(in_specs) - 3: 0},
        compiler_params=_params("arbitrary", "arbitrary"),
        name="ret_prompt",
    )(cdec, decay, qdec, kdec, qkvg, qkvg, qkvg, qkvg, qkvg, mem_kv, mem_kv, w_out, x, g, b)


def _mem_attend_row(q, mk_ref, mv_ref):
    s = jnp.sum(mk_ref[0] * q[None], axis=-1, keepdims=True) * ATT_SCALE
    p = jnp.exp(s - jnp.max(s, axis=0, keepdims=True))
    p = p * (1.0 / jnp.sum(p, axis=0, keepdims=True))
    return jnp.sum(p * mv_ref[0], axis=0)


def _swa_step_kernel(sink_ref, rows_ref, kbuf_ref, vbuf_ref, mk_ref, mv_ref, o_ref, nk_ref, nv_ref):
    krow0 = SWA_Q_HEADS
    vrow0 = krow0 + SWA_KV_HEADS
    mrow0 = vrow0 + SWA_KV_HEADS
    wb = kbuf_ref.shape[1]
    kb = kbuf_ref[0]
    vb = vbuf_ref[0]
    k_new = rows_ref[0, krow0:krow0 + SWA_KV_HEADS, :]
    v_new = rows_ref[0, vrow0:vrow0 + SWA_KV_HEADS, :]
    for g in range(SWA_GROUP):
        group_rows = pl.ds(g, SWA_KV_HEADS, stride=SWA_GROUP)
        q = rows_ref[0, group_rows, :]
        s_buf = jnp.sum(kb * q[None], axis=-1, keepdims=True) * ATT_SCALE
        s_new = jnp.sum(k_new * q, axis=-1, keepdims=True) * ATT_SCALE
        sink = sink_ref[g]
        m = jnp.maximum(jnp.maximum(jnp.max(s_buf, axis=0), s_new), sink)
        p_buf = jnp.exp(s_buf - m[None])
        p_new = jnp.exp(s_new - m)
        inv = 1.0 / (jnp.sum(p_buf, axis=0) + p_new + jnp.exp(sink - m))
        o_ref[0, group_rows, :] = jnp.sum((p_buf * inv[None]) * vb, axis=0) + (p_new * inv) * v_new
    nk_ref[0, 0:wb - 1] = kbuf_ref[0, 1:wb]
    nv_ref[0, 0:wb - 1] = vbuf_ref[0, 1:wb]
    nk_ref[0, wb - 1] = k_new
    nv_ref[0, wb - 1] = v_new
    o_ref[0, SWA_Q_HEADS:SWA_Q_HEADS + MEM_HEADS, :] = _mem_attend_row(
        rows_ref[0, mrow0:mrow0 + MEM_HEADS, :], mk_ref, mv_ref)


def _swa_step(rows, cache_k, cache_v, j, mem_k, mem_v, i, sinks):
    bsz = rows.shape[0]
    wb = cache_k.shape[2]
    cache = pl.BlockSpec((None, 1, wb, SWA_KV_HEADS, HEAD_DIM), lambda b: (j, b, 0, 0, 0))
    mem = pl.BlockSpec((None, 1, MEM_LEN, MEM_HEADS, HEAD_DIM), lambda b: (i, b, 0, 0, 0))
    new = pl.BlockSpec((1, wb, SWA_KV_HEADS, HEAD_DIM), lambda b: (b, 0, 0, 0))
    nrows = D_MODEL // HEAD_DIM
    sink_gk = sinks.reshape(SWA_KV_HEADS, SWA_GROUP).T.reshape(SWA_GROUP, SWA_KV_HEADS, 1)
    return pl.pallas_call(
        _swa_step_kernel,
        grid=(bsz,),
        in_specs=[
            pl.BlockSpec(sink_gk.shape, lambda b: (0, 0, 0)),
            pl.BlockSpec((1,) + rows.shape[1:], lambda b: (b, 0, 0)),
            cache, cache, mem, mem,
        ],
        out_specs=[pl.BlockSpec((1, nrows, HEAD_DIM), lambda b: (b, 0, 0)), new, new],
        out_shape=[
            jax.ShapeDtypeStruct((bsz, nrows, HEAD_DIM), F32),
            jax.ShapeDtypeStruct((bsz, wb, SWA_KV_HEADS, HEAD_DIM), F32),
            jax.ShapeDtypeStruct((bsz, wb, SWA_KV_HEADS, HEAD_DIM), F32),
        ],
        compiler_params=_params("parallel"),
        name="swa_step",
    )(sink_gk, rows, cache_k, cache_v, mem_k, mem_v)


def _ret_step_kernel(dec_ref, rows_ref, cols_ref, s_ref, mk_ref, mv_ref, o_ref, s_out_ref):
    krow0 = RET_HEADS
    vrow0 = 2 * RET_HEADS
    grow0 = vrow0 + 2 * RET_HEADS
    mrow0 = grow0 + 2 * RET_HEADS
    for h in range(RET_HEADS):
        q_row = rows_ref[0, h:h + 1, :]
        k_row = rows_ref[0, krow0 + h:krow0 + h + 1, :] * RET_K_SCALE
        q_col = cols_ref[0, :, h:h + 1]
        k_col = cols_ref[0, :, krow0 + h:krow0 + h + 1] * RET_K_SCALE
        inner = jnp.sum(q_row * k_row, axis=1, keepdims=True) * dec_ref[0, h]
        qd = q_col * dec_ref[1, h]
        kd = k_col * dec_ref[2, h]
        halves = []
        for t in range(2):
            lsl = slice(t * HEAD_DIM, (t + 1) * HEAD_DIM)
            v = rows_ref[0, vrow0 + 2 * h + t:vrow0 + 2 * h + t + 1, :]
            st = s_ref[0, h, :, lsl]
            halves.append(inner * v + jnp.sum(qd * st, axis=0, keepdims=True))
            s_out_ref[0, h, :, lsl] = dec_ref[3, h] * st + kd * v
        mu = (jnp.sum(halves[0], axis=1, keepdims=True)
              + jnp.sum(halves[1], axis=1, keepdims=True)) * (1.0 / RET_DV)
        cen = [o - mu for o in halves]
        var = (jnp.sum(cen[0] * cen[0], axis=1, keepdims=True)
               + jnp.sum(cen[1] * cen[1], axis=1, keepdims=True)) * (1.0 / RET_DV)
        rstd = lax.rsqrt(var + HEAD_NORM_EPS)
        for t in range(2):
            gate = rows_ref[0, grow0 + 2 * h + t:grow0 + 2 * h + t + 1, :]
            o_ref[0, 2 * h + t:2 * h + t + 1, :] = jax.nn.silu(gate) * (cen[t] * rstd)
    o_ref[0, 2 * RET_HEADS:2 * RET_HEADS + MEM_HEADS, :] = _mem_attend_row(
        rows_ref[0, mrow0:mrow0 + MEM_HEADS, :], mk_ref, mv_ref)


def _ret_step(rows, cols, state, j, mem_k, mem_v, i, dec):
    bsz = rows.shape[0]
    st_in = pl.BlockSpec((None, 1, RET_HEADS, RET_DK, RET_DV), lambda b: (j, b, 0, 0, 0))
    st_out = pl.BlockSpec((1, RET_HEADS, RET_DK, RET_DV), lambda b: (b, 0, 0, 0))
    mem = pl.BlockSpec((None, 1, MEM_LEN, MEM_HEADS, HEAD_DIM), lambda b: (i, b, 0, 0, 0))
    nrows = D_MODEL // HEAD_DIM
    return pl.pallas_call(
        _ret_step_kernel,
        grid=(bsz,),
        in_specs=[
            pl.BlockSpec(memory_space=pltpu.SMEM),
            pl.BlockSpec((1,) + rows.shape[1:], lambda b: (b, 0, 0)),
            pl.BlockSpec((1,) + cols.shape[1:], lambda b: (b, 0, 0)),
            st_in, mem, mem,
        ],
        out_specs=[pl.BlockSpec((1, nrows, HEAD_DIM), lambda b: (b, 0, 0)), st_out],
        out_shape=[
            jax.ShapeDtypeStruct((bsz, nrows, HEAD_DIM), F32),
            jax.ShapeDtypeStruct((bsz, RET_HEADS, RET_DK, RET_DV), F32),
        ],
        compiler_params=_params("parallel"),
        name="ret_step",
    )(dec, rows, cols, state, mem_k, mem_v)


def _rope_tables(pos):
    half = ROPE_DIM // 2
    inv = ROPE_THETA ** (-jnp.arange(half, dtype=F32) / half)
    ang = pos.astype(F32)[:, None] * inv[None, :]
    cos, sin = jnp.cos(ang), jnp.sin(ang)
    n = pos.shape[0]
    rest = HEAD_DIM - ROPE_DIM
    c = jnp.concatenate([cos, cos, jnp.ones((n, rest), F32)], axis=-1)
    a = jnp.concatenate([-sin, jnp.zeros((n, HEAD_DIM - half), F32)], axis=-1)
    b = jnp.concatenate([jnp.zeros((n, half), F32), sin, jnp.zeros((n, rest), F32)], axis=-1)
    return (c, a, b), (HEAD_DIM - half, half)


def _ret_rot_tables(pos):
    half = RET_DK // 2
    angle = RET_ROT_BASE ** (-jnp.linspace(0.0, 1.0, half, dtype=F32))
    ang = pos.astype(F32)[:, None] * angle[None, :]
    cos, sin = jnp.cos(ang), jnp.sin(ang)
    n = pos.shape[0]
    zero = jnp.zeros_like(sin)
    c = jnp.stack([cos, cos], axis=-1).reshape(n, RET_DK)
    a = jnp.stack([-sin, zero], axis=-1).reshape(n, RET_DK)
    b = jnp.stack([zero, sin], axis=-1).reshape(n, RET_DK)
    return (c, a, b), (RET_DK - 1, 1)


def _ret_decay(chunk):
    log_g = jnp.log1p(-jnp.exp2(-5.0 - jnp.arange(RET_HEADS, dtype=F32)))
    n = jnp.arange(chunk, dtype=F32)
    rel = n[:, None] - n[None, :]
    decay = jnp.where(rel >= 0, jnp.exp(jnp.maximum(rel, 0.0) * log_g[:, None, None]), 0.0)
    q_dec = jnp.exp((n + 1.0) * log_g[:, None])
    k_dec = jnp.exp((chunk - 1.0 - n) * log_g[:, None])
    c_dec = jnp.exp(chunk * log_g)
    return decay, q_dec, k_dec, c_dec


def _trunk(x3, pos, swa_k, swa_v, ret_s, mem_k, mem_v, ln_g, ln_b, ffn_w_gu, ffn_w_down,
           swa_w_in, swa_w_out, swa_sinks, ret_w_in, ret_w_out):
    bsz, seq, _ = x3.shape
    prompt = ret_s is None
    m = bsz * seq
    x = x3.reshape(m, D_MODEL)
    rows_per_tile = _row_tile(m)
    tab_rows = seq if prompt else rows_per_tile
    tab_pos = pos if prompt else jnp.broadcast_to(pos, (tab_rows,))
    rope_tabs, rope_shifts = _rope_tables(tab_pos)
    rot_tabs, rot_shifts = _ret_rot_tables(tab_pos)
    if prompt:
        decay, q_dec, k_dec, c_dec = _ret_decay(RET_CHUNK)
        ret_tabs = (decay,
                    jnp.broadcast_to(q_dec[:, :, None], decay.shape),
                    jnp.broadcast_to(k_dec[:, :, None], decay.shape),
                    c_dec)
    else:
        decay, q_dec, k_dec, c_dec = _ret_decay(seq)
        step_dec = jnp.stack([decay[:, 0, 0], q_dec[:, 0], k_dec[:, 0], c_dec])

    def ln(i, s):
        return ln_g[i, s].reshape(1, D_MODEL), ln_b[i, s].reshape(1, D_MODEL)

    new_k, new_v, new_s = [], [], []
    for i in range(DEPTH):
        j = i // 2
        x = _ffn(x, ffn_w_gu, ffn_w_down, i, 0, *ln(i, 0))
        if i % 2 == 0:
            qkv = _proj(x, swa_w_in, j, n_rot=(SWA_Q + SWA_KV) // PROJ_TILE_N, tables=rope_tabs,
                        shifts=rope_shifts, seq=tab_rows)
            if prompt:
                qkv3 = qkv.reshape(bsz, seq, SWA_IN_WIDTH)
                x = _swa_prompt(qkv, bsz, seq, mem_k, i, swa_sinks[j], swa_w_out, x, *ln(i, 1))
                wlen = min(WINDOW, seq)
                new_k.append(qkv3[:, seq - wlen:, SWA_Q:SWA_Q + SWA_KV]
                             .reshape(bsz, wlen, SWA_KV_HEADS, HEAD_DIM))
                new_v.append(qkv3[:, seq - wlen:, SWA_Q + SWA_KV:SWA_Q + 2 * SWA_KV]
                             .reshape(bsz, wlen, SWA_KV_HEADS, HEAD_DIM))
            else:
                rows = qkv.reshape(bsz, SWA_IN_WIDTH // HEAD_DIM, HEAD_DIM)
                att, nk, nv = _swa_step(rows, swa_k, swa_v, j, mem_k, mem_v, i, swa_sinks[j])
                new_k.append(nk)
                new_v.append(nv)
                x = _out_ln(att.reshape(m, D_MODEL), swa_w_out, j, x, *ln(i, 1))
        else:
            qkvg = _proj(x, ret_w_in, j, n_rot=(2 * RET_QK) // PROJ_TILE_N, tables=rot_tabs,
                         shifts=rot_shifts, seq=tab_rows)
            if prompt:
                x, ns = _ret_prompt(qkvg, bsz, seq, mem_k, i, ret_tabs, ret_w_out, x, *ln(i, 1))
            else:
                rows = qkvg.reshape(bsz, RET_IN_WIDTH // HEAD_DIM, HEAD_DIM)
                cols = jnp.swapaxes(rows, 1, 2)
                att, ns = _ret_step(rows, cols, ret_s, j, mem_k, mem_v, i, step_dec)
                x = _out_ln(att.reshape(m, D_MODEL), ret_w_out, j, x, *ln(i, 1))
            new_s.append(ns)
        x = _ffn(x, ffn_w_gu, ffn_w_down, i, 1, *ln(i, 2))
    return x.reshape(bsz, seq, D_MODEL), jnp.stack(new_k), jnp.stack(new_v), jnp.stack(new_s)


def kernel(x_prompt, x_sample, cache_swa_k, cache_swa_v, state_ret, cache_mem_k, cache_mem_v,
           mem_prompt, ln_g, ln_b, ffn_w_gu, ffn_w_down, w_mem_kv, swa_w_in, swa_w_out,
           swa_sinks, ret_w_in, ret_w_out):
    bp = mem_prompt.shape[0]
    mem2 = mem_prompt.reshape(bp * MEM_LEN, D_MODEL)
    mem_kv = jnp.stack([_proj(mem2, w_mem_kv, l) for l in range(DEPTH)])
    mem_kv = mem_kv.reshape(DEPTH, bp, MEM_LEN, 2 * MEM_DIM)
    mem_k_prompt = mem_kv[..., :MEM_DIM].reshape(DEPTH, bp, MEM_LEN, MEM_HEADS, HEAD_DIM)
    mem_v_prompt = mem_kv[..., MEM_DIM:].reshape(DEPTH, bp, MEM_LEN, MEM_HEADS, HEAD_DIM)

    pos_prompt = jnp.arange(x_prompt.shape[1], dtype=jnp.int32)
    pos_sample = PAST_LEN + jnp.arange(x_sample.shape[1], dtype=jnp.int32)
    weights = (ln_g, ln_b, ffn_w_gu, ffn_w_down, swa_w_in, swa_w_out, swa_sinks, ret_w_in, ret_w_out)

    y_prompt, swa_k_prompt, swa_v_prompt, ret_state_prompt = _trunk(
        x_prompt, pos_prompt, None, None, None, mem_kv, None, *weights)
    y_sample, swa_k_sample, swa_v_sample, ret_state_sample = _trunk(
        x_sample, pos_sample, cache_swa_k, cache_swa_v, state_ret, cache_mem_k, cache_mem_v, *weights)
    return (y_prompt, y_sample, swa_k_prompt, swa_v_prompt, swa_k_sample, swa_v_sample,
            ret_state_prompt, ret_state_sample, mem_k_prompt, mem_v_prompt)
```

```python
import functools

import jax
import jax.numpy as jnp
from jax import lax
from jax.experimental import pallas as pl
from jax.experimental.pallas import tpu as pltpu

F32 = jnp.float32
BF16 = jnp.bfloat16

D_MODEL = 2048
DEPTH = 4
PAST_LEN = 16384
HEAD_DIM = 128
MEM_LEN = 256
MEM_HEADS = 4
MEM_DIM = MEM_HEADS * HEAD_DIM
SELF_WIDTH = D_MODEL - MEM_DIM
SWA_Q_HEADS = SELF_WIDTH // HEAD_DIM
SWA_KV_HEADS = SWA_Q_HEADS // 3
SWA_GROUP = SWA_Q_HEADS // SWA_KV_HEADS
WINDOW = 128
ROPE_THETA = 500000.0
ROPE_DIM = HEAD_DIM // 4
RET_DK = 128
RET_DV = 2 * RET_DK
RET_HEADS = SELF_WIDTH // RET_DV
RET_CHUNK = 128
RET_ROT_BASE = 10000.0
D_FF = ((8 * D_MODEL // 3 + 255) // 256) * 256
LN_EPS = 1e-5
HEAD_NORM_EPS = 1e-6
ALPHA = (2.0 * DEPTH) ** 0.25
NEG_INF = -1e30
ATT_SCALE = HEAD_DIM ** -0.5
RET_K_SCALE = RET_DK ** -0.5

SWA_Q = SWA_Q_HEADS * HEAD_DIM
SWA_KV = SWA_KV_HEADS * HEAD_DIM
SWA_IN_WIDTH = SWA_Q + 2 * SWA_KV + MEM_DIM
RET_QK = RET_HEADS * RET_DK
RET_V = RET_HEADS * RET_DV
RET_IN_WIDTH = 2 * RET_QK + 2 * RET_V + MEM_DIM

VMEM_LIMIT_BYTES = 58 * 1024 * 1024
LANES = 128
FFN_TILE_K = 256
FFN_TILE_K_SMALL_M = 512
ATT_BLOCKS_PER_STEP = 2
PROJ_TILE_N = 512
OUT_TILE_N = 512
MAX_ROW_CHUNK = 256
BF16_SUBLANES = 16


def _row_tile(m):
    return 1024 if m % 1024 == 0 else m


def _row_chunk(tm):
    for c in range(MAX_ROW_CHUNK, BF16_SUBLANES - 1, -BF16_SUBLANES):
        if tm % c == 0:
            return c
    return tm


def _params(*sem):
    return pltpu.CompilerParams(dimension_semantics=sem, vmem_limit_bytes=VMEM_LIMIT_BYTES)


def _layer_norm_rows(z, g, b):
    mu = jnp.mean(z, axis=-1, keepdims=True)
    zc = z - mu
    var = jnp.mean(zc * zc, axis=-1, keepdims=True)
    return zc * lax.rsqrt(var + LN_EPS) * g + b


def _ffn_kernel(x_ref, wg_ref, wu_ref, wd_ref, g_ref, b_ref, o_ref, xb_ref, wgu_ref, *, nk, tk, chunk):
    k = pl.program_id(1)
    tm = o_ref.shape[0]
    chunks = [slice(r, r + chunk) for r in range(0, tm, chunk)]

    def cast_weights():
        wgu_ref[:, :tk] = wg_ref[...].astype(BF16)
        wgu_ref[:, tk:] = wu_ref[...].astype(BF16)
        return wd_ref[...].astype(BF16)

    def down(xb, wd):
        gu = jnp.dot(xb, wgu_ref[...], preferred_element_type=F32)
        h = (jax.nn.silu(gu[:, :tk]) * gu[:, tk:]).astype(BF16)
        return jnp.dot(h, wd, preferred_element_type=F32)

    @pl.when(k == 0)
    def _():
        wd = cast_weights()
        for rows in chunks:
            xb = x_ref[rows, :].astype(BF16)
            xb_ref[rows, :] = xb
            o_ref[rows, :] = down(xb, wd)

    @pl.when(jnp.logical_and(k > 0, k < nk - 1))
    def _():
        wd = cast_weights()
        o_ref[...] += down(xb_ref[...], wd)

    @pl.when(k == nk - 1)
    def _():
        wd = cast_weights()
        for rows in chunks:
            acc = o_ref[rows, :] + down(xb_ref[rows, :], wd)
            z = ALPHA * x_ref[rows, :] + 0.5 * acc
            o_ref[rows, :] = _layer_norm_rows(z, g_ref[...], b_ref[...])


def _ffn(x, w_gu, w_down, layer, slot, g, b):
    m = x.shape[0]
    tm = _row_tile(m)
    tk = FFN_TILE_K if tm >= 1024 else FFN_TILE_K_SMALL_M
    nk = D_FF // tk
    assert nk >= 2
    return pl.pallas_call(
        functools.partial(_ffn_kernel, nk=nk, tk=tk, chunk=_row_chunk(tm)),
        grid=(m // tm, nk),
        in_specs=[
            pl.BlockSpec((tm, D_MODEL), lambda i, k: (i, 0), pipeline_mode=pl.Buffered(1)),
            pl.BlockSpec((None, None, D_MODEL, tk), lambda i, k: (layer, slot, 0, k)),
            pl.BlockSpec((None, None, D_MODEL, tk), lambda i, k: (layer, slot, 0, nk + k)),
            pl.BlockSpec((None, None, tk, D_MODEL), lambda i, k: (layer, slot, k, 0)),
            pl.BlockSpec((1, D_MODEL), lambda i, k: (0, 0)),
            pl.BlockSpec((1, D_MODEL), lambda i, k: (0, 0)),
        ],
        out_specs=pl.BlockSpec((tm, D_MODEL), lambda i, k: (i, 0)),
        out_shape=jax.ShapeDtypeStruct((m, D_MODEL), F32),
        scratch_shapes=[pltpu.VMEM((tm, D_MODEL), BF16), pltpu.VMEM((D_MODEL, 2 * tk), BF16)],
        compiler_params=_params("parallel", "arbitrary"),
        name="ffn_ln",
    )(x, w_gu, w_gu, w_down, g, b)


def _proj_kernel(x_ref, w_ref, *rest, n_rot, shifts):
    if n_rot:
        c_ref, a_ref, b_ref, o_ref, xb_ref = rest
    else:
        o_ref, xb_ref = rest
    j = pl.program_id(1)

    @pl.when(j == 0)
    def _():
        xb_ref[...] = x_ref[...].astype(BF16)

    if not n_rot:
        o_ref[...] = jnp.dot(xb_ref[...], w_ref[...].astype(BF16), preferred_element_type=F32)
        return

    tm, tn = o_ref.shape
    rc = _row_chunk(tm)

    @pl.when(j < n_rot)
    def _():
        wb = w_ref[...].astype(BF16)
        for r0 in range(0, tm, rc):
            rows = slice(r0, r0 + rc)
            y = jnp.dot(xb_ref[rows, :], wb, preferred_element_type=F32)
            c, a, b = c_ref[rows, :], a_ref[rows, :], b_ref[rows, :]
            for h in range(tn // LANES):
                yh = y[:, h * LANES:(h + 1) * LANES]
                o_ref[rows, h * LANES:(h + 1) * LANES] = (
                    yh * c + pltpu.roll(yh, shifts[0], 1) * a + pltpu.roll(yh, shifts[1], 1) * b)

    @pl.when(j >= n_rot)
    def _():
        o_ref[...] = jnp.dot(xb_ref[...], w_ref[...].astype(BF16), preferred_element_type=F32)


def _proj(x, w, layer, n_rot=0, tables=None, shifts=None, seq=None):
    m = x.shape[0]
    n = w.shape[-1]
    tm = _row_tile(m)
    tn = PROJ_TILE_N
    in_specs = [
        pl.BlockSpec((tm, D_MODEL), lambda i, j: (i, 0)),
        pl.BlockSpec((None, D_MODEL, tn), lambda i, j: (layer, 0, j)),
    ]
    args = [x, w]
    if n_rot:
        tiles_per_seq = seq // tm
        tab = pl.BlockSpec((tm, LANES), lambda i, j: (i % tiles_per_seq, 0))
        in_specs += [tab, tab, tab]
        args += list(tables)
    return pl.pallas_call(
        functools.partial(_proj_kernel, n_rot=n_rot, shifts=shifts),
        grid=(m // tm, n // tn),
        in_specs=in_specs,
        out_specs=pl.BlockSpec((tm, tn), lambda i, j: (i, j)),
        out_shape=jax.ShapeDtypeStruct((m, n), F32),
        scratch_shapes=[pltpu.VMEM((tm, D_MODEL), BF16)],
        compiler_params=_params("parallel", "arbitrary"),
        name="proj",
    )(*args)


def _out_ln_kernel(att_ref, w_ref, x_ref, g_ref, b_ref, o_ref, y_ref, *, nn, tn):
    n = pl.program_id(1)
    y_ref[n] = jnp.dot(att_ref[...].astype(BF16), w_ref[...].astype(BF16),
                       preferred_element_type=F32)

    @pl.when(n == nn - 1)
    def _():
        cols = [slice(c * tn, (c + 1) * tn) for c in range(nn)]
        total = None
        for c in range(nn):
            z = ALPHA * x_ref[:, cols[c]] + y_ref[c]
            y_ref[c] = z
            part = jnp.sum(z, axis=-1, keepdims=True)
            total = part if total is None else total + part
        mu = total * (1.0 / D_MODEL)
        total = None
        for c in range(nn):
            zc = y_ref[c] - mu
            part = jnp.sum(zc * zc, axis=-1, keepdims=True)
            total = part if total is None else total + part
        rstd = lax.rsqrt(total * (1.0 / D_MODEL) + LN_EPS)
        for c in range(nn):
            o_ref[:, cols[c]] = (y_ref[c] - mu) * rstd * g_ref[:, cols[c]] + b_ref[:, cols[c]]


def _out_ln(att, w_out, layer, x, g, b):
    m = x.shape[0]
    tm = _row_tile(m)
    tn = OUT_TILE_N
    nn = D_MODEL // tn
    return pl.pallas_call(
        functools.partial(_out_ln_kernel, nn=nn, tn=tn),
        grid=(m // tm, nn),
        in_specs=[
            pl.BlockSpec((tm, D_MODEL), lambda i, n: (i, 0), pipeline_mode=pl.Buffered(1)),
            pl.BlockSpec((None, D_MODEL, tn), lambda i, n: (layer, 0, n)),
            pl.BlockSpec((tm, D_MODEL), lambda i, n: (i, 0), pipeline_mode=pl.Buffered(1)),
            pl.BlockSpec((1, D_MODEL), lambda i, n: (0, 0)),
            pl.BlockSpec((1, D_MODEL), lambda i, n: (0, 0)),
        ],
        out_specs=pl.BlockSpec((tm, D_MODEL), lambda i, n: (i, 0)),
        out_shape=jax.ShapeDtypeStruct((m, D_MODEL), F32),
        scratch_shapes=[pltpu.VMEM((nn, tm, tn), F32)],
        compiler_params=_params("parallel", "arbitrary"),
        name="out_ln",
    )(att, w_out, x, g, b)


def _dot_nt(a, b):
    return lax.dot_general(a, b, (((1,), (1,)), ((), ())), preferred_element_type=F32)


def _out_proj_ln(first, att_ref, w_ref, wb_ref, x_ref, g_ref, b_ref, o_ref):
    @pl.when(first)
    def _():
        wb_ref[...] = w_ref[...].astype(BF16)

    y = jnp.dot(att_ref[...], wb_ref[...], preferred_element_type=F32)
    o_ref[...] = _layer_norm_rows(ALPHA * x_ref[...] + y, g_ref[...], b_ref[...])


def _mem_attend_block(qm_ref, mk_ref, mv_ref, att_ref, col0):
    for h in range(MEM_HEADS):
        sl = slice(h * HEAD_DIM, (h + 1) * HEAD_DIM)
        q = qm_ref[:, sl].astype(BF16)
        s = _dot_nt(q, mk_ref[0, :, sl].astype(BF16)) * ATT_SCALE
        p = jnp.exp(s - jnp.max(s, axis=-1, keepdims=True))
        p = p * (1.0 / jnp.sum(p, axis=-1, keepdims=True))
        o = jnp.dot(p.astype(BF16), mv_ref[0, :, sl].astype(BF16), preferred_element_type=F32)
        att_ref[:, col0 + h * HEAD_DIM:col0 + (h + 1) * HEAD_DIM] = o.astype(att_ref.dtype)


def _swa_prompt_kernel(sink_ref, q_ref, kc_ref, kp_ref, vc_ref, vp_ref, qm_ref, mk_ref, mv_ref,
                       w_ref, x_ref, g_ref, b_ref, o_ref, s_ref, p_ref, att_ref, wb_ref, *, nblk):
    n = pl.program_id(1)
    nq = SWA_Q_HEADS
    w2 = 2 * WINDOW

    def prev_cur(cur_ref, prev_ref, i, sl):
        rows = slice(i * WINDOW, (i + 1) * WINDOW)
        before = prev_ref[:, sl] if i == 0 else cur_ref[(i - 1) * WINDOW:i * WINDOW, sl]
        return jnp.concatenate([before, cur_ref[rows, sl]], axis=0).astype(BF16)

    for i in range(nblk):
        rows = slice(i * WINDOW, (i + 1) * WINDOW)
        for h in range(SWA_KV_HEADS):
            sl = slice(h * HEAD_DIM, (h + 1) * HEAD_DIM)
            q3 = jnp.concatenate(
                [q_ref[rows, (h * SWA_GROUP + g) * HEAD_DIM:(h * SWA_GROUP + g + 1) * HEAD_DIM]
                 for g in range(SWA_GROUP)], axis=0).astype(BF16)
            s = _dot_nt(q3, prev_cur(kc_ref, kp_ref, i, sl)) * ATT_SCALE
            s_ref[i, h * SWA_GROUP:(h + 1) * SWA_GROUP] = s.reshape(SWA_GROUP, WINDOW, w2)
    for h in range(MEM_HEADS):
        sl = slice(h * HEAD_DIM, (h + 1) * HEAD_DIM)
        sm = _dot_nt(qm_ref[:, sl].astype(BF16), mk_ref[0, :, sl].astype(BF16)) * ATT_SCALE
        s_ref[:, nq + h] = sm.reshape(nblk, WINDOW, w2)

    qi = lax.broadcasted_iota(jnp.int32, (WINDOW, w2), 0)
    kj = lax.broadcasted_iota(jnp.int32, (WINDOW, w2), 1)
    sink = sink_ref[...]
    for i in range(nblk):
        first_key = jnp.where(n > 0, qi, WINDOW) if i == 0 else qi
        ok = (kj >= first_key) & (kj <= qi + WINDOW)
        s = jnp.where(ok[None], s_ref[i, 0:nq], NEG_INF)
        m = jnp.maximum(jnp.max(s, axis=-1, keepdims=True), sink)
        p = jnp.exp(s - m)
        den = jnp.sum(p, axis=-1, keepdims=True) + jnp.exp(sink - m)
        p_ref[i, 0:nq] = (p * (1.0 / den)).astype(BF16)
    s = s_ref[:, nq:nq + MEM_HEADS]
    p = jnp.exp(s - jnp.max(s, axis=-1, keepdims=True))
    p_ref[:, nq:nq + MEM_HEADS] = (p * (1.0 / jnp.sum(p, axis=-1, keepdims=True))).astype(BF16)

    for i in range(nblk):
        rows = slice(i * WINDOW, (i + 1) * WINDOW)
        for h in range(SWA_KV_HEADS):
            sl = slice(h * HEAD_DIM, (h + 1) * HEAD_DIM)
            p3 = p_ref[i, h * SWA_GROUP:(h + 1) * SWA_GROUP].reshape(SWA_GROUP * WINDOW, w2)
            o = jnp.dot(p3, prev_cur(vc_ref, vp_ref, i, sl), preferred_element_type=F32)
            for g in range(SWA_GROUP):
                hq = h * SWA_GROUP + g
                att_ref[rows, hq * HEAD_DIM:(hq + 1) * HEAD_DIM] = (
                    o[g * WINDOW:(g + 1) * WINDOW].astype(BF16))
    for h in range(MEM_HEADS):
        sl = slice(h * HEAD_DIM, (h + 1) * HEAD_DIM)
        pm = p_ref[:, nq + h].reshape(nblk * WINDOW, w2)
        om = jnp.dot(pm, mv_ref[0, :, sl].astype(BF16), preferred_element_type=F32)
        att_ref[:, SWA_Q + h * HEAD_DIM:SWA_Q + (h + 1) * HEAD_DIM] = om.astype(BF16)

    first = jnp.logical_and(pl.program_id(0) == 0, n == 0)
    _out_proj_ln(first, att_ref, w_ref, wb_ref, x_ref, g_ref, b_ref, o_ref)


def _mixer_tail_specs(layer, row_of, rows):
    mem = (None, 1, MEM_LEN, MEM_DIM)
    vec = pl.BlockSpec((1, D_MODEL), lambda b, n: (0, 0))
    mixer = layer // 2
    return [
        pl.BlockSpec(mem, lambda b, n: (layer, b, 0, 0)),
        pl.BlockSpec(mem, lambda b, n: (layer, b, 0, 1)),
        pl.BlockSpec((None, D_MODEL, D_MODEL), lambda b, n: (mixer, 0, 0), pipeline_mode=pl.Buffered(1)),
        pl.BlockSpec((rows, D_MODEL), lambda b, n: (row_of(b, n), 0)),
        vec, vec,
    ]


def _swa_prompt(qkv, bsz, seq, mem_kv, layer, sinks, w_out, x, g, b):
    nblk = ATT_BLOCKS_PER_STEP
    rows = nblk * WINDOW
    steps = seq // rows
    blocks_per_seq = seq // WINDOW
    kcol = SWA_Q // SWA_KV
    vcol = kcol + 1
    mcol = vcol + 1
    row_of = lambda b, n: b * steps + n
    prev_of = lambda b, n: b * blocks_per_seq + jnp.maximum(n * nblk - 1, 0)
    cur = lambda w, col: pl.BlockSpec((rows, w), lambda b, n: (row_of(b, n), col))
    prev = lambda col: pl.BlockSpec((WINDOW, SWA_KV), lambda b, n: (prev_of(b, n), col))
    nheads = SWA_Q_HEADS + MEM_HEADS
    assert MEM_LEN == 2 * WINDOW
    in_specs = [
        pl.BlockSpec((SWA_Q_HEADS, 1, 1), lambda b, n: (0, 0, 0)),
        cur(SWA_Q, 0), cur(SWA_KV, kcol), prev(kcol), cur(SWA_KV, vcol), prev(vcol), cur(MEM_DIM, mcol),
    ] + _mixer_tail_specs(layer, row_of, rows)
    return pl.pallas_call(
        functools.partial(_swa_prompt_kernel, nblk=nblk),
        grid=(bsz, steps),
        in_specs=in_specs,
        out_specs=pl.BlockSpec((rows, D_MODEL), lambda b, n: (row_of(b, n), 0)),
        out_shape=jax.ShapeDtypeStruct(x.shape, F32),
        scratch_shapes=[pltpu.VMEM((nblk, nheads, WINDOW, 2 * WINDOW), F32),
                        pltpu.VMEM((nblk, nheads, WINDOW, 2 * WINDOW), BF16),
                        pltpu.VMEM((rows, D_MODEL), BF16),
                        pltpu.VMEM((D_MODEL, D_MODEL), BF16)],
        input_output_aliases={len(in_specs) - 3: 0},
        compiler_params=_params("arbitrary", "arbitrary"),
        name="swa_prompt",
    )(sinks.reshape(SWA_Q_HEADS, 1, 1), qkv, qkv, qkv, qkv, qkv, qkv, mem_kv, mem_kv, w_out, x, g, b)


def _head_norm_gate(o, gate):
    mu = jnp.mean(o, axis=-1, keepdims=True)
    oc = o - mu
    var = jnp.mean(oc * oc, axis=-1, keepdims=True)
    return jax.nn.silu(gate) * (oc * lax.rsqrt(var + HEAD_NORM_EPS))


def _ret_prompt_kernel(cdec_ref, decay_ref, qdec_ref, kdec_ref, q_ref, k_ref, v_ref, gate_ref, qm_ref,
                       mk_ref, mv_ref, w_ref, x_ref, g_ref, b_ref, o_ref, s_out_ref,
                       state_ref, att_ref, wb_ref, *, steps, nblk):
    c = pl.program_id(1)

    @pl.when(c == 0)
    def _():
        state_ref[...] = jnp.zeros_like(state_ref)

    for i in range(nblk):
        rows = slice(i * RET_CHUNK, (i + 1) * RET_CHUNK)
        for h in range(RET_HEADS):
            ksl = slice(h * RET_DK, (h + 1) * RET_DK)
            vsl = slice(h * RET_DV, (h + 1) * RET_DV)
            qc = q_ref[rows, ksl]
            kc = k_ref[rows, ksl] * RET_K_SCALE
            vb = v_ref[rows, vsl].astype(BF16)
            st = state_ref[h]
            inner = _dot_nt(qc.astype(BF16), kc.astype(BF16)) * decay_ref[h]
            o = (jnp.dot(inner.astype(BF16), vb, preferred_element_type=F32)
                 + jnp.dot((qc * qdec_ref[h]).astype(BF16), st.astype(BF16),
                           preferred_element_type=F32))
            kd = (kc * kdec_ref[h]).astype(BF16)
            state_ref[h] = cdec_ref[h] * st + lax.dot_general(
                kd, vb, (((0,), (0,)), ((), ())), preferred_element_type=F32)
            att_ref[rows, vsl] = _head_norm_gate(o, gate_ref[rows, vsl]).astype(BF16)
    _mem_attend_block(qm_ref, mk_ref, mv_ref, att_ref, RET_V)

    @pl.when(c == steps - 1)
    def _():
        s_out_ref[0] = state_ref[...]

    first = jnp.logical_and(pl.program_id(0) == 0, c == 0)
    _out_proj_ln(first, att_ref, w_ref, wb_ref, x_ref, g_ref, b_ref, o_ref)


def _ret_prompt(qkvg, bsz, seq, mem_kv, layer, tables, w_out, x, g, b):
    nblk = ATT_BLOCKS_PER_STEP
    rows = nblk * RET_CHUNK
    steps = seq // rows
    decay, qdec, kdec, cdec = tables
    row_of = lambda b, n: b * steps + n
    cur = lambda w, col: pl.BlockSpec((rows, w), lambda b, n: (row_of(b, n), col))
    tab = pl.BlockSpec((RET_HEADS, RET_CHUNK, RET_CHUNK), lambda b, n: (0, 0, 0))
    in_specs = [
        pl.BlockSpec(memory_space=pltpu.SMEM), tab, tab, tab,
        cur(RET_QK, 0), cur(RET_QK, 1), cur(RET_V, 1), cur(RET_V, 2),
        cur(MEM_DIM, (2 * RET_QK + 2 * RET_V) // MEM_DIM),
    ] + _mixer_tail_specs(layer, row_of, rows)
    return pl.pallas_call(
        functools.partial(_ret_prompt_kernel, steps=steps, nblk=nblk),
        grid=(bsz, steps),
        in_specs=in_specs,
        out_specs=[
            pl.BlockSpec((rows, D_MODEL), lambda b, n: (row_of(b, n), 0)),
            pl.BlockSpec((1, RET_HEADS, RET_DK, RET_DV), lambda b, n: (b, 0, 0, 0)),
        ],
        out_shape=[
            jax.ShapeDtypeStruct(x.shape, F32),
            jax.ShapeDtypeStruct((bsz, RET_HEADS, RET_DK, RET_DV), F32),
        ],
        scratch_shapes=[pltpu.VMEM((RET_HEADS, RET_DK, RET_DV), F32),
                        pltpu.VMEM((rows, D_MODEL), BF16),
                        pltpu.VMEM((D_MODEL, D_MODEL), BF16)],
        input_output_aliases={len(in_specs) - 3: 0},
        compiler_params=_params("arbitrary", "arbitrary"),
        name="ret_prompt",
    )(cdec, decay, qdec, kdec, qkvg, qkvg, qkvg, qkvg, qkvg, mem_kv, mem_kv, w_out, x, g, b)


def _mem_attend_row(q, mk_ref, mv_ref):
    s = jnp.sum(mk_ref[0] * q[None], axis=-1, keepdims=True) * ATT_SCALE
    p = jnp.exp(s - jnp.max(s, axis=0, keepdims=True))
    p = p * (1.0 / jnp.sum(p, axis=0, keepdims=True))
    return jnp.sum(p * mv_ref[0], axis=0)


def _swa_step_kernel(sink_ref, rows_ref, kbuf_ref, vbuf_ref, mk_ref, mv_ref, o_ref, nk_ref, nv_ref):
    krow0 = SWA_Q_HEADS
    vrow0 = krow0 + SWA_KV_HEADS
    mrow0 = vrow0 + SWA_KV_HEADS
    wb = kbuf_ref.shape[1]
    kb = kbuf_ref[0]
    vb = vbuf_ref[0]
    k_new = rows_ref[0, krow0:krow0 + SWA_KV_HEADS, :]
    v_new = rows_ref[0, vrow0:vrow0 + SWA_KV_HEADS, :]
    for g in range(SWA_GROUP):
        group_rows = pl.ds(g, SWA_KV_HEADS, stride=SWA_GROUP)
        q = rows_ref[0, group_rows, :]
        s_buf = jnp.sum(kb * q[None], axis=-1, keepdims=True) * ATT_SCALE
        s_new = jnp.sum(k_new * q, axis=-1, keepdims=True) * ATT_SCALE
        sink = sink_ref[g]
        m = jnp.maximum(jnp.maximum(jnp.max(s_buf, axis=0), s_new), sink)
        p_buf = jnp.exp(s_buf - m[None])
        p_new = jnp.exp(s_new - m)
        inv = 1.0 / (jnp.sum(p_buf, axis=0) + p_new + jnp.exp(sink - m))
        o_ref[0, group_rows, :] = jnp.sum((p_buf * inv[None]) * vb, axis=0) + (p_new * inv) * v_new
    nk_ref[0, 0:wb - 1] = kbuf_ref[0, 1:wb]
    nv_ref[0, 0:wb - 1] = vbuf_ref[0, 1:wb]
    nk_ref[0, wb - 1] = k_new
    nv_ref[0, wb - 1] = v_new
    o_ref[0, SWA_Q_HEADS:SWA_Q_HEADS + MEM_HEADS, :] = _mem_attend_row(
        rows_ref[0, mrow0:mrow0 + MEM_HEADS, :], mk_ref, mv_ref)


def _swa_step(rows, cache_k, cache_v, j, mem_k, mem_v, i, sinks):
    bsz = rows.shape[0]
    wb = cache_k.shape[2]
    cache = pl.BlockSpec((None, 1, wb, SWA_KV_HEADS, HEAD_DIM), lambda b: (j, b, 0, 0, 0))
    mem = pl.BlockSpec((None, 1, MEM_LEN, MEM_HEADS, HEAD_DIM), lambda b: (i, b, 0, 0, 0))
    new = pl.BlockSpec((1, wb, SWA_KV_HEADS, HEAD_DIM), lambda b: (b, 0, 0, 0))
    nrows = D_MODEL // HEAD_DIM
    sink_gk = sinks.reshape(SWA_KV_HEADS, SWA_GROUP).T.reshape(SWA_GROUP, SWA_KV_HEADS, 1)
    return pl.pallas_call(
        _swa_step_kernel,
        grid=(bsz,),
        in_specs=[
            pl.BlockSpec(sink_gk.shape, lambda b: (0, 0, 0)),
            pl.BlockSpec((1,) + rows.shape[1:], lambda b: (b, 0, 0)),
            cache, cache, mem, mem,
        ],
        out_specs=[pl.BlockSpec((1, nrows, HEAD_DIM), lambda b: (b, 0, 0)), new, new],
        out_shape=[
            jax.ShapeDtypeStruct((bsz, nrows, HEAD_DIM), F32),
            jax.ShapeDtypeStruct((bsz, wb, SWA_KV_HEADS, HEAD_DIM), F32),
            jax.ShapeDtypeStruct((bsz, wb, SWA_KV_HEADS, HEAD_DIM), F32),
        ],
        compiler_params=_params("parallel"),
        name="swa_step",
    )(sink_gk, rows, cache_k, cache_v, mem_k, mem_v)


def _ret_step_kernel(dec_ref, rows_ref, cols_ref, s_ref, mk_ref, mv_ref, o_ref, s_out_ref):
    krow0 = RET_HEADS
    vrow0 = 2 * RET_HEADS
    grow0 = vrow0 + 2 * RET_HEADS
    mrow0 = grow0 + 2 * RET_HEADS
    for h in range(RET_HEADS):
        q_row = rows_ref[0, h:h + 1, :]
        k_row = rows_ref[0, krow0 + h:krow0 + h + 1, :] * RET_K_SCALE
        q_col = cols_ref[0, :, h:h + 1]
        k_col = cols_ref[0, :, krow0 + h:krow0 + h + 1] * RET_K_SCALE
        inner = jnp.sum(q_row * k_row, axis=1, keepdims=True) * dec_ref[0, h]
        qd = q_col * dec_ref[1, h]
        kd = k_col * dec_ref[2, h]
        halves = []
        for t in range(2):
            lsl = slice(t * HEAD_DIM, (t + 1) * HEAD_DIM)
            v = rows_ref[0, vrow0 + 2 * h + t:vrow0 + 2 * h + t + 1, :]
            st = s_ref[0, h, :, lsl]
            halves.append(inner * v + jnp.sum(qd * st, axis=0, keepdims=True))
            s_out_ref[0, h, :, lsl] = dec_ref[3, h] * st + kd * v
        mu = (jnp.sum(halves[0], axis=1, keepdims=True)
              + jnp.sum(halves[1], axis=1, keepdims=True)) * (1.0 / RET_DV)
        cen = [o - mu for o in halves]
        var = (jnp.sum(cen[0] * cen[0], axis=1, keepdims=True)
               + jnp.sum(cen[1] * cen[1], axis=1, keepdims=True)) * (1.0 / RET_DV)
        rstd = lax.rsqrt(var + HEAD_NORM_EPS)
        for t in range(2):
            gate = rows_ref[0, grow0 + 2 * h + t:grow0 + 2 * h + t + 1, :]
            o_ref[0, 2 * h + t:2 * h + t + 1, :] = jax.nn.silu(gate) * (cen[t] * rstd)
    o_ref[0, 2 * RET_HEADS:2 * RET_HEADS + MEM_HEADS, :] = _mem_attend_row(
        rows_ref[0, mrow0:mrow0 + MEM_HEADS, :], mk_ref, mv_ref)


def _ret_step(rows, cols, state, j, mem_k, mem_v, i, dec):
    bsz = rows.shape[0]
    st_in = pl.BlockSpec((None, 1, RET_HEADS, RET_DK, RET_DV), lambda b: (j, b, 0, 0, 0))
    st_out = pl.BlockSpec((1, RET_HEADS, RET_DK, RET_DV), lambda b: (b, 0, 0, 0))
    mem = pl.BlockSpec((None, 1, MEM_LEN, MEM_HEADS, HEAD_DIM), lambda b: (i, b, 0, 0, 0))
    nrows = D_MODEL // HEAD_DIM
    return pl.pallas_call(
        _ret_step_kernel,
        grid=(bsz,),
        in_specs=[
            pl.BlockSpec(memory_space=pltpu.SMEM),
            pl.BlockSpec((1,) + rows.shape[1:], lambda b: (b, 0, 0)),
            pl.BlockSpec((1,) + cols.shape[1:], lambda b: (b, 0, 0)),
            st_in, mem, mem,
        ],
        out_specs=[pl.BlockSpec((1, nrows, HEAD_DIM), lambda b: (b, 0, 0)), st_out],
        out_shape=[
            jax.ShapeDtypeStruct((bsz, nrows, HEAD_DIM), F32),
            jax.ShapeDtypeStruct((bsz, RET_HEADS, RET_DK, RET_DV), F32),
        ],
        compiler_params=_params("parallel"),
        name="ret_step",
    )(dec, rows, cols, state, mem_k, mem_v)


def _rope_tables(pos):
    half = ROPE_DIM // 2
    inv = ROPE_THETA ** (-jnp.arange(half, dtype=F32) / half)
    ang = pos.astype(F32)[:, None] * inv[None, :]
    cos, sin = jnp.cos(ang), jnp.sin(ang)
    n = pos.shape[0]
    rest = HEAD_DIM - ROPE_DIM
    c = jnp.concatenate([cos, cos, jnp.ones((n, rest), F32)], axis=-1)
    a = jnp.concatenate([-sin, jnp.zeros((n, HEAD_DIM - half), F32)], axis=-1)
    b = jnp.concatenate([jnp.zeros((n, half), F32), sin, jnp.zeros((n, rest), F32)], axis=-1)
    return (c, a, b), (HEAD_DIM - half, half)


def _ret_rot_tables(pos):
    half = RET_DK // 2
    angle = RET_ROT_BASE ** (-jnp.linspace(0.0, 1.0, half, dtype=F32))
    ang = pos.astype(F32)[:, None] * angle[None, :]
    cos, sin = jnp.cos(ang), jnp.sin(ang)
    n = pos.shape[0]
    zero = jnp.zeros_like(sin)
    c = jnp.stack([cos, cos], axis=-1).reshape(n, RET_DK)
    a = jnp.stack([-sin, zero], axis=-1).reshape(n, RET_DK)
    b = jnp.stack([zero, sin], axis=-1).reshape(n, RET_DK)
    return (c, a, b), (RET_DK - 1, 1)


def _ret_decay(chunk):
    log_g = jnp.log1p(-jnp.exp2(-5.0 - jnp.arange(RET_HEADS, dtype=F32)))
    n = jnp.arange(chunk, dtype=F32)
    rel = n[:, None] - n[None, :]
    decay = jnp.where(rel >= 0, jnp.exp(jnp.maximum(rel, 0.0) * log_g[:, None, None]), 0.0)
    q_dec = jnp.exp((n + 1.0) * log_g[:, None])
    k_dec = jnp.exp((chunk - 1.0 - n) * log_g[:, None])
    c_dec = jnp.exp(chunk * log_g)
    return decay, q_dec, k_dec, c_dec


def _trunk(x3, pos, swa_k, swa_v, ret_s, mem_k, mem_v, ln_g, ln_b, ffn_w_gu, ffn_w_down,
           swa_w_in, swa_w_out, swa_sinks, ret_w_in, ret_w_out):
    bsz, seq, _ = x3.shape
    prompt = ret_s is None
    m = bsz * seq
    x = x3.reshape(m, D_MODEL)
    rows_per_tile = _row_tile(m)
    tab_rows = seq if prompt else rows_per_tile
    tab_pos = pos if prompt else jnp.broadcast_to(pos, (tab_rows,))
    rope_tabs, rope_shifts = _rope_tables(tab_pos)
    rot_tabs, rot_shifts = _ret_rot_tables(tab_pos)
    if prompt:
        decay, q_dec, k_dec, c_dec = _ret_decay(RET_CHUNK)
        ret_tabs = (decay,
                    jnp.broadcast_to(q_dec[:, :, None], decay.shape),
                    jnp.broadcast_to(k_dec[:, :, None], decay.shape),
                    c_dec)
    else:
        decay, q_dec, k_dec, c_dec = _ret_decay(seq)
        step_dec = jnp.stack([decay[:, 0, 0], q_dec[:, 0], k_dec[:, 0], c_dec])

    def ln(i, s):
        return ln_g[i, s].reshape(1, D_MODEL), ln_b[i, s].reshape(1, D_MODEL)

    new_k, new_v, new_s = [], [], []
    for i in range(DEPTH):
        j = i // 2
        x = _ffn(x, ffn_w_gu, ffn_w_down, i, 0, *ln(i, 0))
        if i % 2 == 0:
            qkv = _proj(x, swa_w_in, j, n_rot=(SWA_Q + SWA_KV) // PROJ_TILE_N, tables=rope_tabs,
                        shifts=rope_shifts, seq=tab_rows)
            if prompt:
                qkv3 = qkv.reshape(bsz, seq, SWA_IN_WIDTH)
                x = _swa_prompt(qkv, bsz, seq, mem_k, i, swa_sinks[j], swa_w_out, x, *ln(i, 1))
                wlen = min(WINDOW, seq)
                new_k.append(qkv3[:, seq - wlen:, SWA_Q:SWA_Q + SWA_KV]
                             .reshape(bsz, wlen, SWA_KV_HEADS, HEAD_DIM))
                new_v.append(qkv3[:, seq - wlen:, SWA_Q + SWA_KV:SWA_Q + 2 * SWA_KV]
                             .reshape(bsz, wlen, SWA_KV_HEADS, HEAD_DIM))
            else:
                rows = qkv.reshape(bsz, SWA_IN_WIDTH // HEAD_DIM, HEAD_DIM)
                att, nk, nv = _swa_step(rows, swa_k, swa_v, j, mem_k, mem_v, i, swa_sinks[j])
                new_k.append(nk)
                new_v.append(nv)
                x = _out_ln(att.reshape(m, D_MODEL), swa_w_out, j, x, *ln(i, 1))
        else:
            qkvg = _proj(x, ret_w_in, j, n_rot=(2 * RET_QK) // PROJ_TILE_N, tables=rot_tabs,
                         shifts=rot_shifts, seq=tab_rows)
            if prompt:
                x, ns = _ret_prompt(qkvg, bsz, seq, mem_k, i, ret_tabs, ret_w_out, x, *ln(i, 1))
            else:
                rows = qkvg.reshape(bsz, RET_IN_WIDTH // HEAD_DIM, HEAD_DIM)
                cols = jnp.swapaxes(rows, 1, 2)
                att, ns = _ret_step(rows, cols, ret_s, j, mem_k, mem_v, i, step_dec)
                x = _out_ln(att.reshape(m, D_MODEL), ret_w_out, j, x, *ln(i, 1))
            new_s.append(ns)
        x = _ffn(x, ffn_w_gu, ffn_w_down, i, 1, *ln(i, 2))
    return x.reshape(bsz, seq, D_MODEL), jnp.stack(new_k), jnp.stack(new_v), jnp.stack(new_s)


def kernel(x_prompt, x_sample, cache_swa_k, cache_swa_v, state_ret, cache_mem_k, cache_mem_v,
           mem_prompt, ln_g, ln_b, ffn_w_gu, ffn_w_down, w_mem_kv, swa_w_in, swa_w_out,
           swa_sinks, ret_w_in, ret_w_out):
    bp = mem_prompt.shape[0]
    mem2 = mem_prompt.reshape(bp * MEM_LEN, D_MODEL)
    mem_kv = jnp.stack([_proj(mem2, w_mem_kv, l) for l in range(DEPTH)])
    mem_kv = mem_kv.reshape(DEPTH, bp, MEM_LEN, 2 * MEM_DIM)
    mem_k_prompt = mem_kv[..., :MEM_DIM].reshape(DEPTH, bp, MEM_LEN, MEM_HEADS, HEAD_DIM)
    mem_v_prompt = mem_kv[..., MEM_DIM:].reshape(DEPTH, bp, MEM_LEN, MEM_HEADS, HEAD_DIM)

    pos_prompt = jnp.arange(x_prompt.shape[1], dtype=jnp.int32)
    pos_sample = PAST_LEN + jnp.arange(x_sample.shape[1], dtype=jnp.int32)
    weights = (ln_g, ln_b, ffn_w_gu, ffn_w_down, swa_w_in, swa_w_out, swa_sinks, ret_w_in, ret_w_out)

    y_prompt, swa_k_prompt, swa_v_prompt, ret_state_prompt = _trunk(
        x_prompt, pos_prompt, None, None, None, mem_kv, None, *weights)
    y_sample, swa_k_sample, swa_v_sample, ret_state_sample = _trunk(
        x_sample, pos_sample, cache_swa_k, cache_swa_v, state_ret, cache_mem_k, cache_mem_v, *weights)
    return (y_prompt, y_sample, swa_k_prompt, swa_v_prompt, swa_k_sample, swa_v_sample,
            ret_state_prompt, ret_state_sample, mem_k_prompt, mem_v_prompt)
```

```python
import functools

import jax
import jax.numpy as jnp
from jax import lax
from jax.experimental import pallas as pl
from jax.experimental.pallas import tpu as pltpu

F32 = jnp.float32
BF16 = jnp.bfloat16

D_MODEL = 2048
DEPTH = 4
PAST_LEN = 16384
HEAD_DIM = 128
MEM_LEN = 256
MEM_HEADS = 4
MEM_DIM = MEM_HEADS * HEAD_DIM
SELF_WIDTH = D_MODEL - MEM_DIM
SWA_Q_HEADS = SELF_WIDTH // HEAD_DIM
SWA_KV_HEADS = SWA_Q_HEADS // 3
SWA_GROUP = SWA_Q_HEADS // SWA_KV_HEADS
WINDOW = 128
ROPE_THETA = 500000.0
ROPE_DIM = HEAD_DIM // 4
RET_DK = 128
RET_DV = 2 * RET_DK
RET_HEADS = SELF_WIDTH // RET_DV
RET_CHUNK = 128
RET_ROT_BASE = 10000.0
D_FF = ((8 * D_MODEL // 3 + 255) // 256) * 256
LN_EPS = 1e-5
HEAD_NORM_EPS = 1e-6
ALPHA = (2.0 * DEPTH) ** 0.25
NEG_INF = -1e30
ATT_SCALE = HEAD_DIM ** -0.5
RET_K_SCALE = RET_DK ** -0.5

SWA_Q = SWA_Q_HEADS * HEAD_DIM
SWA_KV = SWA_KV_HEADS * HEAD_DIM
SWA_IN_WIDTH = SWA_Q + 2 * SWA_KV + MEM_DIM
RET_QK = RET_HEADS * RET_DK
RET_V = RET_HEADS * RET_DV
RET_IN_WIDTH = 2 * RET_QK + 2 * RET_V + MEM_DIM

VMEM_LIMIT_BYTES = 58 * 1024 * 1024
LANES = 128
FFN_TILE_K = 256
ROW_TILE = 1024
ATT_BLOCKS_PER_STEP = 2
PROJ_TILE_N = 512
OUT_TILE_N = 512
MAX_ROW_CHUNK = 256
BF16_SUBLANES = 16


def _merged_rows(prompt_rows, sample_rows):
    tiles = max(prompt_rows // ROW_TILE, 1)
    per_tile = -(-(prompt_rows + sample_rows) // tiles)
    tm = -(-per_tile // BF16_SUBLANES) * BF16_SUBLANES
    return tm, tiles * tm


def _row_chunk(tm):
    for c in range(MAX_ROW_CHUNK, BF16_SUBLANES - 1, -BF16_SUBLANES):
        if tm % c == 0:
            return c
    return tm


def _params(*sem):
    return pltpu.CompilerParams(dimension_semantics=sem, vmem_limit_bytes=VMEM_LIMIT_BYTES)


def _layer_norm_rows(z, g, b):
    mu = jnp.mean(z, axis=-1, keepdims=True)
    zc = z - mu
    var = jnp.mean(zc * zc, axis=-1, keepdims=True)
    return zc * lax.rsqrt(var + LN_EPS) * g + b


def _ffn_kernel(x_ref, wg_ref, wu_ref, wd_ref, g_ref, b_ref, o_ref, xb_ref, wgu_ref, *, nk, tk, chunk):
    k = pl.program_id(1)
    tm = o_ref.shape[0]
    chunks = [slice(r, r + chunk) for r in range(0, tm, chunk)]

    def cast_weights():
        wgu_ref[:, :tk] = wg_ref[...].astype(BF16)
        wgu_ref[:, tk:] = wu_ref[...].astype(BF16)
        return wd_ref[...].astype(BF16)

    def down(xb, wd):
        gu = jnp.dot(xb, wgu_ref[...], preferred_element_type=F32)
        h = (jax.nn.silu(gu[:, :tk]) * gu[:, tk:]).astype(BF16)
        return jnp.dot(h, wd, preferred_element_type=F32)

    @pl.when(k == 0)
    def _():
        wd = cast_weights()
        for rows in chunks:
            xb = x_ref[rows, :].astype(BF16)
            xb_ref[rows, :] = xb
            o_ref[rows, :] = down(xb, wd)

    @pl.when(jnp.logical_and(k > 0, k < nk - 1))
    def _():
        wd = cast_weights()
        o_ref[...] += down(xb_ref[...], wd)

    @pl.when(k == nk - 1)
    def _():
        wd = cast_weights()
        for rows in chunks:
            acc = o_ref[rows, :] + down(xb_ref[rows, :], wd)
            z = ALPHA * x_ref[rows, :] + 0.5 * acc
            o_ref[rows, :] = _layer_norm_rows(z, g_ref[...], b_ref[...])


def _ffn(x, w_gu, w_down, layer, slot, g, b, tm):
    m = x.shape[0]
    tk = FFN_TILE_K
    nk = D_FF // tk
    assert nk >= 2
    return pl.pallas_call(
        functools.partial(_ffn_kernel, nk=nk, tk=tk, chunk=_row_chunk(tm)),
        grid=(m // tm, nk),
        in_specs=[
            pl.BlockSpec((tm, D_MODEL), lambda i, k: (i, 0), pipeline_mode=pl.Buffered(1)),
            pl.BlockSpec((None, None, D_MODEL, tk), lambda i, k: (layer, slot, 0, k)),
            pl.BlockSpec((None, None, D_MODEL, tk), lambda i, k: (layer, slot, 0, nk + k)),
            pl.BlockSpec((None, None, tk, D_MODEL), lambda i, k: (layer, slot, k, 0)),
            pl.BlockSpec((1, D_MODEL), lambda i, k: (0, 0)),
            pl.BlockSpec((1, D_MODEL), lambda i, k: (0, 0)),
        ],
        out_specs=pl.BlockSpec((tm, D_MODEL), lambda i, k: (i, 0)),
        out_shape=jax.ShapeDtypeStruct((m, D_MODEL), F32),
        scratch_shapes=[pltpu.VMEM((tm, D_MODEL), BF16), pltpu.VMEM((D_MODEL, 2 * tk), BF16)],
        compiler_params=_params("parallel", "arbitrary"),
        name="ffn_ln",
    )(x, w_gu, w_gu, w_down, g, b)


def _proj_kernel(x_ref, w_ref, *rest, n_rot, shifts):
    if n_rot:
        c_ref, a_ref, b_ref, o_ref, xb_ref = rest
    else:
        o_ref, xb_ref = rest
    j = pl.program_id(1)

    @pl.when(j == 0)
    def _():
        xb_ref[...] = x_ref[...].astype(BF16)

    if not n_rot:
        o_ref[...] = jnp.dot(xb_ref[...], w_ref[...].astype(BF16), preferred_element_type=F32)
        return

    tm, tn = o_ref.shape
    rc = _row_chunk(tm)

    @pl.when(j < n_rot)
    def _():
        wb = w_ref[...].astype(BF16)
        for r0 in range(0, tm, rc):
            rows = slice(r0, r0 + rc)
            y = jnp.dot(xb_ref[rows, :], wb, preferred_element_type=F32)
            c, a, b = c_ref[rows, :], a_ref[rows, :], b_ref[rows, :]
            for h in range(tn // LANES):
                yh = y[:, h * LANES:(h + 1) * LANES]
                o_ref[rows, h * LANES:(h + 1) * LANES] = (
                    yh * c + pltpu.roll(yh, shifts[0], 1) * a + pltpu.roll(yh, shifts[1], 1) * b)

    @pl.when(j >= n_rot)
    def _():
        o_ref[...] = jnp.dot(xb_ref[...], w_ref[...].astype(BF16), preferred_element_type=F32)


def _proj(x, w, layer, tm, n_rot=0, tables=None, shifts=None):
    m = x.shape[0]
    n = w.shape[-1]
    tn = PROJ_TILE_N
    in_specs = [
        pl.BlockSpec((tm, D_MODEL), lambda i, j: (i, 0)),
        pl.BlockSpec((None, D_MODEL, tn), lambda i, j: (layer, 0, j)),
    ]
    args = [x, w]
    if n_rot:
        tab = pl.BlockSpec((tm, LANES), lambda i, j: (i, 0))
        in_specs += [tab, tab, tab]
        args += list(tables)
    return pl.pallas_call(
        functools.partial(_proj_kernel, n_rot=n_rot, shifts=shifts),
        grid=(m // tm, n // tn),
        in_specs=in_specs,
        out_specs=pl.BlockSpec((tm, tn), lambda i, j: (i, j)),
        out_shape=jax.ShapeDtypeStruct((m, n), F32),
        scratch_shapes=[pltpu.VMEM((tm, D_MODEL), BF16)],
        compiler_params=_params("parallel", "arbitrary"),
        name="proj",
    )(*args)


def _out_ln_kernel(att_ref, w_ref, x_ref, g_ref, b_ref, o_ref, y_ref, *, nn, tn):
    n = pl.program_id(1)
    y_ref[n] = jnp.dot(att_ref[...].astype(BF16), w_ref[...].astype(BF16),
                       preferred_element_type=F32)

    @pl.when(n == nn - 1)
    def _():
        cols = [slice(c * tn, (c + 1) * tn) for c in range(nn)]
        total = None
        for c in range(nn):
            z = ALPHA * x_ref[:, cols[c]] + y_ref[c]
            y_ref[c] = z
            part = jnp.sum(z, axis=-1, keepdims=True)
            total = part if total is None else total + part
        mu = total * (1.0 / D_MODEL)
        total = None
        for c in range(nn):
            zc = y_ref[c] - mu
            part = jnp.sum(zc * zc, axis=-1, keepdims=True)
            total = part if total is None else total + part
        rstd = lax.rsqrt(total * (1.0 / D_MODEL) + LN_EPS)
        for c in range(nn):
            o_ref[:, cols[c]] = (y_ref[c] - mu) * rstd * g_ref[:, cols[c]] + b_ref[:, cols[c]]


def _out_ln(att, w_out, layer, x, g, b, row0):
    tm = att.shape[0]
    assert row0 % tm == 0
    tn = OUT_TILE_N
    nn = D_MODEL // tn
    rows = pl.BlockSpec((tm, D_MODEL), lambda i, n: (row0 // tm, 0))
    return pl.pallas_call(
        functools.partial(_out_ln_kernel, nn=nn, tn=tn),
        grid=(1, nn),
        in_specs=[
            pl.BlockSpec((tm, D_MODEL), lambda i, n: (0, 0)),
            pl.BlockSpec((None, D_MODEL, tn), lambda i, n: (layer, 0, n)),
            rows,
            pl.BlockSpec((1, D_MODEL), lambda i, n: (0, 0)),
            pl.BlockSpec((1, D_MODEL), lambda i, n: (0, 0)),
        ],
        out_specs=rows,
        out_shape=jax.ShapeDtypeStruct(x.shape, F32),
        scratch_shapes=[pltpu.VMEM((nn, tm, tn), F32)],
        input_output_aliases={2: 0},
        compiler_params=_params("arbitrary", "arbitrary"),
        name="out_ln",
    )(att, w_out, x, g, b)


def _dot_nt(a, b):
    return lax.dot_general(a, b, (((1,), (1,)), ((), ())), preferred_element_type=F32)


def _out_proj_ln(first, att_ref, w_ref, wb_ref, x_ref, g_ref, b_ref, o_ref):
    @pl.when(first)
    def _():
        wb_ref[...] = w_ref[...].astype(BF16)

    y = jnp.dot(att_ref[...], wb_ref[...], preferred_element_type=F32)
    o_ref[...] = _layer_norm_rows(ALPHA * x_ref[...] + y, g_ref[...], b_ref[...])


def _mem_attend_block(qm_ref, mk_ref, mv_ref, att_ref, col0):
    for h in range(MEM_HEADS):
        sl = slice(h * HEAD_DIM, (h + 1) * HEAD_DIM)
        q = qm_ref[:, sl].astype(BF16)
        s = _dot_nt(q, mk_ref[0, :, sl].astype(BF16)) * ATT_SCALE
        p = jnp.exp(s - jnp.max(s, axis=-1, keepdims=True))
        p = p * (1.0 / jnp.sum(p, axis=-1, keepdims=True))
        o = jnp.dot(p.astype(BF16), mv_ref[0, :, sl].astype(BF16), preferred_element_type=F32)
        att_ref[:, col0 + h * HEAD_DIM:col0 + (h + 1) * HEAD_DIM] = o.astype(att_ref.dtype)


def _swa_prompt_kernel(sink_ref, q_ref, kc_ref, kp_ref, vc_ref, vp_ref, qm_ref, mk_ref, mv_ref,
                       w_ref, x_ref, g_ref, b_ref, o_ref, s_ref, p_ref, att_ref, wb_ref, *, nblk):
    n = pl.program_id(1)
    nq = SWA_Q_HEADS
    w2 = 2 * WINDOW

    def prev_cur(cur_ref, prev_ref, i, sl):
        rows = slice(i * WINDOW, (i + 1) * WINDOW)
        before = prev_ref[:, sl] if i == 0 else cur_ref[(i - 1) * WINDOW:i * WINDOW, sl]
        return jnp.concatenate([before, cur_ref[rows, sl]], axis=0).astype(BF16)

    for i in range(nblk):
        rows = slice(i * WINDOW, (i + 1) * WINDOW)
        for h in range(SWA_KV_HEADS):
            sl = slice(h * HEAD_DIM, (h + 1) * HEAD_DIM)
            q3 = jnp.concatenate(
                [q_ref[rows, (h * SWA_GROUP + g) * HEAD_DIM:(h * SWA_GROUP + g + 1) * HEAD_DIM]
                 for g in range(SWA_GROUP)], axis=0).astype(BF16)
            s = _dot_nt(q3, prev_cur(kc_ref, kp_ref, i, sl)) * ATT_SCALE
            s_ref[i, h * SWA_GROUP:(h + 1) * SWA_GROUP] = s.reshape(SWA_GROUP, WINDOW, w2)
    for h in range(MEM_HEADS):
        sl = slice(h * HEAD_DIM, (h + 1) * HEAD_DIM)
        sm = _dot_nt(qm_ref[:, sl].astype(BF16), mk_ref[0, :, sl].astype(BF16)) * ATT_SCALE
        s_ref[:, nq + h] = sm.reshape(nblk, WINDOW, w2)

    qi = lax.broadcasted_iota(jnp.int32, (WINDOW, w2), 0)
    kj = lax.broadcasted_iota(jnp.int32, (WINDOW, w2), 1)
    sink = sink_ref[...]
    for i in range(nblk):
        first_key = jnp.where(n > 0, qi, WINDOW) if i == 0 else qi
        ok = (kj >= first_key) & (kj <= qi + WINDOW)
        s = jnp.where(ok[None], s_ref[i, 0:nq], NEG_INF)
        m = jnp.maximum(jnp.max(s, axis=-1, keepdims=True), sink)
        p = jnp.exp(s - m)
        den = jnp.sum(p, axis=-1, keepdims=True) + jnp.exp(sink - m)
        p_ref[i, 0:nq] = (p * (1.0 / den)).astype(BF16)
    s = s_ref[:, nq:nq + MEM_HEADS]
    p = jnp.exp(s - jnp.max(s, axis=-1, keepdims=True))
    p_ref[:, nq:nq + MEM_HEADS] = (p * (1.0 / jnp.sum(p, axis=-1, keepdims=True))).astype(BF16)

    for i in range(nblk):
        rows = slice(i * WINDOW, (i + 1) * WINDOW)
        for h in range(SWA_KV_HEADS):
            sl = slice(h * HEAD_DIM, (h + 1) * HEAD_DIM)
            p3 = p_ref[i, h * SWA_GROUP:(h + 1) * SWA_GROUP].reshape(SWA_GROUP * WINDOW, w2)
            o = jnp.dot(p3, prev_cur(vc_ref, vp_ref, i, sl), preferred_element_type=F32)
            for g in range(SWA_GROUP):
                hq = h * SWA_GROUP + g
                att_ref[rows, hq * HEAD_DIM:(hq + 1) * HEAD_DIM] = (
                    o[g * WINDOW:(g + 1) * WINDOW].astype(BF16))
    for h in range(MEM_HEADS):
        sl = slice(h * HEAD_DIM, (h + 1) * HEAD_DIM)
        pm = p_ref[:, nq + h].reshape(nblk * WINDOW, w2)
        om = jnp.dot(pm, mv_ref[0, :, sl].astype(BF16), preferred_element_type=F32)
        att_ref[:, SWA_Q + h * HEAD_DIM:SWA_Q + (h + 1) * HEAD_DIM] = om.astype(BF16)

    first = jnp.logical_and(pl.program_id(0) == 0, n == 0)
    _out_proj_ln(first, att_ref, w_ref, wb_ref, x_ref, g_ref, b_ref, o_ref)


def _mixer_tail_specs(layer, row_of, rows):
    mem = (None, 1, MEM_LEN, MEM_DIM)
    vec = pl.BlockSpec((1, D_MODEL), lambda b, n: (0, 0))
    mixer = layer // 2
    return [
        pl.BlockSpec(mem, lambda b, n: (layer, b, 0, 0)),
        pl.BlockSpec(mem, lambda b, n: (layer, b, 0, 1)),
        pl.BlockSpec((None, D_MODEL, D_MODEL), lambda b, n: (mixer, 0, 0), pipeline_mode=pl.Buffered(1)),
        pl.BlockSpec((rows, D_MODEL), lambda b, n: (row_of(b, n), 0)),
        vec, vec,
    ]


def _swa_prompt(qkv, bsz, seq, mem_kv, layer, sinks, w_out, x, g, b):
    nblk = ATT_BLOCKS_PER_STEP
    rows = nblk * WINDOW
    steps = seq // rows
    blocks_per_seq = seq // WINDOW
    kcol = SWA_Q // SWA_KV
    vcol = kcol + 1
    mcol = vcol + 1
    row_of = lambda b, n: b * steps + n
    prev_of = lambda b, n: b * blocks_per_seq + jnp.maximum(n * nblk - 1, 0)
    cur = lambda w, col: pl.BlockSpec((rows, w), lambda b, n: (row_of(b, n), col))
    prev = lambda col: pl.BlockSpec((WINDOW, SWA_KV), lambda b, n: (prev_of(b, n), col))
    nheads = SWA_Q_HEADS + MEM_HEADS
    assert MEM_LEN == 2 * WINDOW
    in_specs = [
        pl.BlockSpec((SWA_Q_HEADS, 1, 1), lambda b, n: (0, 0, 0)),
        cur(SWA_Q, 0), cur(SWA_KV, kcol), prev(kcol), cur(SWA_KV, vcol), prev(vcol), cur(MEM_DIM, mcol),
    ] + _mixer_tail_specs(layer, row_of, rows)
    return pl.pallas_call(
        functools.partial(_swa_prompt_kernel, nblk=nblk),
        grid=(bsz, steps),
        in_specs=in_specs,
        out_specs=pl.BlockSpec((rows, D_MODEL), lambda b, n: (row_of(b, n), 0)),
        out_shape=jax.ShapeDtypeStruct(x.shape, F32),
        scratch_shapes=[pltpu.VMEM((nblk, nheads, WINDOW, 2 * WINDOW), F32),
                        pltpu.VMEM((nblk, nheads, WINDOW, 2 * WINDOW), BF16),
                        pltpu.VMEM((rows, D_MODEL), BF16),
                        pltpu.VMEM((D_MODEL, D_MODEL), BF16)],
        input_output_aliases={len(in_specs) - 3: 0},
        compiler_params=_params("arbitrary", "arbitrary"),
        name="swa_prompt",
    )(sinks.reshape(SWA_Q_HEADS, 1, 1), qkv, qkv, qkv, qkv, qkv, qkv, mem_kv, mem_kv, w_out, x, g, b)


def _head_norm_gate(o, gate):
    mu = jnp.mean(o, axis=-1, keepdims=True)
    oc = o - mu
    var = jnp.mean(oc * oc, axis=-1, keepdims=True)
    return jax.nn.silu(gate) * (oc * lax.rsqrt(var + HEAD_NORM_EPS))


def _ret_prompt_kernel(cdec_ref, decay_ref, qdec_ref, kdec_ref, q_ref, k_ref, v_ref, gate_ref, qm_ref,
                       mk_ref, mv_ref, w_ref, x_ref, g_ref, b_ref, o_ref, s_out_ref,
                       state_ref, att_ref, wb_ref, *, steps, nblk):
    c = pl.program_id(1)

    @pl.when(c == 0)
    def _():
        state_ref[...] = jnp.zeros_like(state_ref)

    for i in range(nblk):
        rows = slice(i * RET_CHUNK, (i + 1) * RET_CHUNK)
        for h in range(RET_HEADS):
            ksl = slice(h * RET_DK, (h + 1) * RET_DK)
            vsl = slice(h * RET_DV, (h + 1) * RET_DV)
            qc = q_ref[rows, ksl]
            kc = k_ref[rows, ksl] * RET_K_SCALE
            vb = v_ref[rows, vsl].astype(BF16)
            st = state_ref[h]
            inner = _dot_nt(qc.astype(BF16), kc.astype(BF16)) * decay_ref[h]
            o = (jnp.dot(inner.astype(BF16), vb, preferred_element_type=F32)
                 + jnp.dot((qc * qdec_ref[h]).astype(BF16), st.astype(BF16),
                           preferred_element_type=F32))
            kd = (kc * kdec_ref[h]).astype(BF16)
            state_ref[h] = cdec_ref[h] * st + lax.dot_general(
                kd, vb, (((0,), (0,)), ((), ())), preferred_element_type=F32)
            att_ref[rows, vsl] = _head_norm_gate(o, gate_ref[rows, vsl]).astype(BF16)
    _mem_attend_block(qm_ref, mk_ref, mv_ref, att_ref, RET_V)

    @pl.when(c == steps - 1)
    def _():
        s_out_ref[0] = state_ref[...]

    first = jnp.logical_and(pl.program_id(0) == 0, c == 0)
    _out_proj_ln(first, att_ref, w_ref, wb_ref, x_ref, g_ref, b_ref, o_ref)


def _drop_carried(kernel, first, count):
    def body(*refs):
        return kernel(*refs[:first], *refs[first + count:])
    return body


def _carry(prev_outputs):
    prev_outputs = [] if prev_outputs is None else list(prev_outputs)
    return [pl.BlockSpec(memory_space=pl.ANY)] * len(prev_outputs), prev_outputs


def _ret_prompt(qkvg, bsz, seq, mem_kv, layer, tables, w_out, x, g, b, carried):
    nblk = ATT_BLOCKS_PER_STEP
    rows = nblk * RET_CHUNK
    steps = seq // rows
    mixer = layer // 2
    decay, qdec, kdec, cdec = tables
    row_of = lambda b, n: b * steps + n
    cur = lambda w, col: pl.BlockSpec((rows, w), lambda b, n: (row_of(b, n), col))
    tab = pl.BlockSpec((RET_HEADS, RET_CHUNK, RET_CHUNK), lambda b, n: (0, 0, 0))
    in_specs = [
        pl.BlockSpec(memory_space=pltpu.SMEM), tab, tab, tab,
        cur(RET_QK, 0), cur(RET_QK, 1), cur(RET_V, 1), cur(RET_V, 2),
        cur(MEM_DIM, (2 * RET_QK + 2 * RET_V) // MEM_DIM),
    ] + _mixer_tail_specs(layer, row_of, rows)
    n_in = len(in_specs)
    carry_specs, carry_args = _carry(carried)
    aliases = {n_in - 3: 0}
    aliases.update({n_in + c: 1 + c for c in range(len(carry_args))})
    return pl.pallas_call(
        _drop_carried(functools.partial(_ret_prompt_kernel, steps=steps, nblk=nblk), n_in, len(carry_args)),
        grid=(bsz, steps),
        in_specs=in_specs + carry_specs,
        out_specs=[
            pl.BlockSpec((rows, D_MODEL), lambda b, n: (row_of(b, n), 0)),
            pl.BlockSpec((None, 1, RET_HEADS, RET_DK, RET_DV), lambda b, n: (mixer, b, 0, 0, 0)),
        ],
        out_shape=[
            jax.ShapeDtypeStruct(x.shape, F32),
            jax.ShapeDtypeStruct((DEPTH // 2, bsz, RET_HEADS, RET_DK, RET_DV), F32),
        ],
        scratch_shapes=[pltpu.VMEM((RET_HEADS, RET_DK, RET_DV), F32),
                        pltpu.VMEM((rows, D_MODEL), BF16),
                        pltpu.VMEM((D_MODEL, D_MODEL), BF16)],
        input_output_aliases=aliases,
        compiler_params=_params("arbitrary", "arbitrary"),
        name="ret_prompt",
    )(cdec, decay, qdec, kdec, qkvg, qkvg, qkvg, qkvg, qkvg, mem_kv, mem_kv, w_out, x, g, b, *carry_args)


def _mem_attend_row(q, mk_ref, mv_ref):
    s = jnp.sum(mk_ref[0] * q[None], axis=-1, keepdims=True) * ATT_SCALE
    p = jnp.exp(s - jnp.max(s, axis=0, keepdims=True))
    p = p * (1.0 / jnp.sum(p, axis=0, keepdims=True))
    return jnp.sum(p * mv_ref[0], axis=0)


def _swa_step_kernel(sink_ref, rows_ref, kbuf_ref, vbuf_ref, mk_ref, mv_ref, o_ref, nk_ref, nv_ref):
    krow0 = SWA_Q_HEADS
    vrow0 = krow0 + SWA_KV_HEADS
    mrow0 = vrow0 + SWA_KV_HEADS
    wb = kbuf_ref.shape[1]
    kb = kbuf_ref[0]
    vb = vbuf_ref[0]
    k_new = rows_ref[0, krow0:krow0 + SWA_KV_HEADS, :]
    v_new = rows_ref[0, vrow0:vrow0 + SWA_KV_HEADS, :]
    for g in range(SWA_GROUP):
        group_rows = pl.ds(g, SWA_KV_HEADS, stride=SWA_GROUP)
        q = rows_ref[0, group_rows, :]
        s_buf = jnp.sum(kb * q[None], axis=-1, keepdims=True) * ATT_SCALE
        s_new = jnp.sum(k_new * q, axis=-1, keepdims=True) * ATT_SCALE
        sink = sink_ref[g]
        m = jnp.maximum(jnp.maximum(jnp.max(s_buf, axis=0), s_new), sink)
        p_buf = jnp.exp(s_buf - m[None])
        p_new = jnp.exp(s_new - m)
        inv = 1.0 / (jnp.sum(p_buf, axis=0) + p_new + jnp.exp(sink - m))
        o_ref[0, group_rows, :] = jnp.sum((p_buf * inv[None]) * vb, axis=0) + (p_new * inv) * v_new
    nk_ref[0, 0:wb - 1] = kbuf_ref[0, 1:wb]
    nv_ref[0, 0:wb - 1] = vbuf_ref[0, 1:wb]
    nk_ref[0, wb - 1] = k_new
    nv_ref[0, wb - 1] = v_new
    o_ref[0, SWA_Q_HEADS:SWA_Q_HEADS + MEM_HEADS, :] = _mem_attend_row(
        rows_ref[0, mrow0:mrow0 + MEM_HEADS, :], mk_ref, mv_ref)


def _swa_step(rows, cache_k, cache_v, j, mem_k, mem_v, i, sinks, carried):
    bsz = rows.shape[0]
    wb = cache_k.shape[2]
    cache = pl.BlockSpec((None, 1, wb, SWA_KV_HEADS, HEAD_DIM), lambda b: (j, b, 0, 0, 0))
    mem = pl.BlockSpec((None, 1, MEM_LEN, MEM_HEADS, HEAD_DIM), lambda b: (i, b, 0, 0, 0))
    nrows = D_MODEL // HEAD_DIM
    sink_gk = sinks.reshape(SWA_KV_HEADS, SWA_GROUP).T.reshape(SWA_GROUP, SWA_KV_HEADS, 1)
    in_specs = [
        pl.BlockSpec(sink_gk.shape, lambda b: (0, 0, 0)),
        pl.BlockSpec((1,) + rows.shape[1:], lambda b: (b, 0, 0)),
        cache, cache, mem, mem,
    ]
    n_in = len(in_specs)
    carry_specs, carry_args = _carry(carried)
    return pl.pallas_call(
        _drop_carried(_swa_step_kernel, n_in, len(carry_args)),
        grid=(bsz,),
        in_specs=in_specs + carry_specs,
        out_specs=[pl.BlockSpec((1, nrows, HEAD_DIM), lambda b: (b, 0, 0)), cache, cache],
        out_shape=[
            jax.ShapeDtypeStruct((bsz, nrows, HEAD_DIM), F32),
            jax.ShapeDtypeStruct(cache_k.shape, F32),
            jax.ShapeDtypeStruct(cache_v.shape, F32),
        ],
        input_output_aliases={n_in + c: 1 + c for c in range(len(carry_args))},
        compiler_params=_params("parallel"),
        name="swa_step",
    )(sink_gk, rows, cache_k, cache_v, mem_k, mem_v, *carry_args)


def _ret_step_kernel(dec_ref, rows_ref, cols_ref, s_ref, mk_ref, mv_ref, o_ref, s_out_ref):
    krow0 = RET_HEADS
    vrow0 = 2 * RET_HEADS
    grow0 = vrow0 + 2 * RET_HEADS
    mrow0 = grow0 + 2 * RET_HEADS
    for h in range(RET_HEADS):
        q_row = rows_ref[0, h:h + 1, :]
        k_row = rows_ref[0, krow0 + h:krow0 + h + 1, :] * RET_K_SCALE
        q_col = cols_ref[0, :, h:h + 1]
        k_col = cols_ref[0, :, krow0 + h:krow0 + h + 1] * RET_K_SCALE
        inner = jnp.sum(q_row * k_row, axis=1, keepdims=True) * dec_ref[0, h]
        qd = q_col * dec_ref[1, h]
        kd = k_col * dec_ref[2, h]
        halves = []
        for t in range(2):
            lsl = slice(t * HEAD_DIM, (t + 1) * HEAD_DIM)
            v = rows_ref[0, vrow0 + 2 * h + t:vrow0 + 2 * h + t + 1, :]
            st = s_ref[0, h, :, lsl]
            halves.append(inner * v + jnp.sum(qd * st, axis=0, keepdims=True))
            s_out_ref[0, h, :, lsl] = dec_ref[3, h] * st + kd * v
        mu = (jnp.sum(halves[0], axis=1, keepdims=True)
              + jnp.sum(halves[1], axis=1, keepdims=True)) * (1.0 / RET_DV)
        cen = [o - mu for o in halves]
        var = (jnp.sum(cen[0] * cen[0], axis=1, keepdims=True)
               + jnp.sum(cen[1] * cen[1], axis=1, keepdims=True)) * (1.0 / RET_DV)
        rstd = lax.rsqrt(var + HEAD_NORM_EPS)
        for t in range(2):
            gate = rows_ref[0, grow0 + 2 * h + t:grow0 + 2 * h + t + 1, :]
            o_ref[0, 2 * h + t:2 * h + t + 1, :] = jax.nn.silu(gate) * (cen[t] * rstd)
    o_ref[0, 2 * RET_HEADS:2 * RET_HEADS + MEM_HEADS, :] = _mem_attend_row(
        rows_ref[0, mrow0:mrow0 + MEM_HEADS, :], mk_ref, mv_ref)


def _ret_step(rows, cols, state, j, mem_k, mem_v, i, dec, carried):
    bsz = rows.shape[0]
    st = pl.BlockSpec((None, 1, RET_HEADS, RET_DK, RET_DV), lambda b: (j, b, 0, 0, 0))
    mem = pl.BlockSpec((None, 1, MEM_LEN, MEM_HEADS, HEAD_DIM), lambda b: (i, b, 0, 0, 0))
    nrows = D_MODEL // HEAD_DIM
    in_specs = [
        pl.BlockSpec(memory_space=pltpu.SMEM),
        pl.BlockSpec((1,) + rows.shape[1:], lambda b: (b, 0, 0)),
        pl.BlockSpec((1,) + cols.shape[1:], lambda b: (b, 0, 0)),
        st, mem, mem,
    ]
    n_in = len(in_specs)
    carry_specs, carry_args = _carry(carried)
    return pl.pallas_call(
        _drop_carried(_ret_step_kernel, n_in, len(carry_args)),
        grid=(bsz,),
        in_specs=in_specs + carry_specs,
        out_specs=[pl.BlockSpec((1, nrows, HEAD_DIM), lambda b: (b, 0, 0)), st],
        out_shape=[
            jax.ShapeDtypeStruct((bsz, nrows, HEAD_DIM), F32),
            jax.ShapeDtypeStruct(state.shape, F32),
        ],
        input_output_aliases={n_in + c: 1 + c for c in range(len(carry_args))},
        compiler_params=_params("parallel"),
        name="ret_step",
    )(dec, rows, cols, state, mem_k, mem_v, *carry_args)


def _rope_tables(pos):
    half = ROPE_DIM // 2
    inv = ROPE_THETA ** (-jnp.arange(half, dtype=F32) / half)
    ang = pos.astype(F32)[:, None] * inv[None, :]
    cos, sin = jnp.cos(ang), jnp.sin(ang)
    n = pos.shape[0]
    rest = HEAD_DIM - ROPE_DIM
    c = jnp.concatenate([cos, cos, jnp.ones((n, rest), F32)], axis=-1)
    a = jnp.concatenate([-sin, jnp.zeros((n, HEAD_DIM - half), F32)], axis=-1)
    b = jnp.concatenate([jnp.zeros((n, half), F32), sin, jnp.zeros((n, rest), F32)], axis=-1)
    return (c, a, b), (HEAD_DIM - half, half)


def _ret_rot_tables(pos):
    half = RET_DK // 2
    angle = RET_ROT_BASE ** (-jnp.linspace(0.0, 1.0, half, dtype=F32))
    ang = pos.astype(F32)[:, None] * angle[None, :]
    cos, sin = jnp.cos(ang), jnp.sin(ang)
    n = pos.shape[0]
    zero = jnp.zeros_like(sin)
    c = jnp.stack([cos, cos], axis=-1).reshape(n, RET_DK)
    a = jnp.stack([-sin, zero], axis=-1).reshape(n, RET_DK)
    b = jnp.stack([zero, sin], axis=-1).reshape(n, RET_DK)
    return (c, a, b), (RET_DK - 1, 1)


def _ret_decay(chunk):
    log_g = jnp.log1p(-jnp.exp2(-5.0 - jnp.arange(RET_HEADS, dtype=F32)))
    n = jnp.arange(chunk, dtype=F32)
    rel = n[:, None] - n[None, :]
    decay = jnp.where(rel >= 0, jnp.exp(jnp.maximum(rel, 0.0) * log_g[:, None, None]), 0.0)
    q_dec = jnp.exp((n + 1.0) * log_g[:, None])
    k_dec = jnp.exp((chunk - 1.0 - n) * log_g[:, None])
    c_dec = jnp.exp(chunk * log_g)
    return decay, q_dec, k_dec, c_dec


def kernel(x_prompt, x_sample, cache_swa_k, cache_swa_v, state_ret, cache_mem_k, cache_mem_v,
           mem_prompt, ln_g, ln_b, ffn_w_gu, ffn_w_down, w_mem_kv, swa_w_in, swa_w_out,
           swa_sinks, ret_w_in, ret_w_out):
    bp, seq, _ = x_prompt.shape
    bs, sample_seq, _ = x_sample.shape
    assert sample_seq == 1
    mp, ms = bp * seq, bs
    assert mp % ms == 0

    mem2 = mem_prompt.reshape(bp * MEM_LEN, D_MODEL)
    mem_kv = jnp.stack([_proj(mem2, w_mem_kv, l, bp * MEM_LEN) for l in range(DEPTH)])
    mem_kv = mem_kv.reshape(DEPTH, bp, MEM_LEN, 2 * MEM_DIM)
    mem_k_prompt = mem_kv[..., :MEM_DIM].reshape(DEPTH, bp, MEM_LEN, MEM_HEADS, HEAD_DIM)
    mem_v_prompt = mem_kv[..., MEM_DIM:].reshape(DEPTH, bp, MEM_LEN, MEM_HEADS, HEAD_DIM)

    tm, m_all = _merged_rows(mp, ms)
    pad = m_all - mp - ms
    x = jnp.concatenate([x_prompt.reshape(mp, D_MODEL), x_sample.reshape(ms, D_MODEL),
                         jnp.zeros((pad, D_MODEL), F32)], axis=0)
    pos = jnp.concatenate([jnp.tile(jnp.arange(seq, dtype=jnp.int32), bp),
                           jnp.full((ms,), PAST_LEN, jnp.int32), jnp.zeros((pad,), jnp.int32)])
    rope_tabs, rope_shifts = _rope_tables(pos)
    rot_tabs, rot_shifts = _ret_rot_tables(pos)
    decay, q_dec, k_dec, c_dec = _ret_decay(RET_CHUNK)
    ret_tabs = (decay, jnp.broadcast_to(q_dec[:, :, None], decay.shape),
                jnp.broadcast_to(k_dec[:, :, None], decay.shape), c_dec)
    decay, q_dec, k_dec, c_dec = _ret_decay(sample_seq)
    step_dec = jnp.stack([decay[:, 0, 0], q_dec[:, 0], k_dec[:, 0], c_dec])

    def ln(i, s):
        return ln_g[i, s].reshape(1, D_MODEL), ln_b[i, s].reshape(1, D_MODEL)

    wlen = min(WINDOW, seq)
    swa_k_prompt, swa_v_prompt = [], []
    swa_sample = None
    ret_prompt = None
    ret_sample = None
    for i in range(DEPTH):
        j = i // 2
        x = _ffn(x, ffn_w_gu, ffn_w_down, i, 0, *ln(i, 0), tm)
        if i % 2 == 0:
            qkv = _proj(x, swa_w_in, j, tm, n_rot=(SWA_Q + SWA_KV) // PROJ_TILE_N, tables=rope_tabs,
                        shifts=rope_shifts)
            x = _swa_prompt(qkv, bp, seq, mem_kv, i, swa_sinks[j], swa_w_out, x, *ln(i, 1))
            tail = qkv[:mp].reshape(bp, seq, SWA_IN_WIDTH)[:, seq - wlen:]
            swa_k_prompt.append(tail[..., SWA_Q:SWA_Q + SWA_KV].reshape(bp, wlen, SWA_KV_HEADS, HEAD_DIM))
            swa_v_prompt.append(tail[..., SWA_Q + SWA_KV:SWA_Q + 2 * SWA_KV]
                                .reshape(bp, wlen, SWA_KV_HEADS, HEAD_DIM))
            rows = qkv[mp:mp + ms].reshape(ms, SWA_IN_WIDTH // HEAD_DIM, HEAD_DIM)
            att, *swa_sample = _swa_step(rows, cache_swa_k, cache_swa_v, j, cache_mem_k, cache_mem_v, i,
                                         swa_sinks[j], swa_sample)
            x = _out_ln(att.reshape(ms, D_MODEL), swa_w_out, j, x, *ln(i, 1), mp)
        else:
            qkvg = _proj(x, ret_w_in, j, tm, n_rot=(2 * RET_QK) // PROJ_TILE_N, tables=rot_tabs,
                         shifts=rot_shifts)
            x, ret_prompt = _ret_prompt(qkvg, bp, seq, mem_kv, i, ret_tabs, ret_w_out, x, *ln(i, 1),
                                        None if ret_prompt is None else [ret_prompt])
            rows = qkvg[mp:mp + ms].reshape(ms, RET_IN_WIDTH // HEAD_DIM, HEAD_DIM)
            att, ret_sample = _ret_step(rows, jnp.swapaxes(rows, 1, 2), state_ret, j, cache_mem_k,
                                        cache_mem_v, i, step_dec,
                                        None if ret_sample is None else [ret_sample])
            x = _out_ln(att.reshape(ms, D_MODEL), ret_w_out, j, x, *ln(i, 1), mp)
        x = _ffn(x, ffn_w_gu, ffn_w_down, i, 1, *ln(i, 2), tm)

    y_prompt = x[:mp].reshape(bp, seq, D_MODEL)
    y_sample = x[mp:mp + ms].reshape(bs, sample_seq, D_MODEL)
    return (y_prompt, y_sample, jnp.stack(swa_k_prompt), jnp.stack(swa_v_prompt), swa_sample[0],
            swa_sample[1], ret_prompt, ret_sample, mem_k_prompt, mem_v_prompt)
```

```python
import functools

import jax
import jax.numpy as jnp
from jax import lax
from jax.experimental import pallas as pl
from jax.experimental.pallas import tpu as pltpu

F32 = jnp.float32
BF16 = jnp.bfloat16

D_MODEL = 2048
DEPTH = 4
PAST_LEN = 16384
HEAD_DIM = 128
MEM_LEN = 256
MEM_HEADS = 4
MEM_DIM = MEM_HEADS * HEAD_DIM
SELF_WIDTH = D_MODEL - MEM_DIM
SWA_Q_HEADS = SELF_WIDTH // HEAD_DIM
SWA_KV_HEADS = SWA_Q_HEADS // 3
SWA_GROUP = SWA_Q_HEADS // SWA_KV_HEADS
WINDOW = 128
ROPE_THETA = 500000.0
ROPE_DIM = HEAD_DIM // 4
RET_DK = 128
RET_DV = 2 * RET_DK
RET_HEADS = SELF_WIDTH // RET_DV
RET_CHUNK = 128
RET_ROT_BASE = 10000.0
D_FF = ((8 * D_MODEL // 3 + 255) // 256) * 256
LN_EPS = 1e-5
HEAD_NORM_EPS = 1e-6
ALPHA = (2.0 * DEPTH) ** 0.25
NEG_INF = -1e30
ATT_SCALE = HEAD_DIM ** -0.5
RET_K_SCALE = RET_DK ** -0.5

SWA_Q = SWA_Q_HEADS * HEAD_DIM
SWA_KV = SWA_KV_HEADS * HEAD_DIM
SWA_IN_WIDTH = SWA_Q + 2 * SWA_KV + MEM_DIM
RET_QK = RET_HEADS * RET_DK
RET_V = RET_HEADS * RET_DV
RET_IN_WIDTH = 2 * RET_QK + 2 * RET_V + MEM_DIM

VMEM_LIMIT_BYTES = 58 * 1024 * 1024
LANES = 128
FFN_TILE_K = 256
ROW_TILE = 1024
ATT_BLOCKS_PER_STEP = 2
PROJ_TILE_N = 512
OUT_TILE_N = 512
MAX_ROW_CHUNK = 256
BF16_SUBLANES = 16


def _merged_rows(prompt_rows, sample_rows):
    tiles = max(prompt_rows // ROW_TILE, 1)
    per_tile = -(-(prompt_rows + sample_rows) // tiles)
    tm = -(-per_tile // BF16_SUBLANES) * BF16_SUBLANES
    return tm, tiles * tm


def _row_chunk(tm):
    for c in range(MAX_ROW_CHUNK, BF16_SUBLANES - 1, -BF16_SUBLANES):
        if tm % c == 0:
            return c
    return tm


def _params(*sem):
    return pltpu.CompilerParams(dimension_semantics=sem, vmem_limit_bytes=VMEM_LIMIT_BYTES)


def _layer_norm_rows(z, g, b):
    mu = jnp.mean(z, axis=-1, keepdims=True)
    zc = z - mu
    var = jnp.mean(zc * zc, axis=-1, keepdims=True)
    return zc * lax.rsqrt(var + LN_EPS) * g + b


def _ffn_kernel(x_ref, wg_ref, wu_ref, wd_ref, g_ref, b_ref, o_ref, xb_ref, wgu_ref, *, nk, tk, chunk):
    k = pl.program_id(1)
    tm = o_ref.shape[0]
    chunks = [slice(r, r + chunk) for r in range(0, tm, chunk)]

    def cast_weights():
        wgu_ref[:, :tk] = wg_ref[...].astype(BF16)
        wgu_ref[:, tk:] = wu_ref[...].astype(BF16)
        return wd_ref[...].astype(BF16)

    def down(xb, wd):
        gu = jnp.dot(xb, wgu_ref[...], preferred_element_type=F32)
        h = (jax.nn.silu(gu[:, :tk]) * gu[:, tk:]).astype(BF16)
        return jnp.dot(h, wd, preferred_element_type=F32)

    @pl.when(k == 0)
    def _():
        wd = cast_weights()
        for rows in chunks:
            xb = x_ref[rows, :].astype(BF16)
            xb_ref[rows, :] = xb
            o_ref[rows, :] = down(xb, wd)

    @pl.when(jnp.logical_and(k > 0, k < nk - 1))
    def _():
        wd = cast_weights()
        o_ref[...] += down(xb_ref[...], wd)

    @pl.when(k == nk - 1)
    def _():
        wd = cast_weights()
        for rows in chunks:
            acc = o_ref[rows, :] + down(xb_ref[rows, :], wd)
            z = ALPHA * x_ref[rows, :] + 0.5 * acc
            o_ref[rows, :] = _layer_norm_rows(z, g_ref[...], b_ref[...])


def _ffn(x, w_gu, w_down, layer, slot, g, b, tm):
    m = x.shape[0]
    tk = FFN_TILE_K
    nk = D_FF // tk
    assert nk >= 2
    return pl.pallas_call(
        functools.partial(_ffn_kernel, nk=nk, tk=tk, chunk=_row_chunk(tm)),
        grid=(m // tm, nk),
        in_specs=[
            pl.BlockSpec((tm, D_MODEL), lambda i, k: (i, 0), pipeline_mode=pl.Buffered(1)),
            pl.BlockSpec((None, None, D_MODEL, tk), lambda i, k: (layer, slot, 0, k)),
            pl.BlockSpec((None, None, D_MODEL, tk), lambda i, k: (layer, slot, 0, nk + k)),
            pl.BlockSpec((None, None, tk, D_MODEL), lambda i, k: (layer, slot, k, 0)),
            pl.BlockSpec((1, D_MODEL), lambda i, k: (0, 0)),
            pl.BlockSpec((1, D_MODEL), lambda i, k: (0, 0)),
        ],
        out_specs=pl.BlockSpec((tm, D_MODEL), lambda i, k: (i, 0)),
        out_shape=jax.ShapeDtypeStruct((m, D_MODEL), F32),
        scratch_shapes=[pltpu.VMEM((tm, D_MODEL), BF16), pltpu.VMEM((D_MODEL, 2 * tk), BF16)],
        compiler_params=_params("parallel", "arbitrary"),
        name="ffn_ln",
    )(x, w_gu, w_gu, w_down, g, b)


def _proj_kernel(x_ref, w_ref, *rest, n_rot, shifts):
    if n_rot:
        c_ref, a_ref, b_ref, o_ref, xb_ref, wb_ref = rest
    else:
        o_ref, xb_ref, wb_ref = rest
    i = pl.program_id(0)
    j = pl.program_id(1)

    @pl.when(j == 0)
    def _():
        xb_ref[...] = x_ref[...].astype(BF16)

    @pl.when(i == 0)
    def _():
        wb_ref[j] = w_ref[...].astype(BF16)

    if not n_rot:
        o_ref[...] = jnp.dot(xb_ref[...], wb_ref[j], preferred_element_type=F32)
        return

    tm, tn = o_ref.shape
    rc = _row_chunk(tm)

    @pl.when(j < n_rot)
    def _():
        wb = wb_ref[j]
        for r0 in range(0, tm, rc):
            rows = slice(r0, r0 + rc)
            y = jnp.dot(xb_ref[rows, :], wb, preferred_element_type=F32)
            c, a, b = c_ref[rows, :], a_ref[rows, :], b_ref[rows, :]
            for h in range(tn // LANES):
                yh = y[:, h * LANES:(h + 1) * LANES]
                o_ref[rows, h * LANES:(h + 1) * LANES] = (
                    yh * c + pltpu.roll(yh, shifts[0], 1) * a + pltpu.roll(yh, shifts[1], 1) * b)

    @pl.when(j >= n_rot)
    def _():
        o_ref[...] = jnp.dot(xb_ref[...], wb_ref[j], preferred_element_type=F32)


def _proj(x, w, layer, tm, n_rot=0, tables=None, shifts=None):
    m = x.shape[0]
    n = w.shape[-1]
    tn = PROJ_TILE_N
    nj = n // tn
    in_specs = [
        pl.BlockSpec((tm, D_MODEL), lambda i, j: (i, 0)),
        pl.BlockSpec((None, D_MODEL, tn), lambda i, j: (layer, 0, jnp.where(i == 0, j, nj - 1)),
                     pipeline_mode=pl.Buffered(1)),
    ]
    args = [x, w]
    if n_rot:
        tab = pl.BlockSpec((tm, LANES), lambda i, j: (i, 0))
        in_specs += [tab, tab, tab]
        args += list(tables)
    return pl.pallas_call(
        functools.partial(_proj_kernel, n_rot=n_rot, shifts=shifts),
        grid=(m // tm, nj),
        in_specs=in_specs,
        out_specs=pl.BlockSpec((tm, tn), lambda i, j: (i, j)),
        out_shape=jax.ShapeDtypeStruct((m, n), F32),
        scratch_shapes=[pltpu.VMEM((tm, D_MODEL), BF16), pltpu.VMEM((nj, D_MODEL, tn), BF16)],
        compiler_params=_params("arbitrary", "arbitrary"),
        name="proj",
    )(*args)


def _out_ln_kernel(att_ref, w_ref, x_ref, g_ref, b_ref, o_ref, y_ref, *, nn, tn):
    n = pl.program_id(1)
    y_ref[n] = jnp.dot(att_ref[...].astype(BF16), w_ref[...].astype(BF16),
                       preferred_element_type=F32)

    @pl.when(n == nn - 1)
    def _():
        cols = [slice(c * tn, (c + 1) * tn) for c in range(nn)]
        total = None
        for c in range(nn):
            z = ALPHA * x_ref[:, cols[c]] + y_ref[c]
            y_ref[c] = z
            part = jnp.sum(z, axis=-1, keepdims=True)
            total = part if total is None else total + part
        mu = total * (1.0 / D_MODEL)
        total = None
        for c in range(nn):
            zc = y_ref[c] - mu
            part = jnp.sum(zc * zc, axis=-1, keepdims=True)
            total = part if total is None else total + part
        rstd = lax.rsqrt(total * (1.0 / D_MODEL) + LN_EPS)
        for c in range(nn):
            o_ref[:, cols[c]] = (y_ref[c] - mu) * rstd * g_ref[:, cols[c]] + b_ref[:, cols[c]]


def _out_ln(att, w_out, layer, x, g, b, row0):
    tm = att.shape[0]
    assert row0 % tm == 0
    tn = OUT_TILE_N
    nn = D_MODEL // tn
    rows = pl.BlockSpec((tm, D_MODEL), lambda i, n: (row0 // tm, 0))
    return pl.pallas_call(
        functools.partial(_out_ln_kernel, nn=nn, tn=tn),
        grid=(1, nn),
        in_specs=[
            pl.BlockSpec((tm, D_MODEL), lambda i, n: (0, 0)),
            pl.BlockSpec((None, D_MODEL, tn), lambda i, n: (layer, 0, n)),
            rows,
            pl.BlockSpec((1, D_MODEL), lambda i, n: (0, 0)),
            pl.BlockSpec((1, D_MODEL), lambda i, n: (0, 0)),
        ],
        out_specs=rows,
        out_shape=jax.ShapeDtypeStruct(x.shape, F32),
        scratch_shapes=[pltpu.VMEM((nn, tm, tn), F32)],
        input_output_aliases={2: 0},
        compiler_params=_params("arbitrary", "arbitrary"),
        name="out_ln",
    )(att, w_out, x, g, b)


def _dot_nt(a, b):
    return lax.dot_general(a, b, (((1,), (1,)), ((), ())), preferred_element_type=F32)


def _out_proj_ln(first, att_ref, w_ref, wb_ref, x_ref, g_ref, b_ref, o_ref):
    @pl.when(first)
    def _():
        wb_ref[...] = w_ref[...].astype(BF16)

    y = jnp.dot(att_ref[...], wb_ref[...], preferred_element_type=F32)
    o_ref[...] = _layer_norm_rows(ALPHA * x_ref[...] + y, g_ref[...], b_ref[...])


def _mem_attend_block(qm_ref, mk_ref, mv_ref, att_ref, col0):
    for h in range(MEM_HEADS):
        sl = slice(h * HEAD_DIM, (h + 1) * HEAD_DIM)
        q = qm_ref[:, sl].astype(BF16)
        s = _dot_nt(q, mk_ref[0, :, sl].astype(BF16)) * ATT_SCALE
        p = jnp.exp(s - jnp.max(s, axis=-1, keepdims=True))
        p = p * (1.0 / jnp.sum(p, axis=-1, keepdims=True))
        o = jnp.dot(p.astype(BF16), mv_ref[0, :, sl].astype(BF16), preferred_element_type=F32)
        att_ref[:, col0 + h * HEAD_DIM:col0 + (h + 1) * HEAD_DIM] = o.astype(att_ref.dtype)


def _swa_prompt_kernel(sink_ref, q_ref, kc_ref, kp_ref, vc_ref, vp_ref, qm_ref, mk_ref, mv_ref,
                       w_ref, x_ref, g_ref, b_ref, o_ref, kt_ref, vt_ref, s_ref, p_ref, att_ref, wb_ref,
                       *, nblk, steps):
    n = pl.program_id(1)
    nq = SWA_Q_HEADS
    w2 = 2 * WINDOW

    def prev_cur(cur_ref, prev_ref, i, sl):
        rows = slice(i * WINDOW, (i + 1) * WINDOW)
        before = prev_ref[:, sl] if i == 0 else cur_ref[(i - 1) * WINDOW:i * WINDOW, sl]
        return jnp.concatenate([before, cur_ref[rows, sl]], axis=0).astype(BF16)

    for i in range(nblk):
        rows = slice(i * WINDOW, (i + 1) * WINDOW)
        for h in range(SWA_KV_HEADS):
            sl = slice(h * HEAD_DIM, (h + 1) * HEAD_DIM)
            q3 = jnp.concatenate(
                [q_ref[rows, (h * SWA_GROUP + g) * HEAD_DIM:(h * SWA_GROUP + g + 1) * HEAD_DIM]
                 for g in range(SWA_GROUP)], axis=0).astype(BF16)
            s = _dot_nt(q3, prev_cur(kc_ref, kp_ref, i, sl)) * ATT_SCALE
            s_ref[i, h * SWA_GROUP:(h + 1) * SWA_GROUP] = s.reshape(SWA_GROUP, WINDOW, w2)
    for h in range(MEM_HEADS):
        sl = slice(h * HEAD_DIM, (h + 1) * HEAD_DIM)
        sm = _dot_nt(qm_ref[:, sl].astype(BF16), mk_ref[0, :, sl].astype(BF16)) * ATT_SCALE
        s_ref[:, nq + h] = sm.reshape(nblk, WINDOW, w2)

    qi = lax.broadcasted_iota(jnp.int32, (WINDOW, w2), 0)
    kj = lax.broadcasted_iota(jnp.int32, (WINDOW, w2), 1)
    sink = sink_ref[...]
    for i in range(nblk):
        first_key = jnp.where(n > 0, qi, WINDOW) if i == 0 else qi
        ok = (kj >= first_key) & (kj <= qi + WINDOW)
        s = jnp.where(ok[None], s_ref[i, 0:nq], NEG_INF)
        m = jnp.maximum(jnp.max(s, axis=-1, keepdims=True), sink)
        p = jnp.exp(s - m)
        den = jnp.sum(p, axis=-1, keepdims=True) + jnp.exp(sink - m)
        p_ref[i, 0:nq] = (p * (1.0 / den)).astype(BF16)
    s = s_ref[:, nq:nq + MEM_HEADS]
    p = jnp.exp(s - jnp.max(s, axis=-1, keepdims=True))
    p_ref[:, nq:nq + MEM_HEADS] = (p * (1.0 / jnp.sum(p, axis=-1, keepdims=True))).astype(BF16)

    for i in range(nblk):
        rows = slice(i * WINDOW, (i + 1) * WINDOW)
        for h in range(SWA_KV_HEADS):
            sl = slice(h * HEAD_DIM, (h + 1) * HEAD_DIM)
            p3 = p_ref[i, h * SWA_GROUP:(h + 1) * SWA_GROUP].reshape(SWA_GROUP * WINDOW, w2)
            o = jnp.dot(p3, prev_cur(vc_ref, vp_ref, i, sl), preferred_element_type=F32)
            for g in range(SWA_GROUP):
                hq = h * SWA_GROUP + g
                att_ref[rows, hq * HEAD_DIM:(hq + 1) * HEAD_DIM] = (
                    o[g * WINDOW:(g + 1) * WINDOW].astype(BF16))
    for h in range(MEM_HEADS):
        sl = slice(h * HEAD_DIM, (h + 1) * HEAD_DIM)
        pm = p_ref[:, nq + h].reshape(nblk * WINDOW, w2)
        om = jnp.dot(pm, mv_ref[0, :, sl].astype(BF16), preferred_element_type=F32)
        att_ref[:, SWA_Q + h * HEAD_DIM:SWA_Q + (h + 1) * HEAD_DIM] = om.astype(BF16)

    @pl.when(n == steps - 1)
    def _():
        last = slice((nblk - 1) * WINDOW, nblk * WINDOW)
        kt_ref[0] = kc_ref[last, :]
        vt_ref[0] = vc_ref[last, :]

    first = jnp.logical_and(pl.program_id(0) == 0, n == 0)
    _out_proj_ln(first, att_ref, w_ref, wb_ref, x_ref, g_ref, b_ref, o_ref)


def _mixer_tail_specs(layer, row_of, rows):
    mem = (None, 1, MEM_LEN, MEM_DIM)
    vec = pl.BlockSpec((1, D_MODEL), lambda b, n: (0, 0))
    mixer = layer // 2
    return [
        pl.BlockSpec(mem, lambda b, n: (layer, b, 0, 0)),
        pl.BlockSpec(mem, lambda b, n: (layer, b, 0, 1)),
        pl.BlockSpec((None, D_MODEL, D_MODEL), lambda b, n: (mixer, 0, 0), pipeline_mode=pl.Buffered(1)),
        pl.BlockSpec((rows, D_MODEL), lambda b, n: (row_of(b, n), 0)),
        vec, vec,
    ]


def _swa_prompt(qkv, bsz, seq, mem_kv, layer, sinks, w_out, x, g, b):
    nblk = ATT_BLOCKS_PER_STEP
    rows = nblk * WINDOW
    steps = seq // rows
    assert seq % rows == 0 and seq >= WINDOW
    tail = pl.BlockSpec((1, WINDOW, SWA_KV), lambda b, n: (b, 0, 0))
    blocks_per_seq = seq // WINDOW
    kcol = SWA_Q // SWA_KV
    vcol = kcol + 1
    mcol = vcol + 1
    row_of = lambda b, n: b * steps + n
    prev_of = lambda b, n: b * blocks_per_seq + jnp.maximum(n * nblk - 1, 0)
    cur = lambda w, col: pl.BlockSpec((rows, w), lambda b, n: (row_of(b, n), col))
    prev = lambda col: pl.BlockSpec((WINDOW, SWA_KV), lambda b, n: (prev_of(b, n), col))
    nheads = SWA_Q_HEADS + MEM_HEADS
    assert MEM_LEN == 2 * WINDOW
    in_specs = [
        pl.BlockSpec((SWA_Q_HEADS, 1, 1), lambda b, n: (0, 0, 0)),
        cur(SWA_Q, 0), cur(SWA_KV, kcol), prev(kcol), cur(SWA_KV, vcol), prev(vcol), cur(MEM_DIM, mcol),
    ] + _mixer_tail_specs(layer, row_of, rows)
    return pl.pallas_call(
        functools.partial(_swa_prompt_kernel, nblk=nblk, steps=steps),
        grid=(bsz, steps),
        in_specs=in_specs,
        out_specs=[pl.BlockSpec((rows, D_MODEL), lambda b, n: (row_of(b, n), 0)), tail, tail],
        out_shape=[jax.ShapeDtypeStruct(x.shape, F32),
                   jax.ShapeDtypeStruct((bsz, WINDOW, SWA_KV), F32),
                   jax.ShapeDtypeStruct((bsz, WINDOW, SWA_KV), F32)],
        scratch_shapes=[pltpu.VMEM((nblk, nheads, WINDOW, 2 * WINDOW), F32),
                        pltpu.VMEM((nblk, nheads, WINDOW, 2 * WINDOW), BF16),
                        pltpu.VMEM((rows, D_MODEL), BF16),
                        pltpu.VMEM((D_MODEL, D_MODEL), BF16)],
        input_output_aliases={len(in_specs) - 3: 0},
        compiler_params=_params("arbitrary", "arbitrary"),
        name="swa_prompt",
    )(sinks.reshape(SWA_Q_HEADS, 1, 1), qkv, qkv, qkv, qkv, qkv, qkv, mem_kv, mem_kv, w_out, x, g, b)


def _head_norm_gate(o, gate):
    mu = jnp.mean(o, axis=-1, keepdims=True)
    oc = o - mu
    var = jnp.mean(oc * oc, axis=-1, keepdims=True)
    return jax.nn.silu(gate) * (oc * lax.rsqrt(var + HEAD_NORM_EPS))


def _ret_prompt_kernel(cdec_ref, decay_ref, qdec_ref, kdec_ref, q_ref, k_ref, v_ref, gate_ref, qm_ref,
                       mk_ref, mv_ref, w_ref, x_ref, g_ref, b_ref, o_ref, s_out_ref,
                       state_ref, att_ref, wb_ref, *, steps, nblk):
    c = pl.program_id(1)

    @pl.when(c == 0)
    def _():
        state_ref[...] = jnp.zeros_like(state_ref)

    for i in range(nblk):
        rows = slice(i * RET_CHUNK, (i + 1) * RET_CHUNK)
        for h in range(RET_HEADS):
            ksl = slice(h * RET_DK, (h + 1) * RET_DK)
            vsl = slice(h * RET_DV, (h + 1) * RET_DV)
            qc = q_ref[rows, ksl]
            kc = k_ref[rows, ksl] * RET_K_SCALE
            vb = v_ref[rows, vsl].astype(BF16)
            st = state_ref[h]
            inner = _dot_nt(qc.astype(BF16), kc.astype(BF16)) * decay_ref[h]
            o = (jnp.dot(inner.astype(BF16), vb, preferred_element_type=F32)
                 + jnp.dot((qc * qdec_ref[h]).astype(BF16), st.astype(BF16),
                           preferred_element_type=F32))
            kd = (kc * kdec_ref[h]).astype(BF16)
            state_ref[h] = cdec_ref[h] * st + lax.dot_general(
                kd, vb, (((0,), (0,)), ((), ())), preferred_element_type=F32)
            att_ref[rows, vsl] = _head_norm_gate(o, gate_ref[rows, vsl]).astype(BF16)
    _mem_attend_block(qm_ref, mk_ref, mv_ref, att_ref, RET_V)

    @pl.when(c == steps - 1)
    def _():
        s_out_ref[0] = state_ref[...]

    first = jnp.logical_and(pl.program_id(0) == 0, c == 0)
    _out_proj_ln(first, att_ref, w_ref, wb_ref, x_ref, g_ref, b_ref, o_ref)


def _drop_carried(kernel, first, count):
    def body(*refs):
        return kernel(*refs[:first], *refs[first + count:])
    return body


def _carry(prev_outputs):
    prev_outputs = [] if prev_outputs is None else list(prev_outputs)
    return [pl.BlockSpec(memory_space=pl.ANY)] * len(prev_outputs), prev_outputs


def _ret_prompt(qkvg, bsz, seq, mem_kv, layer, tables, w_out, x, g, b, carried):
    nblk = ATT_BLOCKS_PER_STEP
    rows = nblk * RET_CHUNK
    steps = seq // rows
    mixer = layer // 2
    decay, qdec, kdec, cdec = tables
    row_of = lambda b, n: b * steps + n
    cur = lambda w, col: pl.BlockSpec((rows, w), lambda b, n: (row_of(b, n), col))
    tab = pl.BlockSpec((RET_HEADS, RET_CHUNK, RET_CHUNK), lambda b, n: (0, 0, 0))
    in_specs = [
        pl.BlockSpec(memory_space=pltpu.SMEM), tab, tab, tab,
        cur(RET_QK, 0), cur(RET_QK, 1), cur(RET_V, 1), cur(RET_V, 2),
        cur(MEM_DIM, (2 * RET_QK + 2 * RET_V) // MEM_DIM),
    ] + _mixer_tail_specs(layer, row_of, rows)
    n_in = len(in_specs)
    carry_specs, carry_args = _carry(carried)
    aliases = {n_in - 3: 0}
    aliases.update({n_in + c: 1 + c for c in range(len(carry_args))})
    return pl.pallas_call(
        _drop_carried(functools.partial(_ret_prompt_kernel, steps=steps, nblk=nblk), n_in, len(carry_args)),
        grid=(bsz, steps),
        in_specs=in_specs + carry_specs,
        out_specs=[
            pl.BlockSpec((rows, D_MODEL), lambda b, n: (row_of(b, n), 0)),
            pl.BlockSpec((None, 1, RET_HEADS, RET_DK, RET_DV), lambda b, n: (mixer, b, 0, 0, 0)),
        ],
        out_shape=[
            jax.ShapeDtypeStruct(x.shape, F32),
            jax.ShapeDtypeStruct((DEPTH // 2, bsz, RET_HEADS, RET_DK, RET_DV), F32),
        ],
        scratch_shapes=[pltpu.VMEM((RET_HEADS, RET_DK, RET_DV), F32),
                        pltpu.VMEM((rows, D_MODEL), BF16),
                        pltpu.VMEM((D_MODEL, D_MODEL), BF16)],
        input_output_aliases=aliases,
        compiler_params=_params("arbitrary", "arbitrary"),
        name="ret_prompt",
    )(cdec, decay, qdec, kdec, qkvg, qkvg, qkvg, qkvg, qkvg, mem_kv, mem_kv, w_out, x, g, b, *carry_args)


def _mem_attend_row(q, mk_ref, mv_ref):
    s = jnp.sum(mk_ref[0] * q[None], axis=-1, keepdims=True) * ATT_SCALE
    p = jnp.exp(s - jnp.max(s, axis=0, keepdims=True))
    p = p * (1.0 / jnp.sum(p, axis=0, keepdims=True))
    return jnp.sum(p * mv_ref[0], axis=0)


def _swa_step_kernel(sink_ref, rows_ref, kbuf_ref, vbuf_ref, mk_ref, mv_ref, o_ref, nk_ref, nv_ref):
    krow0 = SWA_Q_HEADS
    vrow0 = krow0 + SWA_KV_HEADS
    mrow0 = vrow0 + SWA_KV_HEADS
    wb = kbuf_ref.shape[1]
    kb = kbuf_ref[0]
    vb = vbuf_ref[0]
    k_new = rows_ref[0, krow0:krow0 + SWA_KV_HEADS, :]
    v_new = rows_ref[0, vrow0:vrow0 + SWA_KV_HEADS, :]
    for g in range(SWA_GROUP):
        group_rows = pl.ds(g, SWA_KV_HEADS, stride=SWA_GROUP)
        q = rows_ref[0, group_rows, :]
        s_buf = jnp.sum(kb * q[None], axis=-1, keepdims=True) * ATT_SCALE
        s_new = jnp.sum(k_new * q, axis=-1, keepdims=True) * ATT_SCALE
        sink = sink_ref[g]
        m = jnp.maximum(jnp.maximum(jnp.max(s_buf, axis=0), s_new), sink)
        p_buf = jnp.exp(s_buf - m[None])
        p_new = jnp.exp(s_new - m)
        inv = 1.0 / (jnp.sum(p_buf, axis=0) + p_new + jnp.exp(sink - m))
        o_ref[0, group_rows, :] = jnp.sum((p_buf * inv[None]) * vb, axis=0) + (p_new * inv) * v_new
    nk_ref[0, 0:wb - 1] = kbuf_ref[0, 1:wb]
    nv_ref[0, 0:wb - 1] = vbuf_ref[0, 1:wb]
    nk_ref[0, wb - 1] = k_new
    nv_ref[0, wb - 1] = v_new
    o_ref[0, SWA_Q_HEADS:SWA_Q_HEADS + MEM_HEADS, :] = _mem_attend_row(
        rows_ref[0, mrow0:mrow0 + MEM_HEADS, :], mk_ref, mv_ref)


def _swa_step(rows, cache_k, cache_v, j, mem_k, mem_v, i, sinks, carried):
    bsz = rows.shape[0]
    wb = cache_k.shape[2]
    cache = pl.BlockSpec((None, 1, wb, SWA_KV_HEADS, HEAD_DIM), lambda b: (j, b, 0, 0, 0))
    mem = pl.BlockSpec((None, 1, MEM_LEN, MEM_HEADS, HEAD_DIM), lambda b: (i, b, 0, 0, 0))
    nrows = D_MODEL // HEAD_DIM
    sink_gk = sinks.reshape(SWA_KV_HEADS, SWA_GROUP).T.reshape(SWA_GROUP, SWA_KV_HEADS, 1)
    in_specs = [
        pl.BlockSpec(sink_gk.shape, lambda b: (0, 0, 0)),
        pl.BlockSpec((1,) + rows.shape[1:], lambda b: (b, 0, 0)),
        cache, cache, mem, mem,
    ]
    n_in = len(in_specs)
    carry_specs, carry_args = _carry(carried)
    return pl.pallas_call(
        _drop_carried(_swa_step_kernel, n_in, len(carry_args)),
        grid=(bsz,),
        in_specs=in_specs + carry_specs,
        out_specs=[pl.BlockSpec((1, nrows, HEAD_DIM), lambda b: (b, 0, 0)), cache, cache],
        out_shape=[
            jax.ShapeDtypeStruct((bsz, nrows, HEAD_DIM), F32),
            jax.ShapeDtypeStruct(cache_k.shape, F32),
            jax.ShapeDtypeStruct(cache_v.shape, F32),
        ],
        input_output_aliases={n_in + c: 1 + c for c in range(len(carry_args))},
        compiler_params=_params("parallel"),
        name="swa_step",
    )(sink_gk, rows, cache_k, cache_v, mem_k, mem_v, *carry_args)


def _ret_step_kernel(dec_ref, rows_ref, cols_ref, s_ref, mk_ref, mv_ref, o_ref, s_out_ref):
    krow0 = RET_HEADS
    vrow0 = 2 * RET_HEADS
    grow0 = vrow0 + 2 * RET_HEADS
    mrow0 = grow0 + 2 * RET_HEADS
    for h in range(RET_HEADS):
        q_row = rows_ref[0, h:h + 1, :]
        k_row = rows_ref[0, krow0 + h:krow0 + h + 1, :] * RET_K_SCALE
        q_col = cols_ref[0, :, h:h + 1]
        k_col = cols_ref[0, :, krow0 + h:krow0 + h + 1] * RET_K_SCALE
        inner = jnp.sum(q_row * k_row, axis=1, keepdims=True) * dec_ref[0, h]
        qd = q_col * dec_ref[1, h]
        kd = k_col * dec_ref[2, h]
        halves = []
        for t in range(2):
            lsl = slice(t * HEAD_DIM, (t + 1) * HEAD_DIM)
            v = rows_ref[0, vrow0 + 2 * h + t:vrow0 + 2 * h + t + 1, :]
            st = s_ref[0, h, :, lsl]
            halves.append(inner * v + jnp.sum(qd * st, axis=0, keepdims=True))
            s_out_ref[0, h, :, lsl] = dec_ref[3, h] * st + kd * v
        mu = (jnp.sum(halves[0], axis=1, keepdims=True)
              + jnp.sum(halves[1], axis=1, keepdims=True)) * (1.0 / RET_DV)
        cen = [o - mu for o in halves]
        var = (jnp.sum(cen[0] * cen[0], axis=1, keepdims=True)
               + jnp.sum(cen[1] * cen[1], axis=1, keepdims=True)) * (1.0 / RET_DV)
        rstd = lax.rsqrt(var + HEAD_NORM_EPS)
        for t in range(2):
            gate = rows_ref[0, grow0 + 2 * h + t:grow0 + 2 * h + t + 1, :]
            o_ref[0, 2 * h + t:2 * h + t + 1, :] = jax.nn.silu(gate) * (cen[t] * rstd)
    o_ref[0, 2 * RET_HEADS:2 * RET_HEADS + MEM_HEADS, :] = _mem_attend_row(
        rows_ref[0, mrow0:mrow0 + MEM_HEADS, :], mk_ref, mv_ref)


def _ret_step(rows, cols, state, j, mem_k, mem_v, i, dec, carried):
    bsz = rows.shape[0]
    st = pl.BlockSpec((None, 1, RET_HEADS, RET_DK, RET_DV), lambda b: (j, b, 0, 0, 0))
    mem = pl.BlockSpec((None, 1, MEM_LEN, MEM_HEADS, HEAD_DIM), lambda b: (i, b, 0, 0, 0))
    nrows = D_MODEL // HEAD_DIM
    in_specs = [
        pl.BlockSpec(memory_space=pltpu.SMEM),
        pl.BlockSpec((1,) + rows.shape[1:], lambda b: (b, 0, 0)),
        pl.BlockSpec((1,) + cols.shape[1:], lambda b: (b, 0, 0)),
        st, mem, mem,
    ]
    n_in = len(in_specs)
    carry_specs, carry_args = _carry(carried)
    return pl.pallas_call(
        _drop_carried(_ret_step_kernel, n_in, len(carry_args)),
        grid=(bsz,),
        in_specs=in_specs + carry_specs,
        out_specs=[pl.BlockSpec((1, nrows, HEAD_DIM), lambda b: (b, 0, 0)), st],
        out_shape=[
            jax.ShapeDtypeStruct((bsz, nrows, HEAD_DIM), F32),
            jax.ShapeDtypeStruct(state.shape, F32),
        ],
        input_output_aliases={n_in + c: 1 + c for c in range(len(carry_args))},
        compiler_params=_params("parallel"),
        name="ret_step",
    )(dec, rows, cols, state, mem_k, mem_v, *carry_args)


def _rope_tables(pos):
    half = ROPE_DIM // 2
    inv = ROPE_THETA ** (-jnp.arange(half, dtype=F32) / half)
    ang = pos.astype(F32)[:, None] * inv[None, :]
    cos, sin = jnp.cos(ang), jnp.sin(ang)
    n = pos.shape[0]
    rest = HEAD_DIM - ROPE_DIM
    c = jnp.concatenate([cos, cos, jnp.ones((n, rest), F32)], axis=-1)
    a = jnp.concatenate([-sin, jnp.zeros((n, HEAD_DIM - half), F32)], axis=-1)
    b = jnp.concatenate([jnp.zeros((n, half), F32), sin, jnp.zeros((n, rest), F32)], axis=-1)
    return (c, a, b), (HEAD_DIM - half, half)


def _ret_rot_tables(pos):
    half = RET_DK // 2
    angle = RET_ROT_BASE ** (-jnp.linspace(0.0, 1.0, half, dtype=F32))
    ang = pos.astype(F32)[:, None] * angle[None, :]
    cos, sin = jnp.cos(ang), jnp.sin(ang)
    n = pos.shape[0]
    zero = jnp.zeros_like(sin)
    c = jnp.stack([cos, cos], axis=-1).reshape(n, RET_DK)
    a = jnp.stack([-sin, zero], axis=-1).reshape(n, RET_DK)
    b = jnp.stack([zero, sin], axis=-1).reshape(n, RET_DK)
    return (c, a, b), (RET_DK - 1, 1)


def _ret_decay(chunk):
    log_g = jnp.log1p(-jnp.exp2(-5.0 - jnp.arange(RET_HEADS, dtype=F32)))
    n = jnp.arange(chunk, dtype=F32)
    rel = n[:, None] - n[None, :]
    decay = jnp.where(rel >= 0, jnp.exp(jnp.maximum(rel, 0.0) * log_g[:, None, None]), 0.0)
    q_dec = jnp.exp((n + 1.0) * log_g[:, None])
    k_dec = jnp.exp((chunk - 1.0 - n) * log_g[:, None])
    c_dec = jnp.exp(chunk * log_g)
    return decay, q_dec, k_dec, c_dec


def kernel(x_prompt, x_sample, cache_swa_k, cache_swa_v, state_ret, cache_mem_k, cache_mem_v,
           mem_prompt, ln_g, ln_b, ffn_w_gu, ffn_w_down, w_mem_kv, swa_w_in, swa_w_out,
           swa_sinks, ret_w_in, ret_w_out):
    bp, seq, _ = x_prompt.shape
    bs, sample_seq, _ = x_sample.shape
    assert sample_seq == 1
    mp, ms = bp * seq, bs
    assert mp % ms == 0

    mem2 = mem_prompt.reshape(bp * MEM_LEN, D_MODEL)
    mem_kv = jnp.stack([_proj(mem2, w_mem_kv, l, bp * MEM_LEN) for l in range(DEPTH)])
    mem_kv = mem_kv.reshape(DEPTH, bp, MEM_LEN, 2 * MEM_DIM)
    mem_k_prompt = mem_kv[..., :MEM_DIM].reshape(DEPTH, bp, MEM_LEN, MEM_HEADS, HEAD_DIM)
    mem_v_prompt = mem_kv[..., MEM_DIM:].reshape(DEPTH, bp, MEM_LEN, MEM_HEADS, HEAD_DIM)

    tm, m_all = _merged_rows(mp, ms)
    pad = m_all - mp - ms
    x = jnp.concatenate([x_prompt.reshape(mp, D_MODEL), x_sample.reshape(ms, D_MODEL),
                         jnp.zeros((pad, D_MODEL), F32)], axis=0)
    def per_row(table):
        return jnp.concatenate([jnp.tile(table[:seq], (bp, 1)),
                                jnp.broadcast_to(table[seq:], (ms, table.shape[1])),
                                jnp.zeros((pad, table.shape[1]), F32)], axis=0)

    pos = jnp.concatenate([jnp.arange(seq, dtype=jnp.int32), jnp.full((1,), PAST_LEN, jnp.int32)])
    rope_tabs, rope_shifts = _rope_tables(pos)
    rot_tabs, rot_shifts = _ret_rot_tables(pos)
    rope_tabs = tuple(per_row(t) for t in rope_tabs)
    rot_tabs = tuple(per_row(t) for t in rot_tabs)
    decay, q_dec, k_dec, c_dec = _ret_decay(RET_CHUNK)
    ret_tabs = (decay, jnp.broadcast_to(q_dec[:, :, None], decay.shape),
                jnp.broadcast_to(k_dec[:, :, None], decay.shape), c_dec)
    decay, q_dec, k_dec, c_dec = _ret_decay(sample_seq)
    step_dec = jnp.stack([decay[:, 0, 0], q_dec[:, 0], k_dec[:, 0], c_dec])

    def ln(i, s):
        return ln_g[i, s].reshape(1, D_MODEL), ln_b[i, s].reshape(1, D_MODEL)

    swa_k_prompt, swa_v_prompt = [], []
    swa_sample = None
    ret_prompt = None
    ret_sample = None
    for i in range(DEPTH):
        j = i // 2
        x = _ffn(x, ffn_w_gu, ffn_w_down, i, 0, *ln(i, 0), tm)
        if i % 2 == 0:
            qkv = _proj(x, swa_w_in, j, tm, n_rot=(SWA_Q + SWA_KV) // PROJ_TILE_N, tables=rope_tabs,
                        shifts=rope_shifts)
            x, k_tail, v_tail = _swa_prompt(qkv, bp, seq, mem_kv, i, swa_sinks[j], swa_w_out, x, *ln(i, 1))
            swa_k_prompt.append(k_tail.reshape(bp, WINDOW, SWA_KV_HEADS, HEAD_DIM))
            swa_v_prompt.append(v_tail.reshape(bp, WINDOW, SWA_KV_HEADS, HEAD_DIM))
            rows = qkv[mp:mp + ms].reshape(ms, SWA_IN_WIDTH // HEAD_DIM, HEAD_DIM)
            att, *swa_sample = _swa_step(rows, cache_swa_k, cache_swa_v, j, cache_mem_k, cache_mem_v, i,
                                         swa_sinks[j], swa_sample)
            x = _out_ln(att.reshape(ms, D_MODEL), swa_w_out, j, x, *ln(i, 1), mp)
        else:
            qkvg = _proj(x, ret_w_in, j, tm, n_rot=(2 * RET_QK) // PROJ_TILE_N, tables=rot_tabs,
                         shifts=rot_shifts)
            x, ret_prompt = _ret_prompt(qkvg, bp, seq, mem_kv, i, ret_tabs, ret_w_out, x, *ln(i, 1),
                                        None if ret_prompt is None else [ret_prompt])
            rows = qkvg[mp:mp + ms].reshape(ms, RET_IN_WIDTH // HEAD_DIM, HEAD_DIM)
            att, ret_sample = _ret_step(rows, jnp.swapaxes(rows, 1, 2), state_ret, j, cache_mem_k,
                                        cache_mem_v, i, step_dec,
                                        None if ret_sample is None else [ret_sample])
            x = _out_ln(att.reshape(ms, D_MODEL), ret_w_out, j, x, *ln(i, 1), mp)
        x = _ffn(x, ffn_w_gu, ffn_w_down, i, 1, *ln(i, 2), tm)

    y_prompt = x[:mp].reshape(bp, seq, D_MODEL)
    y_sample = x[mp:mp + ms].reshape(bs, sample_seq, D_MODEL)
    return (y_prompt, y_sample, jnp.stack(swa_k_prompt), jnp.stack(swa_v_prompt), swa_sample[0],
            swa_sample[1], ret_prompt, ret_sample, mem_k_prompt, mem_v_prompt)
```

```python
import functools

import jax
import jax.numpy as jnp
from jax import lax
from jax.experimental import pallas as pl
from jax.experimental.pallas import tpu as pltpu

F32 = jnp.float32
BF16 = jnp.bfloat16

D_MODEL = 2048
DEPTH = 4
PAST_LEN = 16384
HEAD_DIM = 128
MEM_LEN = 256
MEM_HEADS = 4
MEM_DIM = MEM_HEADS * HEAD_DIM
SELF_WIDTH = D_MODEL - MEM_DIM
SWA_Q_HEADS = SELF_WIDTH // HEAD_DIM
SWA_KV_HEADS = SWA_Q_HEADS // 3
SWA_GROUP = SWA_Q_HEADS // SWA_KV_HEADS
WINDOW = 128
ROPE_THETA = 500000.0
ROPE_DIM = HEAD_DIM // 4
RET_DK = 128
RET_DV = 2 * RET_DK
RET_HEADS = SELF_WIDTH // RET_DV
RET_CHUNK = 128
RET_ROT_BASE = 10000.0
D_FF = ((8 * D_MODEL // 3 + 255) // 256) * 256
LN_EPS = 1e-5
HEAD_NORM_EPS = 1e-6
ALPHA = (2.0 * DEPTH) ** 0.25
NEG_INF = -1e30
ATT_SCALE = HEAD_DIM ** -0.5
RET_K_SCALE = RET_DK ** -0.5

SWA_Q = SWA_Q_HEADS * HEAD_DIM
SWA_KV = SWA_KV_HEADS * HEAD_DIM
SWA_IN_WIDTH = SWA_Q + 2 * SWA_KV + MEM_DIM
RET_QK = RET_HEADS * RET_DK
RET_V = RET_HEADS * RET_DV
RET_IN_WIDTH = 2 * RET_QK + 2 * RET_V + MEM_DIM

VMEM_LIMIT_BYTES = 58 * 1024 * 1024
LANES = 128
FFN_TILE_K = 256
ROW_TILE = 1024
ATT_BLOCKS_PER_STEP = 2
PROJ_TILE_N = 1024
OUT_TILE_N = 512
MAX_ROW_CHUNK = 256
BF16_SUBLANES = 16


def _merged_rows(prompt_rows, sample_rows):
    tiles = max(prompt_rows // ROW_TILE, 1)
    per_tile = -(-(prompt_rows + sample_rows) // tiles)
    tm = -(-per_tile // BF16_SUBLANES) * BF16_SUBLANES
    return tm, tiles * tm


def _row_chunk(tm):
    for c in range(MAX_ROW_CHUNK, BF16_SUBLANES - 1, -BF16_SUBLANES):
        if tm % c == 0:
            return c
    return tm


def _params(*sem):
    return pltpu.CompilerParams(dimension_semantics=sem, vmem_limit_bytes=VMEM_LIMIT_BYTES)


def _layer_norm_rows(z, g, b):
    mu = jnp.mean(z, axis=-1, keepdims=True)
    zc = z - mu
    var = jnp.mean(zc * zc, axis=-1, keepdims=True)
    return zc * lax.rsqrt(var + LN_EPS) * g + b


def _ffn_kernel(*refs, nk, tk, chunk, tiles, tail, split_in, split_out):
    refs = list(refs)
    x_ref = refs.pop(0)
    xs_ref = refs.pop(0) if split_in else None
    wg_ref, wu_ref, wd_ref, g_ref, b_ref, o_ref = refs[:6]
    os_ref = refs[6] if split_out else None
    xb_ref, wgu_ref = refs[-2:]
    i = pl.program_id(0)
    k = pl.program_id(1)
    tm = o_ref.shape[0]
    chunks = [slice(r, r + chunk) for r in range(0, tm, chunk)]

    def load_x(rows):
        xv = x_ref[rows, :]
        if not split_in or rows.stop <= tail[0]:
            return xv
        first, count = tail
        lo, hi = rows.start, rows.stop
        pieces = []
        if lo < first:
            pieces.append(xv[:first - lo])
        s0, s1 = max(lo, first), min(hi, first + count)
        if s0 < s1:
            pieces.append(xs_ref[s0 - first:s1 - first, :])
        if hi > first + count:
            pieces.append(jnp.zeros((hi - max(lo, first + count), xv.shape[1]), xv.dtype))
        return jnp.where(i == tiles - 1, jnp.concatenate(pieces, axis=0), xv)

    def cast_weights():
        wgu_ref[:, :tk] = wg_ref[...].astype(BF16)
        wgu_ref[:, tk:] = wu_ref[...].astype(BF16)
        return wd_ref[...].astype(BF16)

    def down(xb, wd):
        gu = jnp.dot(xb, wgu_ref[...], preferred_element_type=F32)
        h = (jax.nn.silu(gu[:, :tk]) * gu[:, tk:]).astype(BF16)
        return jnp.dot(h, wd, preferred_element_type=F32)

    @pl.when(k == 0)
    def _():
        wd = cast_weights()
        for rows in chunks:
            xb = load_x(rows).astype(BF16)
            xb_ref[rows, :] = xb
            o_ref[rows, :] = down(xb, wd)

    @pl.when(jnp.logical_and(k > 0, k < nk - 1))
    def _():
        wd = cast_weights()
        o_ref[...] += down(xb_ref[...], wd)

    @pl.when(k == nk - 1)
    def _():
        wd = cast_weights()
        for rows in chunks:
            acc = o_ref[rows, :] + down(xb_ref[rows, :], wd)
            z = ALPHA * load_x(rows) + 0.5 * acc
            o_ref[rows, :] = _layer_norm_rows(z, g_ref[...], b_ref[...])
        if split_out:
            @pl.when(i == tiles - 1)
            def _():
                os_ref[...] = o_ref[tail[0]:tail[0] + tail[1], :]


def _ffn(x, w_gu, w_down, layer, slot, g, b, tm, tiles, tail, x_sample=None, split_out=False):
    tk = FFN_TILE_K
    nk = D_FF // tk
    assert nk >= 2
    split_in = x_sample is not None
    rows = pl.BlockSpec((tm, D_MODEL), lambda i, k: (i, 0))
    small = pl.BlockSpec((tail[1], D_MODEL), lambda i, k: (0, 0))
    in_specs = [pl.BlockSpec((tm, D_MODEL), lambda i, k: (i, 0), pipeline_mode=pl.Buffered(1))]
    args = [x]
    if split_in:
        in_specs.append(small)
        args.append(x_sample)
    in_specs += [
        pl.BlockSpec((None, None, D_MODEL, tk), lambda i, k: (layer, slot, 0, k)),
        pl.BlockSpec((None, None, D_MODEL, tk), lambda i, k: (layer, slot, 0, nk + k)),
        pl.BlockSpec((None, None, tk, D_MODEL), lambda i, k: (layer, slot, k, 0)),
        pl.BlockSpec((1, D_MODEL), lambda i, k: (0, 0)),
        pl.BlockSpec((1, D_MODEL), lambda i, k: (0, 0)),
    ]
    if split_out:
        prompt_rows = (tiles - 1) * tm + tail[0]
        out_specs = [rows, small]
        out_shape = [jax.ShapeDtypeStruct((prompt_rows, D_MODEL), F32),
                     jax.ShapeDtypeStruct((tail[1], D_MODEL), F32)]
    else:
        out_specs = rows
        out_shape = jax.ShapeDtypeStruct((tiles * tm, D_MODEL), F32)
    return pl.pallas_call(
        functools.partial(_ffn_kernel, nk=nk, tk=tk, chunk=_row_chunk(tm), tiles=tiles, tail=tail,
                          split_in=split_in, split_out=split_out),
        grid=(tiles, nk),
        in_specs=in_specs,
        out_specs=out_specs,
        out_shape=out_shape,
        scratch_shapes=[pltpu.VMEM((tm, D_MODEL), BF16), pltpu.VMEM((D_MODEL, 2 * tk), BF16)],
        compiler_params=_params("arbitrary", "arbitrary"),
        name="ffn_ln",
    )(*args, w_gu, w_gu, w_down, g, b)


def _proj_kernel(x_ref, w_ref, *rest, rot_heads, shifts):
    if rot_heads:
        c_ref, a_ref, b_ref, o_ref, xb_ref = rest
    else:
        o_ref, xb_ref = rest
    j = pl.program_id(1)

    @pl.when(j == 0)
    def _():
        xb_ref[...] = x_ref[...].astype(BF16)

    def plain():
        o_ref[...] = jnp.dot(xb_ref[...], w_ref[...].astype(BF16), preferred_element_type=F32)

    if not rot_heads:
        plain()
        return

    tm, tn = o_ref.shape
    heads = tn // LANES
    full_tiles, part_heads = divmod(rot_heads, heads)
    rc = _row_chunk(tm)

    def rotated(n_heads):
        wb = w_ref[...].astype(BF16)
        for r0 in range(0, tm, rc):
            rows = slice(r0, r0 + rc)
            y = jnp.dot(xb_ref[rows, :], wb, preferred_element_type=F32)
            c, a, b = c_ref[rows, :], a_ref[rows, :], b_ref[rows, :]
            for h in range(heads):
                yh = y[:, h * LANES:(h + 1) * LANES]
                if h < n_heads:
                    yh = yh * c + pltpu.roll(yh, shifts[0], 1) * a + pltpu.roll(yh, shifts[1], 1) * b
                o_ref[rows, h * LANES:(h + 1) * LANES] = yh

    pl.when(j < full_tiles)(lambda: rotated(heads))
    if part_heads:
        pl.when(j == full_tiles)(lambda: rotated(part_heads))
    pl.when(j >= full_tiles + (1 if part_heads else 0))(plain)


def _proj(x, w, layer, tm, rot_heads=0, tables=None, shifts=None):
    m = x.shape[0]
    n = w.shape[-1]
    tn = PROJ_TILE_N
    in_specs = [
        pl.BlockSpec((tm, D_MODEL), lambda i, j: (i, 0)),
        pl.BlockSpec((None, D_MODEL, tn), lambda i, j: (layer, 0, j)),
    ]
    args = [x, w]
    if rot_heads:
        tab = pl.BlockSpec((tm, LANES), lambda i, j: (i, 0))
        in_specs += [tab, tab, tab]
        args += list(tables)
    return pl.pallas_call(
        functools.partial(_proj_kernel, rot_heads=rot_heads, shifts=shifts),
        grid=(m // tm, n // tn),
        in_specs=in_specs,
        out_specs=pl.BlockSpec((tm, tn), lambda i, j: (i, j)),
        out_shape=jax.ShapeDtypeStruct((m, n), F32),
        scratch_shapes=[pltpu.VMEM((tm, D_MODEL), BF16)],
        compiler_params=_params("parallel", "arbitrary"),
        name="proj",
    )(*args)


def _out_ln_kernel(att_ref, w_ref, x_ref, g_ref, b_ref, o_ref, y_ref, *, nn, tn):
    n = pl.program_id(1)
    y_ref[n] = jnp.dot(att_ref[...].astype(BF16), w_ref[...].astype(BF16),
                       preferred_element_type=F32)

    @pl.when(n == nn - 1)
    def _():
        cols = [slice(c * tn, (c + 1) * tn) for c in range(nn)]
        total = None
        for c in range(nn):
            z = ALPHA * x_ref[:, cols[c]] + y_ref[c]
            y_ref[c] = z
            part = jnp.sum(z, axis=-1, keepdims=True)
            total = part if total is None else total + part
        mu = total * (1.0 / D_MODEL)
        total = None
        for c in range(nn):
            zc = y_ref[c] - mu
            part = jnp.sum(zc * zc, axis=-1, keepdims=True)
            total = part if total is None else total + part
        rstd = lax.rsqrt(total * (1.0 / D_MODEL) + LN_EPS)
        for c in range(nn):
            o_ref[:, cols[c]] = (y_ref[c] - mu) * rstd * g_ref[:, cols[c]] + b_ref[:, cols[c]]


def _out_ln(att, w_out, layer, x, g, b, row0):
    tm = att.shape[0]
    assert row0 % tm == 0
    tn = OUT_TILE_N
    nn = D_MODEL // tn
    rows = pl.BlockSpec((tm, D_MODEL), lambda i, n: (row0 // tm, 0))
    return pl.pallas_call(
        functools.partial(_out_ln_kernel, nn=nn, tn=tn),
        grid=(1, nn),
        in_specs=[
            pl.BlockSpec((tm, D_MODEL), lambda i, n: (0, 0)),
            pl.BlockSpec((None, D_MODEL, tn), lambda i, n: (layer, 0, n)),
            rows,
            pl.BlockSpec((1, D_MODEL), lambda i, n: (0, 0)),
            pl.BlockSpec((1, D_MODEL), lambda i, n: (0, 0)),
        ],
        out_specs=rows,
        out_shape=jax.ShapeDtypeStruct(x.shape, F32),
        scratch_shapes=[pltpu.VMEM((nn, tm, tn), F32)],
        input_output_aliases={2: 0},
        compiler_params=_params("arbitrary", "arbitrary"),
        name="out_ln",
    )(att, w_out, x, g, b)


def _dot_nt(a, b):
    return lax.dot_general(a, b, (((1,), (1,)), ((), ())), preferred_element_type=F32)


def _out_proj_ln(first, att_ref, w_ref, wb_ref, x_ref, g_ref, b_ref, o_ref):
    @pl.when(first)
    def _():
        wb_ref[...] = w_ref[...].astype(BF16)

    y = jnp.dot(att_ref[...], wb_ref[...], preferred_element_type=F32)
    o_ref[...] = _layer_norm_rows(ALPHA * x_ref[...] + y, g_ref[...], b_ref[...])


def _mem_attend_block(qm_ref, mk_ref, mv_ref, att_ref, col0):
    for h in range(MEM_HEADS):
        sl = slice(h * HEAD_DIM, (h + 1) * HEAD_DIM)
        q = qm_ref[:, sl].astype(BF16)
        s = _dot_nt(q, mk_ref[0, :, sl].astype(BF16)) * ATT_SCALE
        p = jnp.exp(s - jnp.max(s, axis=-1, keepdims=True))
        p = p * (1.0 / jnp.sum(p, axis=-1, keepdims=True))
        o = jnp.dot(p.astype(BF16), mv_ref[0, :, sl].astype(BF16), preferred_element_type=F32)
        att_ref[:, col0 + h * HEAD_DIM:col0 + (h + 1) * HEAD_DIM] = o.astype(att_ref.dtype)


def _swa_prompt_kernel(sink_ref, q_ref, kc_ref, kp_ref, vc_ref, vp_ref, qm_ref, mk_ref, mv_ref,
                       w_ref, x_ref, g_ref, b_ref, o_ref, kt_ref, vt_ref, s_ref, p_ref, att_ref, wb_ref,
                       *, nblk, steps):
    n = pl.program_id(1)
    nq = SWA_Q_HEADS
    w2 = 2 * WINDOW

    def prev_cur(cur_ref, prev_ref, i, sl):
        rows = slice(i * WINDOW, (i + 1) * WINDOW)
        before = prev_ref[:, sl] if i == 0 else cur_ref[(i - 1) * WINDOW:i * WINDOW, sl]
        return jnp.concatenate([before, cur_ref[rows, sl]], axis=0).astype(BF16)

    for i in range(nblk):
        rows = slice(i * WINDOW, (i + 1) * WINDOW)
        for h in range(SWA_KV_HEADS):
            sl = slice(h * HEAD_DIM, (h + 1) * HEAD_DIM)
            q3 = jnp.concatenate(
                [q_ref[rows, (h * SWA_GROUP + g) * HEAD_DIM:(h * SWA_GROUP + g + 1) * HEAD_DIM]
                 for g in range(SWA_GROUP)], axis=0).astype(BF16)
            s = _dot_nt(q3, prev_cur(kc_ref, kp_ref, i, sl)) * ATT_SCALE
            s_ref[i, h * SWA_GROUP:(h + 1) * SWA_GROUP] = s.reshape(SWA_GROUP, WINDOW, w2)
    for h in range(MEM_HEADS):
        sl = slice(h * HEAD_DIM, (h + 1) * HEAD_DIM)
        sm = _dot_nt(qm_ref[:, sl].astype(BF16), mk_ref[0, :, sl].astype(BF16)) * ATT_SCALE
        s_ref[:, nq + h] = sm.reshape(nblk, WINDOW, w2)

    qi = lax.broadcasted_iota(jnp.int32, (WINDOW, w2), 0)
    kj = lax.broadcasted_iota(jnp.int32, (WINDOW, w2), 1)
    sink = sink_ref[...]
    for i in range(nblk):
        first_key = jnp.where(n > 0, qi, WINDOW) if i == 0 else qi
        ok = (kj >= first_key) & (kj <= qi + WINDOW)
        s = jnp.where(ok[None], s_ref[i, 0:nq], NEG_INF)
        m = jnp.maximum(jnp.max(s, axis=-1, keepdims=True), sink)
        p = jnp.exp(s - m)
        den = jnp.sum(p, axis=-1, keepdims=True) + jnp.exp(sink - m)
        p_ref[i, 0:nq] = (p * (1.0 / den)).astype(BF16)
    s = s_ref[:, nq:nq + MEM_HEADS]
    p = jnp.exp(s - jnp.max(s, axis=-1, keepdims=True))
    p_ref[:, nq:nq + MEM_HEADS] = (p * (1.0 / jnp.sum(p, axis=-1, keepdims=True))).astype(BF16)

    for i in range(nblk):
        rows = slice(i * WINDOW, (i + 1) * WINDOW)
        for h in range(SWA_KV_HEADS):
            sl = slice(h * HEAD_DIM, (h + 1) * HEAD_DIM)
            p3 = p_ref[i, h * SWA_GROUP:(h + 1) * SWA_GROUP].reshape(SWA_GROUP * WINDOW, w2)
            o = jnp.dot(p3, prev_cur(vc_ref, vp_ref, i, sl), preferred_element_type=F32)
            for g in range(SWA_GROUP):
                hq = h * SWA_GROUP + g
                att_ref[rows, hq * HEAD_DIM:(hq + 1) * HEAD_DIM] = (
                    o[g * WINDOW:(g + 1) * WINDOW].astype(BF16))
    for h in range(MEM_HEADS):
        sl = slice(h * HEAD_DIM, (h + 1) * HEAD_DIM)
        pm = p_ref[:, nq + h].reshape(nblk * WINDOW, w2)
        om = jnp.dot(pm, mv_ref[0, :, sl].astype(BF16), preferred_element_type=F32)
        att_ref[:, SWA_Q + h * HEAD_DIM:SWA_Q + (h + 1) * HEAD_DIM] = om.astype(BF16)

    @pl.when(n == steps - 1)
    def _():
        last = slice((nblk - 1) * WINDOW, nblk * WINDOW)
        kt_ref[0] = kc_ref[last, :]
        vt_ref[0] = vc_ref[last, :]

    first = jnp.logical_and(pl.program_id(0) == 0, n == 0)
    _out_proj_ln(first, att_ref, w_ref, wb_ref, x_ref, g_ref, b_ref, o_ref)


def _mixer_tail_specs(layer, row_of, rows):
    mem = (None, 1, MEM_LEN, MEM_DIM)
    vec = pl.BlockSpec((1, D_MODEL), lambda b, n: (0, 0))
    mixer = layer // 2
    return [
        pl.BlockSpec(mem, lambda b, n: (layer, b, 0, 0)),
        pl.BlockSpec(mem, lambda b, n: (layer, b, 0, 1)),
        pl.BlockSpec((None, D_MODEL, D_MODEL), lambda b, n: (mixer, 0, 0), pipeline_mode=pl.Buffered(1)),
        pl.BlockSpec((rows, D_MODEL), lambda b, n: (row_of(b, n), 0)),
        vec, vec,
    ]


def _swa_prompt(qkv, bsz, seq, mem_kv, layer, sinks, w_out, x, g, b):
    nblk = ATT_BLOCKS_PER_STEP
    rows = nblk * WINDOW
    steps = seq // rows
    assert seq % rows == 0 and seq >= WINDOW
    tail = pl.BlockSpec((1, WINDOW, SWA_KV), lambda b, n: (b, 0, 0))
    blocks_per_seq = seq // WINDOW
    kcol = SWA_Q // SWA_KV
    vcol = kcol + 1
    mcol = vcol + 1
    row_of = lambda b, n: b * steps + n
    prev_of = lambda b, n: b * blocks_per_seq + jnp.maximum(n * nblk - 1, 0)
    cur = lambda w, col: pl.BlockSpec((rows, w), lambda b, n: (row_of(b, n), col))
    prev = lambda col: pl.BlockSpec((WINDOW, SWA_KV), lambda b, n: (prev_of(b, n), col))
    nheads = SWA_Q_HEADS + MEM_HEADS
    assert MEM_LEN == 2 * WINDOW
    in_specs = [
        pl.BlockSpec((SWA_Q_HEADS, 1, 1), lambda b, n: (0, 0, 0)),
        cur(SWA_Q, 0), cur(SWA_KV, kcol), prev(kcol), cur(SWA_KV, vcol), prev(vcol), cur(MEM_DIM, mcol),
    ] + _mixer_tail_specs(layer, row_of, rows)
    return pl.pallas_call(
        functools.partial(_swa_prompt_kernel, nblk=nblk, steps=steps),
        grid=(bsz, steps),
        in_specs=in_specs,
        out_specs=[pl.BlockSpec((rows, D_MODEL), lambda b, n: (row_of(b, n), 0)), tail, tail],
        out_shape=[jax.ShapeDtypeStruct(x.shape, F32),
                   jax.ShapeDtypeStruct((bsz, WINDOW, SWA_KV), F32),
                   jax.ShapeDtypeStruct((bsz, WINDOW, SWA_KV), F32)],
        scratch_shapes=[pltpu.VMEM((nblk, nheads, WINDOW, 2 * WINDOW), F32),
                        pltpu.VMEM((nblk, nheads, WINDOW, 2 * WINDOW), BF16),
                        pltpu.VMEM((rows, D_MODEL), BF16),
                        pltpu.VMEM((D_MODEL, D_MODEL), BF16)],
        input_output_aliases={len(in_specs) - 3: 0},
        compiler_params=_params("arbitrary", "arbitrary"),
        name="swa_prompt",
    )(sinks.reshape(SWA_Q_HEADS, 1, 1), qkv, qkv, qkv, qkv, qkv, qkv, mem_kv, mem_kv, w_out, x, g, b)


def _head_norm_gate(o, gate):
    mu = jnp.mean(o, axis=-1, keepdims=True)
    oc = o - mu
    var = jnp.mean(oc * oc, axis=-1, keepdims=True)
    return jax.nn.silu(gate) * (oc * lax.rsqrt(var + HEAD_NORM_EPS))


def _ret_prompt_kernel(cdec_ref, decay_ref, qdec_ref, kdec_ref, q_ref, k_ref, v_ref, gate_ref, qm_ref,
                       mk_ref, mv_ref, w_ref, x_ref, g_ref, b_ref, o_ref, s_out_ref,
                       state_ref, att_ref, wb_ref, *, steps, nblk):
    c = pl.program_id(1)

    @pl.when(c == 0)
    def _():
        state_ref[...] = jnp.zeros_like(state_ref)

    for i in range(nblk):
        rows = slice(i * RET_CHUNK, (i + 1) * RET_CHUNK)
        for h in range(RET_HEADS):
            ksl = slice(h * RET_DK, (h + 1) * RET_DK)
            vsl = slice(h * RET_DV, (h + 1) * RET_DV)
            qc = q_ref[rows, ksl]
            kc = k_ref[rows, ksl] * RET_K_SCALE
            vb = v_ref[rows, vsl].astype(BF16)
            st = state_ref[h]
            inner = _dot_nt(qc.astype(BF16), kc.astype(BF16)) * decay_ref[h]
            o = (jnp.dot(inner.astype(BF16), vb, preferred_element_type=F32)
                 + jnp.dot((qc * qdec_ref[h]).astype(BF16), st.astype(BF16),
                           preferred_element_type=F32))
            kd = (kc * kdec_ref[h]).astype(BF16)
            state_ref[h] = cdec_ref[h] * st + lax.dot_general(
                kd, vb, (((0,), (0,)), ((), ())), preferred_element_type=F32)
            att_ref[rows, vsl] = _head_norm_gate(o, gate_ref[rows, vsl]).astype(BF16)
    _mem_attend_block(qm_ref, mk_ref, mv_ref, att_ref, RET_V)

    @pl.when(c == steps - 1)
    def _():
        s_out_ref[0] = state_ref[...]

    first = jnp.logical_and(pl.program_id(0) == 0, c == 0)
    _out_proj_ln(first, att_ref, w_ref, wb_ref, x_ref, g_ref, b_ref, o_ref)


def _drop_carried(kernel, first, count):
    def body(*refs):
        return kernel(*refs[:first], *refs[first + count:])
    return body


def _carry(prev_outputs):
    prev_outputs = [] if prev_outputs is None else list(prev_outputs)
    return [pl.BlockSpec(memory_space=pl.ANY)] * len(prev_outputs), prev_outputs


def _ret_prompt(qkvg, bsz, seq, mem_kv, layer, tables, w_out, x, g, b, carried):
    nblk = ATT_BLOCKS_PER_STEP
    rows = nblk * RET_CHUNK
    steps = seq // rows
    mixer = layer // 2
    decay, qdec, kdec, cdec = tables
    row_of = lambda b, n: b * steps + n
    cur = lambda w, col: pl.BlockSpec((rows, w), lambda b, n: (row_of(b, n), col))
    tab = pl.BlockSpec((RET_HEADS, RET_CHUNK, RET_CHUNK), lambda b, n: (0, 0, 0))
    in_specs = [
        pl.BlockSpec(memory_space=pltpu.SMEM), tab, tab, tab,
        cur(RET_QK, 0), cur(RET_QK, 1), cur(RET_V, 1), cur(RET_V, 2),
        cur(MEM_DIM, (2 * RET_QK + 2 * RET_V) // MEM_DIM),
    ] + _mixer_tail_specs(layer, row_of, rows)
    n_in = len(in_specs)
    carry_specs, carry_args = _carry(carried)
    aliases = {n_in - 3: 0}
    aliases.update({n_in + c: 1 + c for c in range(len(carry_args))})
    return pl.pallas_call(
        _drop_carried(functools.partial(_ret_prompt_kernel, steps=steps, nblk=nblk), n_in, len(carry_args)),
        grid=(bsz, steps),
        in_specs=in_specs + carry_specs,
        out_specs=[
            pl.BlockSpec((rows, D_MODEL), lambda b, n: (row_of(b, n), 0)),
            pl.BlockSpec((None, 1, RET_HEADS, RET_DK, RET_DV), lambda b, n: (mixer, b, 0, 0, 0)),
        ],
        out_shape=[
            jax.ShapeDtypeStruct(x.shape, F32),
            jax.ShapeDtypeStruct((DEPTH // 2, bsz, RET_HEADS, RET_DK, RET_DV), F32),
        ],
        scratch_shapes=[pltpu.VMEM((RET_HEADS, RET_DK, RET_DV), F32),
                        pltpu.VMEM((rows, D_MODEL), BF16),
                        pltpu.VMEM((D_MODEL, D_MODEL), BF16)],
        input_output_aliases=aliases,
        compiler_params=_params("arbitrary", "arbitrary"),
        name="ret_prompt",
    )(cdec, decay, qdec, kdec, qkvg, qkvg, qkvg, qkvg, qkvg, mem_kv, mem_kv, w_out, x, g, b, *carry_args)


def _mem_attend_row(q, mk_ref, mv_ref):
    s = jnp.sum(mk_ref[0] * q[None], axis=-1, keepdims=True) * ATT_SCALE
    p = jnp.exp(s - jnp.max(s, axis=0, keepdims=True))
    p = p * (1.0 / jnp.sum(p, axis=0, keepdims=True))
    return jnp.sum(p * mv_ref[0], axis=0)


def _swa_step_kernel(sink_ref, rows_ref, kbuf_ref, vbuf_ref, mk_ref, mv_ref, o_ref, nk_ref, nv_ref):
    krow0 = SWA_Q_HEADS
    vrow0 = krow0 + SWA_KV_HEADS
    mrow0 = vrow0 + SWA_KV_HEADS
    wb = kbuf_ref.shape[1]
    kb = kbuf_ref[0]
    vb = vbuf_ref[0]
    k_new = rows_ref[0, krow0:krow0 + SWA_KV_HEADS, :]
    v_new = rows_ref[0, vrow0:vrow0 + SWA_KV_HEADS, :]
    for g in range(SWA_GROUP):
        group_rows = pl.ds(g, SWA_KV_HEADS, stride=SWA_GROUP)
        q = rows_ref[0, group_rows, :]
        s_buf = jnp.sum(kb * q[None], axis=-1, keepdims=True) * ATT_SCALE
        s_new = jnp.sum(k_new * q, axis=-1, keepdims=True) * ATT_SCALE
        sink = sink_ref[g]
        m = jnp.maximum(jnp.maximum(jnp.max(s_buf, axis=0), s_new), sink)
        p_buf = jnp.exp(s_buf - m[None])
        p_new = jnp.exp(s_new - m)
        inv = 1.0 / (jnp.sum(p_buf, axis=0) + p_new + jnp.exp(sink - m))
        o_ref[0, group_rows, :] = jnp.sum((p_buf * inv[None]) * vb, axis=0) + (p_new * inv) * v_new
    nk_ref[0, 0:wb - 1] = kbuf_ref[0, 1:wb]
    nv_ref[0, 0:wb - 1] = vbuf_ref[0, 1:wb]
    nk_ref[0, wb - 1] = k_new
    nv_ref[0, wb - 1] = v_new
    o_ref[0, SWA_Q_HEADS:SWA_Q_HEADS + MEM_HEADS, :] = _mem_attend_row(
        rows_ref[0, mrow0:mrow0 + MEM_HEADS, :], mk_ref, mv_ref)


def _swa_step(rows, cache_k, cache_v, j, mem_k, mem_v, i, sinks, carried):
    bsz = rows.shape[0]
    wb = cache_k.shape[2]
    cache = pl.BlockSpec((None, 1, wb, SWA_KV_HEADS, HEAD_DIM), lambda b: (j, b, 0, 0, 0))
    mem = pl.BlockSpec((None, 1, MEM_LEN, MEM_HEADS, HEAD_DIM), lambda b: (i, b, 0, 0, 0))
    nrows = D_MODEL // HEAD_DIM
    sink_gk = sinks.reshape(SWA_KV_HEADS, SWA_GROUP).T.reshape(SWA_GROUP, SWA_KV_HEADS, 1)
    in_specs = [
        pl.BlockSpec(sink_gk.shape, lambda b: (0, 0, 0)),
        pl.BlockSpec((1,) + rows.shape[1:], lambda b: (b, 0, 0)),
        cache, cache, mem, mem,
    ]
    n_in = len(in_specs)
    carry_specs, carry_args = _carry(carried)
    return pl.pallas_call(
        _drop_carried(_swa_step_kernel, n_in, len(carry_args)),
        grid=(bsz,),
        in_specs=in_specs + carry_specs,
        out_specs=[pl.BlockSpec((1, nrows, HEAD_DIM), lambda b: (b, 0, 0)), cache, cache],
        out_shape=[
            jax.ShapeDtypeStruct((bsz, nrows, HEAD_DIM), F32),
            jax.ShapeDtypeStruct(cache_k.shape, F32),
            jax.ShapeDtypeStruct(cache_v.shape, F32),
        ],
        input_output_aliases={n_in + c: 1 + c for c in range(len(carry_args))},
        compiler_params=_params("parallel"),
        name="swa_step",
    )(sink_gk, rows, cache_k, cache_v, mem_k, mem_v, *carry_args)


def _ret_step_kernel(dec_ref, rows_ref, cols_ref, s_ref, mk_ref, mv_ref, o_ref, s_out_ref):
    krow0 = RET_HEADS
    vrow0 = 2 * RET_HEADS
    grow0 = vrow0 + 2 * RET_HEADS
    mrow0 = grow0 + 2 * RET_HEADS
    for h in range(RET_HEADS):
        q_row = rows_ref[0, h:h + 1, :]
        k_row = rows_ref[0, krow0 + h:krow0 + h + 1, :] * RET_K_SCALE
        q_col = cols_ref[0, :, h:h + 1]
        k_col = cols_ref[0, :, krow0 + h:krow0 + h + 1] * RET_K_SCALE
        inner = jnp.sum(q_row * k_row, axis=1, keepdims=True) * dec_ref[0, h]
        qd = q_col * dec_ref[1, h]
        kd = k_col * dec_ref[2, h]
        halves = []
        for t in range(2):
            lsl = slice(t * HEAD_DIM, (t + 1) * HEAD_DIM)
            v = rows_ref[0, vrow0 + 2 * h + t:vrow0 + 2 * h + t + 1, :]
            st = s_ref[0, h, :, lsl]
            halves.append(inner * v + jnp.sum(qd * st, axis=0, keepdims=True))
            s_out_ref[0, h, :, lsl] = dec_ref[3, h] * st + kd * v
        mu = (jnp.sum(halves[0], axis=1, keepdims=True)
              + jnp.sum(halves[1], axis=1, keepdims=True)) * (1.0 / RET_DV)
        cen = [o - mu for o in halves]
        var = (jnp.sum(cen[0] * cen[0], axis=1, keepdims=True)
               + jnp.sum(cen[1] * cen[1], axis=1, keepdims=True)) * (1.0 / RET_DV)
        rstd = lax.rsqrt(var + HEAD_NORM_EPS)
        for t in range(2):
            gate = rows_ref[0, grow0 + 2 * h + t:grow0 + 2 * h + t + 1, :]
            o_ref[0, 2 * h + t:2 * h + t + 1, :] = jax.nn.silu(gate) * (cen[t] * rstd)
    o_ref[0, 2 * RET_HEADS:2 * RET_HEADS + MEM_HEADS, :] = _mem_attend_row(
        rows_ref[0, mrow0:mrow0 + MEM_HEADS, :], mk_ref, mv_ref)


def _ret_step(rows, cols, state, j, mem_k, mem_v, i, dec, carried):
    bsz = rows.shape[0]
    st = pl.BlockSpec((None, 1, RET_HEADS, RET_DK, RET_DV), lambda b: (j, b, 0, 0, 0))
    mem = pl.BlockSpec((None, 1, MEM_LEN, MEM_HEADS, HEAD_DIM), lambda b: (i, b, 0, 0, 0))
    nrows = D_MODEL // HEAD_DIM
    in_specs = [
        pl.BlockSpec(memory_space=pltpu.SMEM),
        pl.BlockSpec((1,) + rows.shape[1:], lambda b: (b, 0, 0)),
        pl.BlockSpec((1,) + cols.shape[1:], lambda b: (b, 0, 0)),
        st, mem, mem,
    ]
    n_in = len(in_specs)
    carry_specs, carry_args = _carry(carried)
    return pl.pallas_call(
        _drop_carried(_ret_step_kernel, n_in, len(carry_args)),
        grid=(bsz,),
        in_specs=in_specs + carry_specs,
        out_specs=[pl.BlockSpec((1, nrows, HEAD_DIM), lambda b: (b, 0, 0)), st],
        out_shape=[
            jax.ShapeDtypeStruct((bsz, nrows, HEAD_DIM), F32),
            jax.ShapeDtypeStruct(state.shape, F32),
        ],
        input_output_aliases={n_in + c: 1 + c for c in range(len(carry_args))},
        compiler_params=_params("parallel"),
        name="ret_step",
    )(dec, rows, cols, state, mem_k, mem_v, *carry_args)


def _rope_tables(pos):
    half = ROPE_DIM // 2
    inv = ROPE_THETA ** (-jnp.arange(half, dtype=F32) / half)
    ang = pos.astype(F32)[:, None] * inv[None, :]
    cos, sin = jnp.cos(ang), jnp.sin(ang)
    n = pos.shape[0]
    rest = HEAD_DIM - ROPE_DIM
    c = jnp.concatenate([cos, cos, jnp.ones((n, rest), F32)], axis=-1)
    a = jnp.concatenate([-sin, jnp.zeros((n, HEAD_DIM - half), F32)], axis=-1)
    b = jnp.concatenate([jnp.zeros((n, half), F32), sin, jnp.zeros((n, rest), F32)], axis=-1)
    return (c, a, b), (HEAD_DIM - half, half)


def _ret_rot_tables(pos):
    half = RET_DK // 2
    angle = RET_ROT_BASE ** (-jnp.linspace(0.0, 1.0, half, dtype=F32))
    ang = pos.astype(F32)[:, None] * angle[None, :]
    cos, sin = jnp.cos(ang), jnp.sin(ang)
    n = pos.shape[0]
    zero = jnp.zeros_like(sin)
    c = jnp.stack([cos, cos], axis=-1).reshape(n, RET_DK)
    a = jnp.stack([-sin, zero], axis=-1).reshape(n, RET_DK)
    b = jnp.stack([zero, sin], axis=-1).reshape(n, RET_DK)
    return (c, a, b), (RET_DK - 1, 1)


def _ret_decay(chunk):
    log_g = jnp.log1p(-jnp.exp2(-5.0 - jnp.arange(RET_HEADS, dtype=F32)))
    n = jnp.arange(chunk, dtype=F32)
    rel = n[:, None] - n[None, :]
    decay = jnp.where(rel >= 0, jnp.exp(jnp.maximum(rel, 0.0) * log_g[:, None, None]), 0.0)
    q_dec = jnp.exp((n + 1.0) * log_g[:, None])
    k_dec = jnp.exp((chunk - 1.0 - n) * log_g[:, None])
    c_dec = jnp.exp(chunk * log_g)
    return decay, q_dec, k_dec, c_dec


def kernel(x_prompt, x_sample, cache_swa_k, cache_swa_v, state_ret, cache_mem_k, cache_mem_v,
           mem_prompt, ln_g, ln_b, ffn_w_gu, ffn_w_down, w_mem_kv, swa_w_in, swa_w_out,
           swa_sinks, ret_w_in, ret_w_out):
    bp, seq, _ = x_prompt.shape
    bs, sample_seq, _ = x_sample.shape
    assert sample_seq == 1
    mp, ms = bp * seq, bs
    assert mp % ms == 0

    mem2 = mem_prompt.reshape(bp * MEM_LEN, D_MODEL)
    mem_kv = jnp.stack([_proj(mem2, w_mem_kv, l, bp * MEM_LEN) for l in range(DEPTH)])
    mem_kv = mem_kv.reshape(DEPTH, bp, MEM_LEN, 2 * MEM_DIM)
    mem_k_prompt = mem_kv[..., :MEM_DIM].reshape(DEPTH, bp, MEM_LEN, MEM_HEADS, HEAD_DIM)
    mem_v_prompt = mem_kv[..., MEM_DIM:].reshape(DEPTH, bp, MEM_LEN, MEM_HEADS, HEAD_DIM)

    tm, m_all = _merged_rows(mp, ms)
    tiles = m_all // tm
    pad = m_all - mp - ms
    tail = (mp - (tiles - 1) * tm, ms)
    assert tail[0] >= 0 and tail[0] % 8 == 0 and ms % 8 == 0 and tail[0] + ms <= tm
    def per_row(table):
        return jnp.concatenate([jnp.tile(table[:seq], (bp, 1)),
                                jnp.broadcast_to(table[seq:], (ms, table.shape[1])),
                                jnp.zeros((pad, table.shape[1]), F32)], axis=0)

    pos = jnp.concatenate([jnp.arange(seq, dtype=jnp.int32), jnp.full((1,), PAST_LEN, jnp.int32)])
    rope_tabs, rope_shifts = _rope_tables(pos)
    rot_tabs, rot_shifts = _ret_rot_tables(pos)
    rope_tabs = tuple(per_row(t) for t in rope_tabs)
    rot_tabs = tuple(per_row(t) for t in rot_tabs)
    decay, q_dec, k_dec, c_dec = _ret_decay(RET_CHUNK)
    ret_tabs = (decay, jnp.broadcast_to(q_dec[:, :, None], decay.shape),
                jnp.broadcast_to(k_dec[:, :, None], decay.shape), c_dec)
    decay, q_dec, k_dec, c_dec = _ret_decay(sample_seq)
    step_dec = jnp.stack([decay[:, 0, 0], q_dec[:, 0], k_dec[:, 0], c_dec])

    def ln(i, s):
        return ln_g[i, s].reshape(1, D_MODEL), ln_b[i, s].reshape(1, D_MODEL)

    swa_k_prompt, swa_v_prompt = [], []
    swa_sample = None
    ret_prompt = None
    ret_sample = None
    for i in range(DEPTH):
        j = i // 2
        if i == 0:
            x = _ffn(x_prompt.reshape(mp, D_MODEL), ffn_w_gu, ffn_w_down, i, 0, *ln(i, 0), tm, tiles, tail,
                     x_sample=x_sample.reshape(ms, D_MODEL))
        else:
            x = _ffn(x, ffn_w_gu, ffn_w_down, i, 0, *ln(i, 0), tm, tiles, tail)
        if i % 2 == 0:
            qkv = _proj(x, swa_w_in, j, tm, rot_heads=SWA_Q_HEADS + SWA_KV_HEADS, tables=rope_tabs,
                        shifts=rope_shifts)
            x, k_tail, v_tail = _swa_prompt(qkv, bp, seq, mem_kv, i, swa_sinks[j], swa_w_out, x, *ln(i, 1))
            swa_k_prompt.append(k_tail.reshape(bp, WINDOW, SWA_KV_HEADS, HEAD_DIM))
            swa_v_prompt.append(v_tail.reshape(bp, WINDOW, SWA_KV_HEADS, HEAD_DIM))
            rows = qkv[mp:mp + ms].reshape(ms, SWA_IN_WIDTH // HEAD_DIM, HEAD_DIM)
            att, *swa_sample = _swa_step(rows, cache_swa_k, cache_swa_v, j, cache_mem_k, cache_mem_v, i,
                                         swa_sinks[j], swa_sample)
            x = _out_ln(att.reshape(ms, D_MODEL), swa_w_out, j, x, *ln(i, 1), mp)
        else:
            qkvg = _proj(x, ret_w_in, j, tm, rot_heads=2 * RET_HEADS, tables=rot_tabs,
                         shifts=rot_shifts)
            x, ret_prompt = _ret_prompt(qkvg, bp, seq, mem_kv, i, ret_tabs, ret_w_out, x, *ln(i, 1),
                                        None if ret_prompt is None else [ret_prompt])
            rows = qkvg[mp:mp + ms].reshape(ms, RET_IN_WIDTH // HEAD_DIM, HEAD_DIM)
            att, ret_sample = _ret_step(rows, jnp.swapaxes(rows, 1, 2), state_ret, j, cache_mem_k,
                                        cache_mem_v, i, step_dec,
                                        None if ret_sample is None else [ret_sample])
            x = _out_ln(att.reshape(ms, D_MODEL), ret_w_out, j, x, *ln(i, 1), mp)
        x = _ffn(x, ffn_w_gu, ffn_w_down, i, 1, *ln(i, 2), tm, tiles, tail, split_out=(i == DEPTH - 1))

    y_prompt = x[0].reshape(bp, seq, D_MODEL)
    y_sample = x[1].reshape(bs, sample_seq, D_MODEL)
    return (y_prompt, y_sample, jnp.stack(swa_k_prompt), jnp.stack(swa_v_prompt), swa_sample[0],
            swa_sample[1], ret_prompt, ret_sample, mem_k_prompt, mem_v_prompt)
```

```python
import functools

import jax
import jax.numpy as jnp
from jax import lax
from jax.experimental import pallas as pl
from jax.experimental.pallas import tpu as pltpu

F32 = jnp.float32
BF16 = jnp.bfloat16

D_MODEL = 2048
DEPTH = 4
PAST_LEN = 16384
HEAD_DIM = 128
MEM_LEN = 256
MEM_HEADS = 4
MEM_DIM = MEM_HEADS * HEAD_DIM
SELF_WIDTH = D_MODEL - MEM_DIM
SWA_Q_HEADS = SELF_WIDTH // HEAD_DIM
SWA_KV_HEADS = SWA_Q_HEADS // 3
SWA_GROUP = SWA_Q_HEADS // SWA_KV_HEADS
WINDOW = 128
ROPE_THETA = 500000.0
ROPE_DIM = HEAD_DIM // 4
RET_DK = 128
RET_DV = 2 * RET_DK
RET_HEADS = SELF_WIDTH // RET_DV
RET_CHUNK = 128
RET_ROT_BASE = 10000.0
D_FF = ((8 * D_MODEL // 3 + 255) // 256) * 256
LN_EPS = 1e-5
HEAD_NORM_EPS = 1e-6
ALPHA = (2.0 * DEPTH) ** 0.25
NEG_INF = -1e30
ATT_SCALE = HEAD_DIM ** -0.5
RET_K_SCALE = RET_DK ** -0.5

SWA_Q = SWA_Q_HEADS * HEAD_DIM
SWA_KV = SWA_KV_HEADS * HEAD_DIM
SWA_IN_WIDTH = SWA_Q + 2 * SWA_KV + MEM_DIM
RET_QK = RET_HEADS * RET_DK
RET_V = RET_HEADS * RET_DV
RET_IN_WIDTH = 2 * RET_QK + 2 * RET_V + MEM_DIM

VMEM_LIMIT_BYTES = 58 * 1024 * 1024
LANES = 128
FFN_TILE_K = 256
ROW_TILE = 1024
ATT_BLOCKS_PER_STEP = 2
PROJ_TILE_N = 1024
OUT_TILE_N = 512
MAX_ROW_CHUNK = 256
BF16_SUBLANES = 16


def _merged_rows(prompt_rows, sample_rows):
    tiles = max(prompt_rows // ROW_TILE, 1)
    per_tile = -(-(prompt_rows + sample_rows) // tiles)
    tm = -(-per_tile // BF16_SUBLANES) * BF16_SUBLANES
    return tm, tiles * tm


def _row_chunk(tm):
    for c in range(MAX_ROW_CHUNK, BF16_SUBLANES - 1, -BF16_SUBLANES):
        if tm % c == 0:
            return c
    return tm


def _params(*sem):
    return pltpu.CompilerParams(dimension_semantics=sem, vmem_limit_bytes=VMEM_LIMIT_BYTES)


def _layer_norm_rows(z, g, b):
    mu = jnp.mean(z, axis=-1, keepdims=True)
    zc = z - mu
    var = jnp.mean(zc * zc, axis=-1, keepdims=True)
    return zc * lax.rsqrt(var + LN_EPS) * g + b


def _ffn_kernel(*refs, nk, tk, chunk, tiles, tail, split_in, split_out):
    refs = list(refs)
    x_ref = refs.pop(0)
    xs_ref = refs.pop(0) if split_in else None
    wg_ref, wu_ref, wd_ref, g_ref, b_ref, o_ref = refs[:6]
    os_ref = refs[6] if split_out else None
    xb_ref, wgu_ref = refs[-2:]
    i = pl.program_id(0)
    k = pl.program_id(1)
    tm = o_ref.shape[0]
    chunks = [slice(r, r + chunk) for r in range(0, tm, chunk)]

    def load_x(rows):
        xv = x_ref[rows, :]
        if not split_in or rows.stop <= tail[0]:
            return xv
        first, count = tail
        lo, hi = rows.start, rows.stop
        pieces = []
        if lo < first:
            pieces.append(xv[:first - lo])
        s0, s1 = max(lo, first), min(hi, first + count)
        if s0 < s1:
            pieces.append(xs_ref[s0 - first:s1 - first, :])
        if hi > first + count:
            pieces.append(jnp.zeros((hi - max(lo, first + count), xv.shape[1]), xv.dtype))
        return jnp.where(i == tiles - 1, jnp.concatenate(pieces, axis=0), xv)

    def cast_weights():
        wgu_ref[:, :tk] = wg_ref[...].astype(BF16)
        wgu_ref[:, tk:] = wu_ref[...].astype(BF16)
        return wd_ref[...].astype(BF16)

    def down(xb, wd):
        gu = jnp.dot(xb, wgu_ref[...], preferred_element_type=F32)
        h = (jax.nn.silu(gu[:, :tk]) * gu[:, tk:]).astype(BF16)
        return jnp.dot(h, wd, preferred_element_type=F32)

    @pl.when(k == 0)
    def _():
        wd = cast_weights()
        for rows in chunks:
            xb = load_x(rows).astype(BF16)
            xb_ref[rows, :] = xb
            o_ref[rows, :] = down(xb, wd)

    @pl.when(jnp.logical_and(k > 0, k < nk - 1))
    def _():
        wd = cast_weights()
        o_ref[...] += down(xb_ref[...], wd)

    @pl.when(k == nk - 1)
    def _():
        wd = cast_weights()
        for rows in chunks:
            acc = o_ref[rows, :] + down(xb_ref[rows, :], wd)
            z = ALPHA * load_x(rows) + 0.5 * acc
            o_ref[rows, :] = _layer_norm_rows(z, g_ref[...], b_ref[...])
        if split_out:
            @pl.when(i == tiles - 1)
            def _():
                os_ref[...] = o_ref[tail[0]:tail[0] + tail[1], :]


def _ffn(x, w_gu, w_down, layer, slot, g, b, tm, tiles, tail, x_sample=None, split_out=False):
    tk = FFN_TILE_K
    nk = D_FF // tk
    assert nk >= 2
    split_in = x_sample is not None
    rows = pl.BlockSpec((tm, D_MODEL), lambda i, k: (i, 0))
    small = pl.BlockSpec((tail[1], D_MODEL), lambda i, k: (0, 0))
    in_specs = [pl.BlockSpec((tm, D_MODEL), lambda i, k: (i, 0), pipeline_mode=pl.Buffered(1))]
    args = [x]
    if split_in:
        in_specs.append(small)
        args.append(x_sample)
    in_specs += [
        pl.BlockSpec((None, None, D_MODEL, tk), lambda i, k: (layer, slot, 0, k)),
        pl.BlockSpec((None, None, D_MODEL, tk), lambda i, k: (layer, slot, 0, nk + k)),
        pl.BlockSpec((None, None, tk, D_MODEL), lambda i, k: (layer, slot, k, 0)),
        pl.BlockSpec((1, D_MODEL), lambda i, k: (0, 0)),
        pl.BlockSpec((1, D_MODEL), lambda i, k: (0, 0)),
    ]
    if split_out:
        prompt_rows = (tiles - 1) * tm + tail[0]
        out_specs = [rows, small]
        out_shape = [jax.ShapeDtypeStruct((prompt_rows, D_MODEL), F32),
                     jax.ShapeDtypeStruct((tail[1], D_MODEL), F32)]
    else:
        out_specs = rows
        out_shape = jax.ShapeDtypeStruct((tiles * tm, D_MODEL), F32)
    return pl.pallas_call(
        functools.partial(_ffn_kernel, nk=nk, tk=tk, chunk=_row_chunk(tm), tiles=tiles, tail=tail,
                          split_in=split_in, split_out=split_out),
        grid=(tiles, nk),
        in_specs=in_specs,
        out_specs=out_specs,
        out_shape=out_shape,
        scratch_shapes=[pltpu.VMEM((tm, D_MODEL), BF16), pltpu.VMEM((D_MODEL, 2 * tk), BF16)],
        compiler_params=_params("arbitrary", "arbitrary"),
        name="ffn_ln",
    )(*args, w_gu, w_gu, w_down, g, b)


def _proj_kernel(x_ref, w_ref, *rest, rot_heads, shifts):
    if rot_heads:
        c_ref, a_ref, b_ref, o_ref, xb_ref = rest
    else:
        o_ref, xb_ref = rest
    j = pl.program_id(1)

    @pl.when(j == 0)
    def _():
        xb_ref[...] = x_ref[...].astype(BF16)

    def plain():
        o_ref[...] = jnp.dot(xb_ref[...], w_ref[...].astype(BF16), preferred_element_type=F32)

    if not rot_heads:
        plain()
        return

    tm, tn = o_ref.shape
    heads = tn // LANES
    full_tiles, part_heads = divmod(rot_heads, heads)
    rc = _row_chunk(tm)

    def rotated(n_heads):
        wb = w_ref[...].astype(BF16)
        for r0 in range(0, tm, rc):
            rows = slice(r0, r0 + rc)
            y = jnp.dot(xb_ref[rows, :], wb, preferred_element_type=F32)
            c, a, b = c_ref[rows, :], a_ref[rows, :], b_ref[rows, :]
            for h in range(heads):
                yh = y[:, h * LANES:(h + 1) * LANES]
                if h < n_heads:
                    yh = yh * c + pltpu.roll(yh, shifts[0], 1) * a + pltpu.roll(yh, shifts[1], 1) * b
                o_ref[rows, h * LANES:(h + 1) * LANES] = yh

    pl.when(j < full_tiles)(lambda: rotated(heads))
    if part_heads:
        pl.when(j == full_tiles)(lambda: rotated(part_heads))
    pl.when(j >= full_tiles + (1 if part_heads else 0))(plain)


def _proj(x, w, layer, tm, rot_heads=0, tables=None, shifts=None):
    m = x.shape[0]
    n = w.shape[-1]
    tn = PROJ_TILE_N
    in_specs = [
        pl.BlockSpec((tm, D_MODEL), lambda i, j: (i, 0)),
        pl.BlockSpec((None, D_MODEL, tn), lambda i, j: (layer, 0, j)),
    ]
    args = [x, w]
    if rot_heads:
        tab = pl.BlockSpec((tm, LANES), lambda i, j: (i, 0))
        in_specs += [tab, tab, tab]
        args += list(tables)
    return pl.pallas_call(
        functools.partial(_proj_kernel, rot_heads=rot_heads, shifts=shifts),
        grid=(m // tm, n // tn),
        in_specs=in_specs,
        out_specs=pl.BlockSpec((tm, tn), lambda i, j: (i, j)),
        out_shape=jax.ShapeDtypeStruct((m, n), F32),
        scratch_shapes=[pltpu.VMEM((tm, D_MODEL), BF16)],
        compiler_params=_params("parallel", "arbitrary"),
        name="proj",
    )(*args)


def _mem_kv_kernel(x_ref, w_ref, o_ref, xb_ref):
    @pl.when(pl.program_id(0) == 0)
    def _():
        xb_ref[...] = x_ref[...].astype(BF16)

    o_ref[...] = jnp.dot(xb_ref[...], w_ref[...].astype(BF16), preferred_element_type=F32)


def _mem_kv(mem, w):
    rows = mem.shape[0]
    layers, _, n = w.shape
    return pl.pallas_call(
        _mem_kv_kernel,
        grid=(layers,),
        in_specs=[
            pl.BlockSpec((rows, D_MODEL), lambda l: (0, 0)),
            pl.BlockSpec((None, D_MODEL, n), lambda l: (l, 0, 0)),
        ],
        out_specs=pl.BlockSpec((None, rows, n), lambda l: (l, 0, 0)),
        out_shape=jax.ShapeDtypeStruct((layers, rows, n), F32),
        scratch_shapes=[pltpu.VMEM((rows, D_MODEL), BF16)],
        compiler_params=_params("arbitrary"),
        name="mem_kv",
    )(mem, w)


def _out_ln_kernel(att_ref, w_ref, x_ref, g_ref, b_ref, o_ref, y_ref, *, nn, tn):
    n = pl.program_id(1)
    y_ref[n] = jnp.dot(att_ref[...].astype(BF16), w_ref[...].astype(BF16),
                       preferred_element_type=F32)

    @pl.when(n == nn - 1)
    def _():
        cols = [slice(c * tn, (c + 1) * tn) for c in range(nn)]
        total = None
        for c in range(nn):
            z = ALPHA * x_ref[:, cols[c]] + y_ref[c]
            y_ref[c] = z
            part = jnp.sum(z, axis=-1, keepdims=True)
            total = part if total is None else total + part
        mu = total * (1.0 / D_MODEL)
        total = None
        for c in range(nn):
            zc = y_ref[c] - mu
            part = jnp.sum(zc * zc, axis=-1, keepdims=True)
            total = part if total is None else total + part
        rstd = lax.rsqrt(total * (1.0 / D_MODEL) + LN_EPS)
        for c in range(nn):
            o_ref[:, cols[c]] = (y_ref[c] - mu) * rstd * g_ref[:, cols[c]] + b_ref[:, cols[c]]


def _out_ln(att, w_out, layer, x, g, b, row0):
    tm = att.shape[0]
    assert row0 % tm == 0
    tn = OUT_TILE_N
    nn = D_MODEL // tn
    rows = pl.BlockSpec((tm, D_MODEL), lambda i, n: (row0 // tm, 0))
    return pl.pallas_call(
        functools.partial(_out_ln_kernel, nn=nn, tn=tn),
        grid=(1, nn),
        in_specs=[
            pl.BlockSpec((tm, D_MODEL), lambda i, n: (0, 0)),
            pl.BlockSpec((None, D_MODEL, tn), lambda i, n: (layer, 0, n)),
            rows,
            pl.BlockSpec((1, D_MODEL), lambda i, n: (0, 0)),
            pl.BlockSpec((1, D_MODEL), lambda i, n: (0, 0)),
        ],
        out_specs=rows,
        out_shape=jax.ShapeDtypeStruct(x.shape, F32),
        scratch_shapes=[pltpu.VMEM((nn, tm, tn), F32)],
        input_output_aliases={2: 0},
        compiler_params=_params("arbitrary", "arbitrary"),
        name="out_ln",
    )(att, w_out, x, g, b)


def _dot_nt(a, b):
    return lax.dot_general(a, b, (((1,), (1,)), ((), ())), preferred_element_type=F32)


def _out_proj_ln(first, att_ref, w_ref, wb_ref, x_ref, g_ref, b_ref, o_ref):
    @pl.when(first)
    def _():
        wb_ref[...] = w_ref[...].astype(BF16)

    y = jnp.dot(att_ref[...], wb_ref[...], preferred_element_type=F32)
    o_ref[...] = _layer_norm_rows(ALPHA * x_ref[...] + y, g_ref[...], b_ref[...])


def _mem_attend_block(qm_ref, mk_ref, mv_ref, att_ref, col0):
    for h in range(MEM_HEADS):
        sl = slice(h * HEAD_DIM, (h + 1) * HEAD_DIM)
        q = qm_ref[:, sl].astype(BF16)
        s = _dot_nt(q, mk_ref[0, :, sl].astype(BF16)) * ATT_SCALE
        p = jnp.exp(s - jnp.max(s, axis=-1, keepdims=True))
        p = p * (1.0 / jnp.sum(p, axis=-1, keepdims=True))
        o = jnp.dot(p.astype(BF16), mv_ref[0, :, sl].astype(BF16), preferred_element_type=F32)
        att_ref[:, col0 + h * HEAD_DIM:col0 + (h + 1) * HEAD_DIM] = o.astype(att_ref.dtype)


def _swa_prompt_kernel(sink_ref, q_ref, kc_ref, kp_ref, vc_ref, vp_ref, qm_ref, mk_ref, mv_ref,
                       w_ref, x_ref, g_ref, b_ref, o_ref, kt_ref, vt_ref, s_ref, p_ref, att_ref, wb_ref,
                       *, nblk, steps):
    n = pl.program_id(1)
    nq = SWA_Q_HEADS
    w2 = 2 * WINDOW

    def prev_cur(cur_ref, prev_ref, i, sl):
        rows = slice(i * WINDOW, (i + 1) * WINDOW)
        before = prev_ref[:, sl] if i == 0 else cur_ref[(i - 1) * WINDOW:i * WINDOW, sl]
        return jnp.concatenate([before, cur_ref[rows, sl]], axis=0).astype(BF16)

    for i in range(nblk):
        rows = slice(i * WINDOW, (i + 1) * WINDOW)
        for h in range(SWA_KV_HEADS):
            sl = slice(h * HEAD_DIM, (h + 1) * HEAD_DIM)
            q3 = jnp.concatenate(
                [q_ref[rows, (h * SWA_GROUP + g) * HEAD_DIM:(h * SWA_GROUP + g + 1) * HEAD_DIM]
                 for g in range(SWA_GROUP)], axis=0).astype(BF16)
            s = _dot_nt(q3, prev_cur(kc_ref, kp_ref, i, sl)) * ATT_SCALE
            s_ref[i, h * SWA_GROUP:(h + 1) * SWA_GROUP] = s.reshape(SWA_GROUP, WINDOW, w2)
    for h in range(MEM_HEADS):
        sl = slice(h * HEAD_DIM, (h + 1) * HEAD_DIM)
        sm = _dot_nt(qm_ref[:, sl].astype(BF16), mk_ref[0, :, sl].astype(BF16)) * ATT_SCALE
        s_ref[:, nq + h] = sm.reshape(nblk, WINDOW, w2)

    qi = lax.broadcasted_iota(jnp.int32, (WINDOW, w2), 0)
    kj = lax.broadcasted_iota(jnp.int32, (WINDOW, w2), 1)
    sink = sink_ref[...]
    for i in range(nblk):
        first_key = jnp.where(n > 0, qi, WINDOW) if i == 0 else qi
        ok = (kj >= first_key) & (kj <= qi + WINDOW)
        s = jnp.where(ok[None], s_ref[i, 0:nq], NEG_INF)
        m = jnp.maximum(jnp.max(s, axis=-1, keepdims=True), sink)
        p = jnp.exp(s - m)
        den = jnp.sum(p, axis=-1, keepdims=True) + jnp.exp(sink - m)
        p_ref[i, 0:nq] = (p * (1.0 / den)).astype(BF16)
    s = s_ref[:, nq:nq + MEM_HEADS]
    p = jnp.exp(s - jnp.max(s, axis=-1, keepdims=True))
    p_ref[:, nq:nq + MEM_HEADS] = (p * (1.0 / jnp.sum(p, axis=-1, keepdims=True))).astype(BF16)

    for i in range(nblk):
        rows = slice(i * WINDOW, (i + 1) * WINDOW)
        for h in range(SWA_KV_HEADS):
            sl = slice(h * HEAD_DIM, (h + 1) * HEAD_DIM)
            p3 = p_ref[i, h * SWA_GROUP:(h + 1) * SWA_GROUP].reshape(SWA_GROUP * WINDOW, w2)
            o = jnp.dot(p3, prev_cur(vc_ref, vp_ref, i, sl), preferred_element_type=F32)
            for g in range(SWA_GROUP):
                hq = h * SWA_GROUP + g
                att_ref[rows, hq * HEAD_DIM:(hq + 1) * HEAD_DIM] = (
                    o[g * WINDOW:(g + 1) * WINDOW].astype(BF16))
    for h in range(MEM_HEADS):
        sl = slice(h * HEAD_DIM, (h + 1) * HEAD_DIM)
        pm = p_ref[:, nq + h].reshape(nblk * WINDOW, w2)
        om = jnp.dot(pm, mv_ref[0, :, sl].astype(BF16), preferred_element_type=F32)
        att_ref[:, SWA_Q + h * HEAD_DIM:SWA_Q + (h + 1) * HEAD_DIM] = om.astype(BF16)

    @pl.when(n == steps - 1)
    def _():
        last = slice((nblk - 1) * WINDOW, nblk * WINDOW)
        kt_ref[0] = kc_ref[last, :]
        vt_ref[0] = vc_ref[last, :]

    first = jnp.logical_and(pl.program_id(0) == 0, n == 0)
    _out_proj_ln(first, att_ref, w_ref, wb_ref, x_ref, g_ref, b_ref, o_ref)


def _mixer_tail_specs(layer, row_of, rows):
    mem = (None, 1, MEM_LEN, MEM_DIM)
    vec = pl.BlockSpec((1, D_MODEL), lambda b, n: (0, 0))
    mixer = layer // 2
    return [
        pl.BlockSpec(mem, lambda b, n: (layer, b, 0, 0)),
        pl.BlockSpec(mem, lambda b, n: (layer, b, 0, 1)),
        pl.BlockSpec((None, D_MODEL, D_MODEL), lambda b, n: (mixer, 0, 0), pipeline_mode=pl.Buffered(1)),
        pl.BlockSpec((rows, D_MODEL), lambda b, n: (row_of(b, n), 0)),
        vec, vec,
    ]


def _swa_prompt(qkv, bsz, seq, mem_kv, layer, sinks, w_out, x, g, b):
    nblk = ATT_BLOCKS_PER_STEP
    rows = nblk * WINDOW
    steps = seq // rows
    assert seq % rows == 0 and seq >= WINDOW
    tail = pl.BlockSpec((1, WINDOW, SWA_KV), lambda b, n: (b, 0, 0))
    blocks_per_seq = seq // WINDOW
    kcol = SWA_Q // SWA_KV
    vcol = kcol + 1
    mcol = vcol + 1
    row_of = lambda b, n: b * steps + n
    prev_of = lambda b, n: b * blocks_per_seq + jnp.maximum(n * nblk - 1, 0)
    cur = lambda w, col: pl.BlockSpec((rows, w), lambda b, n: (row_of(b, n), col))
    prev = lambda col: pl.BlockSpec((WINDOW, SWA_KV), lambda b, n: (prev_of(b, n), col))
    nheads = SWA_Q_HEADS + MEM_HEADS
    assert MEM_LEN == 2 * WINDOW
    in_specs = [
        pl.BlockSpec((SWA_Q_HEADS, 1, 1), lambda b, n: (0, 0, 0)),
        cur(SWA_Q, 0), cur(SWA_KV, kcol), prev(kcol), cur(SWA_KV, vcol), prev(vcol), cur(MEM_DIM, mcol),
    ] + _mixer_tail_specs(layer, row_of, rows)
    return pl.pallas_call(
        functools.partial(_swa_prompt_kernel, nblk=nblk, steps=steps),
        grid=(bsz, steps),
        in_specs=in_specs,
        out_specs=[pl.BlockSpec((rows, D_MODEL), lambda b, n: (row_of(b, n), 0)), tail, tail],
        out_shape=[jax.ShapeDtypeStruct(x.shape, F32),
                   jax.ShapeDtypeStruct((bsz, WINDOW, SWA_KV), F32),
                   jax.ShapeDtypeStruct((bsz, WINDOW, SWA_KV), F32)],
        scratch_shapes=[pltpu.VMEM((nblk, nheads, WINDOW, 2 * WINDOW), F32),
                        pltpu.VMEM((nblk, nheads, WINDOW, 2 * WINDOW), BF16),
                        pltpu.VMEM((rows, D_MODEL), BF16),
                        pltpu.VMEM((D_MODEL, D_MODEL), BF16)],
        input_output_aliases={len(in_specs) - 3: 0},
        compiler_params=_params("arbitrary", "arbitrary"),
        name="swa_prompt",
    )(sinks.reshape(SWA_Q_HEADS, 1, 1), qkv, qkv, qkv, qkv, qkv, qkv, mem_kv, mem_kv, w_out, x, g, b)


def _head_norm_gate(o, gate):
    mu = jnp.mean(o, axis=-1, keepdims=True)
    oc = o - mu
    var = jnp.mean(oc * oc, axis=-1, keepdims=True)
    return jax.nn.silu(gate) * (oc * lax.rsqrt(var + HEAD_NORM_EPS))


def _ret_prompt_kernel(cdec_ref, decay_ref, qdec_ref, kdec_ref, q_ref, k_ref, v_ref, gate_ref, qm_ref,
                       mk_ref, mv_ref, w_ref, x_ref, g_ref, b_ref, o_ref, s_out_ref,
                       state_ref, att_ref, wb_ref, *, steps, nblk):
    c = pl.program_id(1)

    @pl.when(c == 0)
    def _():
        state_ref[...] = jnp.zeros_like(state_ref)

    for i in range(nblk):
        rows = slice(i * RET_CHUNK, (i + 1) * RET_CHUNK)
        for h in range(RET_HEADS):
            ksl = slice(h * RET_DK, (h + 1) * RET_DK)
            vsl = slice(h * RET_DV, (h + 1) * RET_DV)
            qc = q_ref[rows, ksl]
            kc = k_ref[rows, ksl] * RET_K_SCALE
            vb = v_ref[rows, vsl].astype(BF16)
            st = state_ref[h]
            inner = _dot_nt(qc.astype(BF16), kc.astype(BF16)) * decay_ref[h]
            o = (jnp.dot(inner.astype(BF16), vb, preferred_element_type=F32)
                 + jnp.dot((qc * qdec_ref[h]).astype(BF16), st.astype(BF16),
                           preferred_element_type=F32))
            kd = (kc * kdec_ref[h]).astype(BF16)
            state_ref[h] = cdec_ref[h] * st + lax.dot_general(
                kd, vb, (((0,), (0,)), ((), ())), preferred_element_type=F32)
            att_ref[rows, vsl] = _head_norm_gate(o, gate_ref[rows, vsl]).astype(BF16)
    _mem_attend_block(qm_ref, mk_ref, mv_ref, att_ref, RET_V)

    @pl.when(c == steps - 1)
    def _():
        s_out_ref[0] = state_ref[...]

    first = jnp.logical_and(pl.program_id(0) == 0, c == 0)
    _out_proj_ln(first, att_ref, w_ref, wb_ref, x_ref, g_ref, b_ref, o_ref)


def _drop_carried(kernel, first, count):
    def body(*refs):
        return kernel(*refs[:first], *refs[first + count:])
    return body


def _carry(prev_outputs):
    prev_outputs = [] if prev_outputs is None else list(prev_outputs)
    return [pl.BlockSpec(memory_space=pl.ANY)] * len(prev_outputs), prev_outputs


def _ret_prompt(qkvg, bsz, seq, mem_kv, layer, tables, w_out, x, g, b, carried):
    nblk = ATT_BLOCKS_PER_STEP
    rows = nblk * RET_CHUNK
    steps = seq // rows
    mixer = layer // 2
    decay, qdec, kdec, cdec = tables
    row_of = lambda b, n: b * steps + n
    cur = lambda w, col: pl.BlockSpec((rows, w), lambda b, n: (row_of(b, n), col))
    tab = pl.BlockSpec((RET_HEADS, RET_CHUNK, RET_CHUNK), lambda b, n: (0, 0, 0))
    in_specs = [
        pl.BlockSpec(memory_space=pltpu.SMEM), tab, tab, tab,
        cur(RET_QK, 0), cur(RET_QK, 1), cur(RET_V, 1), cur(RET_V, 2),
        cur(MEM_DIM, (2 * RET_QK + 2 * RET_V) // MEM_DIM),
    ] + _mixer_tail_specs(layer, row_of, rows)
    n_in = len(in_specs)
    carry_specs, carry_args = _carry(carried)
    aliases = {n_in - 3: 0}
    aliases.update({n_in + c: 1 + c for c in range(len(carry_args))})
    return pl.pallas_call(
        _drop_carried(functools.partial(_ret_prompt_kernel, steps=steps, nblk=nblk), n_in, len(carry_args)),
        grid=(bsz, steps),
        in_specs=in_specs + carry_specs,
        out_specs=[
            pl.BlockSpec((rows, D_MODEL), lambda b, n: (row_of(b, n), 0)),
            pl.BlockSpec((None, 1, RET_HEADS, RET_DK, RET_DV), lambda b, n: (mixer, b, 0, 0, 0)),
        ],
        out_shape=[
            jax.ShapeDtypeStruct(x.shape, F32),
            jax.ShapeDtypeStruct((DEPTH // 2, bsz, RET_HEADS, RET_DK, RET_DV), F32),
        ],
        scratch_shapes=[pltpu.VMEM((RET_HEADS, RET_DK, RET_DV), F32),
                        pltpu.VMEM((rows, D_MODEL), BF16),
                        pltpu.VMEM((D_MODEL, D_MODEL), BF16)],
        input_output_aliases=aliases,
        compiler_params=_params("arbitrary", "arbitrary"),
        name="ret_prompt",
    )(cdec, decay, qdec, kdec, qkvg, qkvg, qkvg, qkvg, qkvg, mem_kv, mem_kv, w_out, x, g, b, *carry_args)


def _mem_attend_row(q, mk_ref, mv_ref):
    s = jnp.sum(mk_ref[0] * q[None], axis=-1, keepdims=True) * ATT_SCALE
    p = jnp.exp(s - jnp.max(s, axis=0, keepdims=True))
    p = p * (1.0 / jnp.sum(p, axis=0, keepdims=True))
    return jnp.sum(p * mv_ref[0], axis=0)


def _swa_step_kernel(sink_ref, rows_ref, kbuf_ref, vbuf_ref, mk_ref, mv_ref, o_ref, nk_ref, nv_ref):
    krow0 = SWA_Q_HEADS
    vrow0 = krow0 + SWA_KV_HEADS
    mrow0 = vrow0 + SWA_KV_HEADS
    wb = kbuf_ref.shape[1]
    kb = kbuf_ref[0]
    vb = vbuf_ref[0]
    k_new = rows_ref[0, krow0:krow0 + SWA_KV_HEADS, :]
    v_new = rows_ref[0, vrow0:vrow0 + SWA_KV_HEADS, :]
    for g in range(SWA_GROUP):
        group_rows = pl.ds(g, SWA_KV_HEADS, stride=SWA_GROUP)
        q = rows_ref[0, group_rows, :]
        s_buf = jnp.sum(kb * q[None], axis=-1, keepdims=True) * ATT_SCALE
        s_new = jnp.sum(k_new * q, axis=-1, keepdims=True) * ATT_SCALE
        sink = sink_ref[g]
        m = jnp.maximum(jnp.maximum(jnp.max(s_buf, axis=0), s_new), sink)
        p_buf = jnp.exp(s_buf - m[None])
        p_new = jnp.exp(s_new - m)
        inv = 1.0 / (jnp.sum(p_buf, axis=0) + p_new + jnp.exp(sink - m))
        o_ref[0, group_rows, :] = jnp.sum((p_buf * inv[None]) * vb, axis=0) + (p_new * inv) * v_new
    nk_ref[0, 0:wb - 1] = kbuf_ref[0, 1:wb]
    nv_ref[0, 0:wb - 1] = vbuf_ref[0, 1:wb]
    nk_ref[0, wb - 1] = k_new
    nv_ref[0, wb - 1] = v_new
    o_ref[0, SWA_Q_HEADS:SWA_Q_HEADS + MEM_HEADS, :] = _mem_attend_row(
        rows_ref[0, mrow0:mrow0 + MEM_HEADS, :], mk_ref, mv_ref)


def _swa_step(rows, cache_k, cache_v, j, mem_k, mem_v, i, sinks, carried):
    bsz = rows.shape[0]
    wb = cache_k.shape[2]
    cache = pl.BlockSpec((None, 1, wb, SWA_KV_HEADS, HEAD_DIM), lambda b: (j, b, 0, 0, 0))
    mem = pl.BlockSpec((None, 1, MEM_LEN, MEM_HEADS, HEAD_DIM), lambda b: (i, b, 0, 0, 0))
    nrows = D_MODEL // HEAD_DIM
    sink_gk = sinks.reshape(SWA_KV_HEADS, SWA_GROUP).T.reshape(SWA_GROUP, SWA_KV_HEADS, 1)
    in_specs = [
        pl.BlockSpec(sink_gk.shape, lambda b: (0, 0, 0)),
        pl.BlockSpec((1,) + rows.shape[1:], lambda b: (b, 0, 0)),
        cache, cache, mem, mem,
    ]
    n_in = len(in_specs)
    carry_specs, carry_args = _carry(carried)
    return pl.pallas_call(
        _drop_carried(_swa_step_kernel, n_in, len(carry_args)),
        grid=(bsz,),
        in_specs=in_specs + carry_specs,
        out_specs=[pl.BlockSpec((1, nrows, HEAD_DIM), lambda b: (b, 0, 0)), cache, cache],
        out_shape=[
            jax.ShapeDtypeStruct((bsz, nrows, HEAD_DIM), F32),
            jax.ShapeDtypeStruct(cache_k.shape, F32),
            jax.ShapeDtypeStruct(cache_v.shape, F32),
        ],
        input_output_aliases={n_in + c: 1 + c for c in range(len(carry_args))},
        compiler_params=_params("parallel"),
        name="swa_step",
    )(sink_gk, rows, cache_k, cache_v, mem_k, mem_v, *carry_args)


def _ret_step_kernel(dec_ref, rows_ref, cols_ref, s_ref, mk_ref, mv_ref, o_ref, s_out_ref):
    krow0 = RET_HEADS
    vrow0 = 2 * RET_HEADS
    grow0 = vrow0 + 2 * RET_HEADS
    mrow0 = grow0 + 2 * RET_HEADS
    for h in range(RET_HEADS):
        q_row = rows_ref[0, h:h + 1, :]
        k_row = rows_ref[0, krow0 + h:krow0 + h + 1, :] * RET_K_SCALE
        q_col = cols_ref[0, :, h:h + 1]
        k_col = cols_ref[0, :, krow0 + h:krow0 + h + 1] * RET_K_SCALE
        inner = jnp.sum(q_row * k_row, axis=1, keepdims=True) * dec_ref[0, h]
        qd = q_col * dec_ref[1, h]
        kd = k_col * dec_ref[2, h]
        halves = []
        for t in range(2):
            lsl = slice(t * HEAD_DIM, (t + 1) * HEAD_DIM)
            v = rows_ref[0, vrow0 + 2 * h + t:vrow0 + 2 * h + t + 1, :]
            st = s_ref[0, h, :, lsl]
            halves.append(inner * v + jnp.sum(qd * st, axis=0, keepdims=True))
            s_out_ref[0, h, :, lsl] = dec_ref[3, h] * st + kd * v
        mu = (jnp.sum(halves[0], axis=1, keepdims=True)
              + jnp.sum(halves[1], axis=1, keepdims=True)) * (1.0 / RET_DV)
        cen = [o - mu for o in halves]
        var = (jnp.sum(cen[0] * cen[0], axis=1, keepdims=True)
               + jnp.sum(cen[1] * cen[1], axis=1, keepdims=True)) * (1.0 / RET_DV)
        rstd = lax.rsqrt(var + HEAD_NORM_EPS)
        for t in range(2):
            gate = rows_ref[0, grow0 + 2 * h + t:grow0 + 2 * h + t + 1, :]
            o_ref[0, 2 * h + t:2 * h + t + 1, :] = jax.nn.silu(gate) * (cen[t] * rstd)
    o_ref[0, 2 * RET_HEADS:2 * RET_HEADS + MEM_HEADS, :] = _mem_attend_row(
        rows_ref[0, mrow0:mrow0 + MEM_HEADS, :], mk_ref, mv_ref)


def _ret_step(rows, cols, state, j, mem_k, mem_v, i, dec, carried):
    bsz = rows.shape[0]
    st = pl.BlockSpec((None, 1, RET_HEADS, RET_DK, RET_DV), lambda b: (j, b, 0, 0, 0))
    mem = pl.BlockSpec((None, 1, MEM_LEN, MEM_HEADS, HEAD_DIM), lambda b: (i, b, 0, 0, 0))
    nrows = D_MODEL // HEAD_DIM
    in_specs = [
        pl.BlockSpec(memory_space=pltpu.SMEM),
        pl.BlockSpec((1,) + rows.shape[1:], lambda b: (b, 0, 0)),
        pl.BlockSpec((1,) + cols.shape[1:], lambda b: (b, 0, 0)),
        st, mem, mem,
    ]
    n_in = len(in_specs)
    carry_specs, carry_args = _carry(carried)
    return pl.pallas_call(
        _drop_carried(_ret_step_kernel, n_in, len(carry_args)),
        grid=(bsz,),
        in_specs=in_specs + carry_specs,
        out_specs=[pl.BlockSpec((1, nrows, HEAD_DIM), lambda b: (b, 0, 0)), st],
        out_shape=[
            jax.ShapeDtypeStruct((bsz, nrows, HEAD_DIM), F32),
            jax.ShapeDtypeStruct(state.shape, F32),
        ],
        input_output_aliases={n_in + c: 1 + c for c in range(len(carry_args))},
        compiler_params=_params("parallel"),
        name="ret_step",
    )(dec, rows, cols, state, mem_k, mem_v, *carry_args)


def _rope_tables(pos):
    half = ROPE_DIM // 2
    inv = ROPE_THETA ** (-jnp.arange(half, dtype=F32) / half)
    ang = pos.astype(F32)[:, None] * inv[None, :]
    cos, sin = jnp.cos(ang), jnp.sin(ang)
    n = pos.shape[0]
    rest = HEAD_DIM - ROPE_DIM
    c = jnp.concatenate([cos, cos, jnp.ones((n, rest), F32)], axis=-1)
    a = jnp.concatenate([-sin, jnp.zeros((n, HEAD_DIM - half), F32)], axis=-1)
    b = jnp.concatenate([jnp.zeros((n, half), F32), sin, jnp.zeros((n, rest), F32)], axis=-1)
    return (c, a, b), (HEAD_DIM - half, half)


def _ret_rot_tables(pos):
    half = RET_DK // 2
    angle = RET_ROT_BASE ** (-jnp.linspace(0.0, 1.0, half, dtype=F32))
    ang = pos.astype(F32)[:, None] * angle[None, :]
    cos, sin = jnp.cos(ang), jnp.sin(ang)
    n = pos.shape[0]
    zero = jnp.zeros_like(sin)
    c = jnp.stack([cos, cos], axis=-1).reshape(n, RET_DK)
    a = jnp.stack([-sin, zero], axis=-1).reshape(n, RET_DK)
    b = jnp.stack([zero, sin], axis=-1).reshape(n, RET_DK)
    return (c, a, b), (RET_DK - 1, 1)


def _ret_decay(chunk):
    log_g = jnp.log1p(-jnp.exp2(-5.0 - jnp.arange(RET_HEADS, dtype=F32)))
    n = jnp.arange(chunk, dtype=F32)
    rel = n[:, None] - n[None, :]
    decay = jnp.where(rel >= 0, jnp.exp(jnp.maximum(rel, 0.0) * log_g[:, None, None]), 0.0)
    q_dec = jnp.exp((n + 1.0) * log_g[:, None])
    k_dec = jnp.exp((chunk - 1.0 - n) * log_g[:, None])
    c_dec = jnp.exp(chunk * log_g)
    return decay, q_dec, k_dec, c_dec


def kernel(x_prompt, x_sample, cache_swa_k, cache_swa_v, state_ret, cache_mem_k, cache_mem_v,
           mem_prompt, ln_g, ln_b, ffn_w_gu, ffn_w_down, w_mem_kv, swa_w_in, swa_w_out,
           swa_sinks, ret_w_in, ret_w_out):
    bp, seq, _ = x_prompt.shape
    bs, sample_seq, _ = x_sample.shape
    assert sample_seq == 1
    mp, ms = bp * seq, bs
    assert mp % ms == 0

    mem2 = mem_prompt.reshape(bp * MEM_LEN, D_MODEL)
    mem_kv = _mem_kv(mem2, w_mem_kv).reshape(DEPTH, bp, MEM_LEN, 2 * MEM_DIM)
    mem_k_prompt = mem_kv[..., :MEM_DIM].reshape(DEPTH, bp, MEM_LEN, MEM_HEADS, HEAD_DIM)
    mem_v_prompt = mem_kv[..., MEM_DIM:].reshape(DEPTH, bp, MEM_LEN, MEM_HEADS, HEAD_DIM)

    tm, m_all = _merged_rows(mp, ms)
    tiles = m_all // tm
    pad = m_all - mp - ms
    tail = (mp - (tiles - 1) * tm, ms)
    assert tail[0] >= 0 and tail[0] % 8 == 0 and ms % 8 == 0 and tail[0] + ms <= tm
    def per_row(table):
        return jnp.concatenate([jnp.tile(table[:seq], (bp, 1)),
                                jnp.broadcast_to(table[seq:], (ms, table.shape[1])),
                                jnp.zeros((pad, table.shape[1]), F32)], axis=0)

    pos = jnp.concatenate([jnp.arange(seq, dtype=jnp.int32), jnp.full((1,), PAST_LEN, jnp.int32)])
    rope_tabs, rope_shifts = _rope_tables(pos)
    rot_tabs, rot_shifts = _ret_rot_tables(pos)
    rope_tabs = tuple(per_row(t) for t in rope_tabs)
    rot_tabs = tuple(per_row(t) for t in rot_tabs)
    decay, q_dec, k_dec, c_dec = _ret_decay(RET_CHUNK)
    ret_tabs = (decay, jnp.broadcast_to(q_dec[:, :, None], decay.shape),
                jnp.broadcast_to(k_dec[:, :, None], decay.shape), c_dec)
    decay, q_dec, k_dec, c_dec = _ret_decay(sample_seq)
    step_dec = jnp.stack([decay[:, 0, 0], q_dec[:, 0], k_dec[:, 0], c_dec])

    def ln(i, s):
        return ln_g[i, s].reshape(1, D_MODEL), ln_b[i, s].reshape(1, D_MODEL)

    swa_k_prompt, swa_v_prompt = [], []
    swa_sample = None
    ret_prompt = None
    ret_sample = None
    for i in range(DEPTH):
        j = i // 2
        if i == 0:
            x = _ffn(x_prompt.reshape(mp, D_MODEL), ffn_w_gu, ffn_w_down, i, 0, *ln(i, 0), tm, tiles, tail,
                     x_sample=x_sample.reshape(ms, D_MODEL))
        else:
            x = _ffn(x, ffn_w_gu, ffn_w_down, i, 0, *ln(i, 0), tm, tiles, tail)
        if i % 2 == 0:
            qkv = _proj(x, swa_w_in, j, tm, rot_heads=SWA_Q_HEADS + SWA_KV_HEADS, tables=rope_tabs,
                        shifts=rope_shifts)
            x, k_tail, v_tail = _swa_prompt(qkv, bp, seq, mem_kv, i, swa_sinks[j], swa_w_out, x, *ln(i, 1))
            swa_k_prompt.append(k_tail.reshape(bp, WINDOW, SWA_KV_HEADS, HEAD_DIM))
            swa_v_prompt.append(v_tail.reshape(bp, WINDOW, SWA_KV_HEADS, HEAD_DIM))
            rows = qkv[mp:mp + ms].reshape(ms, SWA_IN_WIDTH // HEAD_DIM, HEAD_DIM)
            att, *swa_sample = _swa_step(rows, cache_swa_k, cache_swa_v, j, cache_mem_k, cache_mem_v, i,
                                         swa_sinks[j], swa_sample)
            x = _out_ln(att.reshape(ms, D_MODEL), swa_w_out, j, x, *ln(i, 1), mp)
        else:
            qkvg = _proj(x, ret_w_in, j, tm, rot_heads=2 * RET_HEADS, tables=rot_tabs,
                         shifts=rot_shifts)
            x, ret_prompt = _ret_prompt(qkvg, bp, seq, mem_kv, i, ret_tabs, ret_w_out, x, *ln(i, 1),
                                        None if ret_prompt is None else [ret_prompt])
            rows = qkvg[mp:mp + ms].reshape(ms, RET_IN_WIDTH // HEAD_DIM, HEAD_DIM)
            att, ret_sample = _ret_step(rows, jnp.swapaxes(rows, 1, 2), state_ret, j, cache_mem_k,
                                        cache_mem_v, i, step_dec,
                                        None if ret_sample is None else [ret_sample])
            x = _out_ln(att.reshape(ms, D_MODEL), ret_w_out, j, x, *ln(i, 1), mp)
        x = _ffn(x, ffn_w_gu, ffn_w_down, i, 1, *ln(i, 2), tm, tiles, tail, split_out=(i == DEPTH - 1))

    y_prompt = x[0].reshape(bp, seq, D_MODEL)
    y_sample = x[1].reshape(bs, sample_seq, D_MODEL)
    return (y_prompt, y_sample, jnp.stack(swa_k_prompt), jnp.stack(swa_v_prompt), swa_sample[0],
            swa_sample[1], ret_prompt, ret_sample, mem_k_prompt, mem_v_prompt)
```

```python
import functools

import jax
import jax.numpy as jnp
from jax import lax
from jax.experimental import pallas as pl
from jax.experimental.pallas import tpu as pltpu

F32 = jnp.float32
BF16 = jnp.bfloat16

D_MODEL = 2048
DEPTH = 4
PAST_LEN = 16384
HEAD_DIM = 128
MEM_LEN = 256
MEM_HEADS = 4
MEM_DIM = MEM_HEADS * HEAD_DIM
SELF_WIDTH = D_MODEL - MEM_DIM
SWA_Q_HEADS = SELF_WIDTH // HEAD_DIM
SWA_KV_HEADS = SWA_Q_HEADS // 3
SWA_GROUP = SWA_Q_HEADS // SWA_KV_HEADS
WINDOW = 128
ROPE_THETA = 500000.0
ROPE_DIM = HEAD_DIM // 4
RET_DK = 128
RET_DV = 2 * RET_DK
RET_HEADS = SELF_WIDTH // RET_DV
RET_CHUNK = 128
RET_ROT_BASE = 10000.0
D_FF = ((8 * D_MODEL // 3 + 255) // 256) * 256
LN_EPS = 1e-5
HEAD_NORM_EPS = 1e-6
ALPHA = (2.0 * DEPTH) ** 0.25
NEG_INF = -1e30
ATT_SCALE = HEAD_DIM ** -0.5
RET_K_SCALE = RET_DK ** -0.5

SWA_Q = SWA_Q_HEADS * HEAD_DIM
SWA_KV = SWA_KV_HEADS * HEAD_DIM
SWA_IN_WIDTH = SWA_Q + 2 * SWA_KV + MEM_DIM
RET_QK = RET_HEADS * RET_DK
RET_V = RET_HEADS * RET_DV
RET_IN_WIDTH = 2 * RET_QK + 2 * RET_V + MEM_DIM

VMEM_LIMIT_BYTES = 58 * 1024 * 1024
LANES = 128
FFN_TILE_K = 256
FFN_MAX_ROWS = 1200
PROJ_MAX_ROWS = 1040
ATT_BLOCKS_PER_STEP = 2
PROJ_TILE_N = 1024
OUT_TILE_N = 512
MAX_ROW_CHUNK = 256
BF16_SUBLANES = 16


def _row_tiles(rows, max_rows):
    tiles = -(-rows // max_rows)
    per_tile = -(-rows // tiles)
    return -(-per_tile // BF16_SUBLANES) * BF16_SUBLANES, tiles


def _row_chunks(tm):
    count = -(-tm // MAX_ROW_CHUNK)
    size = -(-(-(-tm // count)) // BF16_SUBLANES) * BF16_SUBLANES
    return [slice(r, min(r + size, tm)) for r in range(0, tm, size)]


def _params(*sem):
    return pltpu.CompilerParams(dimension_semantics=sem, vmem_limit_bytes=VMEM_LIMIT_BYTES)


def _layer_norm_rows(z, g, b):
    mu = jnp.mean(z, axis=-1, keepdims=True)
    zc = z - mu
    var = jnp.mean(zc * zc, axis=-1, keepdims=True)
    return zc * lax.rsqrt(var + LN_EPS) * g + b


def _ffn_kernel(*refs, nk, tk, tiles, tail, split_in, split_out):
    refs = list(refs)
    x_ref = refs.pop(0)
    xs_ref = refs.pop(0) if split_in else None
    wg_ref, wu_ref, wd_ref, g_ref, b_ref, o_ref = refs[:6]
    os_ref = refs[6] if split_out else None
    xb_ref, wgu_ref = refs[-2:]
    i = pl.program_id(0)
    k = pl.program_id(1)
    chunks = _row_chunks(o_ref.shape[0])

    def load_x(rows):
        xv = x_ref[rows, :]
        if not split_in or rows.stop <= tail[0]:
            return xv
        first, count = tail
        lo, hi = rows.start, rows.stop
        pieces = []
        if lo < first:
            pieces.append(xv[:first - lo])
        s0, s1 = max(lo, first), min(hi, first + count)
        if s0 < s1:
            pieces.append(xs_ref[s0 - first:s1 - first, :])
        if hi > first + count:
            pieces.append(jnp.zeros((hi - max(lo, first + count), xv.shape[1]), xv.dtype))
        return jnp.where(i == tiles - 1, jnp.concatenate(pieces, axis=0), xv)

    def cast_weights():
        wgu_ref[:, :tk] = wg_ref[...].astype(BF16)
        wgu_ref[:, tk:] = wu_ref[...].astype(BF16)
        return wd_ref[...].astype(BF16)

    def down(xb, wd):
        gu = jnp.dot(xb, wgu_ref[...], preferred_element_type=F32)
        h = (jax.nn.silu(gu[:, :tk]) * gu[:, tk:]).astype(BF16)
        return jnp.dot(h, wd, preferred_element_type=F32)

    @pl.when(k == 0)
    def _():
        wd = cast_weights()
        for rows in chunks:
            xb = load_x(rows).astype(BF16)
            xb_ref[rows, :] = xb
            o_ref[rows, :] = down(xb, wd)

    @pl.when(jnp.logical_and(k > 0, k < nk - 1))
    def _():
        wd = cast_weights()
        o_ref[...] += down(xb_ref[...], wd)

    @pl.when(k == nk - 1)
    def _():
        wd = cast_weights()
        for rows in chunks:
            acc = o_ref[rows, :] + down(xb_ref[rows, :], wd)
            z = ALPHA * load_x(rows) + 0.5 * acc
            o_ref[rows, :] = _layer_norm_rows(z, g_ref[...], b_ref[...])
        if split_out:
            @pl.when(i == tiles - 1)
            def _():
                os_ref[...] = o_ref[tail[0]:tail[0] + tail[1], :]


def _ffn(x, w_gu, w_down, layer, slot, g, b, tm, tiles, tail, x_sample=None, split_out=False):
    tk = FFN_TILE_K
    nk = D_FF // tk
    assert nk >= 2
    split_in = x_sample is not None
    rows = pl.BlockSpec((tm, D_MODEL), lambda i, k: (i, 0))
    small = pl.BlockSpec((tail[1], D_MODEL), lambda i, k: (0, 0))
    in_specs = [pl.BlockSpec((tm, D_MODEL), lambda i, k: (i, 0), pipeline_mode=pl.Buffered(1))]
    args = [x]
    if split_in:
        in_specs.append(small)
        args.append(x_sample)
    in_specs += [
        pl.BlockSpec((None, None, D_MODEL, tk), lambda i, k: (layer, slot, 0, k)),
        pl.BlockSpec((None, None, D_MODEL, tk), lambda i, k: (layer, slot, 0, nk + k)),
        pl.BlockSpec((None, None, tk, D_MODEL), lambda i, k: (layer, slot, k, 0)),
        pl.BlockSpec((1, D_MODEL), lambda i, k: (0, 0)),
        pl.BlockSpec((1, D_MODEL), lambda i, k: (0, 0)),
    ]
    if split_out:
        prompt_rows = (tiles - 1) * tm + tail[0]
        out_specs = [rows, small]
        out_shape = [jax.ShapeDtypeStruct((prompt_rows, D_MODEL), F32),
                     jax.ShapeDtypeStruct((tail[1], D_MODEL), F32)]
    else:
        out_specs = rows
        out_shape = jax.ShapeDtypeStruct((tiles * tm, D_MODEL), F32)
    return pl.pallas_call(
        functools.partial(_ffn_kernel, nk=nk, tk=tk, tiles=tiles, tail=tail,
                          split_in=split_in, split_out=split_out),
        grid=(tiles, nk),
        in_specs=in_specs,
        out_specs=out_specs,
        out_shape=out_shape,
        scratch_shapes=[pltpu.VMEM((tm, D_MODEL), BF16), pltpu.VMEM((D_MODEL, 2 * tk), BF16)],
        compiler_params=_params("arbitrary", "arbitrary"),
        name="ffn_ln",
    )(*args, w_gu, w_gu, w_down, g, b)


def _proj_kernel(x_ref, w_ref, *rest, rot_heads, shifts):
    if rot_heads:
        c_ref, a_ref, b_ref, o_ref, xb_ref = rest
    else:
        o_ref, xb_ref = rest
    j = pl.program_id(1)

    @pl.when(j == 0)
    def _():
        xb_ref[...] = x_ref[...].astype(BF16)

    def plain():
        o_ref[...] = jnp.dot(xb_ref[...], w_ref[...].astype(BF16), preferred_element_type=F32)

    if not rot_heads:
        plain()
        return

    tm, tn = o_ref.shape
    heads = tn // LANES
    full_tiles, part_heads = divmod(rot_heads, heads)

    def rotated(n_heads):
        wb = w_ref[...].astype(BF16)
        for rows in _row_chunks(tm):
            y = jnp.dot(xb_ref[rows, :], wb, preferred_element_type=F32)
            c, a, b = c_ref[rows, :], a_ref[rows, :], b_ref[rows, :]
            for h in range(heads):
                yh = y[:, h * LANES:(h + 1) * LANES]
                if h < n_heads:
                    yh = yh * c + pltpu.roll(yh, shifts[0], 1) * a + pltpu.roll(yh, shifts[1], 1) * b
                o_ref[rows, h * LANES:(h + 1) * LANES] = yh

    pl.when(j < full_tiles)(lambda: rotated(heads))
    if part_heads:
        pl.when(j == full_tiles)(lambda: rotated(part_heads))
    pl.when(j >= full_tiles + (1 if part_heads else 0))(plain)


def _proj(x, w, layer, tm, rot_heads=0, tables=None, shifts=None):
    m = x.shape[0]
    n = w.shape[-1]
    tn = PROJ_TILE_N
    in_specs = [
        pl.BlockSpec((tm, D_MODEL), lambda i, j: (i, 0)),
        pl.BlockSpec((None, D_MODEL, tn), lambda i, j: (layer, 0, j)),
    ]
    args = [x, w]
    if rot_heads:
        tab = pl.BlockSpec((tm, LANES), lambda i, j: (i, 0))
        in_specs += [tab, tab, tab]
        args += list(tables)
    return pl.pallas_call(
        functools.partial(_proj_kernel, rot_heads=rot_heads, shifts=shifts),
        grid=(pl.cdiv(m, tm), n // tn),
        in_specs=in_specs,
        out_specs=pl.BlockSpec((tm, tn), lambda i, j: (i, j)),
        out_shape=jax.ShapeDtypeStruct((m, n), F32),
        scratch_shapes=[pltpu.VMEM((tm, D_MODEL), BF16)],
        compiler_params=_params("parallel", "arbitrary"),
        name="proj",
    )(*args)


def _mem_kv_kernel(x_ref, w_ref, o_ref, xb_ref):
    @pl.when(pl.program_id(0) == 0)
    def _():
        xb_ref[...] = x_ref[...].astype(BF16)

    o_ref[...] = jnp.dot(xb_ref[...], w_ref[...].astype(BF16), preferred_element_type=F32)


def _mem_kv(mem, w):
    rows = mem.shape[0]
    layers, _, n = w.shape
    return pl.pallas_call(
        _mem_kv_kernel,
        grid=(layers,),
        in_specs=[
            pl.BlockSpec((rows, D_MODEL), lambda l: (0, 0)),
            pl.BlockSpec((None, D_MODEL, n), lambda l: (l, 0, 0)),
        ],
        out_specs=pl.BlockSpec((None, rows, n), lambda l: (l, 0, 0)),
        out_shape=jax.ShapeDtypeStruct((layers, rows, n), F32),
        scratch_shapes=[pltpu.VMEM((rows, D_MODEL), BF16)],
        compiler_params=_params("arbitrary"),
        name="mem_kv",
    )(mem, w)


def _out_ln_kernel(att_ref, w_ref, x_ref, g_ref, b_ref, o_ref, y_ref, *, nn, tn):
    n = pl.program_id(1)
    y_ref[n] = jnp.dot(att_ref[...].astype(BF16), w_ref[...].astype(BF16),
                       preferred_element_type=F32)

    @pl.when(n == nn - 1)
    def _():
        cols = [slice(c * tn, (c + 1) * tn) for c in range(nn)]
        total = None
        for c in range(nn):
            z = ALPHA * x_ref[:, cols[c]] + y_ref[c]
            y_ref[c] = z
            part = jnp.sum(z, axis=-1, keepdims=True)
            total = part if total is None else total + part
        mu = total * (1.0 / D_MODEL)
        total = None
        for c in range(nn):
            zc = y_ref[c] - mu
            part = jnp.sum(zc * zc, axis=-1, keepdims=True)
            total = part if total is None else total + part
        rstd = lax.rsqrt(total * (1.0 / D_MODEL) + LN_EPS)
        for c in range(nn):
            o_ref[:, cols[c]] = (y_ref[c] - mu) * rstd * g_ref[:, cols[c]] + b_ref[:, cols[c]]


def _out_ln(att, w_out, layer, x, g, b, row0):
    tm = att.shape[0]
    assert row0 % tm == 0
    tn = OUT_TILE_N
    nn = D_MODEL // tn
    rows = pl.BlockSpec((tm, D_MODEL), lambda i, n: (row0 // tm, 0))
    return pl.pallas_call(
        functools.partial(_out_ln_kernel, nn=nn, tn=tn),
        grid=(1, nn),
        in_specs=[
            pl.BlockSpec((tm, D_MODEL), lambda i, n: (0, 0)),
            pl.BlockSpec((None, D_MODEL, tn), lambda i, n: (layer, 0, n)),
            rows,
            pl.BlockSpec((1, D_MODEL), lambda i, n: (0, 0)),
            pl.BlockSpec((1, D_MODEL), lambda i, n: (0, 0)),
        ],
        out_specs=rows,
        out_shape=jax.ShapeDtypeStruct(x.shape, F32),
        scratch_shapes=[pltpu.VMEM((nn, tm, tn), F32)],
        input_output_aliases={2: 0},
        compiler_params=_params("arbitrary", "arbitrary"),
        name="out_ln",
    )(att, w_out, x, g, b)


def _dot_nt(a, b):
    return lax.dot_general(a, b, (((1,), (1,)), ((), ())), preferred_element_type=F32)


def _out_proj_ln(first, att_ref, w_ref, wb_ref, x_ref, g_ref, b_ref, o_ref):
    @pl.when(first)
    def _():
        wb_ref[...] = w_ref[...].astype(BF16)

    y = jnp.dot(att_ref[...], wb_ref[...], preferred_element_type=F32)
    o_ref[...] = _layer_norm_rows(ALPHA * x_ref[...] + y, g_ref[...], b_ref[...])


def _mem_attend_block(qm_ref, mk_ref, mv_ref, att_ref, col0):
    for h in range(MEM_HEADS):
        sl = slice(h * HEAD_DIM, (h + 1) * HEAD_DIM)
        q = qm_ref[:, sl].astype(BF16)
        s = _dot_nt(q, mk_ref[0, :, sl].astype(BF16)) * ATT_SCALE
        p = jnp.exp(s - jnp.max(s, axis=-1, keepdims=True))
        p = p * (1.0 / jnp.sum(p, axis=-1, keepdims=True))
        o = jnp.dot(p.astype(BF16), mv_ref[0, :, sl].astype(BF16), preferred_element_type=F32)
        att_ref[:, col0 + h * HEAD_DIM:col0 + (h + 1) * HEAD_DIM] = o.astype(att_ref.dtype)


def _swa_prompt_kernel(sink_ref, q_ref, kc_ref, kp_ref, vc_ref, vp_ref, qm_ref, mk_ref, mv_ref,
                       w_ref, x_ref, g_ref, b_ref, o_ref, kt_ref, vt_ref, s_ref, p_ref, att_ref, wb_ref,
                       *, nblk, steps):
    n = pl.program_id(1)
    nq = SWA_Q_HEADS
    w2 = 2 * WINDOW

    def prev_cur(cur_ref, prev_ref, i, sl):
        rows = slice(i * WINDOW, (i + 1) * WINDOW)
        before = prev_ref[:, sl] if i == 0 else cur_ref[(i - 1) * WINDOW:i * WINDOW, sl]
        return jnp.concatenate([before, cur_ref[rows, sl]], axis=0).astype(BF16)

    for i in range(nblk):
        rows = slice(i * WINDOW, (i + 1) * WINDOW)
        for h in range(SWA_KV_HEADS):
            sl = slice(h * HEAD_DIM, (h + 1) * HEAD_DIM)
            q3 = jnp.concatenate(
                [q_ref[rows, (h * SWA_GROUP + g) * HEAD_DIM:(h * SWA_GROUP + g + 1) * HEAD_DIM]
                 for g in range(SWA_GROUP)], axis=0).astype(BF16)
            s = _dot_nt(q3, prev_cur(kc_ref, kp_ref, i, sl)) * ATT_SCALE
            s_ref[i, h * SWA_GROUP:(h + 1) * SWA_GROUP] = s.reshape(SWA_GROUP, WINDOW, w2)
    for h in range(MEM_HEADS):
        sl = slice(h * HEAD_DIM, (h + 1) * HEAD_DIM)
        sm = _dot_nt(qm_ref[:, sl].astype(BF16), mk_ref[0, :, sl].astype(BF16)) * ATT_SCALE
        s_ref[:, nq + h] = sm.reshape(nblk, WINDOW, w2)

    qi = lax.broadcasted_iota(jnp.int32, (WINDOW, w2), 0)
    kj = lax.broadcasted_iota(jnp.int32, (WINDOW, w2), 1)
    sink = sink_ref[...]
    for i in range(nblk):
        first_key = jnp.where(n > 0, qi, WINDOW) if i == 0 else qi
        ok = (kj >= first_key) & (kj <= qi + WINDOW)
        s = jnp.where(ok[None], s_ref[i, 0:nq], NEG_INF)
        m = jnp.maximum(jnp.max(s, axis=-1, keepdims=True), sink)
        p = jnp.exp(s - m)
        den = jnp.sum(p, axis=-1, keepdims=True) + jnp.exp(sink - m)
        p_ref[i, 0:nq] = (p * (1.0 / den)).astype(BF16)
    s = s_ref[:, nq:nq + MEM_HEADS]
    p = jnp.exp(s - jnp.max(s, axis=-1, keepdims=True))
    p_ref[:, nq:nq + MEM_HEADS] = (p * (1.0 / jnp.sum(p, axis=-1, keepdims=True))).astype(BF16)

    for i in range(nblk):
        rows = slice(i * WINDOW, (i + 1) * WINDOW)
        for h in range(SWA_KV_HEADS):
            sl = slice(h * HEAD_DIM, (h + 1) * HEAD_DIM)
            p3 = p_ref[i, h * SWA_GROUP:(h + 1) * SWA_GROUP].reshape(SWA_GROUP * WINDOW, w2)
            o = jnp.dot(p3, prev_cur(vc_ref, vp_ref, i, sl), preferred_element_type=F32)
            for g in range(SWA_GROUP):
                hq = h * SWA_GROUP + g
                att_ref[rows, hq * HEAD_DIM:(hq + 1) * HEAD_DIM] = (
                    o[g * WINDOW:(g + 1) * WINDOW].astype(BF16))
    for h in range(MEM_HEADS):
        sl = slice(h * HEAD_DIM, (h + 1) * HEAD_DIM)
        pm = p_ref[:, nq + h].reshape(nblk * WINDOW, w2)
        om = jnp.dot(pm, mv_ref[0, :, sl].astype(BF16), preferred_element_type=F32)
        att_ref[:, SWA_Q + h * HEAD_DIM:SWA_Q + (h + 1) * HEAD_DIM] = om.astype(BF16)

    @pl.when(n == steps - 1)
    def _():
        last = slice((nblk - 1) * WINDOW, nblk * WINDOW)
        kt_ref[0] = kc_ref[last, :]
        vt_ref[0] = vc_ref[last, :]

    first = jnp.logical_and(pl.program_id(0) == 0, n == 0)
    _out_proj_ln(first, att_ref, w_ref, wb_ref, x_ref, g_ref, b_ref, o_ref)


def _mixer_tail_specs(layer, row_of, rows):
    mem = (None, 1, MEM_LEN, MEM_DIM)
    vec = pl.BlockSpec((1, D_MODEL), lambda b, n: (0, 0))
    mixer = layer // 2
    return [
        pl.BlockSpec(mem, lambda b, n: (layer, b, 0, 0)),
        pl.BlockSpec(mem, lambda b, n: (layer, b, 0, 1)),
        pl.BlockSpec((None, D_MODEL, D_MODEL), lambda b, n: (mixer, 0, 0), pipeline_mode=pl.Buffered(1)),
        pl.BlockSpec((rows, D_MODEL), lambda b, n: (row_of(b, n), 0)),
        vec, vec,
    ]


def _swa_prompt(qkv, bsz, seq, mem_kv, layer, sinks, w_out, x, g, b):
    nblk = ATT_BLOCKS_PER_STEP
    rows = nblk * WINDOW
    steps = seq // rows
    assert seq % rows == 0 and seq >= WINDOW
    tail = pl.BlockSpec((1, WINDOW, SWA_KV), lambda b, n: (b, 0, 0))
    blocks_per_seq = seq // WINDOW
    kcol = SWA_Q // SWA_KV
    vcol = kcol + 1
    mcol = vcol + 1
    row_of = lambda b, n: b * steps + n
    prev_of = lambda b, n: b * blocks_per_seq + jnp.maximum(n * nblk - 1, 0)
    cur = lambda w, col: pl.BlockSpec((rows, w), lambda b, n: (row_of(b, n), col))
    prev = lambda col: pl.BlockSpec((WINDOW, SWA_KV), lambda b, n: (prev_of(b, n), col))
    nheads = SWA_Q_HEADS + MEM_HEADS
    assert MEM_LEN == 2 * WINDOW
    in_specs = [
        pl.BlockSpec((SWA_Q_HEADS, 1, 1), lambda b, n: (0, 0, 0)),
        cur(SWA_Q, 0), cur(SWA_KV, kcol), prev(kcol), cur(SWA_KV, vcol), prev(vcol), cur(MEM_DIM, mcol),
    ] + _mixer_tail_specs(layer, row_of, rows)
    return pl.pallas_call(
        functools.partial(_swa_prompt_kernel, nblk=nblk, steps=steps),
        grid=(bsz, steps),
        in_specs=in_specs,
        out_specs=[pl.BlockSpec((rows, D_MODEL), lambda b, n: (row_of(b, n), 0)), tail, tail],
        out_shape=[jax.ShapeDtypeStruct(x.shape, F32),
                   jax.ShapeDtypeStruct((bsz, WINDOW, SWA_KV), F32),
                   jax.ShapeDtypeStruct((bsz, WINDOW, SWA_KV), F32)],
        scratch_shapes=[pltpu.VMEM((nblk, nheads, WINDOW, 2 * WINDOW), F32),
                        pltpu.VMEM((nblk, nheads, WINDOW, 2 * WINDOW), BF16),
                        pltpu.VMEM((rows, D_MODEL), BF16),
                        pltpu.VMEM((D_MODEL, D_MODEL), BF16)],
        input_output_aliases={len(in_specs) - 3: 0},
        compiler_params=_params("arbitrary", "arbitrary"),
        name="swa_prompt",
    )(sinks.reshape(SWA_Q_HEADS, 1, 1), qkv, qkv, qkv, qkv, qkv, qkv, mem_kv, mem_kv, w_out, x, g, b)


def _head_norm_gate(o, gate):
    mu = jnp.mean(o, axis=-1, keepdims=True)
    oc = o - mu
    var = jnp.mean(oc * oc, axis=-1, keepdims=True)
    return jax.nn.silu(gate) * (oc * lax.rsqrt(var + HEAD_NORM_EPS))


def _ret_prompt_kernel(cdec_ref, decay_ref, qdec_ref, kdec_ref, q_ref, k_ref, v_ref, gate_ref, qm_ref,
                       mk_ref, mv_ref, w_ref, x_ref, g_ref, b_ref, o_ref, s_out_ref,
                       state_ref, att_ref, wb_ref, *, steps, nblk):
    c = pl.program_id(1)

    @pl.when(c == 0)
    def _():
        state_ref[...] = jnp.zeros_like(state_ref)

    for i in range(nblk):
        rows = slice(i * RET_CHUNK, (i + 1) * RET_CHUNK)
        for h in range(RET_HEADS):
            ksl = slice(h * RET_DK, (h + 1) * RET_DK)
            vsl = slice(h * RET_DV, (h + 1) * RET_DV)
            qc = q_ref[rows, ksl]
            kc = k_ref[rows, ksl] * RET_K_SCALE
            vb = v_ref[rows, vsl].astype(BF16)
            st = state_ref[h]
            inner = _dot_nt(qc.astype(BF16), kc.astype(BF16)) * decay_ref[h]
            o = (jnp.dot(inner.astype(BF16), vb, preferred_element_type=F32)
                 + jnp.dot((qc * qdec_ref[h]).astype(BF16), st.astype(BF16),
                           preferred_element_type=F32))
            kd = (kc * kdec_ref[h]).astype(BF16)
            state_ref[h] = cdec_ref[h] * st + lax.dot_general(
                kd, vb, (((0,), (0,)), ((), ())), preferred_element_type=F32)
            att_ref[rows, vsl] = _head_norm_gate(o, gate_ref[rows, vsl]).astype(BF16)
    _mem_attend_block(qm_ref, mk_ref, mv_ref, att_ref, RET_V)

    @pl.when(c == steps - 1)
    def _():
        s_out_ref[0] = state_ref[...]

    first = jnp.logical_and(pl.program_id(0) == 0, c == 0)
    _out_proj_ln(first, att_ref, w_ref, wb_ref, x_ref, g_ref, b_ref, o_ref)


def _drop_carried(kernel, first, count):
    def body(*refs):
        return kernel(*refs[:first], *refs[first + count:])
    return body


def _carry(prev_outputs):
    prev_outputs = [] if prev_outputs is None else list(prev_outputs)
    return [pl.BlockSpec(memory_space=pl.ANY)] * len(prev_outputs), prev_outputs


def _ret_prompt(qkvg, bsz, seq, mem_kv, layer, tables, w_out, x, g, b, carried):
    nblk = ATT_BLOCKS_PER_STEP
    rows = nblk * RET_CHUNK
    steps = seq // rows
    mixer = layer // 2
    decay, qdec, kdec, cdec = tables
    row_of = lambda b, n: b * steps + n
    cur = lambda w, col: pl.BlockSpec((rows, w), lambda b, n: (row_of(b, n), col))
    tab = pl.BlockSpec((RET_HEADS, RET_CHUNK, RET_CHUNK), lambda b, n: (0, 0, 0))
    in_specs = [
        pl.BlockSpec(memory_space=pltpu.SMEM), tab, tab, tab,
        cur(RET_QK, 0), cur(RET_QK, 1), cur(RET_V, 1), cur(RET_V, 2),
        cur(MEM_DIM, (2 * RET_QK + 2 * RET_V) // MEM_DIM),
    ] + _mixer_tail_specs(layer, row_of, rows)
    n_in = len(in_specs)
    carry_specs, carry_args = _carry(carried)
    aliases = {n_in - 3: 0}
    aliases.update({n_in + c: 1 + c for c in range(len(carry_args))})
    return pl.pallas_call(
        _drop_carried(functools.partial(_ret_prompt_kernel, steps=steps, nblk=nblk), n_in, len(carry_args)),
        grid=(bsz, steps),
        in_specs=in_specs + carry_specs,
        out_specs=[
            pl.BlockSpec((rows, D_MODEL), lambda b, n: (row_of(b, n), 0)),
            pl.BlockSpec((None, 1, RET_HEADS, RET_DK, RET_DV), lambda b, n: (mixer, b, 0, 0, 0)),
        ],
        out_shape=[
            jax.ShapeDtypeStruct(x.shape, F32),
            jax.ShapeDtypeStruct((DEPTH // 2, bsz, RET_HEADS, RET_DK, RET_DV), F32),
        ],
        scratch_shapes=[pltpu.VMEM((RET_HEADS, RET_DK, RET_DV), F32),
                        pltpu.VMEM((rows, D_MODEL), BF16),
                        pltpu.VMEM((D_MODEL, D_MODEL), BF16)],
        input_output_aliases=aliases,
        compiler_params=_params("arbitrary", "arbitrary"),
        name="ret_prompt",
    )(cdec, decay, qdec, kdec, qkvg, qkvg, qkvg, qkvg, qkvg, mem_kv, mem_kv, w_out, x, g, b, *carry_args)


def _mem_attend_row(q, mk_ref, mv_ref):
    s = jnp.sum(mk_ref[0] * q[None], axis=-1, keepdims=True) * ATT_SCALE
    p = jnp.exp(s - jnp.max(s, axis=0, keepdims=True))
    p = p * (1.0 / jnp.sum(p, axis=0, keepdims=True))
    return jnp.sum(p * mv_ref[0], axis=0)


def _swa_step_kernel(sink_ref, rows_ref, kbuf_ref, vbuf_ref, mk_ref, mv_ref, o_ref, nk_ref, nv_ref):
    krow0 = SWA_Q_HEADS
    vrow0 = krow0 + SWA_KV_HEADS
    mrow0 = vrow0 + SWA_KV_HEADS
    wb = kbuf_ref.shape[1]
    kb = kbuf_ref[0]
    vb = vbuf_ref[0]
    k_new = rows_ref[0, krow0:krow0 + SWA_KV_HEADS, :]
    v_new = rows_ref[0, vrow0:vrow0 + SWA_KV_HEADS, :]
    for g in range(SWA_GROUP):
        group_rows = pl.ds(g, SWA_KV_HEADS, stride=SWA_GROUP)
        q = rows_ref[0, group_rows, :]
        s_buf = jnp.sum(kb * q[None], axis=-1, keepdims=True) * ATT_SCALE
        s_new = jnp.sum(k_new * q, axis=-1, keepdims=True) * ATT_SCALE
        sink = sink_ref[g]
        m = jnp.maximum(jnp.maximum(jnp.max(s_buf, axis=0), s_new), sink)
        p_buf = jnp.exp(s_buf - m[None])
        p_new = jnp.exp(s_new - m)
        inv = 1.0 / (jnp.sum(p_buf, axis=0) + p_new + jnp.exp(sink - m))
        o_ref[0, group_rows, :] = jnp.sum((p_buf * inv[None]) * vb, axis=0) + (p_new * inv) * v_new
    nk_ref[0, 0:wb - 1] = kbuf_ref[0, 1:wb]
    nv_ref[0, 0:wb - 1] = vbuf_ref[0, 1:wb]
    nk_ref[0, wb - 1] = k_new
    nv_ref[0, wb - 1] = v_new
    o_ref[0, SWA_Q_HEADS:SWA_Q_HEADS + MEM_HEADS, :] = _mem_attend_row(
        rows_ref[0, mrow0:mrow0 + MEM_HEADS, :], mk_ref, mv_ref)


def _swa_step(rows, cache_k, cache_v, j, mem_k, mem_v, i, sinks, carried):
    bsz = rows.shape[0]
    wb = cache_k.shape[2]
    cache = pl.BlockSpec((None, 1, wb, SWA_KV_HEADS, HEAD_DIM), lambda b: (j, b, 0, 0, 0))
    mem = pl.BlockSpec((None, 1, MEM_LEN, MEM_HEADS, HEAD_DIM), lambda b: (i, b, 0, 0, 0))
    nrows = D_MODEL // HEAD_DIM
    sink_gk = sinks.reshape(SWA_KV_HEADS, SWA_GROUP).T.reshape(SWA_GROUP, SWA_KV_HEADS, 1)
    in_specs = [
        pl.BlockSpec(sink_gk.shape, lambda b: (0, 0, 0)),
        pl.BlockSpec((1,) + rows.shape[1:], lambda b: (b, 0, 0)),
        cache, cache, mem, mem,
    ]
    n_in = len(in_specs)
    carry_specs, carry_args = _carry(carried)
    return pl.pallas_call(
        _drop_carried(_swa_step_kernel, n_in, len(carry_args)),
        grid=(bsz,),
        in_specs=in_specs + carry_specs,
        out_specs=[pl.BlockSpec((1, nrows, HEAD_DIM), lambda b: (b, 0, 0)), cache, cache],
        out_shape=[
            jax.ShapeDtypeStruct((bsz, nrows, HEAD_DIM), F32),
            jax.ShapeDtypeStruct(cache_k.shape, F32),
            jax.ShapeDtypeStruct(cache_v.shape, F32),
        ],
        input_output_aliases={n_in + c: 1 + c for c in range(len(carry_args))},
        compiler_params=_params("parallel"),
        name="swa_step",
    )(sink_gk, rows, cache_k, cache_v, mem_k, mem_v, *carry_args)


def _ret_step_kernel(dec_ref, rows_ref, cols_ref, s_ref, mk_ref, mv_ref, o_ref, s_out_ref):
    krow0 = RET_HEADS
    vrow0 = 2 * RET_HEADS
    grow0 = vrow0 + 2 * RET_HEADS
    mrow0 = grow0 + 2 * RET_HEADS
    for h in range(RET_HEADS):
        q_row = rows_ref[0, h:h + 1, :]
        k_row = rows_ref[0, krow0 + h:krow0 + h + 1, :] * RET_K_SCALE
        q_col = cols_ref[0, :, h:h + 1]
        k_col = cols_ref[0, :, krow0 + h:krow0 + h + 1] * RET_K_SCALE
        inner = jnp.sum(q_row * k_row, axis=1, keepdims=True) * dec_ref[0, h]
        qd = q_col * dec_ref[1, h]
        kd = k_col * dec_ref[2, h]
        halves = []
        for t in range(2):
            lsl = slice(t * HEAD_DIM, (t + 1) * HEAD_DIM)
            v = rows_ref[0, vrow0 + 2 * h + t:vrow0 + 2 * h + t + 1, :]
            st = s_ref[0, h, :, lsl]
            halves.append(inner * v + jnp.sum(qd * st, axis=0, keepdims=True))
            s_out_ref[0, h, :, lsl] = dec_ref[3, h] * st + kd * v
        mu = (jnp.sum(halves[0], axis=1, keepdims=True)
              + jnp.sum(halves[1], axis=1, keepdims=True)) * (1.0 / RET_DV)
        cen = [o - mu for o in halves]
        var = (jnp.sum(cen[0] * cen[0], axis=1, keepdims=True)
               + jnp.sum(cen[1] * cen[1], axis=1, keepdims=True)) * (1.0 / RET_DV)
        rstd = lax.rsqrt(var + HEAD_NORM_EPS)
        for t in range(2):
            gate = rows_ref[0, grow0 + 2 * h + t:grow0 + 2 * h + t + 1, :]
            o_ref[0, 2 * h + t:2 * h + t + 1, :] = jax.nn.silu(gate) * (cen[t] * rstd)
    o_ref[0, 2 * RET_HEADS:2 * RET_HEADS + MEM_HEADS, :] = _mem_attend_row(
        rows_ref[0, mrow0:mrow0 + MEM_HEADS, :], mk_ref, mv_ref)


def _ret_step(rows, cols, state, j, mem_k, mem_v, i, dec, carried):
    bsz = rows.shape[0]
    st = pl.BlockSpec((None, 1, RET_HEADS, RET_DK, RET_DV), lambda b: (j, b, 0, 0, 0))
    mem = pl.BlockSpec((None, 1, MEM_LEN, MEM_HEADS, HEAD_DIM), lambda b: (i, b, 0, 0, 0))
    nrows = D_MODEL // HEAD_DIM
    in_specs = [
        pl.BlockSpec(memory_space=pltpu.SMEM),
        pl.BlockSpec((1,) + rows.shape[1:], lambda b: (b, 0, 0)),
        pl.BlockSpec((1,) + cols.shape[1:], lambda b: (b, 0, 0)),
        st, mem, mem,
    ]
    n_in = len(in_specs)
    carry_specs, carry_args = _carry(carried)
    return pl.pallas_call(
        _drop_carried(_ret_step_kernel, n_in, len(carry_args)),
        grid=(bsz,),
        in_specs=in_specs + carry_specs,
        out_specs=[pl.BlockSpec((1, nrows, HEAD_DIM), lambda b: (b, 0, 0)), st],
        out_shape=[
            jax.ShapeDtypeStruct((bsz, nrows, HEAD_DIM), F32),
            jax.ShapeDtypeStruct(state.shape, F32),
        ],
        input_output_aliases={n_in + c: 1 + c for c in range(len(carry_args))},
        compiler_params=_params("parallel"),
        name="ret_step",
    )(dec, rows, cols, state, mem_k, mem_v, *carry_args)


def _rope_tables(pos):
    half = ROPE_DIM // 2
    inv = ROPE_THETA ** (-jnp.arange(half, dtype=F32) / half)
    ang = pos.astype(F32)[:, None] * inv[None, :]
    cos, sin = jnp.cos(ang), jnp.sin(ang)
    n = pos.shape[0]
    rest = HEAD_DIM - ROPE_DIM
    c = jnp.concatenate([cos, cos, jnp.ones((n, rest), F32)], axis=-1)
    a = jnp.concatenate([-sin, jnp.zeros((n, HEAD_DIM - half), F32)], axis=-1)
    b = jnp.concatenate([jnp.zeros((n, half), F32), sin, jnp.zeros((n, rest), F32)], axis=-1)
    return (c, a, b), (HEAD_DIM - half, half)


def _ret_rot_tables(pos):
    half = RET_DK // 2
    angle = RET_ROT_BASE ** (-jnp.linspace(0.0, 1.0, half, dtype=F32))
    ang = pos.astype(F32)[:, None] * angle[None, :]
    cos, sin = jnp.cos(ang), jnp.sin(ang)
    n = pos.shape[0]
    zero = jnp.zeros_like(sin)
    c = jnp.stack([cos, cos], axis=-1).reshape(n, RET_DK)
    a = jnp.stack([-sin, zero], axis=-1).reshape(n, RET_DK)
    b = jnp.stack([zero, sin], axis=-1).reshape(n, RET_DK)
    return (c, a, b), (RET_DK - 1, 1)


def _ret_decay(chunk):
    log_g = jnp.log1p(-jnp.exp2(-5.0 - jnp.arange(RET_HEADS, dtype=F32)))
    n = jnp.arange(chunk, dtype=F32)
    rel = n[:, None] - n[None, :]
    decay = jnp.where(rel >= 0, jnp.exp(jnp.maximum(rel, 0.0) * log_g[:, None, None]), 0.0)
    q_dec = jnp.exp((n + 1.0) * log_g[:, None])
    k_dec = jnp.exp((chunk - 1.0 - n) * log_g[:, None])
    c_dec = jnp.exp(chunk * log_g)
    return decay, q_dec, k_dec, c_dec


def kernel(x_prompt, x_sample, cache_swa_k, cache_swa_v, state_ret, cache_mem_k, cache_mem_v,
           mem_prompt, ln_g, ln_b, ffn_w_gu, ffn_w_down, w_mem_kv, swa_w_in, swa_w_out,
           swa_sinks, ret_w_in, ret_w_out):
    bp, seq, _ = x_prompt.shape
    bs, sample_seq, _ = x_sample.shape
    assert sample_seq == 1
    mp, ms = bp * seq, bs
    assert mp % ms == 0

    mem2 = mem_prompt.reshape(bp * MEM_LEN, D_MODEL)
    mem_kv = _mem_kv(mem2, w_mem_kv).reshape(DEPTH, bp, MEM_LEN, 2 * MEM_DIM)
    mem_k_prompt = mem_kv[..., :MEM_DIM].reshape(DEPTH, bp, MEM_LEN, MEM_HEADS, HEAD_DIM)
    mem_v_prompt = mem_kv[..., MEM_DIM:].reshape(DEPTH, bp, MEM_LEN, MEM_HEADS, HEAD_DIM)

    tm, tiles = _row_tiles(mp + ms, FFN_MAX_ROWS)
    m_all = tm * tiles
    tm_proj, _ = _row_tiles(m_all, PROJ_MAX_ROWS)
    pad = m_all - mp - ms
    tail = (mp - (tiles - 1) * tm, ms)
    assert tail[0] >= 0 and tail[0] % 8 == 0 and ms % 8 == 0 and tail[0] + ms <= tm
    def per_row(table):
        return jnp.concatenate([jnp.tile(table[:seq], (bp, 1)),
                                jnp.broadcast_to(table[seq:], (ms, table.shape[1])),
                                jnp.zeros((pad, table.shape[1]), F32)], axis=0)

    pos = jnp.concatenate([jnp.arange(seq, dtype=jnp.int32), jnp.full((1,), PAST_LEN, jnp.int32)])
    rope_tabs, rope_shifts = _rope_tables(pos)
    rot_tabs, rot_shifts = _ret_rot_tables(pos)
    rope_tabs = tuple(per_row(t) for t in rope_tabs)
    rot_tabs = tuple(per_row(t) for t in rot_tabs)
    decay, q_dec, k_dec, c_dec = _ret_decay(RET_CHUNK)
    ret_tabs = (decay, jnp.broadcast_to(q_dec[:, :, None], decay.shape),
                jnp.broadcast_to(k_dec[:, :, None], decay.shape), c_dec)
    decay, q_dec, k_dec, c_dec = _ret_decay(sample_seq)
    step_dec = jnp.stack([decay[:, 0, 0], q_dec[:, 0], k_dec[:, 0], c_dec])

    def ln(i, s):
        return ln_g[i, s].reshape(1, D_MODEL), ln_b[i, s].reshape(1, D_MODEL)

    swa_k_prompt, swa_v_prompt = [], []
    swa_sample = None
    ret_prompt = None
    ret_sample = None
    for i in range(DEPTH):
        j = i // 2
        if i == 0:
            x = _ffn(x_prompt.reshape(mp, D_MODEL), ffn_w_gu, ffn_w_down, i, 0, *ln(i, 0), tm, tiles, tail,
                     x_sample=x_sample.reshape(ms, D_MODEL))
        else:
            x = _ffn(x, ffn_w_gu, ffn_w_down, i, 0, *ln(i, 0), tm, tiles, tail)
        if i % 2 == 0:
            qkv = _proj(x, swa_w_in, j, tm_proj, rot_heads=SWA_Q_HEADS + SWA_KV_HEADS, tables=rope_tabs,
                        shifts=rope_shifts)
            x, k_tail, v_tail = _swa_prompt(qkv, bp, seq, mem_kv, i, swa_sinks[j], swa_w_out, x, *ln(i, 1))
            swa_k_prompt.append(k_tail.reshape(bp, WINDOW, SWA_KV_HEADS, HEAD_DIM))
            swa_v_prompt.append(v_tail.reshape(bp, WINDOW, SWA_KV_HEADS, HEAD_DIM))
            rows = qkv[mp:mp + ms].reshape(ms, SWA_IN_WIDTH // HEAD_DIM, HEAD_DIM)
            att, *swa_sample = _swa_step(rows, cache_swa_k, cache_swa_v, j, cache_mem_k, cache_mem_v, i,
                                         swa_sinks[j], swa_sample)
            x = _out_ln(att.reshape(ms, D_MODEL), swa_w_out, j, x, *ln(i, 1), mp)
        else:
            qkvg = _proj(x, ret_w_in, j, tm_proj, rot_heads=2 * RET_HEADS, tables=rot_tabs,
                         shifts=rot_shifts)
            x, ret_prompt = _ret_prompt(qkvg, bp, seq, mem_kv, i, ret_tabs, ret_w_out, x, *ln(i, 1),
                                        None if ret_prompt is None else [ret_prompt])
            rows = qkvg[mp:mp + ms].reshape(ms, RET_IN_WIDTH // HEAD_DIM, HEAD_DIM)
            att, ret_sample = _ret_step(rows, jnp.swapaxes(rows, 1, 2), state_ret, j, cache_mem_k,
                                        cache_mem_v, i, step_dec,
                                        None if ret_sample is None else [ret_sample])
            x = _out_ln(att.reshape(ms, D_MODEL), ret_w_out, j, x, *ln(i, 1), mp)
        x = _ffn(x, ffn_w_gu, ffn_w_down, i, 1, *ln(i, 2), tm, tiles, tail, split_out=(i == DEPTH - 1))

    y_prompt = x[0].reshape(bp, seq, D_MODEL)
    y_sample = x[1].reshape(bs, sample_seq, D_MODEL)
    return (y_prompt, y_sample, jnp.stack(swa_k_prompt), jnp.stack(swa_v_prompt), swa_sample[0],
            swa_sample[1], ret_prompt, ret_sample, mem_k_prompt, mem_v_prompt)
```

```python
import functools

import jax
import jax.numpy as jnp
from jax import lax
from jax.experimental import pallas as pl
from jax.experimental.pallas import tpu as pltpu

F32 = jnp.float32
BF16 = jnp.bfloat16

D_MODEL = 2048
DEPTH = 4
PAST_LEN = 16384
HEAD_DIM = 128
MEM_LEN = 256
MEM_HEADS = 4
MEM_DIM = MEM_HEADS * HEAD_DIM
SELF_WIDTH = D_MODEL - MEM_DIM
SWA_Q_HEADS = SELF_WIDTH // HEAD_DIM
SWA_KV_HEADS = SWA_Q_HEADS // 3
SWA_GROUP = SWA_Q_HEADS // SWA_KV_HEADS
WINDOW = 128
ROPE_THETA = 500000.0
ROPE_DIM = HEAD_DIM // 4
RET_DK = 128
RET_DV = 2 * RET_DK
RET_HEADS = SELF_WIDTH // RET_DV
RET_CHUNK = 128
RET_ROT_BASE = 10000.0
D_FF = ((8 * D_MODEL // 3 + 255) // 256) * 256
LN_EPS = 1e-5
HEAD_NORM_EPS = 1e-6
ALPHA = (2.0 * DEPTH) ** 0.25
NEG_INF = -1e30
ATT_SCALE = HEAD_DIM ** -0.5
RET_K_SCALE = RET_DK ** -0.5

SWA_Q = SWA_Q_HEADS * HEAD_DIM
SWA_KV = SWA_KV_HEADS * HEAD_DIM
SWA_IN_WIDTH = SWA_Q + 2 * SWA_KV + MEM_DIM
RET_QK = RET_HEADS * RET_DK
RET_V = RET_HEADS * RET_DV
RET_IN_WIDTH = 2 * RET_QK + 2 * RET_V + MEM_DIM

VMEM_LIMIT_BYTES = 58 * 1024 * 1024
LANES = 128
FFN_TILE_K = 256
FFN_MAX_ROWS = 1200
PROJ_MAX_ROWS = 1040
ATT_BLOCKS_PER_STEP = 2
PROJ_TILE_N = 1024
OUT_TILE_N = 512
MAX_ROW_CHUNK = 256
BF16_SUBLANES = 16


def _row_tiles(rows, max_rows):
    tiles = -(-rows // max_rows)
    per_tile = -(-rows // tiles)
    return -(-per_tile // BF16_SUBLANES) * BF16_SUBLANES, tiles


def _row_chunks(tm):
    count = -(-tm // MAX_ROW_CHUNK)
    size = -(-(-(-tm // count)) // BF16_SUBLANES) * BF16_SUBLANES
    return [slice(r, min(r + size, tm)) for r in range(0, tm, size)]


def _params(*sem):
    return pltpu.CompilerParams(dimension_semantics=sem, vmem_limit_bytes=VMEM_LIMIT_BYTES)


def _ln_spec(layer, which):
    return pl.BlockSpec((None, 1, D_MODEL), lambda *_: (3 * layer + which, 0, 0))


def _layer_norm_rows(z, g, b):
    mu = jnp.mean(z, axis=-1, keepdims=True)
    zc = z - mu
    var = jnp.mean(zc * zc, axis=-1, keepdims=True)
    return zc * lax.rsqrt(var + LN_EPS) * g + b


def _ffn_kernel(*refs, nk, tk, tiles, tail, split_in, split_out):
    refs = list(refs)
    x_ref = refs.pop(0)
    xs_ref = refs.pop(0) if split_in else None
    wg_ref, wu_ref, wd_ref, g_ref, b_ref, o_ref = refs[:6]
    os_ref = refs[6] if split_out else None
    xb_ref, wgu_ref = refs[-2:]
    i = pl.program_id(0)
    k = pl.program_id(1)
    chunks = _row_chunks(o_ref.shape[0])

    def load_x(rows):
        xv = x_ref[rows, :]
        if not split_in or rows.stop <= tail[0]:
            return xv
        first, count = tail
        lo, hi = rows.start, rows.stop
        pieces = []
        if lo < first:
            pieces.append(xv[:first - lo])
        s0, s1 = max(lo, first), min(hi, first + count)
        if s0 < s1:
            pieces.append(xs_ref[s0 - first:s1 - first, :])
        if hi > first + count:
            pieces.append(jnp.zeros((hi - max(lo, first + count), xv.shape[1]), xv.dtype))
        return jnp.where(i == tiles - 1, jnp.concatenate(pieces, axis=0), xv)

    def cast_weights():
        wgu_ref[:, :tk] = wg_ref[...].astype(BF16)
        wgu_ref[:, tk:] = wu_ref[...].astype(BF16)
        return wd_ref[...].astype(BF16)

    def down(xb, wd):
        gu = jnp.dot(xb, wgu_ref[...], preferred_element_type=F32)
        h = (jax.nn.silu(gu[:, :tk]) * gu[:, tk:]).astype(BF16)
        return jnp.dot(h, wd, preferred_element_type=F32)

    @pl.when(k == 0)
    def _():
        wd = cast_weights()
        for rows in chunks:
            xb = load_x(rows).astype(BF16)
            xb_ref[rows, :] = xb
            o_ref[rows, :] = down(xb, wd)

    @pl.when(jnp.logical_and(k > 0, k < nk - 1))
    def _():
        wd = cast_weights()
        o_ref[...] += down(xb_ref[...], wd)

    @pl.when(k == nk - 1)
    def _():
        wd = cast_weights()
        for rows in chunks:
            acc = o_ref[rows, :] + down(xb_ref[rows, :], wd)
            z = ALPHA * load_x(rows) + 0.5 * acc
            o_ref[rows, :] = _layer_norm_rows(z, g_ref[...], b_ref[...])
        if split_out:
            @pl.when(i == tiles - 1)
            def _():
                os_ref[...] = o_ref[tail[0]:tail[0] + tail[1], :]


def _ffn(x, w_gu, w_down, layer, slot, g, b, tm, tiles, tail, x_sample=None, split_out=False):
    tk = FFN_TILE_K
    nk = D_FF // tk
    assert nk >= 2
    split_in = x_sample is not None
    rows = pl.BlockSpec((tm, D_MODEL), lambda i, k: (i, 0))
    small = pl.BlockSpec((tail[1], D_MODEL), lambda i, k: (0, 0))
    in_specs = [pl.BlockSpec((tm, D_MODEL), lambda i, k: (i, 0), pipeline_mode=pl.Buffered(1))]
    args = [x]
    if split_in:
        in_specs.append(small)
        args.append(x_sample)
    in_specs += [
        pl.BlockSpec((None, None, D_MODEL, tk), lambda i, k: (layer, slot, 0, k)),
        pl.BlockSpec((None, None, D_MODEL, tk), lambda i, k: (layer, slot, 0, nk + k)),
        pl.BlockSpec((None, None, tk, D_MODEL), lambda i, k: (layer, slot, k, 0)),
        _ln_spec(layer, 2 * slot), _ln_spec(layer, 2 * slot),
    ]
    if split_out:
        prompt_rows = (tiles - 1) * tm + tail[0]
        out_specs = [rows, small]
        out_shape = [jax.ShapeDtypeStruct((prompt_rows, D_MODEL), F32),
                     jax.ShapeDtypeStruct((tail[1], D_MODEL), F32)]
    else:
        out_specs = rows
        out_shape = jax.ShapeDtypeStruct((tiles * tm, D_MODEL), F32)
    return pl.pallas_call(
        functools.partial(_ffn_kernel, nk=nk, tk=tk, tiles=tiles, tail=tail,
                          split_in=split_in, split_out=split_out),
        grid=(tiles, nk),
        in_specs=in_specs,
        out_specs=out_specs,
        out_shape=out_shape,
        scratch_shapes=[pltpu.VMEM((tm, D_MODEL), BF16), pltpu.VMEM((D_MODEL, 2 * tk), BF16)],
        compiler_params=_params("arbitrary", "arbitrary"),
        name="ffn_ln",
    )(*args, w_gu, w_gu, w_down, g, b)


def _proj_kernel(x_ref, w_ref, *rest, rot_heads, shifts):
    if rot_heads:
        c_ref, a_ref, b_ref, o_ref, xb_ref = rest
    else:
        o_ref, xb_ref = rest
    j = pl.program_id(1)

    @pl.when(j == 0)
    def _():
        xb_ref[...] = x_ref[...].astype(BF16)

    def plain():
        o_ref[...] = jnp.dot(xb_ref[...], w_ref[...].astype(BF16), preferred_element_type=F32)

    if not rot_heads:
        plain()
        return

    tm, tn = o_ref.shape
    heads = tn // LANES
    full_tiles, part_heads = divmod(rot_heads, heads)

    def rotated(n_heads):
        wb = w_ref[...].astype(BF16)
        for rows in _row_chunks(tm):
            y = jnp.dot(xb_ref[rows, :], wb, preferred_element_type=F32)
            c, a, b = c_ref[rows, :], a_ref[rows, :], b_ref[rows, :]
            for h in range(heads):
                yh = y[:, h * LANES:(h + 1) * LANES]
                if h < n_heads:
                    yh = yh * c + pltpu.roll(yh, shifts[0], 1) * a + pltpu.roll(yh, shifts[1], 1) * b
                o_ref[rows, h * LANES:(h + 1) * LANES] = yh

    pl.when(j < full_tiles)(lambda: rotated(heads))
    if part_heads:
        pl.when(j == full_tiles)(lambda: rotated(part_heads))
    pl.when(j >= full_tiles + (1 if part_heads else 0))(plain)


def _proj(x, w, layer, tm, rot_heads=0, tables=None, shifts=None):
    m = x.shape[0]
    n = w.shape[-1]
    tn = PROJ_TILE_N
    in_specs = [
        pl.BlockSpec((tm, D_MODEL), lambda i, j: (i, 0)),
        pl.BlockSpec((None, D_MODEL, tn), lambda i, j: (layer, 0, j)),
    ]
    args = [x, w]
    if rot_heads:
        tab = pl.BlockSpec((tm, LANES), lambda i, j: (i, 0))
        in_specs += [tab, tab, tab]
        args += list(tables)
    return pl.pallas_call(
        functools.partial(_proj_kernel, rot_heads=rot_heads, shifts=shifts),
        grid=(pl.cdiv(m, tm), n // tn),
        in_specs=in_specs,
        out_specs=pl.BlockSpec((tm, tn), lambda i, j: (i, j)),
        out_shape=jax.ShapeDtypeStruct((m, n), F32),
        scratch_shapes=[pltpu.VMEM((tm, D_MODEL), BF16)],
        compiler_params=_params("parallel", "arbitrary"),
        name="proj",
    )(*args)


def _mem_kv_kernel(x_ref, w_ref, o_ref, xb_ref):
    @pl.when(pl.program_id(0) == 0)
    def _():
        xb_ref[...] = x_ref[...].astype(BF16)

    o_ref[...] = jnp.dot(xb_ref[...], w_ref[...].astype(BF16), preferred_element_type=F32)


def _mem_kv(mem, w):
    rows = mem.shape[0]
    layers, _, n = w.shape
    return pl.pallas_call(
        _mem_kv_kernel,
        grid=(layers,),
        in_specs=[
            pl.BlockSpec((rows, D_MODEL), lambda l: (0, 0)),
            pl.BlockSpec((None, D_MODEL, n), lambda l: (l, 0, 0)),
        ],
        out_specs=pl.BlockSpec((None, rows, n), lambda l: (l, 0, 0)),
        out_shape=jax.ShapeDtypeStruct((layers, rows, n), F32),
        scratch_shapes=[pltpu.VMEM((rows, D_MODEL), BF16)],
        compiler_params=_params("arbitrary"),
        name="mem_kv",
    )(mem, w)


def _out_ln_kernel(att_ref, w_ref, x_ref, g_ref, b_ref, o_ref, y_ref, *, nn, tn):
    n = pl.program_id(1)
    y_ref[n] = jnp.dot(att_ref[...].astype(BF16), w_ref[...].astype(BF16),
                       preferred_element_type=F32)

    @pl.when(n == nn - 1)
    def _():
        cols = [slice(c * tn, (c + 1) * tn) for c in range(nn)]
        total = None
        for c in range(nn):
            z = ALPHA * x_ref[:, cols[c]] + y_ref[c]
            y_ref[c] = z
            part = jnp.sum(z, axis=-1, keepdims=True)
            total = part if total is None else total + part
        mu = total * (1.0 / D_MODEL)
        total = None
        for c in range(nn):
            zc = y_ref[c] - mu
            part = jnp.sum(zc * zc, axis=-1, keepdims=True)
            total = part if total is None else total + part
        rstd = lax.rsqrt(total * (1.0 / D_MODEL) + LN_EPS)
        for c in range(nn):
            o_ref[:, cols[c]] = (y_ref[c] - mu) * rstd * g_ref[:, cols[c]] + b_ref[:, cols[c]]


def _out_ln(att, w_out, trunk_layer, x, g, b, row0):
    layer = trunk_layer // 2
    tm = att.shape[0]
    assert row0 % tm == 0
    tn = OUT_TILE_N
    nn = D_MODEL // tn
    rows = pl.BlockSpec((tm, D_MODEL), lambda i, n: (row0 // tm, 0))
    return pl.pallas_call(
        functools.partial(_out_ln_kernel, nn=nn, tn=tn),
        grid=(1, nn),
        in_specs=[
            pl.BlockSpec((tm, D_MODEL), lambda i, n: (0, 0)),
            pl.BlockSpec((None, D_MODEL, tn), lambda i, n: (layer, 0, n)),
            rows,
            _ln_spec(trunk_layer, 1), _ln_spec(trunk_layer, 1),
        ],
        out_specs=rows,
        out_shape=jax.ShapeDtypeStruct(x.shape, F32),
        scratch_shapes=[pltpu.VMEM((nn, tm, tn), F32)],
        input_output_aliases={2: 0},
        compiler_params=_params("arbitrary", "arbitrary"),
        name="out_ln",
    )(att, w_out, x, g, b)


def _dot_nt(a, b):
    return lax.dot_general(a, b, (((1,), (1,)), ((), ())), preferred_element_type=F32)


def _out_proj_ln(first, att_ref, w_ref, wb_ref, x_ref, g_ref, b_ref, o_ref):
    @pl.when(first)
    def _():
        wb_ref[...] = w_ref[...].astype(BF16)

    y = jnp.dot(att_ref[...], wb_ref[...], preferred_element_type=F32)
    o_ref[...] = _layer_norm_rows(ALPHA * x_ref[...] + y, g_ref[...], b_ref[...])


def _mem_attend_block(qm_ref, mk_ref, mv_ref, att_ref, col0):
    for h in range(MEM_HEADS):
        sl = slice(h * HEAD_DIM, (h + 1) * HEAD_DIM)
        q = qm_ref[:, sl].astype(BF16)
        s = _dot_nt(q, mk_ref[0, :, sl].astype(BF16)) * ATT_SCALE
        p = jnp.exp(s - jnp.max(s, axis=-1, keepdims=True))
        p = p * (1.0 / jnp.sum(p, axis=-1, keepdims=True))
        o = jnp.dot(p.astype(BF16), mv_ref[0, :, sl].astype(BF16), preferred_element_type=F32)
        att_ref[:, col0 + h * HEAD_DIM:col0 + (h + 1) * HEAD_DIM] = o.astype(att_ref.dtype)


def _swa_prompt_kernel(sink_ref, q_ref, kc_ref, kp_ref, vc_ref, vp_ref, qm_ref, mk_ref, mv_ref,
                       w_ref, x_ref, g_ref, b_ref, o_ref, kt_ref, vt_ref, s_ref, p_ref, att_ref, wb_ref,
                       *, nblk, steps):
    n = pl.program_id(1)
    nq = SWA_Q_HEADS
    w2 = 2 * WINDOW

    def prev_cur(cur_ref, prev_ref, i, sl):
        rows = slice(i * WINDOW, (i + 1) * WINDOW)
        before = prev_ref[:, sl] if i == 0 else cur_ref[(i - 1) * WINDOW:i * WINDOW, sl]
        return jnp.concatenate([before, cur_ref[rows, sl]], axis=0).astype(BF16)

    for i in range(nblk):
        rows = slice(i * WINDOW, (i + 1) * WINDOW)
        for h in range(SWA_KV_HEADS):
            sl = slice(h * HEAD_DIM, (h + 1) * HEAD_DIM)
            q3 = jnp.concatenate(
                [q_ref[rows, (h * SWA_GROUP + g) * HEAD_DIM:(h * SWA_GROUP + g + 1) * HEAD_DIM]
                 for g in range(SWA_GROUP)], axis=0).astype(BF16)
            s = _dot_nt(q3, prev_cur(kc_ref, kp_ref, i, sl)) * ATT_SCALE
            s_ref[i, h * SWA_GROUP:(h + 1) * SWA_GROUP] = s.reshape(SWA_GROUP, WINDOW, w2)
    for h in range(MEM_HEADS):
        sl = slice(h * HEAD_DIM, (h + 1) * HEAD_DIM)
        sm = _dot_nt(qm_ref[:, sl].astype(BF16), mk_ref[0, :, sl].astype(BF16)) * ATT_SCALE
        s_ref[:, nq + h] = sm.reshape(nblk, WINDOW, w2)

    qi = lax.broadcasted_iota(jnp.int32, (WINDOW, w2), 0)
    kj = lax.broadcasted_iota(jnp.int32, (WINDOW, w2), 1)
    sink = sink_ref[...]
    for i in range(nblk):
        first_key = jnp.where(n > 0, qi, WINDOW) if i == 0 else qi
        ok = (kj >= first_key) & (kj <= qi + WINDOW)
        s = jnp.where(ok[None], s_ref[i, 0:nq], NEG_INF)
        m = jnp.maximum(jnp.max(s, axis=-1, keepdims=True), sink)
        p = jnp.exp(s - m)
        den = jnp.sum(p, axis=-1, keepdims=True) + jnp.exp(sink - m)
        p_ref[i, 0:nq] = (p * (1.0 / den)).astype(BF16)
    s = s_ref[:, nq:nq + MEM_HEADS]
    p = jnp.exp(s - jnp.max(s, axis=-1, keepdims=True))
    p_ref[:, nq:nq + MEM_HEADS] = (p * (1.0 / jnp.sum(p, axis=-1, keepdims=True))).astype(BF16)

    for i in range(nblk):
        rows = slice(i * WINDOW, (i + 1) * WINDOW)
        for h in range(SWA_KV_HEADS):
            sl = slice(h * HEAD_DIM, (h + 1) * HEAD_DIM)
            p3 = p_ref[i, h * SWA_GROUP:(h + 1) * SWA_GROUP].reshape(SWA_GROUP * WINDOW, w2)
            o = jnp.dot(p3, prev_cur(vc_ref, vp_ref, i, sl), preferred_element_type=F32)
            for g in range(SWA_GROUP):
                hq = h * SWA_GROUP + g
                att_ref[rows, hq * HEAD_DIM:(hq + 1) * HEAD_DIM] = (
                    o[g * WINDOW:(g + 1) * WINDOW].astype(BF16))
    for h in range(MEM_HEADS):
        sl = slice(h * HEAD_DIM, (h + 1) * HEAD_DIM)
        pm = p_ref[:, nq + h].reshape(nblk * WINDOW, w2)
        om = jnp.dot(pm, mv_ref[0, :, sl].astype(BF16), preferred_element_type=F32)
        att_ref[:, SWA_Q + h * HEAD_DIM:SWA_Q + (h + 1) * HEAD_DIM] = om.astype(BF16)

    @pl.when(n == steps - 1)
    def _():
        last = slice((nblk - 1) * WINDOW, nblk * WINDOW)
        kt_ref[0] = kc_ref[last, :]
        vt_ref[0] = vc_ref[last, :]

    first = jnp.logical_and(pl.program_id(0) == 0, n == 0)
    _out_proj_ln(first, att_ref, w_ref, wb_ref, x_ref, g_ref, b_ref, o_ref)


def _mixer_tail_specs(layer, row_of, rows):
    mem = (None, 1, MEM_LEN, MEM_DIM)
    vec = _ln_spec(layer, 1)
    mixer = layer // 2
    return [
        pl.BlockSpec(mem, lambda b, n: (layer, b, 0, 0)),
        pl.BlockSpec(mem, lambda b, n: (layer, b, 0, 1)),
        pl.BlockSpec((None, D_MODEL, D_MODEL), lambda b, n: (mixer, 0, 0), pipeline_mode=pl.Buffered(1)),
        pl.BlockSpec((rows, D_MODEL), lambda b, n: (row_of(b, n), 0)),
        vec, vec,
    ]


def _swa_prompt(qkv, bsz, seq, mem_kv, layer, sinks, w_out, x, g, b):
    nblk = ATT_BLOCKS_PER_STEP
    rows = nblk * WINDOW
    steps = seq // rows
    assert seq % rows == 0 and seq >= WINDOW
    tail = pl.BlockSpec((1, WINDOW, SWA_KV), lambda b, n: (b, 0, 0))
    blocks_per_seq = seq // WINDOW
    kcol = SWA_Q // SWA_KV
    vcol = kcol + 1
    mcol = vcol + 1
    row_of = lambda b, n: b * steps + n
    prev_of = lambda b, n: b * blocks_per_seq + jnp.maximum(n * nblk - 1, 0)
    cur = lambda w, col: pl.BlockSpec((rows, w), lambda b, n: (row_of(b, n), col))
    prev = lambda col: pl.BlockSpec((WINDOW, SWA_KV), lambda b, n: (prev_of(b, n), col))
    nheads = SWA_Q_HEADS + MEM_HEADS
    assert MEM_LEN == 2 * WINDOW
    in_specs = [
        pl.BlockSpec((SWA_Q_HEADS, 1, 1), lambda b, n: (0, 0, 0)),
        cur(SWA_Q, 0), cur(SWA_KV, kcol), prev(kcol), cur(SWA_KV, vcol), prev(vcol), cur(MEM_DIM, mcol),
    ] + _mixer_tail_specs(layer, row_of, rows)
    return pl.pallas_call(
        functools.partial(_swa_prompt_kernel, nblk=nblk, steps=steps),
        grid=(bsz, steps),
        in_specs=in_specs,
        out_specs=[pl.BlockSpec((rows, D_MODEL), lambda b, n: (row_of(b, n), 0)), tail, tail],
        out_shape=[jax.ShapeDtypeStruct(x.shape, F32),
                   jax.ShapeDtypeStruct((bsz, WINDOW, SWA_KV), F32),
                   jax.ShapeDtypeStruct((bsz, WINDOW, SWA_KV), F32)],
        scratch_shapes=[pltpu.VMEM((nblk, nheads, WINDOW, 2 * WINDOW), F32),
                        pltpu.VMEM((nblk, nheads, WINDOW, 2 * WINDOW), BF16),
                        pltpu.VMEM((rows, D_MODEL), BF16),
                        pltpu.VMEM((D_MODEL, D_MODEL), BF16)],
        input_output_aliases={len(in_specs) - 3: 0},
        compiler_params=_params("arbitrary", "arbitrary"),
        name="swa_prompt",
    )(sinks.reshape(SWA_Q_HEADS, 1, 1), qkv, qkv, qkv, qkv, qkv, qkv, mem_kv, mem_kv, w_out, x, g, b)


def _head_norm_gate(o, gate):
    mu = jnp.mean(o, axis=-1, keepdims=True)
    oc = o - mu
    var = jnp.mean(oc * oc, axis=-1, keepdims=True)
    return jax.nn.silu(gate) * (oc * lax.rsqrt(var + HEAD_NORM_EPS))


def _ret_prompt_kernel(cdec_ref, decay_ref, qdec_ref, kdec_ref, q_ref, k_ref, v_ref, gate_ref, qm_ref,
                       mk_ref, mv_ref, w_ref, x_ref, g_ref, b_ref, o_ref, s_out_ref,
                       state_ref, att_ref, wb_ref, *, steps, nblk):
    c = pl.program_id(1)

    @pl.when(c == 0)
    def _():
        state_ref[...] = jnp.zeros_like(state_ref)

    for i in range(nblk):
        rows = slice(i * RET_CHUNK, (i + 1) * RET_CHUNK)
        for h in range(RET_HEADS):
            ksl = slice(h * RET_DK, (h + 1) * RET_DK)
            vsl = slice(h * RET_DV, (h + 1) * RET_DV)
            qc = q_ref[rows, ksl]
            kc = k_ref[rows, ksl] * RET_K_SCALE
            vb = v_ref[rows, vsl].astype(BF16)
            st = state_ref[h]
            inner = _dot_nt(qc.astype(BF16), kc.astype(BF16)) * decay_ref[h]
            o = (jnp.dot(inner.astype(BF16), vb, preferred_element_type=F32)
                 + jnp.dot((qc * qdec_ref[h]).astype(BF16), st.astype(BF16),
                           preferred_element_type=F32))
            kd = (kc * kdec_ref[h]).astype(BF16)
            state_ref[h] = cdec_ref[h] * st + lax.dot_general(
                kd, vb, (((0,), (0,)), ((), ())), preferred_element_type=F32)
            att_ref[rows, vsl] = _head_norm_gate(o, gate_ref[rows, vsl]).astype(BF16)
    _mem_attend_block(qm_ref, mk_ref, mv_ref, att_ref, RET_V)

    @pl.when(c == steps - 1)
    def _():
        s_out_ref[0] = state_ref[...]

    first = jnp.logical_and(pl.program_id(0) == 0, c == 0)
    _out_proj_ln(first, att_ref, w_ref, wb_ref, x_ref, g_ref, b_ref, o_ref)


def _drop_carried(kernel, first, count):
    def body(*refs):
        return kernel(*refs[:first], *refs[first + count:])
    return body


def _carry(prev_outputs):
    prev_outputs = [] if prev_outputs is None else list(prev_outputs)
    return [pl.BlockSpec(memory_space=pl.ANY)] * len(prev_outputs), prev_outputs


def _ret_prompt(qkvg, bsz, seq, mem_kv, layer, tables, w_out, x, g, b, carried):
    nblk = ATT_BLOCKS_PER_STEP
    rows = nblk * RET_CHUNK
    steps = seq // rows
    mixer = layer // 2
    decay, qdec, kdec, cdec = tables
    row_of = lambda b, n: b * steps + n
    cur = lambda w, col: pl.BlockSpec((rows, w), lambda b, n: (row_of(b, n), col))
    tab = pl.BlockSpec((RET_HEADS, RET_CHUNK, RET_CHUNK), lambda b, n: (0, 0, 0))
    in_specs = [
        pl.BlockSpec(memory_space=pltpu.SMEM), tab, tab, tab,
        cur(RET_QK, 0), cur(RET_QK, 1), cur(RET_V, 1), cur(RET_V, 2),
        cur(MEM_DIM, (2 * RET_QK + 2 * RET_V) // MEM_DIM),
    ] + _mixer_tail_specs(layer, row_of, rows)
    n_in = len(in_specs)
    carry_specs, carry_args = _carry(carried)
    aliases = {n_in - 3: 0}
    aliases.update({n_in + c: 1 + c for c in range(len(carry_args))})
    return pl.pallas_call(
        _drop_carried(functools.partial(_ret_prompt_kernel, steps=steps, nblk=nblk), n_in, len(carry_args)),
        grid=(bsz, steps),
        in_specs=in_specs + carry_specs,
        out_specs=[
            pl.BlockSpec((rows, D_MODEL), lambda b, n: (row_of(b, n), 0)),
            pl.BlockSpec((None, 1, RET_HEADS, RET_DK, RET_DV), lambda b, n: (mixer, b, 0, 0, 0)),
        ],
        out_shape=[
            jax.ShapeDtypeStruct(x.shape, F32),
            jax.ShapeDtypeStruct((DEPTH // 2, bsz, RET_HEADS, RET_DK, RET_DV), F32),
        ],
        scratch_shapes=[pltpu.VMEM((RET_HEADS, RET_DK, RET_DV), F32),
                        pltpu.VMEM((rows, D_MODEL), BF16),
                        pltpu.VMEM((D_MODEL, D_MODEL), BF16)],
        input_output_aliases=aliases,
        compiler_params=_params("arbitrary", "arbitrary"),
        name="ret_prompt",
    )(cdec, decay, qdec, kdec, qkvg, qkvg, qkvg, qkvg, qkvg, mem_kv, mem_kv, w_out, x, g, b, *carry_args)


def _mem_attend_row(q, mk_ref, mv_ref):
    s = jnp.sum(mk_ref[0] * q[None], axis=-1, keepdims=True) * ATT_SCALE
    p = jnp.exp(s - jnp.max(s, axis=0, keepdims=True))
    p = p * (1.0 / jnp.sum(p, axis=0, keepdims=True))
    return jnp.sum(p * mv_ref[0], axis=0)


def _swa_step_kernel(sink_ref, rows_ref, kbuf_ref, vbuf_ref, mk_ref, mv_ref, o_ref, nk_ref, nv_ref):
    krow0 = SWA_Q_HEADS
    vrow0 = krow0 + SWA_KV_HEADS
    mrow0 = vrow0 + SWA_KV_HEADS
    wb = kbuf_ref.shape[1]
    kb = kbuf_ref[0]
    vb = vbuf_ref[0]
    k_new = rows_ref[0, krow0:krow0 + SWA_KV_HEADS, :]
    v_new = rows_ref[0, vrow0:vrow0 + SWA_KV_HEADS, :]
    for g in range(SWA_GROUP):
        group_rows = pl.ds(g, SWA_KV_HEADS, stride=SWA_GROUP)
        q = rows_ref[0, group_rows, :]
        s_buf = jnp.sum(kb * q[None], axis=-1, keepdims=True) * ATT_SCALE
        s_new = jnp.sum(k_new * q, axis=-1, keepdims=True) * ATT_SCALE
        sink = sink_ref[g]
        m = jnp.maximum(jnp.maximum(jnp.max(s_buf, axis=0), s_new), sink)
        p_buf = jnp.exp(s_buf - m[None])
        p_new = jnp.exp(s_new - m)
        inv = 1.0 / (jnp.sum(p_buf, axis=0) + p_new + jnp.exp(sink - m))
        o_ref[0, group_rows, :] = jnp.sum((p_buf * inv[None]) * vb, axis=0) + (p_new * inv) * v_new
    nk_ref[0, 0:wb - 1] = kbuf_ref[0, 1:wb]
    nv_ref[0, 0:wb - 1] = vbuf_ref[0, 1:wb]
    nk_ref[0, wb - 1] = k_new
    nv_ref[0, wb - 1] = v_new
    o_ref[0, SWA_Q_HEADS:SWA_Q_HEADS + MEM_HEADS, :] = _mem_attend_row(
        rows_ref[0, mrow0:mrow0 + MEM_HEADS, :], mk_ref, mv_ref)


def _swa_step(rows, cache_k, cache_v, j, mem_k, mem_v, i, sinks, carried):
    bsz = rows.shape[0]
    wb = cache_k.shape[2]
    cache = pl.BlockSpec((None, 1, wb, SWA_KV_HEADS, HEAD_DIM), lambda b: (j, b, 0, 0, 0))
    mem = pl.BlockSpec((None, 1, MEM_LEN, MEM_HEADS, HEAD_DIM), lambda b: (i, b, 0, 0, 0))
    nrows = D_MODEL // HEAD_DIM
    sink_gk = sinks.reshape(SWA_KV_HEADS, SWA_GROUP).T.reshape(SWA_GROUP, SWA_KV_HEADS, 1)
    in_specs = [
        pl.BlockSpec(sink_gk.shape, lambda b: (0, 0, 0)),
        pl.BlockSpec((1,) + rows.shape[1:], lambda b: (b, 0, 0)),
        cache, cache, mem, mem,
    ]
    n_in = len(in_specs)
    carry_specs, carry_args = _carry(carried)
    return pl.pallas_call(
        _drop_carried(_swa_step_kernel, n_in, len(carry_args)),
        grid=(bsz,),
        in_specs=in_specs + carry_specs,
        out_specs=[pl.BlockSpec((1, nrows, HEAD_DIM), lambda b: (b, 0, 0)), cache, cache],
        out_shape=[
            jax.ShapeDtypeStruct((bsz, nrows, HEAD_DIM), F32),
            jax.ShapeDtypeStruct(cache_k.shape, F32),
            jax.ShapeDtypeStruct(cache_v.shape, F32),
        ],
        input_output_aliases={n_in + c: 1 + c for c in range(len(carry_args))},
        compiler_params=_params("parallel"),
        name="swa_step",
    )(sink_gk, rows, cache_k, cache_v, mem_k, mem_v, *carry_args)


def _ret_step_kernel(dec_ref, rows_ref, cols_ref, s_ref, mk_ref, mv_ref, o_ref, s_out_ref):
    krow0 = RET_HEADS
    vrow0 = 2 * RET_HEADS
    grow0 = vrow0 + 2 * RET_HEADS
    mrow0 = grow0 + 2 * RET_HEADS
    for h in range(RET_HEADS):
        q_row = rows_ref[0, h:h + 1, :]
        k_row = rows_ref[0, krow0 + h:krow0 + h + 1, :] * RET_K_SCALE
        q_col = cols_ref[0, :, h:h + 1]
        k_col = cols_ref[0, :, krow0 + h:krow0 + h + 1] * RET_K_SCALE
        inner = jnp.sum(q_row * k_row, axis=1, keepdims=True) * dec_ref[0, h]
        qd = q_col * dec_ref[1, h]
        kd = k_col * dec_ref[2, h]
        halves = []
        for t in range(2):
            lsl = slice(t * HEAD_DIM, (t + 1) * HEAD_DIM)
            v = rows_ref[0, vrow0 + 2 * h + t:vrow0 + 2 * h + t + 1, :]
            st = s_ref[0, h, :, lsl]
            halves.append(inner * v + jnp.sum(qd * st, axis=0, keepdims=True))
            s_out_ref[0, h, :, lsl] = dec_ref[3, h] * st + kd * v
        mu = (jnp.sum(halves[0], axis=1, keepdims=True)
              + jnp.sum(halves[1], axis=1, keepdims=True)) * (1.0 / RET_DV)
        cen = [o - mu for o in halves]
        var = (jnp.sum(cen[0] * cen[0], axis=1, keepdims=True)
               + jnp.sum(cen[1] * cen[1], axis=1, keepdims=True)) * (1.0 / RET_DV)
        rstd = lax.rsqrt(var + HEAD_NORM_EPS)
        for t in range(2):
            gate = rows_ref[0, grow0 + 2 * h + t:grow0 + 2 * h + t + 1, :]
            o_ref[0, 2 * h + t:2 * h + t + 1, :] = jax.nn.silu(gate) * (cen[t] * rstd)
    o_ref[0, 2 * RET_HEADS:2 * RET_HEADS + MEM_HEADS, :] = _mem_attend_row(
        rows_ref[0, mrow0:mrow0 + MEM_HEADS, :], mk_ref, mv_ref)


def _ret_step(rows, cols, state, j, mem_k, mem_v, i, dec, carried):
    bsz = rows.shape[0]
    st = pl.BlockSpec((None, 1, RET_HEADS, RET_DK, RET_DV), lambda b: (j, b, 0, 0, 0))
    mem = pl.BlockSpec((None, 1, MEM_LEN, MEM_HEADS, HEAD_DIM), lambda b: (i, b, 0, 0, 0))
    nrows = D_MODEL // HEAD_DIM
    in_specs = [
        pl.BlockSpec(memory_space=pltpu.SMEM),
        pl.BlockSpec((1,) + rows.shape[1:], lambda b: (b, 0, 0)),
        pl.BlockSpec((1,) + cols.shape[1:], lambda b: (b, 0, 0)),
        st, mem, mem,
    ]
    n_in = len(in_specs)
    carry_specs, carry_args = _carry(carried)
    return pl.pallas_call(
        _drop_carried(_ret_step_kernel, n_in, len(carry_args)),
        grid=(bsz,),
        in_specs=in_specs + carry_specs,
        out_specs=[pl.BlockSpec((1, nrows, HEAD_DIM), lambda b: (b, 0, 0)), st],
        out_shape=[
            jax.ShapeDtypeStruct((bsz, nrows, HEAD_DIM), F32),
            jax.ShapeDtypeStruct(state.shape, F32),
        ],
        input_output_aliases={n_in + c: 1 + c for c in range(len(carry_args))},
        compiler_params=_params("parallel"),
        name="ret_step",
    )(dec, rows, cols, state, mem_k, mem_v, *carry_args)


def _rope_tables(pos):
    half = ROPE_DIM // 2
    inv = ROPE_THETA ** (-jnp.arange(half, dtype=F32) / half)
    ang = pos.astype(F32)[:, None] * inv[None, :]
    cos, sin = jnp.cos(ang), jnp.sin(ang)
    n = pos.shape[0]
    rest = HEAD_DIM - ROPE_DIM
    c = jnp.concatenate([cos, cos, jnp.ones((n, rest), F32)], axis=-1)
    a = jnp.concatenate([-sin, jnp.zeros((n, HEAD_DIM - half), F32)], axis=-1)
    b = jnp.concatenate([jnp.zeros((n, half), F32), sin, jnp.zeros((n, rest), F32)], axis=-1)
    return (c, a, b), (HEAD_DIM - half, half)


def _ret_rot_tables(pos):
    half = RET_DK // 2
    angle = RET_ROT_BASE ** (-jnp.linspace(0.0, 1.0, half, dtype=F32))
    ang = pos.astype(F32)[:, None] * angle[None, :]
    cos, sin = jnp.cos(ang), jnp.sin(ang)
    n = pos.shape[0]
    zero = jnp.zeros_like(sin)
    c = jnp.stack([cos, cos], axis=-1).reshape(n, RET_DK)
    a = jnp.stack([-sin, zero], axis=-1).reshape(n, RET_DK)
    b = jnp.stack([zero, sin], axis=-1).reshape(n, RET_DK)
    return (c, a, b), (RET_DK - 1, 1)


def _ret_decay(chunk):
    log_g = jnp.log1p(-jnp.exp2(-5.0 - jnp.arange(RET_HEADS, dtype=F32)))
    n = jnp.arange(chunk, dtype=F32)
    rel = n[:, None] - n[None, :]
    decay = jnp.where(rel >= 0, jnp.exp(jnp.maximum(rel, 0.0) * log_g[:, None, None]), 0.0)
    q_dec = jnp.exp((n + 1.0) * log_g[:, None])
    k_dec = jnp.exp((chunk - 1.0 - n) * log_g[:, None])
    c_dec = jnp.exp(chunk * log_g)
    return decay, q_dec, k_dec, c_dec


def kernel(x_prompt, x_sample, cache_swa_k, cache_swa_v, state_ret, cache_mem_k, cache_mem_v,
           mem_prompt, ln_g, ln_b, ffn_w_gu, ffn_w_down, w_mem_kv, swa_w_in, swa_w_out,
           swa_sinks, ret_w_in, ret_w_out):
    bp, seq, _ = x_prompt.shape
    bs, sample_seq, _ = x_sample.shape
    assert sample_seq == 1
    mp, ms = bp * seq, bs
    assert mp % ms == 0

    mem2 = mem_prompt.reshape(bp * MEM_LEN, D_MODEL)
    mem_kv = _mem_kv(mem2, w_mem_kv).reshape(DEPTH, bp, MEM_LEN, 2 * MEM_DIM)
    mem_k_prompt = mem_kv[..., :MEM_DIM].reshape(DEPTH, bp, MEM_LEN, MEM_HEADS, HEAD_DIM)
    mem_v_prompt = mem_kv[..., MEM_DIM:].reshape(DEPTH, bp, MEM_LEN, MEM_HEADS, HEAD_DIM)

    tm, tiles = _row_tiles(mp + ms, FFN_MAX_ROWS)
    m_all = tm * tiles
    tm_proj, _ = _row_tiles(m_all, PROJ_MAX_ROWS)
    pad = m_all - mp - ms
    tail = (mp - (tiles - 1) * tm, ms)
    assert tail[0] >= 0 and tail[0] % 8 == 0 and ms % 8 == 0 and tail[0] + ms <= tm
    def per_row(table):
        return jnp.concatenate([jnp.tile(table[:seq], (bp, 1)),
                                jnp.broadcast_to(table[seq:], (ms, table.shape[1])),
                                jnp.zeros((pad, table.shape[1]), F32)], axis=0)

    pos = jnp.concatenate([jnp.arange(seq, dtype=jnp.int32), jnp.full((1,), PAST_LEN, jnp.int32)])
    rope_tabs, rope_shifts = _rope_tables(pos)
    rot_tabs, rot_shifts = _ret_rot_tables(pos)
    rope_tabs = tuple(per_row(t) for t in rope_tabs)
    rot_tabs = tuple(per_row(t) for t in rot_tabs)
    decay, q_dec, k_dec, c_dec = _ret_decay(RET_CHUNK)
    ret_tabs = (decay, jnp.broadcast_to(q_dec[:, :, None], decay.shape),
                jnp.broadcast_to(k_dec[:, :, None], decay.shape), c_dec)
    decay, q_dec, k_dec, c_dec = _ret_decay(sample_seq)
    step_dec = jnp.stack([decay[:, 0, 0], q_dec[:, 0], k_dec[:, 0], c_dec])

    ln_stack = (ln_g.reshape(DEPTH * 3, 1, D_MODEL), ln_b.reshape(DEPTH * 3, 1, D_MODEL))

    swa_k_prompt, swa_v_prompt = [], []
    swa_sample = None
    ret_prompt = None
    ret_sample = None
    for i in range(DEPTH):
        j = i // 2
        if i == 0:
            x = _ffn(x_prompt.reshape(mp, D_MODEL), ffn_w_gu, ffn_w_down, i, 0, *ln_stack, tm, tiles, tail,
                     x_sample=x_sample.reshape(ms, D_MODEL))
        else:
            x = _ffn(x, ffn_w_gu, ffn_w_down, i, 0, *ln_stack, tm, tiles, tail)
        if i % 2 == 0:
            qkv = _proj(x, swa_w_in, j, tm_proj, rot_heads=SWA_Q_HEADS + SWA_KV_HEADS, tables=rope_tabs,
                        shifts=rope_shifts)
            x, k_tail, v_tail = _swa_prompt(qkv, bp, seq, mem_kv, i, swa_sinks[j], swa_w_out, x, *ln_stack)
            swa_k_prompt.append(k_tail.reshape(bp, WINDOW, SWA_KV_HEADS, HEAD_DIM))
            swa_v_prompt.append(v_tail.reshape(bp, WINDOW, SWA_KV_HEADS, HEAD_DIM))
            rows = qkv[mp:mp + ms].reshape(ms, SWA_IN_WIDTH // HEAD_DIM, HEAD_DIM)
            att, *swa_sample = _swa_step(rows, cache_swa_k, cache_swa_v, j, cache_mem_k, cache_mem_v, i,
                                         swa_sinks[j], swa_sample)
            x = _out_ln(att.reshape(ms, D_MODEL), swa_w_out, i, x, *ln_stack, mp)
        else:
            qkvg = _proj(x, ret_w_in, j, tm_proj, rot_heads=2 * RET_HEADS, tables=rot_tabs,
                         shifts=rot_shifts)
            x, ret_prompt = _ret_prompt(qkvg, bp, seq, mem_kv, i, ret_tabs, ret_w_out, x, *ln_stack,
                                        None if ret_prompt is None else [ret_prompt])
            rows = qkvg[mp:mp + ms].reshape(ms, RET_IN_WIDTH // HEAD_DIM, HEAD_DIM)
            att, ret_sample = _ret_step(rows, jnp.swapaxes(rows, 1, 2), state_ret, j, cache_mem_k,
                                        cache_mem_v, i, step_dec,
                                        None if ret_sample is None else [ret_sample])
            x = _out_ln(att.reshape(ms, D_MODEL), ret_w_out, i, x, *ln_stack, mp)
        x = _ffn(x, ffn_w_gu, ffn_w_down, i, 1, *ln_stack, tm, tiles, tail, split_out=(i == DEPTH - 1))

    y_prompt = x[0].reshape(bp, seq, D_MODEL)
    y_sample = x[1].reshape(bs, sample_seq, D_MODEL)
    return (y_prompt, y_sample, jnp.stack(swa_k_prompt), jnp.stack(swa_v_prompt), swa_sample[0],
            swa_sample[1], ret_prompt, ret_sample, mem_k_prompt, mem_v_prompt)
```

```python
import functools

import jax
import jax.numpy as jnp
from jax import lax
from jax.experimental import pallas as pl
from jax.experimental.pallas import tpu as pltpu

F32 = jnp.float32
BF16 = jnp.bfloat16

D_MODEL = 2048
DEPTH = 4
PAST_LEN = 16384
HEAD_DIM = 128
MEM_LEN = 256
MEM_HEADS = 4
MEM_DIM = MEM_HEADS * HEAD_DIM
SELF_WIDTH = D_MODEL - MEM_DIM
SWA_Q_HEADS = SELF_WIDTH // HEAD_DIM
SWA_KV_HEADS = SWA_Q_HEADS // 3
SWA_GROUP = SWA_Q_HEADS // SWA_KV_HEADS
WINDOW = 128
ROPE_THETA = 500000.0
ROPE_DIM = HEAD_DIM // 4
RET_DK = 128
RET_DV = 2 * RET_DK
RET_HEADS = SELF_WIDTH // RET_DV
RET_CHUNK = 128
RET_ROT_BASE = 10000.0
D_FF = ((8 * D_MODEL // 3 + 255) // 256) * 256
LN_EPS = 1e-5
HEAD_NORM_EPS = 1e-6
ALPHA = (2.0 * DEPTH) ** 0.25
NEG_INF = -1e30
ATT_SCALE = HEAD_DIM ** -0.5
RET_K_SCALE = RET_DK ** -0.5

SWA_Q = SWA_Q_HEADS * HEAD_DIM
SWA_KV = SWA_KV_HEADS * HEAD_DIM
SWA_IN_WIDTH = SWA_Q + 2 * SWA_KV + MEM_DIM
RET_QK = RET_HEADS * RET_DK
RET_V = RET_HEADS * RET_DV
RET_IN_WIDTH = 2 * RET_QK + 2 * RET_V + MEM_DIM

VMEM_LIMIT_BYTES = 58 * 1024 * 1024
LANES = 128
FFN_TILE_K = 256
FFN_MAX_ROWS = 1200
PROJ_MAX_ROWS = 1040
ATT_BLOCKS_PER_STEP = 2
PROJ_TILE_N = 1024
OUT_TILE_N = 512
MAX_ROW_CHUNK = 256
BF16_SUBLANES = 16


def _row_tiles(rows, max_rows):
    tiles = -(-rows // max_rows)
    per_tile = -(-rows // tiles)
    return -(-per_tile // BF16_SUBLANES) * BF16_SUBLANES, tiles


def _row_chunks(tm):
    count = -(-tm // MAX_ROW_CHUNK)
    size = -(-(-(-tm // count)) // BF16_SUBLANES) * BF16_SUBLANES
    return [slice(r, min(r + size, tm)) for r in range(0, tm, size)]


def _params(*sem):
    return pltpu.CompilerParams(dimension_semantics=sem, vmem_limit_bytes=VMEM_LIMIT_BYTES)


def _layer_norm_rows(z, g, b):
    mu = jnp.mean(z, axis=-1, keepdims=True)
    zc = z - mu
    var = jnp.mean(zc * zc, axis=-1, keepdims=True)
    return zc * lax.rsqrt(var + LN_EPS) * g + b


def _ffn_kernel(*refs, nk, tk, tiles, tail, split_in, split_out):
    refs = list(refs)
    x_ref = refs.pop(0)
    xs_ref = refs.pop(0) if split_in else None
    wg_ref, wu_ref, wd_ref, g_ref, b_ref, o_ref = refs[:6]
    os_ref = refs[6] if split_out else None
    xb_ref, wgu_ref = refs[-2:]
    i = pl.program_id(0)
    k = pl.program_id(1)
    chunks = _row_chunks(o_ref.shape[0])

    def load_x(rows):
        xv = x_ref[rows, :]
        if not split_in or rows.stop <= tail[0]:
            return xv
        first, count = tail
        lo, hi = rows.start, rows.stop
        pieces = []
        if lo < first:
            pieces.append(xv[:first - lo])
        s0, s1 = max(lo, first), min(hi, first + count)
        if s0 < s1:
            pieces.append(xs_ref[s0 - first:s1 - first, :])
        if hi > first + count:
            pieces.append(jnp.zeros((hi - max(lo, first + count), xv.shape[1]), xv.dtype))
        return jnp.where(i == tiles - 1, jnp.concatenate(pieces, axis=0), xv)

    def cast_weights():
        wgu_ref[:, :tk] = wg_ref[...].astype(BF16)
        wgu_ref[:, tk:] = wu_ref[...].astype(BF16)
        return wd_ref[...].astype(BF16)

    def down(xb, wd):
        gu = jnp.dot(xb, wgu_ref[...], preferred_element_type=F32)
        h = (jax.nn.silu(gu[:, :tk]) * gu[:, tk:]).astype(BF16)
        return jnp.dot(h, wd, preferred_element_type=F32)

    @pl.when(k == 0)
    def _():
        wd = cast_weights()
        for rows in chunks:
            xb = load_x(rows).astype(BF16)
            xb_ref[rows, :] = xb
            o_ref[rows, :] = down(xb, wd)

    @pl.when(jnp.logical_and(k > 0, k < nk - 1))
    def _():
        wd = cast_weights()
        o_ref[...] += down(xb_ref[...], wd)

    @pl.when(k == nk - 1)
    def _():
        wd = cast_weights()
        for rows in chunks:
            acc = o_ref[rows, :] + down(xb_ref[rows, :], wd)
            z = ALPHA * load_x(rows) + 0.5 * acc
            o_ref[rows, :] = _layer_norm_rows(z, g_ref[...], b_ref[...])
        if split_out:
            @pl.when(i == tiles - 1)
            def _():
                os_ref[...] = o_ref[tail[0]:tail[0] + tail[1], :]


def _ffn(x, w_gu, w_down, layer, slot, g, b, tm, tiles, tail, x_sample=None, split_out=False):
    tk = FFN_TILE_K
    nk = D_FF // tk
    assert nk >= 2
    split_in = x_sample is not None
    rows = pl.BlockSpec((tm, D_MODEL), lambda i, k: (i, 0))
    small = pl.BlockSpec((tail[1], D_MODEL), lambda i, k: (0, 0))
    in_specs = [pl.BlockSpec((tm, D_MODEL), lambda i, k: (i, 0), pipeline_mode=pl.Buffered(1))]
    args = [x]
    if split_in:
        in_specs.append(small)
        args.append(x_sample)
    in_specs += [
        pl.BlockSpec((None, None, D_MODEL, tk), lambda i, k: (layer, slot, 0, k)),
        pl.BlockSpec((None, None, D_MODEL, tk), lambda i, k: (layer, slot, 0, nk + k)),
        pl.BlockSpec((None, None, tk, D_MODEL), lambda i, k: (layer, slot, k, 0)),
        pl.BlockSpec((1, D_MODEL), lambda i, k: (0, 0)),
        pl.BlockSpec((1, D_MODEL), lambda i, k: (0, 0)),
    ]
    if split_out:
        prompt_rows = (tiles - 1) * tm + tail[0]
        out_specs = [rows, small]
        out_shape = [jax.ShapeDtypeStruct((prompt_rows, D_MODEL), F32),
                     jax.ShapeDtypeStruct((tail[1], D_MODEL), F32)]
    else:
        out_specs = rows
        out_shape = jax.ShapeDtypeStruct((tiles * tm, D_MODEL), F32)
    return pl.pallas_call(
        functools.partial(_ffn_kernel, nk=nk, tk=tk, tiles=tiles, tail=tail,
                          split_in=split_in, split_out=split_out),
        grid=(tiles, nk),
        in_specs=in_specs,
        out_specs=out_specs,
        out_shape=out_shape,
        scratch_shapes=[pltpu.VMEM((tm, D_MODEL), BF16), pltpu.VMEM((D_MODEL, 2 * tk), BF16)],
        compiler_params=_params("arbitrary", "arbitrary"),
        name="ffn_ln",
    )(*args, w_gu, w_gu, w_down, g, b)


def _proj_kernel(x_ref, w_ref, *rest, rot_heads, shifts):
    if rot_heads:
        c_ref, a_ref, b_ref, o_ref, xb_ref = rest
    else:
        o_ref, xb_ref = rest
    j = pl.program_id(1)

    @pl.when(j == 0)
    def _():
        xb_ref[...] = x_ref[...].astype(BF16)

    def plain():
        o_ref[...] = jnp.dot(xb_ref[...], w_ref[...].astype(BF16), preferred_element_type=F32)

    if not rot_heads:
        plain()
        return

    tm, tn = o_ref.shape
    heads = tn // LANES
    full_tiles, part_heads = divmod(rot_heads, heads)

    def rotated(n_heads):
        wb = w_ref[...].astype(BF16)
        for rows in _row_chunks(tm):
            y = jnp.dot(xb_ref[rows, :], wb, preferred_element_type=F32)
            c, a, b = c_ref[rows, :], a_ref[rows, :], b_ref[rows, :]
            for h in range(heads):
                yh = y[:, h * LANES:(h + 1) * LANES]
                if h < n_heads:
                    yh = yh * c + pltpu.roll(yh, shifts[0], 1) * a + pltpu.roll(yh, shifts[1], 1) * b
                o_ref[rows, h * LANES:(h + 1) * LANES] = yh

    pl.when(j < full_tiles)(lambda: rotated(heads))
    if part_heads:
        pl.when(j == full_tiles)(lambda: rotated(part_heads))
    pl.when(j >= full_tiles + (1 if part_heads else 0))(plain)


def _proj(x, w, layer, tm, rot_heads=0, tables=None, shifts=None):
    m = x.shape[0]
    n = w.shape[-1]
    tn = PROJ_TILE_N
    in_specs = [
        pl.BlockSpec((tm, D_MODEL), lambda i, j: (i, 0)),
        pl.BlockSpec((None, D_MODEL, tn), lambda i, j: (layer, 0, j)),
    ]
    args = [x, w]
    if rot_heads:
        tab = pl.BlockSpec((tm, LANES), lambda i, j: (i, 0))
        in_specs += [tab, tab, tab]
        args += list(tables)
    return pl.pallas_call(
        functools.partial(_proj_kernel, rot_heads=rot_heads, shifts=shifts),
        grid=(pl.cdiv(m, tm), n // tn),
        in_specs=in_specs,
        out_specs=pl.BlockSpec((tm, tn), lambda i, j: (i, j)),
        out_shape=jax.ShapeDtypeStruct((m, n), F32),
        scratch_shapes=[pltpu.VMEM((tm, D_MODEL), BF16)],
        compiler_params=_params("parallel", "arbitrary"),
        name="proj",
    )(*args)


def _mem_kv_kernel(x_ref, w_ref, o_ref, xb_ref):
    @pl.when(pl.program_id(0) == 0)
    def _():
        xb_ref[...] = x_ref[...].astype(BF16)

    o_ref[...] = jnp.dot(xb_ref[...], w_ref[...].astype(BF16), preferred_element_type=F32)


def _mem_kv(mem, w):
    rows = mem.shape[0]
    layers, _, n = w.shape
    return pl.pallas_call(
        _mem_kv_kernel,
        grid=(layers,),
        in_specs=[
            pl.BlockSpec((rows, D_MODEL), lambda l: (0, 0)),
            pl.BlockSpec((None, D_MODEL, n), lambda l: (l, 0, 0)),
        ],
        out_specs=pl.BlockSpec((None, rows, n), lambda l: (l, 0, 0)),
        out_shape=jax.ShapeDtypeStruct((layers, rows, n), F32),
        scratch_shapes=[pltpu.VMEM((rows, D_MODEL), BF16)],
        compiler_params=_params("arbitrary"),
        name="mem_kv",
    )(mem, w)


def _out_ln_kernel(att_ref, w_ref, x_ref, g_ref, b_ref, o_ref, y_ref, *, nn, tn):
    n = pl.program_id(1)
    y_ref[n] = jnp.dot(att_ref[...].astype(BF16), w_ref[...].astype(BF16),
                       preferred_element_type=F32)

    @pl.when(n == nn - 1)
    def _():
        cols = [slice(c * tn, (c + 1) * tn) for c in range(nn)]
        total = None
        for c in range(nn):
            z = ALPHA * x_ref[:, cols[c]] + y_ref[c]
            y_ref[c] = z
            part = jnp.sum(z, axis=-1, keepdims=True)
            total = part if total is None else total + part
        mu = total * (1.0 / D_MODEL)
        total = None
        for c in range(nn):
            zc = y_ref[c] - mu
            part = jnp.sum(zc * zc, axis=-1, keepdims=True)
            total = part if total is None else total + part
        rstd = lax.rsqrt(total * (1.0 / D_MODEL) + LN_EPS)
        for c in range(nn):
            o_ref[:, cols[c]] = (y_ref[c] - mu) * rstd * g_ref[:, cols[c]] + b_ref[:, cols[c]]


def _out_ln(att, w_out, layer, x, g, b, row0):
    tm = att.shape[0]
    assert row0 % tm == 0
    tn = OUT_TILE_N
    nn = D_MODEL // tn
    rows = pl.BlockSpec((tm, D_MODEL), lambda i, n: (row0 // tm, 0))
    return pl.pallas_call(
        functools.partial(_out_ln_kernel, nn=nn, tn=tn),
        grid=(1, nn),
        in_specs=[
            pl.BlockSpec((tm, D_MODEL), lambda i, n: (0, 0)),
            pl.BlockSpec((None, D_MODEL, tn), lambda i, n: (layer, 0, n)),
            rows,
            pl.BlockSpec((1, D_MODEL), lambda i, n: (0, 0)),
            pl.BlockSpec((1, D_MODEL), lambda i, n: (0, 0)),
        ],
        out_specs=rows,
        out_shape=jax.ShapeDtypeStruct(x.shape, F32),
        scratch_shapes=[pltpu.VMEM((nn, tm, tn), F32)],
        input_output_aliases={2: 0},
        compiler_params=_params("arbitrary", "arbitrary"),
        name="out_ln",
    )(att, w_out, x, g, b)


def _dot_nt(a, b):
    return lax.dot_general(a, b, (((1,), (1,)), ((), ())), preferred_element_type=F32)


def _out_proj_ln(first, att_ref, w_ref, wb_ref, x_ref, g_ref, b_ref, o_ref):
    @pl.when(first)
    def _():
        wb_ref[...] = w_ref[...].astype(BF16)

    y = jnp.dot(att_ref[...], wb_ref[...], preferred_element_type=F32)
    o_ref[...] = _layer_norm_rows(ALPHA * x_ref[...] + y, g_ref[...], b_ref[...])


def _normalised_pv(p, v, extra=None):
    den = jnp.dot(p, jnp.ones((p.shape[1], HEAD_DIM), p.dtype), preferred_element_type=F32)
    if extra is not None:
        den = den + extra
    return jnp.dot(p, v, preferred_element_type=F32) * (1.0 / den)


def _mem_attend_block(qm_ref, mk_ref, mv_ref, att_ref, col0):
    for h in range(MEM_HEADS):
        sl = slice(h * HEAD_DIM, (h + 1) * HEAD_DIM)
        q = qm_ref[:, sl].astype(BF16)
        s = _dot_nt(q, mk_ref[0, :, sl].astype(BF16)) * ATT_SCALE
        p = jnp.exp(s - jnp.max(s, axis=-1, keepdims=True))
        p = p * (1.0 / jnp.sum(p, axis=-1, keepdims=True))
        o = jnp.dot(p.astype(BF16), mv_ref[0, :, sl].astype(BF16), preferred_element_type=F32)
        att_ref[:, col0 + h * HEAD_DIM:col0 + (h + 1) * HEAD_DIM] = o.astype(att_ref.dtype)


def _swa_prompt_kernel(sink_ref, q_ref, kc_ref, kp_ref, vc_ref, vp_ref, qm_ref, mk_ref, mv_ref,
                       w_ref, x_ref, g_ref, b_ref, o_ref, kt_ref, vt_ref, s_ref, p_ref, att_ref, wb_ref,
                       *, nblk, steps):
    n = pl.program_id(1)
    nq = SWA_Q_HEADS
    w2 = 2 * WINDOW

    def prev_cur(cur_ref, prev_ref, i, sl):
        rows = slice(i * WINDOW, (i + 1) * WINDOW)
        before = prev_ref[:, sl] if i == 0 else cur_ref[(i - 1) * WINDOW:i * WINDOW, sl]
        return jnp.concatenate([before, cur_ref[rows, sl]], axis=0).astype(BF16)

    for i in range(nblk):
        rows = slice(i * WINDOW, (i + 1) * WINDOW)
        for h in range(SWA_KV_HEADS):
            sl = slice(h * HEAD_DIM, (h + 1) * HEAD_DIM)
            q3 = jnp.concatenate(
                [q_ref[rows, (h * SWA_GROUP + g) * HEAD_DIM:(h * SWA_GROUP + g + 1) * HEAD_DIM]
                 for g in range(SWA_GROUP)], axis=0).astype(BF16)
            s = _dot_nt(q3, prev_cur(kc_ref, kp_ref, i, sl)) * ATT_SCALE
            s_ref[i, h * SWA_GROUP:(h + 1) * SWA_GROUP] = s.reshape(SWA_GROUP, WINDOW, w2)
    for h in range(MEM_HEADS):
        sl = slice(h * HEAD_DIM, (h + 1) * HEAD_DIM)
        sm = _dot_nt(qm_ref[:, sl].astype(BF16), mk_ref[0, :, sl].astype(BF16)) * ATT_SCALE
        s_ref[:, nq + h] = sm.reshape(nblk, WINDOW, w2)

    qi = lax.broadcasted_iota(jnp.int32, (WINDOW, w2), 0)
    kj = lax.broadcasted_iota(jnp.int32, (WINDOW, w2), 1)
    sink = sink_ref[...]
    sink_terms = []
    for i in range(nblk):
        first_key = jnp.where(n > 0, qi, WINDOW) if i == 0 else qi
        ok = (kj >= first_key) & (kj <= qi + WINDOW)
        s = jnp.where(ok[None], s_ref[i, 0:nq], NEG_INF)
        m = jnp.maximum(jnp.max(s, axis=-1, keepdims=True), sink)
        p_ref[i, 0:nq] = jnp.exp(s - m).astype(BF16)
        sink_terms.append(jnp.exp(sink - m))
    s = s_ref[:, nq:nq + MEM_HEADS]
    p_ref[:, nq:nq + MEM_HEADS] = jnp.exp(s - jnp.max(s, axis=-1, keepdims=True)).astype(BF16)

    for i in range(nblk):
        rows = slice(i * WINDOW, (i + 1) * WINDOW)
        for h in range(SWA_KV_HEADS):
            sl = slice(h * HEAD_DIM, (h + 1) * HEAD_DIM)
            group = slice(h * SWA_GROUP, (h + 1) * SWA_GROUP)
            p3 = p_ref[i, group].reshape(SWA_GROUP * WINDOW, w2)
            o = _normalised_pv(p3, prev_cur(vc_ref, vp_ref, i, sl),
                               sink_terms[i][group].reshape(SWA_GROUP * WINDOW, 1))
            for g in range(SWA_GROUP):
                hq = h * SWA_GROUP + g
                att_ref[rows, hq * HEAD_DIM:(hq + 1) * HEAD_DIM] = (
                    o[g * WINDOW:(g + 1) * WINDOW].astype(BF16))
    for h in range(MEM_HEADS):
        sl = slice(h * HEAD_DIM, (h + 1) * HEAD_DIM)
        pm = p_ref[:, nq + h].reshape(nblk * WINDOW, w2)
        om = _normalised_pv(pm, mv_ref[0, :, sl].astype(BF16))
        att_ref[:, SWA_Q + h * HEAD_DIM:SWA_Q + (h + 1) * HEAD_DIM] = om.astype(BF16)

    @pl.when(n == steps - 1)
    def _():
        last = slice((nblk - 1) * WINDOW, nblk * WINDOW)
        kt_ref[0] = kc_ref[last, :]
        vt_ref[0] = vc_ref[last, :]

    first = jnp.logical_and(pl.program_id(0) == 0, n == 0)
    _out_proj_ln(first, att_ref, w_ref, wb_ref, x_ref, g_ref, b_ref, o_ref)


def _mixer_tail_specs(layer, row_of, rows):
    mem = (None, 1, MEM_LEN, MEM_DIM)
    vec = pl.BlockSpec((1, D_MODEL), lambda b, n: (0, 0))
    mixer = layer // 2
    return [
        pl.BlockSpec(mem, lambda b, n: (layer, b, 0, 0)),
        pl.BlockSpec(mem, lambda b, n: (layer, b, 0, 1)),
        pl.BlockSpec((None, D_MODEL, D_MODEL), lambda b, n: (mixer, 0, 0), pipeline_mode=pl.Buffered(1)),
        pl.BlockSpec((rows, D_MODEL), lambda b, n: (row_of(b, n), 0)),
        vec, vec,
    ]


def _swa_prompt(qkv, bsz, seq, mem_kv, layer, sinks, w_out, x, g, b):
    nblk = ATT_BLOCKS_PER_STEP
    rows = nblk * WINDOW
    steps = seq // rows
    assert seq % rows == 0 and seq >= WINDOW
    tail = pl.BlockSpec((1, WINDOW, SWA_KV), lambda b, n: (b, 0, 0))
    blocks_per_seq = seq // WINDOW
    kcol = SWA_Q // SWA_KV
    vcol = kcol + 1
    mcol = vcol + 1
    row_of = lambda b, n: b * steps + n
    prev_of = lambda b, n: b * blocks_per_seq + jnp.maximum(n * nblk - 1, 0)
    cur = lambda w, col: pl.BlockSpec((rows, w), lambda b, n: (row_of(b, n), col))
    prev = lambda col: pl.BlockSpec((WINDOW, SWA_KV), lambda b, n: (prev_of(b, n), col))
    nheads = SWA_Q_HEADS + MEM_HEADS
    assert MEM_LEN == 2 * WINDOW
    in_specs = [
        pl.BlockSpec((SWA_Q_HEADS, 1, 1), lambda b, n: (0, 0, 0)),
        cur(SWA_Q, 0), cur(SWA_KV, kcol), prev(kcol), cur(SWA_KV, vcol), prev(vcol), cur(MEM_DIM, mcol),
    ] + _mixer_tail_specs(layer, row_of, rows)
    return pl.pallas_call(
        functools.partial(_swa_prompt_kernel, nblk=nblk, steps=steps),
        grid=(bsz, steps),
        in_specs=in_specs,
        out_specs=[pl.BlockSpec((rows, D_MODEL), lambda b, n: (row_of(b, n), 0)), tail, tail],
        out_shape=[jax.ShapeDtypeStruct(x.shape, F32),
                   jax.ShapeDtypeStruct((bsz, WINDOW, SWA_KV), F32),
                   jax.ShapeDtypeStruct((bsz, WINDOW, SWA_KV), F32)],
        scratch_shapes=[pltpu.VMEM((nblk, nheads, WINDOW, 2 * WINDOW), F32),
                        pltpu.VMEM((nblk, nheads, WINDOW, 2 * WINDOW), BF16),
                        pltpu.VMEM((rows, D_MODEL), BF16),
                        pltpu.VMEM((D_MODEL, D_MODEL), BF16)],
        input_output_aliases={len(in_specs) - 3: 0},
        compiler_params=_params("arbitrary", "arbitrary"),
        name="swa_prompt",
    )(sinks.reshape(SWA_Q_HEADS, 1, 1), qkv, qkv, qkv, qkv, qkv, qkv, mem_kv, mem_kv, w_out, x, g, b)


def _head_norm_gate(o, gate):
    mu = jnp.mean(o, axis=-1, keepdims=True)
    oc = o - mu
    var = jnp.mean(oc * oc, axis=-1, keepdims=True)
    return jax.nn.silu(gate) * (oc * lax.rsqrt(var + HEAD_NORM_EPS))


def _ret_prompt_kernel(cdec_ref, decay_ref, qdec_ref, kdec_ref, q_ref, k_ref, v_ref, gate_ref, qm_ref,
                       mk_ref, mv_ref, w_ref, x_ref, g_ref, b_ref, o_ref, s_out_ref,
                       state_ref, att_ref, wb_ref, *, steps, nblk):
    c = pl.program_id(1)

    @pl.when(c == 0)
    def _():
        state_ref[...] = jnp.zeros_like(state_ref)

    for i in range(nblk):
        rows = slice(i * RET_CHUNK, (i + 1) * RET_CHUNK)
        for h in range(RET_HEADS):
            ksl = slice(h * RET_DK, (h + 1) * RET_DK)
            vsl = slice(h * RET_DV, (h + 1) * RET_DV)
            qc = q_ref[rows, ksl]
            kc = k_ref[rows, ksl] * RET_K_SCALE
            vb = v_ref[rows, vsl].astype(BF16)
            st = state_ref[h]
            inner = _dot_nt(qc.astype(BF16), kc.astype(BF16)) * decay_ref[h]
            o = (jnp.dot(inner.astype(BF16), vb, preferred_element_type=F32)
                 + jnp.dot((qc * qdec_ref[h]).astype(BF16), st.astype(BF16),
                           preferred_element_type=F32))
            kd = (kc * kdec_ref[h]).astype(BF16)
            state_ref[h] = cdec_ref[h] * st + lax.dot_general(
                kd, vb, (((0,), (0,)), ((), ())), preferred_element_type=F32)
            att_ref[rows, vsl] = _head_norm_gate(o, gate_ref[rows, vsl]).astype(BF16)
    _mem_attend_block(qm_ref, mk_ref, mv_ref, att_ref, RET_V)

    @pl.when(c == steps - 1)
    def _():
        s_out_ref[0] = state_ref[...]

    first = jnp.logical_and(pl.program_id(0) == 0, c == 0)
    _out_proj_ln(first, att_ref, w_ref, wb_ref, x_ref, g_ref, b_ref, o_ref)


def _drop_carried(kernel, first, count):
    def body(*refs):
        return kernel(*refs[:first], *refs[first + count:])
    return body


def _carry(prev_outputs):
    prev_outputs = [] if prev_outputs is None else list(prev_outputs)
    return [pl.BlockSpec(memory_space=pl.ANY)] * len(prev_outputs), prev_outputs


def _ret_prompt(qkvg, bsz, seq, mem_kv, layer, tables, w_out, x, g, b, carried):
    nblk = ATT_BLOCKS_PER_STEP
    rows = nblk * RET_CHUNK
    steps = seq // rows
    mixer = layer // 2
    decay, qdec, kdec, cdec = tables
    row_of = lambda b, n: b * steps + n
    cur = lambda w, col: pl.BlockSpec((rows, w), lambda b, n: (row_of(b, n), col))
    tab = pl.BlockSpec((RET_HEADS, RET_CHUNK, RET_CHUNK), lambda b, n: (0, 0, 0))
    in_specs = [
        pl.BlockSpec(memory_space=pltpu.SMEM), tab, tab, tab,
        cur(RET_QK, 0), cur(RET_QK, 1), cur(RET_V, 1), cur(RET_V, 2),
        cur(MEM_DIM, (2 * RET_QK + 2 * RET_V) // MEM_DIM),
    ] + _mixer_tail_specs(layer, row_of, rows)
    n_in = len(in_specs)
    carry_specs, carry_args = _carry(carried)
    aliases = {n_in - 3: 0}
    aliases.update({n_in + c: 1 + c for c in range(len(carry_args))})
    return pl.pallas_call(
        _drop_carried(functools.partial(_ret_prompt_kernel, steps=steps, nblk=nblk), n_in, len(carry_args)),
        grid=(bsz, steps),
        in_specs=in_specs + carry_specs,
        out_specs=[
            pl.BlockSpec((rows, D_MODEL), lambda b, n: (row_of(b, n), 0)),
            pl.BlockSpec((None, 1, RET_HEADS, RET_DK, RET_DV), lambda b, n: (mixer, b, 0, 0, 0)),
        ],
        out_shape=[
            jax.ShapeDtypeStruct(x.shape, F32),
            jax.ShapeDtypeStruct((DEPTH // 2, bsz, RET_HEADS, RET_DK, RET_DV), F32),
        ],
        scratch_shapes=[pltpu.VMEM((RET_HEADS, RET_DK, RET_DV), F32),
                        pltpu.VMEM((rows, D_MODEL), BF16),
                        pltpu.VMEM((D_MODEL, D_MODEL), BF16)],
        input_output_aliases=aliases,
        compiler_params=_params("arbitrary", "arbitrary"),
        name="ret_prompt",
    )(cdec, decay, qdec, kdec, qkvg, qkvg, qkvg, qkvg, qkvg, mem_kv, mem_kv, w_out, x, g, b, *carry_args)


def _mem_attend_row(q, mk_ref, mv_ref):
    s = jnp.sum(mk_ref[0] * q[None], axis=-1, keepdims=True) * ATT_SCALE
    p = jnp.exp(s - jnp.max(s, axis=0, keepdims=True))
    p = p * (1.0 / jnp.sum(p, axis=0, keepdims=True))
    return jnp.sum(p * mv_ref[0], axis=0)


def _swa_step_kernel(sink_ref, rows_ref, kbuf_ref, vbuf_ref, mk_ref, mv_ref, o_ref, nk_ref, nv_ref):
    krow0 = SWA_Q_HEADS
    vrow0 = krow0 + SWA_KV_HEADS
    mrow0 = vrow0 + SWA_KV_HEADS
    wb = kbuf_ref.shape[1]
    kb = kbuf_ref[0]
    vb = vbuf_ref[0]
    k_new = rows_ref[0, krow0:krow0 + SWA_KV_HEADS, :]
    v_new = rows_ref[0, vrow0:vrow0 + SWA_KV_HEADS, :]
    for g in range(SWA_GROUP):
        group_rows = pl.ds(g, SWA_KV_HEADS, stride=SWA_GROUP)
        q = rows_ref[0, group_rows, :]
        s_buf = jnp.sum(kb * q[None], axis=-1, keepdims=True) * ATT_SCALE
        s_new = jnp.sum(k_new * q, axis=-1, keepdims=True) * ATT_SCALE
        sink = sink_ref[g]
        m = jnp.maximum(jnp.maximum(jnp.max(s_buf, axis=0), s_new), sink)
        p_buf = jnp.exp(s_buf - m[None])
        p_new = jnp.exp(s_new - m)
        inv = 1.0 / (jnp.sum(p_buf, axis=0) + p_new + jnp.exp(sink - m))
        o_ref[0, group_rows, :] = jnp.sum((p_buf * inv[None]) * vb, axis=0) + (p_new * inv) * v_new
    nk_ref[0, 0:wb - 1] = kbuf_ref[0, 1:wb]
    nv_ref[0, 0:wb - 1] = vbuf_ref[0, 1:wb]
    nk_ref[0, wb - 1] = k_new
    nv_ref[0, wb - 1] = v_new
    o_ref[0, SWA_Q_HEADS:SWA_Q_HEADS + MEM_HEADS, :] = _mem_attend_row(
        rows_ref[0, mrow0:mrow0 + MEM_HEADS, :], mk_ref, mv_ref)


def _swa_step(rows, cache_k, cache_v, j, mem_k, mem_v, i, sinks, carried):
    bsz = rows.shape[0]
    wb = cache_k.shape[2]
    cache = pl.BlockSpec((None, 1, wb, SWA_KV_HEADS, HEAD_DIM), lambda b: (j, b, 0, 0, 0))
    mem = pl.BlockSpec((None, 1, MEM_LEN, MEM_HEADS, HEAD_DIM), lambda b: (i, b, 0, 0, 0))
    nrows = D_MODEL // HEAD_DIM
    sink_gk = sinks.reshape(SWA_KV_HEADS, SWA_GROUP).T.reshape(SWA_GROUP, SWA_KV_HEADS, 1)
    in_specs = [
        pl.BlockSpec(sink_gk.shape, lambda b: (0, 0, 0)),
        pl.BlockSpec((1,) + rows.shape[1:], lambda b: (b, 0, 0)),
        cache, cache, mem, mem,
    ]
    n_in = len(in_specs)
    carry_specs, carry_args = _carry(carried)
    return pl.pallas_call(
        _drop_carried(_swa_step_kernel, n_in, len(carry_args)),
        grid=(bsz,),
        in_specs=in_specs + carry_specs,
        out_specs=[pl.BlockSpec((1, nrows, HEAD_DIM), lambda b: (b, 0, 0)), cache, cache],
        out_shape=[
            jax.ShapeDtypeStruct((bsz, nrows, HEAD_DIM), F32),
            jax.ShapeDtypeStruct(cache_k.shape, F32),
            jax.ShapeDtypeStruct(cache_v.shape, F32),
        ],
        input_output_aliases={n_in + c: 1 + c for c in range(len(carry_args))},
        compiler_params=_params("parallel"),
        name="swa_step",
    )(sink_gk, rows, cache_k, cache_v, mem_k, mem_v, *carry_args)


def _ret_step_kernel(dec_ref, rows_ref, cols_ref, s_ref, mk_ref, mv_ref, o_ref, s_out_ref):
    krow0 = RET_HEADS
    vrow0 = 2 * RET_HEADS
    grow0 = vrow0 + 2 * RET_HEADS
    mrow0 = grow0 + 2 * RET_HEADS
    for h in range(RET_HEADS):
        q_row = rows_ref[0, h:h + 1, :]
        k_row = rows_ref[0, krow0 + h:krow0 + h + 1, :] * RET_K_SCALE
        q_col = cols_ref[0, :, h:h + 1]
        k_col = cols_ref[0, :, krow0 + h:krow0 + h + 1] * RET_K_SCALE
        inner = jnp.sum(q_row * k_row, axis=1, keepdims=True) * dec_ref[0, h]
        qd = q_col * dec_ref[1, h]
        kd = k_col * dec_ref[2, h]
        halves = []
        for t in range(2):
            lsl = slice(t * HEAD_DIM, (t + 1) * HEAD_DIM)
            v = rows_ref[0, vrow0 + 2 * h + t:vrow0 + 2 * h + t + 1, :]
            st = s_ref[0, h, :, lsl]
            halves.append(inner * v + jnp.sum(qd * st, axis=0, keepdims=True))
            s_out_ref[0, h, :, lsl] = dec_ref[3, h] * st + kd * v
        mu = (jnp.sum(halves[0], axis=1, keepdims=True)
              + jnp.sum(halves[1], axis=1, keepdims=True)) * (1.0 / RET_DV)
        cen = [o - mu for o in halves]
        var = (jnp.sum(cen[0] * cen[0], axis=1, keepdims=True)
               + jnp.sum(cen[1] * cen[1], axis=1, keepdims=True)) * (1.0 / RET_DV)
        rstd = lax.rsqrt(var + HEAD_NORM_EPS)
        for t in range(2):
            gate = rows_ref[0, grow0 + 2 * h + t:grow0 + 2 * h + t + 1, :]
            o_ref[0, 2 * h + t:2 * h + t + 1, :] = jax.nn.silu(gate) * (cen[t] * rstd)
    o_ref[0, 2 * RET_HEADS:2 * RET_HEADS + MEM_HEADS, :] = _mem_attend_row(
        rows_ref[0, mrow0:mrow0 + MEM_HEADS, :], mk_ref, mv_ref)


def _ret_step(rows, cols, state, j, mem_k, mem_v, i, dec, carried):
    bsz = rows.shape[0]
    st = pl.BlockSpec((None, 1, RET_HEADS, RET_DK, RET_DV), lambda b: (j, b, 0, 0, 0))
    mem = pl.BlockSpec((None, 1, MEM_LEN, MEM_HEADS, HEAD_DIM), lambda b: (i, b, 0, 0, 0))
    nrows = D_MODEL // HEAD_DIM
    in_specs = [
        pl.BlockSpec(memory_space=pltpu.SMEM),
        pl.BlockSpec((1,) + rows.shape[1:], lambda b: (b, 0, 0)),
        pl.BlockSpec((1,) + cols.shape[1:], lambda b: (b, 0, 0)),
        st, mem, mem,
    ]
    n_in = len(in_specs)
    carry_specs, carry_args = _carry(carried)
    return pl.pallas_call(
        _drop_carried(_ret_step_kernel, n_in, len(carry_args)),
        grid=(bsz,),
        in_specs=in_specs + carry_specs,
        out_specs=[pl.BlockSpec((1, nrows, HEAD_DIM), lambda b: (b, 0, 0)), st],
        out_shape=[
            jax.ShapeDtypeStruct((bsz, nrows, HEAD_DIM), F32),
            jax.ShapeDtypeStruct(state.shape, F32),
        ],
        input_output_aliases={n_in + c: 1 + c for c in range(len(carry_args))},
        compiler_params=_params("parallel"),
        name="ret_step",
    )(dec, rows, cols, state, mem_k, mem_v, *carry_args)


def _rope_tables(pos):
    half = ROPE_DIM // 2
    inv = ROPE_THETA ** (-jnp.arange(half, dtype=F32) / half)
    ang = pos.astype(F32)[:, None] * inv[None, :]
    cos, sin = jnp.cos(ang), jnp.sin(ang)
    n = pos.shape[0]
    rest = HEAD_DIM - ROPE_DIM
    c = jnp.concatenate([cos, cos, jnp.ones((n, rest), F32)], axis=-1)
    a = jnp.concatenate([-sin, jnp.zeros((n, HEAD_DIM - half), F32)], axis=-1)
    b = jnp.concatenate([jnp.zeros((n, half), F32), sin, jnp.zeros((n, rest), F32)], axis=-1)
    return (c, a, b), (HEAD_DIM - half, half)


def _ret_rot_tables(pos):
    half = RET_DK // 2
    angle = RET_ROT_BASE ** (-jnp.linspace(0.0, 1.0, half, dtype=F32))
    ang = pos.astype(F32)[:, None] * angle[None, :]
    cos, sin = jnp.cos(ang), jnp.sin(ang)
    n = pos.shape[0]
    zero = jnp.zeros_like(sin)
    c = jnp.stack([cos, cos], axis=-1).reshape(n, RET_DK)
    a = jnp.stack([-sin, zero], axis=-1).reshape(n, RET_DK)
    b = jnp.stack([zero, sin], axis=-1).reshape(n, RET_DK)
    return (c, a, b), (RET_DK - 1, 1)


def _ret_decay(chunk):
    log_g = jnp.log1p(-jnp.exp2(-5.0 - jnp.arange(RET_HEADS, dtype=F32)))
    n = jnp.arange(chunk, dtype=F32)
    rel = n[:, None] - n[None, :]
    decay = jnp.where(rel >= 0, jnp.exp(jnp.maximum(rel, 0.0) * log_g[:, None, None]), 0.0)
    q_dec = jnp.exp((n + 1.0) * log_g[:, None])
    k_dec = jnp.exp((chunk - 1.0 - n) * log_g[:, None])
    c_dec = jnp.exp(chunk * log_g)
    return decay, q_dec, k_dec, c_dec


def kernel(x_prompt, x_sample, cache_swa_k, cache_swa_v, state_ret, cache_mem_k, cache_mem_v,
           mem_prompt, ln_g, ln_b, ffn_w_gu, ffn_w_down, w_mem_kv, swa_w_in, swa_w_out,
           swa_sinks, ret_w_in, ret_w_out):
    bp, seq, _ = x_prompt.shape
    bs, sample_seq, _ = x_sample.shape
    assert sample_seq == 1
    mp, ms = bp * seq, bs
    assert mp % ms == 0

    mem2 = mem_prompt.reshape(bp * MEM_LEN, D_MODEL)
    mem_kv = _mem_kv(mem2, w_mem_kv).reshape(DEPTH, bp, MEM_LEN, 2 * MEM_DIM)
    mem_k_prompt = mem_kv[..., :MEM_DIM].reshape(DEPTH, bp, MEM_LEN, MEM_HEADS, HEAD_DIM)
    mem_v_prompt = mem_kv[..., MEM_DIM:].reshape(DEPTH, bp, MEM_LEN, MEM_HEADS, HEAD_DIM)

    tm, tiles = _row_tiles(mp + ms, FFN_MAX_ROWS)
    m_all = tm * tiles
    tm_proj, _ = _row_tiles(m_all, PROJ_MAX_ROWS)
    pad = m_all - mp - ms
    tail = (mp - (tiles - 1) * tm, ms)
    assert tail[0] >= 0 and tail[0] % 8 == 0 and ms % 8 == 0 and tail[0] + ms <= tm
    def per_row(table):
        return jnp.concatenate([jnp.tile(table[:seq], (bp, 1)),
                                jnp.broadcast_to(table[seq:], (ms, table.shape[1])),
                                jnp.zeros((pad, table.shape[1]), F32)], axis=0)

    pos = jnp.concatenate([jnp.arange(seq, dtype=jnp.int32), jnp.full((1,), PAST_LEN, jnp.int32)])
    rope_tabs, rope_shifts = _rope_tables(pos)
    rot_tabs, rot_shifts = _ret_rot_tables(pos)
    rope_tabs = tuple(per_row(t) for t in rope_tabs)
    rot_tabs = tuple(per_row(t) for t in rot_tabs)
    decay, q_dec, k_dec, c_dec = _ret_decay(RET_CHUNK)
    ret_tabs = (decay, jnp.broadcast_to(q_dec[:, :, None], decay.shape),
                jnp.broadcast_to(k_dec[:, :, None], decay.shape), c_dec)
    decay, q_dec, k_dec, c_dec = _ret_decay(sample_seq)
    step_dec = jnp.stack([decay[:, 0, 0], q_dec[:, 0], k_dec[:, 0], c_dec])

    def ln(i, s):
        return ln_g[i, s].reshape(1, D_MODEL), ln_b[i, s].reshape(1, D_MODEL)

    swa_k_prompt, swa_v_prompt = [], []
    swa_sample = None
    ret_prompt = None
    ret_sample = None
    for i in range(DEPTH):
        j = i // 2
        if i == 0:
            x = _ffn(x_prompt.reshape(mp, D_MODEL), ffn_w_gu, ffn_w_down, i, 0, *ln(i, 0), tm, tiles, tail,
                     x_sample=x_sample.reshape(ms, D_MODEL))
        else:
            x = _ffn(x, ffn_w_gu, ffn_w_down, i, 0, *ln(i, 0), tm, tiles, tail)
        if i % 2 == 0:
            qkv = _proj(x, swa_w_in, j, tm_proj, rot_heads=SWA_Q_HEADS + SWA_KV_HEADS, tables=rope_tabs,
                        shifts=rope_shifts)
            x, k_tail, v_tail = _swa_prompt(qkv, bp, seq, mem_kv, i, swa_sinks[j], swa_w_out, x, *ln(i, 1))
            swa_k_prompt.append(k_tail.reshape(bp, WINDOW, SWA_KV_HEADS, HEAD_DIM))
            swa_v_prompt.append(v_tail.reshape(bp, WINDOW, SWA_KV_HEADS, HEAD_DIM))
            rows = qkv[mp:mp + ms].reshape(ms, SWA_IN_WIDTH // HEAD_DIM, HEAD_DIM)
            att, *swa_sample = _swa_step(rows, cache_swa_k, cache_swa_v, j, cache_mem_k, cache_mem_v, i,
                                         swa_sinks[j], swa_sample)
            x = _out_ln(att.reshape(ms, D_MODEL), swa_w_out, j, x, *ln(i, 1), mp)
        else:
            qkvg = _proj(x, ret_w_in, j, tm_proj, rot_heads=2 * RET_HEADS, tables=rot_tabs,
                         shifts=rot_shifts)
            x, ret_prompt = _ret_prompt(qkvg, bp, seq, mem_kv, i, ret_tabs, ret_w_out, x, *ln(i, 1),
                                        None if ret_prompt is None else [ret_prompt])
            rows = qkvg[mp:mp + ms].reshape(ms, RET_IN_WIDTH // HEAD_DIM, HEAD_DIM)
            att, ret_sample = _ret_step(rows, jnp.swapaxes(rows, 1, 2), state_ret, j, cache_mem_k,
                                        cache_mem_v, i, step_dec,
                                        None if ret_sample is None else [ret_sample])
            x = _out_ln(att.reshape(ms, D_MODEL), ret_w_out, j, x, *ln(i, 1), mp)
        x = _ffn(x, ffn_w_gu, ffn_w_down, i, 1, *ln(i, 2), tm, tiles, tail, split_out=(i == DEPTH - 1))

    y_prompt = x[0].reshape(bp, seq, D_MODEL)
    y_sample = x[1].reshape(bs, sample_seq, D_MODEL)
    return (y_prompt, y_sample, jnp.stack(swa_k_prompt), jnp.stack(swa_v_prompt), swa_sample[0],
            swa_sample[1], ret_prompt, ret_sample, mem_k_prompt, mem_v_prompt)
```

```python
import functools

import jax
import jax.numpy as jnp
from jax import lax
from jax.experimental import pallas as pl
from jax.experimental.pallas import tpu as pltpu

F32 = jnp.float32
BF16 = jnp.bfloat16

D_MODEL = 2048
DEPTH = 4
PAST_LEN = 16384
HEAD_DIM = 128
MEM_LEN = 256
MEM_HEADS = 4
MEM_DIM = MEM_HEADS * HEAD_DIM
SELF_WIDTH = D_MODEL - MEM_DIM
SWA_Q_HEADS = SELF_WIDTH // HEAD_DIM
SWA_KV_HEADS = SWA_Q_HEADS // 3
SWA_GROUP = SWA_Q_HEADS // SWA_KV_HEADS
WINDOW = 128
ROPE_THETA = 500000.0
ROPE_DIM = HEAD_DIM // 4
RET_DK = 128
RET_DV = 2 * RET_DK
RET_HEADS = SELF_WIDTH // RET_DV
RET_CHUNK = 128
RET_ROT_BASE = 10000.0
D_FF = ((8 * D_MODEL // 3 + 255) // 256) * 256
LN_EPS = 1e-5
HEAD_NORM_EPS = 1e-6
ALPHA = (2.0 * DEPTH) ** 0.25
NEG_INF = -1e30
ATT_SCALE = HEAD_DIM ** -0.5
RET_K_SCALE = RET_DK ** -0.5

SWA_Q = SWA_Q_HEADS * HEAD_DIM
SWA_KV = SWA_KV_HEADS * HEAD_DIM
SWA_IN_WIDTH = SWA_Q + 2 * SWA_KV + MEM_DIM
RET_QK = RET_HEADS * RET_DK
RET_V = RET_HEADS * RET_DV
RET_IN_WIDTH = 2 * RET_QK + 2 * RET_V + MEM_DIM

VMEM_LIMIT_BYTES = 58 * 1024 * 1024
LANES = 128
FFN_TILE_K = 256
FFN_MAX_ROWS = 1040
PROJ_MAX_ROWS = 1040
ATT_BLOCKS_PER_STEP = 2
PROJ_TILE_N = 1024
OUT_TILE_N = 512
MAX_ROW_CHUNK = 256
BF16_SUBLANES = 16


def _row_tiles(rows, max_rows):
    tiles = -(-rows // max_rows)
    per_tile = -(-rows // tiles)
    return -(-per_tile // BF16_SUBLANES) * BF16_SUBLANES, tiles


def _row_chunks(tm):
    count = -(-tm // MAX_ROW_CHUNK)
    size = -(-(-(-tm // count)) // BF16_SUBLANES) * BF16_SUBLANES
    return [slice(r, min(r + size, tm)) for r in range(0, tm, size)]


def _params(*sem):
    return pltpu.CompilerParams(dimension_semantics=sem, vmem_limit_bytes=VMEM_LIMIT_BYTES)


def _layer_norm_rows(z, g, b):
    mu = jnp.mean(z, axis=-1, keepdims=True)
    zc = z - mu
    var = jnp.mean(zc * zc, axis=-1, keepdims=True)
    return zc * lax.rsqrt(var + LN_EPS) * g + b


def _ffn_kernel(*refs, nk, tk, tiles, tail, split_in, split_out):
    refs = list(refs)
    x_ref = refs.pop(0)
    xs_ref = refs.pop(0) if split_in else None
    wg_ref, wu_ref, wd_ref, g_ref, b_ref, o_ref = refs[:6]
    os_ref = refs[6] if split_out else None
    xb_ref, wgu_ref = refs[-2:]
    i = pl.program_id(0)
    k = pl.program_id(1)
    chunks = _row_chunks(o_ref.shape[0])

    def load_x(rows):
        xv = x_ref[rows, :]
        if not split_in or rows.stop <= tail[0]:
            return xv
        first, count = tail
        lo, hi = rows.start, rows.stop
        pieces = []
        if lo < first:
            pieces.append(xv[:first - lo])
        s0, s1 = max(lo, first), min(hi, first + count)
        if s0 < s1:
            pieces.append(xs_ref[s0 - first:s1 - first, :])
        if hi > first + count:
            pieces.append(jnp.zeros((hi - max(lo, first + count), xv.shape[1]), xv.dtype))
        return jnp.where(i == tiles - 1, jnp.concatenate(pieces, axis=0), xv)

    def cast_weights():
        wgu_ref[:, :tk] = wg_ref[...].astype(BF16)
        wgu_ref[:, tk:] = wu_ref[...].astype(BF16)
        return wd_ref[...].astype(BF16)

    def down(xb, wd):
        gu = jnp.dot(xb, wgu_ref[...], preferred_element_type=F32)
        h = (jax.nn.silu(gu[:, :tk]) * gu[:, tk:]).astype(BF16)
        return jnp.dot(h, wd, preferred_element_type=F32)

    @pl.when(k == 0)
    def _():
        wd = cast_weights()
        for rows in chunks:
            xb = load_x(rows).astype(BF16)
            xb_ref[rows, :] = xb
            o_ref[rows, :] = down(xb, wd)

    @pl.when(jnp.logical_and(k > 0, k < nk - 1))
    def _():
        wd = cast_weights()
        o_ref[...] += down(xb_ref[...], wd)

    @pl.when(k == nk - 1)
    def _():
        wd = cast_weights()
        for rows in chunks:
            acc = o_ref[rows, :] + down(xb_ref[rows, :], wd)
            z = ALPHA * load_x(rows) + 0.5 * acc
            o_ref[rows, :] = _layer_norm_rows(z, g_ref[...], b_ref[...])
        if split_out:
            @pl.when(i == tiles - 1)
            def _():
                os_ref[...] = o_ref[tail[0]:tail[0] + tail[1], :]


def _ffn(x, w_gu, w_down, layer, slot, g, b, tm, tiles, tail, x_sample=None, split_out=False):
    tk = FFN_TILE_K
    nk = D_FF // tk
    assert nk >= 2
    split_in = x_sample is not None
    rows = pl.BlockSpec((tm, D_MODEL), lambda i, k: (i, 0))
    small = pl.BlockSpec((tail[1], D_MODEL), lambda i, k: (0, 0))
    in_specs = [pl.BlockSpec((tm, D_MODEL), lambda i, k: (i, 0), pipeline_mode=pl.Buffered(1))]
    args = [x]
    if split_in:
        in_specs.append(small)
        args.append(x_sample)
    in_specs += [
        pl.BlockSpec((None, None, D_MODEL, tk), lambda i, k: (layer, slot, 0, k)),
        pl.BlockSpec((None, None, D_MODEL, tk), lambda i, k: (layer, slot, 0, nk + k)),
        pl.BlockSpec((None, None, tk, D_MODEL), lambda i, k: (layer, slot, k, 0)),
        pl.BlockSpec((1, D_MODEL), lambda i, k: (0, 0)),
        pl.BlockSpec((1, D_MODEL), lambda i, k: (0, 0)),
    ]
    if split_out:
        prompt_rows = (tiles - 1) * tm + tail[0]
        out_specs = [rows, small]
        out_shape = [jax.ShapeDtypeStruct((prompt_rows, D_MODEL), F32),
                     jax.ShapeDtypeStruct((tail[1], D_MODEL), F32)]
    else:
        out_specs = rows
        out_shape = jax.ShapeDtypeStruct((tiles * tm, D_MODEL), F32)
    return pl.pallas_call(
        functools.partial(_ffn_kernel, nk=nk, tk=tk, tiles=tiles, tail=tail,
                          split_in=split_in, split_out=split_out),
        grid=(tiles, nk),
        in_specs=in_specs,
        out_specs=out_specs,
        out_shape=out_shape,
        scratch_shapes=[pltpu.VMEM((tm, D_MODEL), BF16), pltpu.VMEM((D_MODEL, 2 * tk), BF16)],
        compiler_params=_params("arbitrary", "arbitrary"),
        name="ffn_ln",
    )(*args, w_gu, w_gu, w_down, g, b)


def _proj_kernel(x_ref, w_ref, *rest, rot_heads, shifts):
    if rot_heads:
        c_ref, a_ref, b_ref, o_ref, xb_ref = rest
    else:
        o_ref, xb_ref = rest
    j = pl.program_id(1)

    @pl.when(j == 0)
    def _():
        xb_ref[...] = x_ref[...].astype(BF16)

    def plain():
        o_ref[...] = jnp.dot(xb_ref[...], w_ref[...].astype(BF16), preferred_element_type=F32)

    if not rot_heads:
        plain()
        return

    tm, tn = o_ref.shape
    heads = tn // LANES
    full_tiles, part_heads = divmod(rot_heads, heads)

    def rotated(n_heads):
        wb = w_ref[...].astype(BF16)
        for rows in _row_chunks(tm):
            y = jnp.dot(xb_ref[rows, :], wb, preferred_element_type=F32)
            c, a, b = c_ref[rows, :], a_ref[rows, :], b_ref[rows, :]
            for h in range(heads):
                yh = y[:, h * LANES:(h + 1) * LANES]
                if h < n_heads:
                    yh = yh * c + pltpu.roll(yh, shifts[0], 1) * a + pltpu.roll(yh, shifts[1], 1) * b
                o_ref[rows, h * LANES:(h + 1) * LANES] = yh

    pl.when(j < full_tiles)(lambda: rotated(heads))
    if part_heads:
        pl.when(j == full_tiles)(lambda: rotated(part_heads))
    pl.when(j >= full_tiles + (1 if part_heads else 0))(plain)


def _proj(x, w, layer, tm, rot_heads=0, tables=None, shifts=None):
    m = x.shape[0]
    n = w.shape[-1]
    tn = PROJ_TILE_N
    in_specs = [
        pl.BlockSpec((tm, D_MODEL), lambda i, j: (i, 0)),
        pl.BlockSpec((None, D_MODEL, tn), lambda i, j: (layer, 0, j)),
    ]
    args = [x, w]
    if rot_heads:
        tab = pl.BlockSpec((tm, LANES), lambda i, j: (i, 0))
        in_specs += [tab, tab, tab]
        args += list(tables)
    return pl.pallas_call(
        functools.partial(_proj_kernel, rot_heads=rot_heads, shifts=shifts),
        grid=(pl.cdiv(m, tm), n // tn),
        in_specs=in_specs,
        out_specs=pl.BlockSpec((tm, tn), lambda i, j: (i, j)),
        out_shape=jax.ShapeDtypeStruct((m, n), F32),
        scratch_shapes=[pltpu.VMEM((tm, D_MODEL), BF16)],
        compiler_params=_params("parallel", "arbitrary"),
        name="proj",
    )(*args)


def _mem_kv_kernel(x_ref, w_ref, o_ref, xb_ref):
    @pl.when(pl.program_id(0) == 0)
    def _():
        xb_ref[...] = x_ref[...].astype(BF16)

    o_ref[...] = jnp.dot(xb_ref[...], w_ref[...].astype(BF16), preferred_element_type=F32)


def _mem_kv(mem, w):
    rows = mem.shape[0]
    layers, _, n = w.shape
    return pl.pallas_call(
        _mem_kv_kernel,
        grid=(layers,),
        in_specs=[
            pl.BlockSpec((rows, D_MODEL), lambda l: (0, 0)),
            pl.BlockSpec((None, D_MODEL, n), lambda l: (l, 0, 0)),
        ],
        out_specs=pl.BlockSpec((None, rows, n), lambda l: (l, 0, 0)),
        out_shape=jax.ShapeDtypeStruct((layers, rows, n), F32),
        scratch_shapes=[pltpu.VMEM((rows, D_MODEL), BF16)],
        compiler_params=_params("arbitrary"),
        name="mem_kv",
    )(mem, w)


def _out_ln_kernel(att_ref, w_ref, x_ref, g_ref, b_ref, o_ref, y_ref, *, nn, tn):
    n = pl.program_id(1)
    y_ref[n] = jnp.dot(att_ref[...].astype(BF16), w_ref[...].astype(BF16),
                       preferred_element_type=F32)

    @pl.when(n == nn - 1)
    def _():
        cols = [slice(c * tn, (c + 1) * tn) for c in range(nn)]
        total = None
        for c in range(nn):
            z = ALPHA * x_ref[:, cols[c]] + y_ref[c]
            y_ref[c] = z
            part = jnp.sum(z, axis=-1, keepdims=True)
            total = part if total is None else total + part
        mu = total * (1.0 / D_MODEL)
        total = None
        for c in range(nn):
            zc = y_ref[c] - mu
            part = jnp.sum(zc * zc, axis=-1, keepdims=True)
            total = part if total is None else total + part
        rstd = lax.rsqrt(total * (1.0 / D_MODEL) + LN_EPS)
        for c in range(nn):
            o_ref[:, cols[c]] = (y_ref[c] - mu) * rstd * g_ref[:, cols[c]] + b_ref[:, cols[c]]


def _out_ln(att, w_out, layer, x, g, b, row0):
    tm = att.shape[0]
    assert row0 % tm == 0
    tn = OUT_TILE_N
    nn = D_MODEL // tn
    rows = pl.BlockSpec((tm, D_MODEL), lambda i, n: (row0 // tm, 0))
    return pl.pallas_call(
        functools.partial(_out_ln_kernel, nn=nn, tn=tn),
        grid=(1, nn),
        in_specs=[
            pl.BlockSpec((tm, D_MODEL), lambda i, n: (0, 0)),
            pl.BlockSpec((None, D_MODEL, tn), lambda i, n: (layer, 0, n)),
            rows,
            pl.BlockSpec((1, D_MODEL), lambda i, n: (0, 0)),
            pl.BlockSpec((1, D_MODEL), lambda i, n: (0, 0)),
        ],
        out_specs=rows,
        out_shape=jax.ShapeDtypeStruct(x.shape, F32),
        scratch_shapes=[pltpu.VMEM((nn, tm, tn), F32)],
        input_output_aliases={2: 0},
        compiler_params=_params("arbitrary", "arbitrary"),
        name="out_ln",
    )(att, w_out, x, g, b)


def _dot_nt(a, b):
    return lax.dot_general(a, b, (((1,), (1,)), ((), ())), preferred_element_type=F32)


def _out_proj_ln(first, att_ref, w_ref, wb_ref, x_ref, g_ref, b_ref, o_ref):
    @pl.when(first)
    def _():
        wb_ref[...] = w_ref[...].astype(BF16)

    y = jnp.dot(att_ref[...], wb_ref[...], preferred_element_type=F32)
    o_ref[...] = _layer_norm_rows(ALPHA * x_ref[...] + y, g_ref[...], b_ref[...])


def _normalised_pv(p, v, extra=None):
    den = jnp.dot(p, jnp.ones((p.shape[1], HEAD_DIM), p.dtype), preferred_element_type=F32)
    if extra is not None:
        den = den + extra
    return jnp.dot(p, v, preferred_element_type=F32) * (1.0 / den)


def _mem_attend_block(qm_ref, mk_ref, mv_ref, att_ref, col0):
    for h in range(MEM_HEADS):
        sl = slice(h * HEAD_DIM, (h + 1) * HEAD_DIM)
        q = qm_ref[:, sl].astype(BF16)
        s = _dot_nt(q, mk_ref[0, :, sl].astype(BF16)) * ATT_SCALE
        p = jnp.exp(s - jnp.max(s, axis=-1, keepdims=True))
        p = p * (1.0 / jnp.sum(p, axis=-1, keepdims=True))
        o = jnp.dot(p.astype(BF16), mv_ref[0, :, sl].astype(BF16), preferred_element_type=F32)
        att_ref[:, col0 + h * HEAD_DIM:col0 + (h + 1) * HEAD_DIM] = o.astype(att_ref.dtype)


def _swa_prompt_kernel(sink_ref, q_ref, kc_ref, kp_ref, vc_ref, vp_ref, qm_ref, mk_ref, mv_ref,
                       w_ref, x_ref, g_ref, b_ref, o_ref, kt_ref, vt_ref, s_ref, p_ref, att_ref, wb_ref,
                       *, nblk, steps):
    n = pl.program_id(1)
    nq = SWA_Q_HEADS
    w2 = 2 * WINDOW

    def prev_cur(cur_ref, prev_ref, i, sl):
        rows = slice(i * WINDOW, (i + 1) * WINDOW)
        before = prev_ref[:, sl] if i == 0 else cur_ref[(i - 1) * WINDOW:i * WINDOW, sl]
        return jnp.concatenate([before, cur_ref[rows, sl]], axis=0).astype(BF16)

    for i in range(nblk):
        rows = slice(i * WINDOW, (i + 1) * WINDOW)
        for h in range(SWA_KV_HEADS):
            sl = slice(h * HEAD_DIM, (h + 1) * HEAD_DIM)
            q3 = jnp.concatenate(
                [q_ref[rows, (h * SWA_GROUP + g) * HEAD_DIM:(h * SWA_GROUP + g + 1) * HEAD_DIM]
                 for g in range(SWA_GROUP)], axis=0).astype(BF16)
            s = _dot_nt(q3, prev_cur(kc_ref, kp_ref, i, sl)) * ATT_SCALE
            s_ref[i, h * SWA_GROUP:(h + 1) * SWA_GROUP] = s.reshape(SWA_GROUP, WINDOW, w2)
    for h in range(MEM_HEADS):
        sl = slice(h * HEAD_DIM, (h + 1) * HEAD_DIM)
        sm = _dot_nt(qm_ref[:, sl].astype(BF16), mk_ref[0, :, sl].astype(BF16)) * ATT_SCALE
        s_ref[:, nq + h] = sm.reshape(nblk, WINDOW, w2)

    qi = lax.broadcasted_iota(jnp.int32, (WINDOW, w2), 0)
    kj = lax.broadcasted_iota(jnp.int32, (WINDOW, w2), 1)
    sink = sink_ref[...]
    sink_terms = []
    for i in range(nblk):
        first_key = jnp.where(n > 0, qi, WINDOW) if i == 0 else qi
        ok = (kj >= first_key) & (kj <= qi + WINDOW)
        s = jnp.where(ok[None], s_ref[i, 0:nq], NEG_INF)
        m = jnp.maximum(jnp.max(s, axis=-1, keepdims=True), sink)
        p_ref[i, 0:nq] = jnp.exp(s - m).astype(BF16)
        sink_terms.append(jnp.exp(sink - m))
    s = s_ref[:, nq:nq + MEM_HEADS]
    p_ref[:, nq:nq + MEM_HEADS] = jnp.exp(s - jnp.max(s, axis=-1, keepdims=True)).astype(BF16)

    for i in range(nblk):
        rows = slice(i * WINDOW, (i + 1) * WINDOW)
        for h in range(SWA_KV_HEADS):
            sl = slice(h * HEAD_DIM, (h + 1) * HEAD_DIM)
            group = slice(h * SWA_GROUP, (h + 1) * SWA_GROUP)
            p3 = p_ref[i, group].reshape(SWA_GROUP * WINDOW, w2)
            o = _normalised_pv(p3, prev_cur(vc_ref, vp_ref, i, sl),
                               sink_terms[i][group].reshape(SWA_GROUP * WINDOW, 1))
            for g in range(SWA_GROUP):
                hq = h * SWA_GROUP + g
                att_ref[rows, hq * HEAD_DIM:(hq + 1) * HEAD_DIM] = (
                    o[g * WINDOW:(g + 1) * WINDOW].astype(BF16))
    for h in range(MEM_HEADS):
        sl = slice(h * HEAD_DIM, (h + 1) * HEAD_DIM)
        pm = p_ref[:, nq + h].reshape(nblk * WINDOW, w2)
        om = _normalised_pv(pm, mv_ref[0, :, sl].astype(BF16))
        att_ref[:, SWA_Q + h * HEAD_DIM:SWA_Q + (h + 1) * HEAD_DIM] = om.astype(BF16)

    @pl.when(n == steps - 1)
    def _():
        last = slice((nblk - 1) * WINDOW, nblk * WINDOW)
        kt_ref[0] = kc_ref[last, :]
        vt_ref[0] = vc_ref[last, :]

    first = jnp.logical_and(pl.program_id(0) == 0, n == 0)
    _out_proj_ln(first, att_ref, w_ref, wb_ref, x_ref, g_ref, b_ref, o_ref)


def _mixer_tail_specs(layer, row_of, rows):
    mem = (None, 1, MEM_LEN, MEM_DIM)
    vec = pl.BlockSpec((1, D_MODEL), lambda b, n: (0, 0))
    mixer = layer // 2
    return [
        pl.BlockSpec(mem, lambda b, n: (layer, b, 0, 0)),
        pl.BlockSpec(mem, lambda b, n: (layer, b, 0, 1)),
        pl.BlockSpec((None, D_MODEL, D_MODEL), lambda b, n: (mixer, 0, 0), pipeline_mode=pl.Buffered(1)),
        pl.BlockSpec((rows, D_MODEL), lambda b, n: (row_of(b, n), 0)),
        vec, vec,
    ]


def _swa_prompt(qkv, bsz, seq, mem_kv, layer, sinks, w_out, x, g, b):
    nblk = ATT_BLOCKS_PER_STEP
    rows = nblk * WINDOW
    steps = seq // rows
    assert seq % rows == 0 and seq >= WINDOW
    tail = pl.BlockSpec((1, WINDOW, SWA_KV), lambda b, n: (b, 0, 0))
    blocks_per_seq = seq // WINDOW
    kcol = SWA_Q // SWA_KV
    vcol = kcol + 1
    mcol = vcol + 1
    row_of = lambda b, n: b * steps + n
    prev_of = lambda b, n: b * blocks_per_seq + jnp.maximum(n * nblk - 1, 0)
    cur = lambda w, col: pl.BlockSpec((rows, w), lambda b, n: (row_of(b, n), col))
    prev = lambda col: pl.BlockSpec((WINDOW, SWA_KV), lambda b, n: (prev_of(b, n), col))
    nheads = SWA_Q_HEADS + MEM_HEADS
    assert MEM_LEN == 2 * WINDOW
    in_specs = [
        pl.BlockSpec((SWA_Q_HEADS, 1, 1), lambda b, n: (0, 0, 0)),
        cur(SWA_Q, 0), cur(SWA_KV, kcol), prev(kcol), cur(SWA_KV, vcol), prev(vcol), cur(MEM_DIM, mcol),
    ] + _mixer_tail_specs(layer, row_of, rows)
    return pl.pallas_call(
        functools.partial(_swa_prompt_kernel, nblk=nblk, steps=steps),
        grid=(bsz, steps),
        in_specs=in_specs,
        out_specs=[pl.BlockSpec((rows, D_MODEL), lambda b, n: (row_of(b, n), 0)), tail, tail],
        out_shape=[jax.ShapeDtypeStruct(x.shape, F32),
                   jax.ShapeDtypeStruct((bsz, WINDOW, SWA_KV), F32),
                   jax.ShapeDtypeStruct((bsz, WINDOW, SWA_KV), F32)],
        scratch_shapes=[pltpu.VMEM((nblk, nheads, WINDOW, 2 * WINDOW), F32),
                        pltpu.VMEM((nblk, nheads, WINDOW, 2 * WINDOW), BF16),
                        pltpu.VMEM((rows, D_MODEL), BF16),
                        pltpu.VMEM((D_MODEL, D_MODEL), BF16)],
        input_output_aliases={len(in_specs) - 3: 0},
        compiler_params=_params("arbitrary", "arbitrary"),
        name="swa_prompt",
    )(sinks.reshape(SWA_Q_HEADS, 1, 1), qkv, qkv, qkv, qkv, qkv, qkv, mem_kv, mem_kv, w_out, x, g, b)


def _head_norm_gate(o, gate):
    mu = jnp.mean(o, axis=-1, keepdims=True)
    oc = o - mu
    var = jnp.mean(oc * oc, axis=-1, keepdims=True)
    return jax.nn.silu(gate) * (oc * lax.rsqrt(var + HEAD_NORM_EPS))


def _ret_prompt_kernel(cdec_ref, decay_ref, qdec_ref, kdec_ref, q_ref, k_ref, v_ref, gate_ref, qm_ref,
                       mk_ref, mv_ref, w_ref, x_ref, g_ref, b_ref, o_ref, s_out_ref,
                       state_ref, att_ref, wb_ref, *, steps, nblk):
    c = pl.program_id(1)

    @pl.when(c == 0)
    def _():
        state_ref[...] = jnp.zeros_like(state_ref)

    for i in range(nblk):
        rows = slice(i * RET_CHUNK, (i + 1) * RET_CHUNK)
        for h in range(RET_HEADS):
            ksl = slice(h * RET_DK, (h + 1) * RET_DK)
            vsl = slice(h * RET_DV, (h + 1) * RET_DV)
            qc = q_ref[rows, ksl]
            kc = k_ref[rows, ksl] * RET_K_SCALE
            vb = v_ref[rows, vsl].astype(BF16)
            st = state_ref[h]
            inner = _dot_nt(qc.astype(BF16), kc.astype(BF16)) * decay_ref[h]
            o = (jnp.dot(inner.astype(BF16), vb, preferred_element_type=F32)
                 + jnp.dot((qc * qdec_ref[h]).astype(BF16), st.astype(BF16),
                           preferred_element_type=F32))
            kd = (kc * kdec_ref[h]).astype(BF16)
            state_ref[h] = cdec_ref[h] * st + lax.dot_general(
                kd, vb, (((0,), (0,)), ((), ())), preferred_element_type=F32)
            att_ref[rows, vsl] = _head_norm_gate(o, gate_ref[rows, vsl]).astype(BF16)
    _mem_attend_block(qm_ref, mk_ref, mv_ref, att_ref, RET_V)

    @pl.when(c == steps - 1)
    def _():
        s_out_ref[0] = state_ref[...]

    first = jnp.logical_and(pl.program_id(0) == 0, c == 0)
    _out_proj_ln(first, att_ref, w_ref, wb_ref, x_ref, g_ref, b_ref, o_ref)


def _drop_carried(kernel, first, count):
    def body(*refs):
        return kernel(*refs[:first], *refs[first + count:])
    return body


def _carry(prev_outputs):
    prev_outputs = list(prev_outputs)
    return [pl.BlockSpec(memory_space=pl.ANY)] * len(prev_outputs), prev_outputs


def _ret_prompt(qkvg, bsz, seq, mem_kv, layer, tables, w_out, x, g, b, carried):
    nblk = ATT_BLOCKS_PER_STEP
    rows = nblk * RET_CHUNK
    steps = seq // rows
    mixer = layer // 2
    decay, qdec, kdec, cdec = tables
    row_of = lambda b, n: b * steps + n
    cur = lambda w, col: pl.BlockSpec((rows, w), lambda b, n: (row_of(b, n), col))
    tab = pl.BlockSpec((RET_HEADS, RET_CHUNK, RET_CHUNK), lambda b, n: (0, 0, 0))
    in_specs = [
        pl.BlockSpec(memory_space=pltpu.SMEM), tab, tab, tab,
        cur(RET_QK, 0), cur(RET_QK, 1), cur(RET_V, 1), cur(RET_V, 2),
        cur(MEM_DIM, (2 * RET_QK + 2 * RET_V) // MEM_DIM),
    ] + _mixer_tail_specs(layer, row_of, rows)
    n_in = len(in_specs)
    carry_specs, carry_args = _carry(carried)
    aliases = {n_in - 3: 0}
    aliases.update({n_in + c: 1 + c for c in range(len(carry_args))})
    return pl.pallas_call(
        _drop_carried(functools.partial(_ret_prompt_kernel, steps=steps, nblk=nblk), n_in, len(carry_args)),
        grid=(bsz, steps),
        in_specs=in_specs + carry_specs,
        out_specs=[
            pl.BlockSpec((rows, D_MODEL), lambda b, n: (row_of(b, n), 0)),
            pl.BlockSpec((None, 1, RET_HEADS, RET_DK, RET_DV), lambda b, n: (mixer, b, 0, 0, 0)),
        ],
        out_shape=[
            jax.ShapeDtypeStruct(x.shape, F32),
            jax.ShapeDtypeStruct((DEPTH // 2, bsz, RET_HEADS, RET_DK, RET_DV), F32),
        ],
        scratch_shapes=[pltpu.VMEM((RET_HEADS, RET_DK, RET_DV), F32),
                        pltpu.VMEM((rows, D_MODEL), BF16),
                        pltpu.VMEM((D_MODEL, D_MODEL), BF16)],
        input_output_aliases=aliases,
        compiler_params=_params("arbitrary", "arbitrary"),
        name="ret_prompt",
    )(cdec, decay, qdec, kdec, qkvg, qkvg, qkvg, qkvg, qkvg, mem_kv, mem_kv, w_out, x, g, b, *carry_args)


def _mem_attend_row(q, mk_ref, mv_ref):
    s = jnp.sum(mk_ref[0] * q[None], axis=-1, keepdims=True) * ATT_SCALE
    p = jnp.exp(s - jnp.max(s, axis=0, keepdims=True))
    p = p * (1.0 / jnp.sum(p, axis=0, keepdims=True))
    return jnp.sum(p * mv_ref[0], axis=0)


def _swa_step_kernel(sink_ref, rows_ref, kbuf_ref, vbuf_ref, mk_ref, mv_ref, o_ref, nk_ref, nv_ref):
    krow0 = SWA_Q_HEADS
    vrow0 = krow0 + SWA_KV_HEADS
    mrow0 = vrow0 + SWA_KV_HEADS
    wb = kbuf_ref.shape[1]
    kb = kbuf_ref[0]
    vb = vbuf_ref[0]
    k_new = rows_ref[0, krow0:krow0 + SWA_KV_HEADS, :]
    v_new = rows_ref[0, vrow0:vrow0 + SWA_KV_HEADS, :]
    for g in range(SWA_GROUP):
        group_rows = pl.ds(g, SWA_KV_HEADS, stride=SWA_GROUP)
        q = rows_ref[0, group_rows, :]
        s_buf = jnp.sum(kb * q[None], axis=-1, keepdims=True) * ATT_SCALE
        s_new = jnp.sum(k_new * q, axis=-1, keepdims=True) * ATT_SCALE
        sink = sink_ref[g]
        m = jnp.maximum(jnp.maximum(jnp.max(s_buf, axis=0), s_new), sink)
        p_buf = jnp.exp(s_buf - m[None])
        p_new = jnp.exp(s_new - m)
        inv = 1.0 / (jnp.sum(p_buf, axis=0) + p_new + jnp.exp(sink - m))
        o_ref[0, group_rows, :] = jnp.sum((p_buf * inv[None]) * vb, axis=0) + (p_new * inv) * v_new
    nk_ref[0, 0:wb - 1] = kbuf_ref[0, 1:wb]
    nv_ref[0, 0:wb - 1] = vbuf_ref[0, 1:wb]
    nk_ref[0, wb - 1] = k_new
    nv_ref[0, wb - 1] = v_new
    o_ref[0, SWA_Q_HEADS:SWA_Q_HEADS + MEM_HEADS, :] = _mem_attend_row(
        rows_ref[0, mrow0:mrow0 + MEM_HEADS, :], mk_ref, mv_ref)


def _swa_step(rows, cache_k, cache_v, j, mem_k, mem_v, i, sinks, carried):
    bsz = rows.shape[0]
    wb = cache_k.shape[2]
    cache = pl.BlockSpec((None, 1, wb, SWA_KV_HEADS, HEAD_DIM), lambda b: (j, b, 0, 0, 0))
    mem = pl.BlockSpec((None, 1, MEM_LEN, MEM_HEADS, HEAD_DIM), lambda b: (i, b, 0, 0, 0))
    nrows = D_MODEL // HEAD_DIM
    sink_gk = sinks.reshape(SWA_KV_HEADS, SWA_GROUP).T.reshape(SWA_GROUP, SWA_KV_HEADS, 1)
    in_specs = [
        pl.BlockSpec(sink_gk.shape, lambda b: (0, 0, 0)),
        pl.BlockSpec((1,) + rows.shape[1:], lambda b: (b, 0, 0)),
        cache, cache, mem, mem,
    ]
    n_in = len(in_specs)
    carry_specs, carry_args = _carry(carried)
    return pl.pallas_call(
        _drop_carried(_swa_step_kernel, n_in, len(carry_args)),
        grid=(bsz,),
        in_specs=in_specs + carry_specs,
        out_specs=[pl.BlockSpec((1, nrows, HEAD_DIM), lambda b: (b, 0, 0)), cache, cache],
        out_shape=[
            jax.ShapeDtypeStruct((bsz, nrows, HEAD_DIM), F32),
            jax.ShapeDtypeStruct(cache_k.shape, F32),
            jax.ShapeDtypeStruct(cache_v.shape, F32),
        ],
        input_output_aliases={n_in + c: 1 + c for c in range(len(carry_args))},
        compiler_params=_params("parallel"),
        name="swa_step",
    )(sink_gk, rows, cache_k, cache_v, mem_k, mem_v, *carry_args)


def _ret_step_kernel(dec_ref, rows_ref, cols_ref, s_ref, mk_ref, mv_ref, o_ref, s_out_ref):
    krow0 = RET_HEADS
    vrow0 = 2 * RET_HEADS
    grow0 = vrow0 + 2 * RET_HEADS
    mrow0 = grow0 + 2 * RET_HEADS
    for h in range(RET_HEADS):
        q_row = rows_ref[0, h:h + 1, :]
        k_row = rows_ref[0, krow0 + h:krow0 + h + 1, :] * RET_K_SCALE
        q_col = cols_ref[0, :, h:h + 1]
        k_col = cols_ref[0, :, krow0 + h:krow0 + h + 1] * RET_K_SCALE
        inner = jnp.sum(q_row * k_row, axis=1, keepdims=True) * dec_ref[0, h]
        qd = q_col * dec_ref[1, h]
        kd = k_col * dec_ref[2, h]
        halves = []
        for t in range(2):
            lsl = slice(t * HEAD_DIM, (t + 1) * HEAD_DIM)
            v = rows_ref[0, vrow0 + 2 * h + t:vrow0 + 2 * h + t + 1, :]
            st = s_ref[0, h, :, lsl]
            halves.append(inner * v + jnp.sum(qd * st, axis=0, keepdims=True))
            s_out_ref[0, h, :, lsl] = dec_ref[3, h] * st + kd * v
        mu = (jnp.sum(halves[0], axis=1, keepdims=True)
              + jnp.sum(halves[1], axis=1, keepdims=True)) * (1.0 / RET_DV)
        cen = [o - mu for o in halves]
        var = (jnp.sum(cen[0] * cen[0], axis=1, keepdims=True)
               + jnp.sum(cen[1] * cen[1], axis=1, keepdims=True)) * (1.0 / RET_DV)
        rstd = lax.rsqrt(var + HEAD_NORM_EPS)
        for t in range(2):
            gate = rows_ref[0, grow0 + 2 * h + t:grow0 + 2 * h + t + 1, :]
            o_ref[0, 2 * h + t:2 * h + t + 1, :] = jax.nn.silu(gate) * (cen[t] * rstd)
    o_ref[0, 2 * RET_HEADS:2 * RET_HEADS + MEM_HEADS, :] = _mem_attend_row(
        rows_ref[0, mrow0:mrow0 + MEM_HEADS, :], mk_ref, mv_ref)


def _ret_step(rows, cols, state, j, mem_k, mem_v, i, dec, carried):
    bsz = rows.shape[0]
    st = pl.BlockSpec((None, 1, RET_HEADS, RET_DK, RET_DV), lambda b: (j, b, 0, 0, 0))
    mem = pl.BlockSpec((None, 1, MEM_LEN, MEM_HEADS, HEAD_DIM), lambda b: (i, b, 0, 0, 0))
    nrows = D_MODEL // HEAD_DIM
    in_specs = [
        pl.BlockSpec(memory_space=pltpu.SMEM),
        pl.BlockSpec((1,) + rows.shape[1:], lambda b: (b, 0, 0)),
        pl.BlockSpec((1,) + cols.shape[1:], lambda b: (b, 0, 0)),
        st, mem, mem,
    ]
    n_in = len(in_specs)
    carry_specs, carry_args = _carry(carried)
    return pl.pallas_call(
        _drop_carried(_ret_step_kernel, n_in, len(carry_args)),
        grid=(bsz,),
        in_specs=in_specs + carry_specs,
        out_specs=[pl.BlockSpec((1, nrows, HEAD_DIM), lambda b: (b, 0, 0)), st],
        out_shape=[
            jax.ShapeDtypeStruct((bsz, nrows, HEAD_DIM), F32),
            jax.ShapeDtypeStruct(state.shape, F32),
        ],
        input_output_aliases={n_in + c: 1 + c for c in range(len(carry_args))},
        compiler_params=_params("parallel"),
        name="ret_step",
    )(dec, rows, cols, state, mem_k, mem_v, *carry_args)


def _rope_tables(pos):
    half = ROPE_DIM // 2
    inv = ROPE_THETA ** (-jnp.arange(half, dtype=F32) / half)
    ang = pos.astype(F32)[:, None] * inv[None, :]
    cos, sin = jnp.cos(ang), jnp.sin(ang)
    n = pos.shape[0]
    rest = HEAD_DIM - ROPE_DIM
    c = jnp.concatenate([cos, cos, jnp.ones((n, rest), F32)], axis=-1)
    a = jnp.concatenate([-sin, jnp.zeros((n, HEAD_DIM - half), F32)], axis=-1)
    b = jnp.concatenate([jnp.zeros((n, half), F32), sin, jnp.zeros((n, rest), F32)], axis=-1)
    return (c, a, b), (HEAD_DIM - half, half)


def _ret_rot_tables(pos):
    half = RET_DK // 2
    angle = RET_ROT_BASE ** (-jnp.linspace(0.0, 1.0, half, dtype=F32))
    ang = pos.astype(F32)[:, None] * angle[None, :]
    cos, sin = jnp.cos(ang), jnp.sin(ang)
    n = pos.shape[0]
    zero = jnp.zeros_like(sin)
    c = jnp.stack([cos, cos], axis=-1).reshape(n, RET_DK)
    a = jnp.stack([-sin, zero], axis=-1).reshape(n, RET_DK)
    b = jnp.stack([zero, sin], axis=-1).reshape(n, RET_DK)
    return (c, a, b), (RET_DK - 1, 1)


def _ret_decay(chunk):
    log_g = jnp.log1p(-jnp.exp2(-5.0 - jnp.arange(RET_HEADS, dtype=F32)))
    n = jnp.arange(chunk, dtype=F32)
    rel = n[:, None] - n[None, :]
    decay = jnp.where(rel >= 0, jnp.exp(jnp.maximum(rel, 0.0) * log_g[:, None, None]), 0.0)
    q_dec = jnp.exp((n + 1.0) * log_g[:, None])
    k_dec = jnp.exp((chunk - 1.0 - n) * log_g[:, None])
    c_dec = jnp.exp(chunk * log_g)
    return decay, q_dec, k_dec, c_dec


def kernel(x_prompt, x_sample, cache_swa_k, cache_swa_v, state_ret, cache_mem_k, cache_mem_v,
           mem_prompt, ln_g, ln_b, ffn_w_gu, ffn_w_down, w_mem_kv, swa_w_in, swa_w_out,
           swa_sinks, ret_w_in, ret_w_out):
    bp, seq, _ = x_prompt.shape
    bs, sample_seq, _ = x_sample.shape
    assert sample_seq == 1
    mp, ms = bp * seq, bs
    assert mp % ms == 0

    mem2 = mem_prompt.reshape(bp * MEM_LEN, D_MODEL)
    mem_kv = _mem_kv(mem2, w_mem_kv).reshape(DEPTH, bp, MEM_LEN, 2 * MEM_DIM)
    mem_k_prompt = mem_kv[..., :MEM_DIM].reshape(DEPTH, bp, MEM_LEN, MEM_HEADS, HEAD_DIM)
    mem_v_prompt = mem_kv[..., MEM_DIM:].reshape(DEPTH, bp, MEM_LEN, MEM_HEADS, HEAD_DIM)

    tm, tiles = _row_tiles(mp + ms, FFN_MAX_ROWS)
    m_all = tm * tiles
    tm_proj, _ = _row_tiles(m_all, PROJ_MAX_ROWS)
    pad = m_all - mp - ms
    tail = (mp - (tiles - 1) * tm, ms)
    assert tail[0] >= 0 and tail[0] % 8 == 0 and ms % 8 == 0 and tail[0] + ms <= tm
    def per_row(table):
        return jnp.concatenate([jnp.tile(table[:seq], (bp, 1)),
                                jnp.broadcast_to(table[seq:], (ms, table.shape[1])),
                                jnp.zeros((pad, table.shape[1]), F32)], axis=0)

    pos = jnp.concatenate([jnp.arange(seq, dtype=jnp.int32), jnp.full((1,), PAST_LEN, jnp.int32)])
    rope_tabs, rope_shifts = _rope_tables(pos)
    rot_tabs, rot_shifts = _ret_rot_tables(pos)
    rope_tabs = tuple(per_row(t) for t in rope_tabs)
    rot_tabs = tuple(per_row(t) for t in rot_tabs)
    decay, q_dec, k_dec, c_dec = _ret_decay(RET_CHUNK)
    ret_tabs = (decay, jnp.broadcast_to(q_dec[:, :, None], decay.shape),
                jnp.broadcast_to(k_dec[:, :, None], decay.shape), c_dec)
    decay, q_dec, k_dec, c_dec = _ret_decay(sample_seq)
    step_dec = jnp.stack([decay[:, 0, 0], q_dec[:, 0], k_dec[:, 0], c_dec])

    def ln(i, s):
        return ln_g[i, s].reshape(1, D_MODEL), ln_b[i, s].reshape(1, D_MODEL)

    swa_k_prompt, swa_v_prompt = [], []
    swa_sample = [jnp.zeros_like(cache_swa_k), jnp.zeros_like(cache_swa_v)]
    ret_prompt = jnp.zeros((DEPTH // 2, bp, RET_HEADS, RET_DK, RET_DV), F32)
    ret_sample = jnp.zeros_like(state_ret)
    for i in range(DEPTH):
        j = i // 2
        if i == 0:
            x = _ffn(x_prompt.reshape(mp, D_MODEL), ffn_w_gu, ffn_w_down, i, 0, *ln(i, 0), tm, tiles, tail,
                     x_sample=x_sample.reshape(ms, D_MODEL))
        else:
            x = _ffn(x, ffn_w_gu, ffn_w_down, i, 0, *ln(i, 0), tm, tiles, tail)
        if i % 2 == 0:
            qkv = _proj(x, swa_w_in, j, tm_proj, rot_heads=SWA_Q_HEADS + SWA_KV_HEADS, tables=rope_tabs,
                        shifts=rope_shifts)
            x, k_tail, v_tail = _swa_prompt(qkv, bp, seq, mem_kv, i, swa_sinks[j], swa_w_out, x, *ln(i, 1))
            swa_k_prompt.append(k_tail.reshape(bp, WINDOW, SWA_KV_HEADS, HEAD_DIM))
            swa_v_prompt.append(v_tail.reshape(bp, WINDOW, SWA_KV_HEADS, HEAD_DIM))
            rows = qkv[mp:mp + ms].reshape(ms, SWA_IN_WIDTH // HEAD_DIM, HEAD_DIM)
            att, *swa_sample = _swa_step(rows, cache_swa_k, cache_swa_v, j, cache_mem_k, cache_mem_v, i,
                                         swa_sinks[j], swa_sample)
            x = _out_ln(att.reshape(ms, D_MODEL), swa_w_out, j, x, *ln(i, 1), mp)
        else:
            qkvg = _proj(x, ret_w_in, j, tm_proj, rot_heads=2 * RET_HEADS, tables=rot_tabs,
                         shifts=rot_shifts)
            x, ret_prompt = _ret_prompt(qkvg, bp, seq, mem_kv, i, ret_tabs, ret_w_out, x, *ln(i, 1),
                                        [ret_prompt])
            rows = qkvg[mp:mp + ms].reshape(ms, RET_IN_WIDTH // HEAD_DIM, HEAD_DIM)
            att, ret_sample = _ret_step(rows, jnp.swapaxes(rows, 1, 2), state_ret, j, cache_mem_k,
                                        cache_mem_v, i, step_dec,
                                        [ret_sample])
            x = _out_ln(att.reshape(ms, D_MODEL), ret_w_out, j, x, *ln(i, 1), mp)
        x = _ffn(x, ffn_w_gu, ffn_w_down, i, 1, *ln(i, 2), tm, tiles, tail, split_out=(i == DEPTH - 1))

    y_prompt = x[0].reshape(bp, seq, D_MODEL)
    y_sample = x[1].reshape(bs, sample_seq, D_MODEL)
    return (y_prompt, y_sample, jnp.stack(swa_k_prompt), jnp.stack(swa_v_prompt), swa_sample[0],
            swa_sample[1], ret_prompt, ret_sample, mem_k_prompt, mem_v_prompt)
```

```python
import functools

import jax
import jax.numpy as jnp
from jax import lax
from jax.experimental import pallas as pl
from jax.experimental.pallas import tpu as pltpu

F32 = jnp.float32
BF16 = jnp.bfloat16

D_MODEL = 2048
DEPTH = 4
PAST_LEN = 16384
HEAD_DIM = 128
MEM_LEN = 256
MEM_HEADS = 4
MEM_DIM = MEM_HEADS * HEAD_DIM
SELF_WIDTH = D_MODEL - MEM_DIM
SWA_Q_HEADS = SELF_WIDTH // HEAD_DIM
SWA_KV_HEADS = SWA_Q_HEADS // 3
SWA_GROUP = SWA_Q_HEADS // SWA_KV_HEADS
WINDOW = 128
ROPE_THETA = 500000.0
ROPE_DIM = HEAD_DIM // 4
RET_DK = 128
RET_DV = 2 * RET_DK
RET_HEADS = SELF_WIDTH // RET_DV
RET_CHUNK = 128
RET_ROT_BASE = 10000.0
D_FF = ((8 * D_MODEL // 3 + 255) // 256) * 256
LN_EPS = 1e-5
HEAD_NORM_EPS = 1e-6
ALPHA = (2.0 * DEPTH) ** 0.25
NEG_INF = -1e30
ATT_SCALE = HEAD_DIM ** -0.5
RET_K_SCALE = RET_DK ** -0.5

SWA_Q = SWA_Q_HEADS * HEAD_DIM
SWA_KV = SWA_KV_HEADS * HEAD_DIM
SWA_IN_WIDTH = SWA_Q + 2 * SWA_KV + MEM_DIM
RET_QK = RET_HEADS * RET_DK
RET_V = RET_HEADS * RET_DV
RET_IN_WIDTH = 2 * RET_QK + 2 * RET_V + MEM_DIM

VMEM_LIMIT_BYTES = 58 * 1024 * 1024
LANES = 128
FFN_TILE_K = 256
FFN_MAX_ROWS = 1040
PROJ_MAX_ROWS = 1040
ATT_BLOCKS_PER_STEP = 2
PROJ_TILE_N = 1024
OUT_TILE_N = 512
MAX_ROW_CHUNK = 256
BF16_SUBLANES = 16


def _row_tiles(rows, max_rows):
    tiles = -(-rows // max_rows)
    per_tile = -(-rows // tiles)
    return -(-per_tile // BF16_SUBLANES) * BF16_SUBLANES, tiles


def _row_chunks(tm):
    count = -(-tm // MAX_ROW_CHUNK)
    size = -(-(-(-tm // count)) // BF16_SUBLANES) * BF16_SUBLANES
    return [slice(r, min(r + size, tm)) for r in range(0, tm, size)]


def _params(*sem):
    return pltpu.CompilerParams(dimension_semantics=sem, vmem_limit_bytes=VMEM_LIMIT_BYTES)


def _layer_norm_rows(z, g, b):
    mu = jnp.mean(z, axis=-1, keepdims=True)
    zc = z - mu
    var = jnp.mean(zc * zc, axis=-1, keepdims=True)
    return zc * lax.rsqrt(var + LN_EPS) * g + b


def _ffn_kernel(*refs, nk, tk, tiles, tail, split_in, split_out):
    refs = list(refs)
    x_ref = refs.pop(0)
    xs_ref = refs.pop(0) if split_in else None
    wg_ref, wu_ref, wd_ref, g_ref, b_ref, o_ref = refs[:6]
    os_ref = refs[6] if split_out else None
    xb_ref, wgu_ref = refs[-2:]
    i = pl.program_id(0)
    k = pl.program_id(1)
    tm = o_ref.shape[0]
    used_last = min(tm, -(-(tail[0] + tail[1]) // BF16_SUBLANES) * BF16_SUBLANES)

    def load_x(rows):
        xv = x_ref[rows, :]
        if not split_in or rows.stop <= tail[0]:
            return xv
        first, count = tail
        lo, hi = rows.start, rows.stop
        pieces = []
        if lo < first:
            pieces.append(xv[:first - lo])
        s0, s1 = max(lo, first), min(hi, first + count)
        if s0 < s1:
            pieces.append(xs_ref[s0 - first:s1 - first, :])
        if hi > first + count:
            pieces.append(jnp.zeros((hi - max(lo, first + count), xv.shape[1]), xv.dtype))
        return jnp.where(i == tiles - 1, jnp.concatenate(pieces, axis=0), xv)

    def cast_weights():
        wgu_ref[:, :tk] = wg_ref[...].astype(BF16)
        wgu_ref[:, tk:] = wu_ref[...].astype(BF16)
        return wd_ref[...].astype(BF16)

    def down(xb, wd):
        gu = jnp.dot(xb, wgu_ref[...], preferred_element_type=F32)
        h = (jax.nn.silu(gu[:, :tk]) * gu[:, tk:]).astype(BF16)
        return jnp.dot(h, wd, preferred_element_type=F32)

    def steps(used):
        chunks = _row_chunks(used)

        @pl.when(k == 0)
        def _():
            wd = cast_weights()
            for rows in chunks:
                xb = load_x(rows).astype(BF16)
                xb_ref[rows, :] = xb
                o_ref[rows, :] = down(xb, wd)

        @pl.when(jnp.logical_and(k > 0, k < nk - 1))
        def _():
            wd = cast_weights()
            o_ref[:used, :] += down(xb_ref[:used, :], wd)

        @pl.when(k == nk - 1)
        def _():
            wd = cast_weights()
            for rows in chunks:
                acc = o_ref[rows, :] + down(xb_ref[rows, :], wd)
                z = ALPHA * load_x(rows) + 0.5 * acc
                o_ref[rows, :] = _layer_norm_rows(z, g_ref[...], b_ref[...])
            if used < tm:
                o_ref[used:, :] = jnp.zeros((tm - used, o_ref.shape[1]), o_ref.dtype)

    if used_last == tm:
        steps(tm)
    else:
        pl.when(i < tiles - 1)(lambda: steps(tm))
        pl.when(i == tiles - 1)(lambda: steps(used_last))

    if split_out:
        @pl.when(jnp.logical_and(i == tiles - 1, k == nk - 1))
        def _():
            os_ref[...] = o_ref[tail[0]:tail[0] + tail[1], :]


def _ffn(x, w_gu, w_down, layer, slot, g, b, tm, tiles, tail, x_sample=None, split_out=False):
    tk = FFN_TILE_K
    nk = D_FF // tk
    assert nk >= 2
    split_in = x_sample is not None
    rows = pl.BlockSpec((tm, D_MODEL), lambda i, k: (i, 0))
    small = pl.BlockSpec((tail[1], D_MODEL), lambda i, k: (0, 0))
    in_specs = [pl.BlockSpec((tm, D_MODEL), lambda i, k: (i, 0), pipeline_mode=pl.Buffered(1))]
    args = [x]
    if split_in:
        in_specs.append(small)
        args.append(x_sample)
    in_specs += [
        pl.BlockSpec((None, None, D_MODEL, tk), lambda i, k: (layer, slot, 0, k)),
        pl.BlockSpec((None, None, D_MODEL, tk), lambda i, k: (layer, slot, 0, nk + k)),
        pl.BlockSpec((None, None, tk, D_MODEL), lambda i, k: (layer, slot, k, 0)),
        pl.BlockSpec((1, D_MODEL), lambda i, k: (0, 0)),
        pl.BlockSpec((1, D_MODEL), lambda i, k: (0, 0)),
    ]
    if split_out:
        prompt_rows = (tiles - 1) * tm + tail[0]
        out_specs = [rows, small]
        out_shape = [jax.ShapeDtypeStruct((prompt_rows, D_MODEL), F32),
                     jax.ShapeDtypeStruct((tail[1], D_MODEL), F32)]
    else:
        out_specs = rows
        out_shape = jax.ShapeDtypeStruct((tiles * tm, D_MODEL), F32)
    return pl.pallas_call(
        functools.partial(_ffn_kernel, nk=nk, tk=tk, tiles=tiles, tail=tail,
                          split_in=split_in, split_out=split_out),
        grid=(tiles, nk),
        in_specs=in_specs,
        out_specs=out_specs,
        out_shape=out_shape,
        scratch_shapes=[pltpu.VMEM((tm, D_MODEL), BF16), pltpu.VMEM((D_MODEL, 2 * tk), BF16)],
        compiler_params=_params("arbitrary", "arbitrary"),
        name="ffn_ln",
    )(*args, w_gu, w_gu, w_down, g, b)


def _proj_kernel(x_ref, w_ref, *rest, rot_heads, shifts):
    if rot_heads:
        c_ref, a_ref, b_ref, o_ref, xb_ref = rest
    else:
        o_ref, xb_ref = rest
    j = pl.program_id(1)

    @pl.when(j == 0)
    def _():
        xb_ref[...] = x_ref[...].astype(BF16)

    def plain():
        o_ref[...] = jnp.dot(xb_ref[...], w_ref[...].astype(BF16), preferred_element_type=F32)

    if not rot_heads:
        plain()
        return

    tm, tn = o_ref.shape
    heads = tn // LANES
    full_tiles, part_heads = divmod(rot_heads, heads)

    def rotated(n_heads):
        wb = w_ref[...].astype(BF16)
        for rows in _row_chunks(tm):
            y = jnp.dot(xb_ref[rows, :], wb, preferred_element_type=F32)
            c, a, b = c_ref[rows, :], a_ref[rows, :], b_ref[rows, :]
            for h in range(heads):
                yh = y[:, h * LANES:(h + 1) * LANES]
                if h < n_heads:
                    yh = yh * c + pltpu.roll(yh, shifts[0], 1) * a + pltpu.roll(yh, shifts[1], 1) * b
                o_ref[rows, h * LANES:(h + 1) * LANES] = yh

    pl.when(j < full_tiles)(lambda: rotated(heads))
    if part_heads:
        pl.when(j == full_tiles)(lambda: rotated(part_heads))
    pl.when(j >= full_tiles + (1 if part_heads else 0))(plain)


def _proj(x, w, layer, tm, rot_heads=0, tables=None, shifts=None):
    m = x.shape[0]
    n = w.shape[-1]
    tn = PROJ_TILE_N
    in_specs = [
        pl.BlockSpec((tm, D_MODEL), lambda i, j: (i, 0)),
        pl.BlockSpec((None, D_MODEL, tn), lambda i, j: (layer, 0, j)),
    ]
    args = [x, w]
    if rot_heads:
        tab = pl.BlockSpec((tm, LANES), lambda i, j: (i, 0))
        in_specs += [tab, tab, tab]
        args += list(tables)
    return pl.pallas_call(
        functools.partial(_proj_kernel, rot_heads=rot_heads, shifts=shifts),
        grid=(pl.cdiv(m, tm), n // tn),
        in_specs=in_specs,
        out_specs=pl.BlockSpec((tm, tn), lambda i, j: (i, j)),
        out_shape=jax.ShapeDtypeStruct((m, n), F32),
        scratch_shapes=[pltpu.VMEM((tm, D_MODEL), BF16)],
        compiler_params=_params("parallel", "arbitrary"),
        name="proj",
    )(*args)


def _mem_kv_kernel(x_ref, w_ref, o_ref, xb_ref):
    @pl.when(pl.program_id(0) == 0)
    def _():
        xb_ref[...] = x_ref[...].astype(BF16)

    o_ref[...] = jnp.dot(xb_ref[...], w_ref[...].astype(BF16), preferred_element_type=F32)


def _mem_kv(mem, w):
    rows = mem.shape[0]
    layers, _, n = w.shape
    return pl.pallas_call(
        _mem_kv_kernel,
        grid=(layers,),
        in_specs=[
            pl.BlockSpec((rows, D_MODEL), lambda l: (0, 0)),
            pl.BlockSpec((None, D_MODEL, n), lambda l: (l, 0, 0)),
        ],
        out_specs=pl.BlockSpec((None, rows, n), lambda l: (l, 0, 0)),
        out_shape=jax.ShapeDtypeStruct((layers, rows, n), F32),
        scratch_shapes=[pltpu.VMEM((rows, D_MODEL), BF16)],
        compiler_params=_params("arbitrary"),
        name="mem_kv",
    )(mem, w)


def _out_ln_kernel(att_ref, w_ref, x_ref, g_ref, b_ref, o_ref, y_ref, *, nn, tn):
    n = pl.program_id(1)
    y_ref[n] = jnp.dot(att_ref[...].astype(BF16), w_ref[...].astype(BF16),
                       preferred_element_type=F32)

    @pl.when(n == nn - 1)
    def _():
        cols = [slice(c * tn, (c + 1) * tn) for c in range(nn)]
        total = None
        for c in range(nn):
            z = ALPHA * x_ref[:, cols[c]] + y_ref[c]
            y_ref[c] = z
            part = jnp.sum(z, axis=-1, keepdims=True)
            total = part if total is None else total + part
        mu = total * (1.0 / D_MODEL)
        total = None
        for c in range(nn):
            zc = y_ref[c] - mu
            part = jnp.sum(zc * zc, axis=-1, keepdims=True)
            total = part if total is None else total + part
        rstd = lax.rsqrt(total * (1.0 / D_MODEL) + LN_EPS)
        for c in range(nn):
            o_ref[:, cols[c]] = (y_ref[c] - mu) * rstd * g_ref[:, cols[c]] + b_ref[:, cols[c]]


def _out_ln(att, w_out, layer, x, g, b, row0):
    tm = att.shape[0]
    assert row0 % tm == 0
    tn = OUT_TILE_N
    nn = D_MODEL // tn
    rows = pl.BlockSpec((tm, D_MODEL), lambda i, n: (row0 // tm, 0))
    return pl.pallas_call(
        functools.partial(_out_ln_kernel, nn=nn, tn=tn),
        grid=(1, nn),
        in_specs=[
            pl.BlockSpec((tm, D_MODEL), lambda i, n: (0, 0)),
            pl.BlockSpec((None, D_MODEL, tn), lambda i, n: (layer, 0, n)),
            rows,
            pl.BlockSpec((1, D_MODEL), lambda i, n: (0, 0)),
            pl.BlockSpec((1, D_MODEL), lambda i, n: (0, 0)),
        ],
        out_specs=rows,
        out_shape=jax.ShapeDtypeStruct(x.shape, F32),
        scratch_shapes=[pltpu.VMEM((nn, tm, tn), F32)],
        input_output_aliases={2: 0},
        compiler_params=_params("arbitrary", "arbitrary"),
        name="out_ln",
    )(att, w_out, x, g, b)


def _dot_nt(a, b):
    return lax.dot_general(a, b, (((1,), (1,)), ((), ())), preferred_element_type=F32)


def _out_proj_ln(first, att_ref, w_ref, wb_ref, x_ref, g_ref, b_ref, o_ref):
    @pl.when(first)
    def _():
        wb_ref[...] = w_ref[...].astype(BF16)

    y = jnp.dot(att_ref[...], wb_ref[...], preferred_element_type=F32)
    o_ref[...] = _layer_norm_rows(ALPHA * x_ref[...] + y, g_ref[...], b_ref[...])


def _normalised_pv(p, v, extra=None):
    den = jnp.dot(p, jnp.ones((p.shape[1], HEAD_DIM), p.dtype), preferred_element_type=F32)
    if extra is not None:
        den = den + extra
    return jnp.dot(p, v, preferred_element_type=F32) * (1.0 / den)


def _mem_attend_block(qm_ref, mk_ref, mv_ref, att_ref, col0):
    for h in range(MEM_HEADS):
        sl = slice(h * HEAD_DIM, (h + 1) * HEAD_DIM)
        q = qm_ref[:, sl].astype(BF16)
        s = _dot_nt(q, mk_ref[0, :, sl].astype(BF16)) * ATT_SCALE
        p = jnp.exp(s - jnp.max(s, axis=-1, keepdims=True))
        p = p * (1.0 / jnp.sum(p, axis=-1, keepdims=True))
        o = jnp.dot(p.astype(BF16), mv_ref[0, :, sl].astype(BF16), preferred_element_type=F32)
        att_ref[:, col0 + h * HEAD_DIM:col0 + (h + 1) * HEAD_DIM] = o.astype(att_ref.dtype)


def _swa_prompt_kernel(sink_ref, q_ref, kc_ref, kp_ref, vc_ref, vp_ref, qm_ref, mk_ref, mv_ref,
                       w_ref, x_ref, g_ref, b_ref, o_ref, kt_ref, vt_ref, s_ref, p_ref, att_ref, wb_ref,
                       *, nblk, steps):
    n = pl.program_id(1)
    nq = SWA_Q_HEADS
    w2 = 2 * WINDOW

    def prev_cur(cur_ref, prev_ref, i, sl):
        rows = slice(i * WINDOW, (i + 1) * WINDOW)
        before = prev_ref[:, sl] if i == 0 else cur_ref[(i - 1) * WINDOW:i * WINDOW, sl]
        return jnp.concatenate([before, cur_ref[rows, sl]], axis=0).astype(BF16)

    for i in range(nblk):
        rows = slice(i * WINDOW, (i + 1) * WINDOW)
        for h in range(SWA_KV_HEADS):
            sl = slice(h * HEAD_DIM, (h + 1) * HEAD_DIM)
            q3 = jnp.concatenate(
                [q_ref[rows, (h * SWA_GROUP + g) * HEAD_DIM:(h * SWA_GROUP + g + 1) * HEAD_DIM]
                 for g in range(SWA_GROUP)], axis=0).astype(BF16)
            s = _dot_nt(q3, prev_cur(kc_ref, kp_ref, i, sl)) * ATT_SCALE
            s_ref[i, h * SWA_GROUP:(h + 1) * SWA_GROUP] = s.reshape(SWA_GROUP, WINDOW, w2)
    for h in range(MEM_HEADS):
        sl = slice(h * HEAD_DIM, (h + 1) * HEAD_DIM)
        sm = _dot_nt(qm_ref[:, sl].astype(BF16), mk_ref[0, :, sl].astype(BF16)) * ATT_SCALE
        s_ref[:, nq + h] = sm.reshape(nblk, WINDOW, w2)

    qi = lax.broadcasted_iota(jnp.int32, (WINDOW, w2), 0)
    kj = lax.broadcasted_iota(jnp.int32, (WINDOW, w2), 1)
    sink = sink_ref[...]
    sink_terms = []
    for i in range(nblk):
        first_key = jnp.where(n > 0, qi, WINDOW) if i == 0 else qi
        ok = (kj >= first_key) & (kj <= qi + WINDOW)
        s = jnp.where(ok[None], s_ref[i, 0:nq], NEG_INF)
        m = jnp.maximum(jnp.max(s, axis=-1, keepdims=True), sink)
        p_ref[i, 0:nq] = jnp.exp(s - m).astype(BF16)
        sink_terms.append(jnp.exp(sink - m))
    s = s_ref[:, nq:nq + MEM_HEADS]
    p_ref[:, nq:nq + MEM_HEADS] = jnp.exp(s - jnp.max(s, axis=-1, keepdims=True)).astype(BF16)

    for i in range(nblk):
        rows = slice(i * WINDOW, (i + 1) * WINDOW)
        for h in range(SWA_KV_HEADS):
            sl = slice(h * HEAD_DIM, (h + 1) * HEAD_DIM)
            group = slice(h * SWA_GROUP, (h + 1) * SWA_GROUP)
            p3 = p_ref[i, group].reshape(SWA_GROUP * WINDOW, w2)
            o = _normalised_pv(p3, prev_cur(vc_ref, vp_ref, i, sl),
                               sink_terms[i][group].reshape(SWA_GROUP * WINDOW, 1))
            for g in range(SWA_GROUP):
                hq = h * SWA_GROUP + g
                att_ref[rows, hq * HEAD_DIM:(hq + 1) * HEAD_DIM] = (
                    o[g * WINDOW:(g + 1) * WINDOW].astype(BF16))
    for h in range(MEM_HEADS):
        sl = slice(h * HEAD_DIM, (h + 1) * HEAD_DIM)
        pm = p_ref[:, nq + h].reshape(nblk * WINDOW, w2)
        om = _normalised_pv(pm, mv_ref[0, :, sl].astype(BF16))
        att_ref[:, SWA_Q + h * HEAD_DIM:SWA_Q + (h + 1) * HEAD_DIM] = om.astype(BF16)

    @pl.when(n == steps - 1)
    def _():
        last = slice((nblk - 1) * WINDOW, nblk * WINDOW)
        kt_ref[0] = kc_ref[last, :]
        vt_ref[0] = vc_ref[last, :]

    first = jnp.logical_and(pl.program_id(0) == 0, n == 0)
    _out_proj_ln(first, att_ref, w_ref, wb_ref, x_ref, g_ref, b_ref, o_ref)


def _mixer_tail_specs(layer, row_of, rows):
    mem = (None, 1, MEM_LEN, MEM_DIM)
    vec = pl.BlockSpec((1, D_MODEL), lambda b, n: (0, 0))
    mixer = layer // 2
    return [
        pl.BlockSpec(mem, lambda b, n: (layer, b, 0, 0)),
        pl.BlockSpec(mem, lambda b, n: (layer, b, 0, 1)),
        pl.BlockSpec((None, D_MODEL, D_MODEL), lambda b, n: (mixer, 0, 0), pipeline_mode=pl.Buffered(1)),
        pl.BlockSpec((rows, D_MODEL), lambda b, n: (row_of(b, n), 0)),
        vec, vec,
    ]


def _swa_prompt(qkv, bsz, seq, mem_kv, layer, sinks, w_out, x, g, b):
    nblk = ATT_BLOCKS_PER_STEP
    rows = nblk * WINDOW
    steps = seq // rows
    assert seq % rows == 0 and seq >= WINDOW
    tail = pl.BlockSpec((1, WINDOW, SWA_KV), lambda b, n: (b, 0, 0))
    blocks_per_seq = seq // WINDOW
    kcol = SWA_Q // SWA_KV
    vcol = kcol + 1
    mcol = vcol + 1
    row_of = lambda b, n: b * steps + n
    prev_of = lambda b, n: b * blocks_per_seq + jnp.maximum(n * nblk - 1, 0)
    cur = lambda w, col: pl.BlockSpec((rows, w), lambda b, n: (row_of(b, n), col))
    prev = lambda col: pl.BlockSpec((WINDOW, SWA_KV), lambda b, n: (prev_of(b, n), col))
    nheads = SWA_Q_HEADS + MEM_HEADS
    assert MEM_LEN == 2 * WINDOW
    in_specs = [
        pl.BlockSpec((SWA_Q_HEADS, 1, 1), lambda b, n: (0, 0, 0)),
        cur(SWA_Q, 0), cur(SWA_KV, kcol), prev(kcol), cur(SWA_KV, vcol), prev(vcol), cur(MEM_DIM, mcol),
    ] + _mixer_tail_specs(layer, row_of, rows)
    return pl.pallas_call(
        functools.partial(_swa_prompt_kernel, nblk=nblk, steps=steps),
        grid=(bsz, steps),
        in_specs=in_specs,
        out_specs=[pl.BlockSpec((rows, D_MODEL), lambda b, n: (row_of(b, n), 0)), tail, tail],
        out_shape=[jax.ShapeDtypeStruct(x.shape, F32),
                   jax.ShapeDtypeStruct((bsz, WINDOW, SWA_KV), F32),
                   jax.ShapeDtypeStruct((bsz, WINDOW, SWA_KV), F32)],
        scratch_shapes=[pltpu.VMEM((nblk, nheads, WINDOW, 2 * WINDOW), F32),
                        pltpu.VMEM((nblk, nheads, WINDOW, 2 * WINDOW), BF16),
                        pltpu.VMEM((rows, D_MODEL), BF16),
                        pltpu.VMEM((D_MODEL, D_MODEL), BF16)],
        input_output_aliases={len(in_specs) - 3: 0},
        compiler_params=_params("arbitrary", "arbitrary"),
        name="swa_prompt",
    )(sinks.reshape(SWA_Q_HEADS, 1, 1), qkv, qkv, qkv, qkv, qkv, qkv, mem_kv, mem_kv, w_out, x, g, b)


def _head_norm_gate(o, gate):
    mu = jnp.mean(o, axis=-1, keepdims=True)
    oc = o - mu
    var = jnp.mean(oc * oc, axis=-1, keepdims=True)
    return jax.nn.silu(gate) * (oc * lax.rsqrt(var + HEAD_NORM_EPS))


def _ret_prompt_kernel(cdec_ref, decay_ref, qdec_ref, kdec_ref, q_ref, k_ref, v_ref, gate_ref, qm_ref,
                       mk_ref, mv_ref, w_ref, x_ref, g_ref, b_ref, o_ref, s_out_ref,
                       state_ref, att_ref, wb_ref, *, steps, nblk):
    c = pl.program_id(1)

    @pl.when(c == 0)
    def _():
        state_ref[...] = jnp.zeros_like(state_ref)

    for i in range(nblk):
        rows = slice(i * RET_CHUNK, (i + 1) * RET_CHUNK)
        for h in range(RET_HEADS):
            ksl = slice(h * RET_DK, (h + 1) * RET_DK)
            vsl = slice(h * RET_DV, (h + 1) * RET_DV)
            qc = q_ref[rows, ksl]
            kc = k_ref[rows, ksl] * RET_K_SCALE
            vb = v_ref[rows, vsl].astype(BF16)
            st = state_ref[h]
            inner = _dot_nt(qc.astype(BF16), kc.astype(BF16)) * decay_ref[h]
            o = (jnp.dot(inner.astype(BF16), vb, preferred_element_type=F32)
                 + jnp.dot((qc * qdec_ref[h]).astype(BF16), st.astype(BF16),
                           preferred_element_type=F32))
            kd = (kc * kdec_ref[h]).astype(BF16)
            state_ref[h] = cdec_ref[h] * st + lax.dot_general(
                kd, vb, (((0,), (0,)), ((), ())), preferred_element_type=F32)
            att_ref[rows, vsl] = _head_norm_gate(o, gate_ref[rows, vsl]).astype(BF16)
    _mem_attend_block(qm_ref, mk_ref, mv_ref, att_ref, RET_V)

    @pl.when(c == steps - 1)
    def _():
        s_out_ref[0] = state_ref[...]

    first = jnp.logical_and(pl.program_id(0) == 0, c == 0)
    _out_proj_ln(first, att_ref, w_ref, wb_ref, x_ref, g_ref, b_ref, o_ref)


def _drop_carried(kernel, first, count):
    def body(*refs):
        return kernel(*refs[:first], *refs[first + count:])
    return body


def _carry(prev_outputs):
    prev_outputs = list(prev_outputs)
    return [pl.BlockSpec(memory_space=pl.ANY)] * len(prev_outputs), prev_outputs


def _ret_prompt(qkvg, bsz, seq, mem_kv, layer, tables, w_out, x, g, b, carried):
    nblk = ATT_BLOCKS_PER_STEP
    rows = nblk * RET_CHUNK
    steps = seq // rows
    mixer = layer // 2
    decay, qdec, kdec, cdec = tables
    row_of = lambda b, n: b * steps + n
    cur = lambda w, col: pl.BlockSpec((rows, w), lambda b, n: (row_of(b, n), col))
    tab = pl.BlockSpec((RET_HEADS, RET_CHUNK, RET_CHUNK), lambda b, n: (0, 0, 0))
    in_specs = [
        pl.BlockSpec(memory_space=pltpu.SMEM), tab, tab, tab,
        cur(RET_QK, 0), cur(RET_QK, 1), cur(RET_V, 1), cur(RET_V, 2),
        cur(MEM_DIM, (2 * RET_QK + 2 * RET_V) // MEM_DIM),
    ] + _mixer_tail_specs(layer, row_of, rows)
    n_in = len(in_specs)
    carry_specs, carry_args = _carry(carried)
    aliases = {n_in - 3: 0}
    aliases.update({n_in + c: 1 + c for c in range(len(carry_args))})
    return pl.pallas_call(
        _drop_carried(functools.partial(_ret_prompt_kernel, steps=steps, nblk=nblk), n_in, len(carry_args)),
        grid=(bsz, steps),
        in_specs=in_specs + carry_specs,
        out_specs=[
            pl.BlockSpec((rows, D_MODEL), lambda b, n: (row_of(b, n), 0)),
            pl.BlockSpec((None, 1, RET_HEADS, RET_DK, RET_DV), lambda b, n: (mixer, b, 0, 0, 0)),
        ],
        out_shape=[
            jax.ShapeDtypeStruct(x.shape, F32),
            jax.ShapeDtypeStruct((DEPTH // 2, bsz, RET_HEADS, RET_DK, RET_DV), F32),
        ],
        scratch_shapes=[pltpu.VMEM((RET_HEADS, RET_DK, RET_DV), F32),
                        pltpu.VMEM((rows, D_MODEL), BF16),
                        pltpu.VMEM((D_MODEL, D_MODEL), BF16)],
        input_output_aliases=aliases,
        compiler_params=_params("arbitrary", "arbitrary"),
        name="ret_prompt",
    )(cdec, decay, qdec, kdec, qkvg, qkvg, qkvg, qkvg, qkvg, mem_kv, mem_kv, w_out, x, g, b, *carry_args)


def _mem_attend_row(q, mk_ref, mv_ref):
    s = jnp.sum(mk_ref[0] * q[None], axis=-1, keepdims=True) * ATT_SCALE
    p = jnp.exp(s - jnp.max(s, axis=0, keepdims=True))
    p = p * (1.0 / jnp.sum(p, axis=0, keepdims=True))
    return jnp.sum(p * mv_ref[0], axis=0)


def _swa_step_kernel(sink_ref, rows_ref, kbuf_ref, vbuf_ref, mk_ref, mv_ref, o_ref, nk_ref, nv_ref):
    krow0 = SWA_Q_HEADS
    vrow0 = krow0 + SWA_KV_HEADS
    mrow0 = vrow0 + SWA_KV_HEADS
    wb = kbuf_ref.shape[1]
    kb = kbuf_ref[0]
    vb = vbuf_ref[0]
    k_new = rows_ref[0, krow0:krow0 + SWA_KV_HEADS, :]
    v_new = rows_ref[0, vrow0:vrow0 + SWA_KV_HEADS, :]
    for g in range(SWA_GROUP):
        group_rows = pl.ds(g, SWA_KV_HEADS, stride=SWA_GROUP)
        q = rows_ref[0, group_rows, :]
        s_buf = jnp.sum(kb * q[None], axis=-1, keepdims=True) * ATT_SCALE
        s_new = jnp.sum(k_new * q, axis=-1, keepdims=True) * ATT_SCALE
        sink = sink_ref[g]
        m = jnp.maximum(jnp.maximum(jnp.max(s_buf, axis=0), s_new), sink)
        p_buf = jnp.exp(s_buf - m[None])
        p_new = jnp.exp(s_new - m)
        inv = 1.0 / (jnp.sum(p_buf, axis=0) + p_new + jnp.exp(sink - m))
        o_ref[0, group_rows, :] = jnp.sum((p_buf * inv[None]) * vb, axis=0) + (p_new * inv) * v_new
    nk_ref[0, 0:wb - 1] = kbuf_ref[0, 1:wb]
    nv_ref[0, 0:wb - 1] = vbuf_ref[0, 1:wb]
    nk_ref[0, wb - 1] = k_new
    nv_ref[0, wb - 1] = v_new
    o_ref[0, SWA_Q_HEADS:SWA_Q_HEADS + MEM_HEADS, :] = _mem_attend_row(
        rows_ref[0, mrow0:mrow0 + MEM_HEADS, :], mk_ref, mv_ref)


def _swa_step(rows, cache_k, cache_v, j, mem_k, mem_v, i, sinks, carried):
    bsz = rows.shape[0]
    wb = cache_k.shape[2]
    cache = pl.BlockSpec((None, 1, wb, SWA_KV_HEADS, HEAD_DIM), lambda b: (j, b, 0, 0, 0))
    mem = pl.BlockSpec((None, 1, MEM_LEN, MEM_HEADS, HEAD_DIM), lambda b: (i, b, 0, 0, 0))
    nrows = D_MODEL // HEAD_DIM
    sink_gk = sinks.reshape(SWA_KV_HEADS, SWA_GROUP).T.reshape(SWA_GROUP, SWA_KV_HEADS, 1)
    in_specs = [
        pl.BlockSpec(sink_gk.shape, lambda b: (0, 0, 0)),
        pl.BlockSpec((1,) + rows.shape[1:], lambda b: (b, 0, 0)),
        cache, cache, mem, mem,
    ]
    n_in = len(in_specs)
    carry_specs, carry_args = _carry(carried)
    return pl.pallas_call(
        _drop_carried(_swa_step_kernel, n_in, len(carry_args)),
        grid=(bsz,),
        in_specs=in_specs + carry_specs,
        out_specs=[pl.BlockSpec((1, nrows, HEAD_DIM), lambda b: (b, 0, 0)), cache, cache],
        out_shape=[
            jax.ShapeDtypeStruct((bsz, nrows, HEAD_DIM), F32),
            jax.ShapeDtypeStruct(cache_k.shape, F32),
            jax.ShapeDtypeStruct(cache_v.shape, F32),
        ],
        input_output_aliases={n_in + c: 1 + c for c in range(len(carry_args))},
        compiler_params=_params("parallel"),
        name="swa_step",
    )(sink_gk, rows, cache_k, cache_v, mem_k, mem_v, *carry_args)


def _ret_step_kernel(dec_ref, rows_ref, cols_ref, s_ref, mk_ref, mv_ref, o_ref, s_out_ref):
    krow0 = RET_HEADS
    vrow0 = 2 * RET_HEADS
    grow0 = vrow0 + 2 * RET_HEADS
    mrow0 = grow0 + 2 * RET_HEADS
    for h in range(RET_HEADS):
        q_row = rows_ref[0, h:h + 1, :]
        k_row = rows_ref[0, krow0 + h:krow0 + h + 1, :] * RET_K_SCALE
        q_col = cols_ref[0, :, h:h + 1]
        k_col = cols_ref[0, :, krow0 + h:krow0 + h + 1] * RET_K_SCALE
        inner = jnp.sum(q_row * k_row, axis=1, keepdims=True) * dec_ref[0, h]
        qd = q_col * dec_ref[1, h]
        kd = k_col * dec_ref[2, h]
        halves = []
        for t in range(2):
            lsl = slice(t * HEAD_DIM, (t + 1) * HEAD_DIM)
            v = rows_ref[0, vrow0 + 2 * h + t:vrow0 + 2 * h + t + 1, :]
            st = s_ref[0, h, :, lsl]
            halves.append(inner * v + jnp.sum(qd * st, axis=0, keepdims=True))
            s_out_ref[0, h, :, lsl] = dec_ref[3, h] * st + kd * v
        mu = (jnp.sum(halves[0], axis=1, keepdims=True)
              + jnp.sum(halves[1], axis=1, keepdims=True)) * (1.0 / RET_DV)
        cen = [o - mu for o in halves]
        var = (jnp.sum(cen[0] * cen[0], axis=1, keepdims=True)
               + jnp.sum(cen[1] * cen[1], axis=1, keepdims=True)) * (1.0 / RET_DV)
        rstd = lax.rsqrt(var + HEAD_NORM_EPS)
        for t in range(2):
            gate = rows_ref[0, grow0 + 2 * h + t:grow0 + 2 * h + t + 1, :]
            o_ref[0, 2 * h + t:2 * h + t + 1, :] = jax.nn.silu(gate) * (cen[t] * rstd)
    o_ref[0, 2 * RET_HEADS:2 * RET_HEADS + MEM_HEADS, :] = _mem_attend_row(
        rows_ref[0, mrow0:mrow0 + MEM_HEADS, :], mk_ref, mv_ref)


def _ret_step(rows, cols, state, j, mem_k, mem_v, i, dec, carried):
    bsz = rows.shape[0]
    st = pl.BlockSpec((None, 1, RET_HEADS, RET_DK, RET_DV), lambda b: (j, b, 0, 0, 0))
    mem = pl.BlockSpec((None, 1, MEM_LEN, MEM_HEADS, HEAD_DIM), lambda b: (i, b, 0, 0, 0))
    nrows = D_MODEL // HEAD_DIM
    in_specs = [
        pl.BlockSpec(memory_space=pltpu.SMEM),
        pl.BlockSpec((1,) + rows.shape[1:], lambda b: (b, 0, 0)),
        pl.BlockSpec((1,) + cols.shape[1:], lambda b: (b, 0, 0)),
        st, mem, mem,
    ]
    n_in = len(in_specs)
    carry_specs, carry_args = _carry(carried)
    return pl.pallas_call(
        _drop_carried(_ret_step_kernel, n_in, len(carry_args)),
        grid=(bsz,),
        in_specs=in_specs + carry_specs,
        out_specs=[pl.BlockSpec((1, nrows, HEAD_DIM), lambda b: (b, 0, 0)), st],
        out_shape=[
            jax.ShapeDtypeStruct((bsz, nrows, HEAD_DIM), F32),
            jax.ShapeDtypeStruct(state.shape, F32),
        ],
        input_output_aliases={n_in + c: 1 + c for c in range(len(carry_args))},
        compiler_params=_params("parallel"),
        name="ret_step",
    )(dec, rows, cols, state, mem_k, mem_v, *carry_args)


def _rope_tables(pos):
    half = ROPE_DIM // 2
    inv = ROPE_THETA ** (-jnp.arange(half, dtype=F32) / half)
    ang = pos.astype(F32)[:, None] * inv[None, :]
    cos, sin = jnp.cos(ang), jnp.sin(ang)
    n = pos.shape[0]
    rest = HEAD_DIM - ROPE_DIM
    c = jnp.concatenate([cos, cos, jnp.ones((n, rest), F32)], axis=-1)
    a = jnp.concatenate([-sin, jnp.zeros((n, HEAD_DIM - half), F32)], axis=-1)
    b = jnp.concatenate([jnp.zeros((n, half), F32), sin, jnp.zeros((n, rest), F32)], axis=-1)
    return (c, a, b), (HEAD_DIM - half, half)


def _ret_rot_tables(pos):
    half = RET_DK // 2
    angle = RET_ROT_BASE ** (-jnp.linspace(0.0, 1.0, half, dtype=F32))
    ang = pos.astype(F32)[:, None] * angle[None, :]
    cos, sin = jnp.cos(ang), jnp.sin(ang)
    n = pos.shape[0]
    zero = jnp.zeros_like(sin)
    c = jnp.stack([cos, cos], axis=-1).reshape(n, RET_DK)
    a = jnp.stack([-sin, zero], axis=-1).reshape(n, RET_DK)
    b = jnp.stack([zero, sin], axis=-1).reshape(n, RET_DK)
    return (c, a, b), (RET_DK - 1, 1)


def _ret_decay(chunk):
    log_g = jnp.log1p(-jnp.exp2(-5.0 - jnp.arange(RET_HEADS, dtype=F32)))
    n = jnp.arange(chunk, dtype=F32)
    rel = n[:, None] - n[None, :]
    decay = jnp.where(rel >= 0, jnp.exp(jnp.maximum(rel, 0.0) * log_g[:, None, None]), 0.0)
    q_dec = jnp.exp((n + 1.0) * log_g[:, None])
    k_dec = jnp.exp((chunk - 1.0 - n) * log_g[:, None])
    c_dec = jnp.exp(chunk * log_g)
    return decay, q_dec, k_dec, c_dec


def kernel(x_prompt, x_sample, cache_swa_k, cache_swa_v, state_ret, cache_mem_k, cache_mem_v,
           mem_prompt, ln_g, ln_b, ffn_w_gu, ffn_w_down, w_mem_kv, swa_w_in, swa_w_out,
           swa_sinks, ret_w_in, ret_w_out):
    bp, seq, _ = x_prompt.shape
    bs, sample_seq, _ = x_sample.shape
    assert sample_seq == 1
    mp, ms = bp * seq, bs
    assert mp % ms == 0

    mem2 = mem_prompt.reshape(bp * MEM_LEN, D_MODEL)
    mem_kv = _mem_kv(mem2, w_mem_kv).reshape(DEPTH, bp, MEM_LEN, 2 * MEM_DIM)
    mem_k_prompt = mem_kv[..., :MEM_DIM].reshape(DEPTH, bp, MEM_LEN, MEM_HEADS, HEAD_DIM)
    mem_v_prompt = mem_kv[..., MEM_DIM:].reshape(DEPTH, bp, MEM_LEN, MEM_HEADS, HEAD_DIM)

    tm, tiles = _row_tiles(mp + ms, FFN_MAX_ROWS)
    m_all = tm * tiles
    tm_proj, _ = _row_tiles(m_all, PROJ_MAX_ROWS)
    pad = m_all - mp - ms
    tail = (mp - (tiles - 1) * tm, ms)
    assert tail[0] >= 0 and tail[0] % 8 == 0 and ms % 8 == 0 and tail[0] + ms <= tm
    def per_row(table):
        return jnp.concatenate([jnp.tile(table[:seq], (bp, 1)),
                                jnp.broadcast_to(table[seq:], (ms, table.shape[1])),
                                jnp.zeros((pad, table.shape[1]), F32)], axis=0)

    pos = jnp.concatenate([jnp.arange(seq, dtype=jnp.int32), jnp.full((1,), PAST_LEN, jnp.int32)])
    rope_tabs, rope_shifts = _rope_tables(pos)
    rot_tabs, rot_shifts = _ret_rot_tables(pos)
    rope_tabs = tuple(per_row(t) for t in rope_tabs)
    rot_tabs = tuple(per_row(t) for t in rot_tabs)
    decay, q_dec, k_dec, c_dec = _ret_decay(RET_CHUNK)
    ret_tabs = (decay, jnp.broadcast_to(q_dec[:, :, None], decay.shape),
                jnp.broadcast_to(k_dec[:, :, None], decay.shape), c_dec)
    decay, q_dec, k_dec, c_dec = _ret_decay(sample_seq)
    step_dec = jnp.stack([decay[:, 0, 0], q_dec[:, 0], k_dec[:, 0], c_dec])

    def ln(i, s):
        return ln_g[i, s].reshape(1, D_MODEL), ln_b[i, s].reshape(1, D_MODEL)

    swa_k_prompt, swa_v_prompt = [], []
    swa_sample = [jnp.zeros_like(cache_swa_k), jnp.zeros_like(cache_swa_v)]
    ret_prompt = jnp.zeros((DEPTH // 2, bp, RET_HEADS, RET_DK, RET_DV), F32)
    ret_sample = jnp.zeros_like(state_ret)
    for i in range(DEPTH):
        j = i // 2
        if i == 0:
            x = _ffn(x_prompt.reshape(mp, D_MODEL), ffn_w_gu, ffn_w_down, i, 0, *ln(i, 0), tm, tiles, tail,
                     x_sample=x_sample.reshape(ms, D_MODEL))
        else:
            x = _ffn(x, ffn_w_gu, ffn_w_down, i, 0, *ln(i, 0), tm, tiles, tail)
        if i % 2 == 0:
            qkv = _proj(x, swa_w_in, j, tm_proj, rot_heads=SWA_Q_HEADS + SWA_KV_HEADS, tables=rope_tabs,
                        shifts=rope_shifts)
            x, k_tail, v_tail = _swa_prompt(qkv, bp, seq, mem_kv, i, swa_sinks[j], swa_w_out, x, *ln(i, 1))
            swa_k_prompt.append(k_tail.reshape(bp, WINDOW, SWA_KV_HEADS, HEAD_DIM))
            swa_v_prompt.append(v_tail.reshape(bp, WINDOW, SWA_KV_HEADS, HEAD_DIM))
            rows = qkv[mp:mp + ms].reshape(ms, SWA_IN_WIDTH // HEAD_DIM, HEAD_DIM)
            att, *swa_sample = _swa_step(rows, cache_swa_k, cache_swa_v, j, cache_mem_k, cache_mem_v, i,
                                         swa_sinks[j], swa_sample)
            x = _out_ln(att.reshape(ms, D_MODEL), swa_w_out, j, x, *ln(i, 1), mp)
        else:
            qkvg = _proj(x, ret_w_in, j, tm_proj, rot_heads=2 * RET_HEADS, tables=rot_tabs,
                         shifts=rot_shifts)
            x, ret_prompt = _ret_prompt(qkvg, bp, seq, mem_kv, i, ret_tabs, ret_w_out, x, *ln(i, 1),
                                        [ret_prompt])
            rows = qkvg[mp:mp + ms].reshape(ms, RET_IN_WIDTH // HEAD_DIM, HEAD_DIM)
            att, ret_sample = _ret_step(rows, jnp.swapaxes(rows, 1, 2), state_ret, j, cache_mem_k,
                                        cache_mem_v, i, step_dec,
                                        [ret_sample])
            x = _out_ln(att.reshape(ms, D_MODEL), ret_w_out, j, x, *ln(i, 1), mp)
        x = _ffn(x, ffn_w_gu, ffn_w_down, i, 1, *ln(i, 2), tm, tiles, tail, split_out=(i == DEPTH - 1))

    y_prompt = x[0].reshape(bp, seq, D_MODEL)
    y_sample = x[1].reshape(bs, sample_seq, D_MODEL)
    return (y_prompt, y_sample, jnp.stack(swa_k_prompt), jnp.stack(swa_v_prompt), swa_sample[0],
            swa_sample[1], ret_prompt, ret_sample, mem_k_prompt, mem_v_prompt)
```

```python
import functools

import jax
import jax.numpy as jnp
from jax import lax
from jax.experimental import pallas as pl
from jax.experimental.pallas import tpu as pltpu

F32 = jnp.float32
BF16 = jnp.bfloat16

D_MODEL = 2048
DEPTH = 4
PAST_LEN = 16384
HEAD_DIM = 128
MEM_LEN = 256
MEM_HEADS = 4
MEM_DIM = MEM_HEADS * HEAD_DIM
SELF_WIDTH = D_MODEL - MEM_DIM
SWA_Q_HEADS = SELF_WIDTH // HEAD_DIM
SWA_KV_HEADS = SWA_Q_HEADS // 3
SWA_GROUP = SWA_Q_HEADS // SWA_KV_HEADS
WINDOW = 128
ROPE_THETA = 500000.0
ROPE_DIM = HEAD_DIM // 4
RET_DK = 128
RET_DV = 2 * RET_DK
RET_HEADS = SELF_WIDTH // RET_DV
RET_CHUNK = 128
RET_ROT_BASE = 10000.0
D_FF = ((8 * D_MODEL // 3 + 255) // 256) * 256
LN_EPS = 1e-5
HEAD_NORM_EPS = 1e-6
ALPHA = (2.0 * DEPTH) ** 0.25
NEG_INF = -1e30
ATT_SCALE = HEAD_DIM ** -0.5
RET_K_SCALE = RET_DK ** -0.5

SWA_Q = SWA_Q_HEADS * HEAD_DIM
SWA_KV = SWA_KV_HEADS * HEAD_DIM
SWA_IN_WIDTH = SWA_Q + 2 * SWA_KV + MEM_DIM
RET_QK = RET_HEADS * RET_DK
RET_V = RET_HEADS * RET_DV
RET_IN_WIDTH = 2 * RET_QK + 2 * RET_V + MEM_DIM

VMEM_LIMIT_BYTES = 58 * 1024 * 1024
LANES = 128
FFN_TILE_K = 256
FFN_MAX_ROWS = 1040
PROJ_MAX_ROWS = 1040
ATT_BLOCKS_PER_STEP = 2
PROJ_TILE_N = 1024
OUT_TILE_N = 512
MAX_ROW_CHUNK = 256
BF16_SUBLANES = 16


def _row_tiles(rows, max_rows):
    tiles = -(-rows // max_rows)
    per_tile = -(-rows // tiles)
    return -(-per_tile // BF16_SUBLANES) * BF16_SUBLANES, tiles


def _row_chunks(tm):
    count = -(-tm // MAX_ROW_CHUNK)
    size = -(-(-(-tm // count)) // BF16_SUBLANES) * BF16_SUBLANES
    return [slice(r, min(r + size, tm)) for r in range(0, tm, size)]


def _params(*sem):
    return pltpu.CompilerParams(dimension_semantics=sem, vmem_limit_bytes=VMEM_LIMIT_BYTES)


def _layer_norm_rows(z, g, b):
    mu = jnp.mean(z, axis=-1, keepdims=True)
    zc = z - mu
    var = jnp.mean(zc * zc, axis=-1, keepdims=True)
    return zc * lax.rsqrt(var + LN_EPS) * g + b


def _ffn_kernel(*refs, nk, tk, tiles, tail, split_in, split_out):
    refs = list(refs)
    x_ref = refs.pop(0)
    xs_ref = refs.pop(0) if split_in else None
    wg_ref, wu_ref, wd_ref, g_ref, b_ref, o_ref = refs[:6]
    os_ref = refs[6] if split_out else None
    xb_ref, wgu_ref = refs[-2:]
    i = pl.program_id(0)
    k = pl.program_id(1)
    chunks = _row_chunks(o_ref.shape[0])

    def load_x(rows):
        xv = x_ref[rows, :]
        if not split_in or rows.stop <= tail[0]:
            return xv
        first, count = tail
        lo, hi = rows.start, rows.stop
        pieces = []
        if lo < first:
            pieces.append(xv[:first - lo])
        s0, s1 = max(lo, first), min(hi, first + count)
        if s0 < s1:
            pieces.append(xs_ref[s0 - first:s1 - first, :])
        if hi > first + count:
            pieces.append(jnp.zeros((hi - max(lo, first + count), xv.shape[1]), xv.dtype))
        return jnp.where(i == tiles - 1, jnp.concatenate(pieces, axis=0), xv)

    def cast_weights():
        wgu_ref[:, :tk] = wg_ref[...].astype(BF16)
        wgu_ref[:, tk:] = wu_ref[...].astype(BF16)
        return wd_ref[...].astype(BF16)

    def down(xb, wd):
        gu = jnp.dot(xb, wgu_ref[...], preferred_element_type=F32)
        h = (jax.nn.silu(gu[:, :tk]) * gu[:, tk:]).astype(BF16)
        return jnp.dot(h, wd, preferred_element_type=F32)

    @pl.when(k == 0)
    def _():
        wd = cast_weights()
        for rows in chunks:
            xb = load_x(rows).astype(BF16)
            xb_ref[rows, :] = xb
            o_ref[rows, :] = down(xb, wd)

    @pl.when(jnp.logical_and(k > 0, k < nk - 1))
    def _():
        wd = cast_weights()
        o_ref[...] += down(xb_ref[...], wd)

    @pl.when(k == nk - 1)
    def _():
        wd = cast_weights()
        for rows in chunks:
            acc = o_ref[rows, :] + down(xb_ref[rows, :], wd)
            z = ALPHA * load_x(rows) + 0.5 * acc
            o_ref[rows, :] = _layer_norm_rows(z, g_ref[...], b_ref[...])
        if split_out:
            @pl.when(i == tiles - 1)
            def _():
                os_ref[...] = o_ref[tail[0]:tail[0] + tail[1], :]


def _ffn(x, w_gu, w_down, layer, slot, g, b, tm, tiles, tail, x_sample=None, split_out=False):
    tk = FFN_TILE_K
    nk = D_FF // tk
    assert nk >= 2
    split_in = x_sample is not None
    rows = pl.BlockSpec((tm, D_MODEL), lambda i, k: (i, 0))
    small = pl.BlockSpec((tail[1], D_MODEL), lambda i, k: (0, 0))
    in_specs = [pl.BlockSpec((tm, D_MODEL), lambda i, k: (i, 0), pipeline_mode=pl.Buffered(1))]
    args = [x]
    if split_in:
        in_specs.append(small)
        args.append(x_sample)
    in_specs += [
        pl.BlockSpec((None, None, D_MODEL, tk), lambda i, k: (layer, slot, 0, k)),
        pl.BlockSpec((None, None, D_MODEL, tk), lambda i, k: (layer, slot, 0, nk + k)),
        pl.BlockSpec((None, None, tk, D_MODEL), lambda i, k: (layer, slot, k, 0)),
        pl.BlockSpec((1, D_MODEL), lambda i, k: (0, 0)),
        pl.BlockSpec((1, D_MODEL), lambda i, k: (0, 0)),
    ]
    if split_out:
        prompt_rows = (tiles - 1) * tm + tail[0]
        out_specs = [rows, small]
        out_shape = [jax.ShapeDtypeStruct((prompt_rows, D_MODEL), F32),
                     jax.ShapeDtypeStruct((tail[1], D_MODEL), F32)]
    else:
        out_specs = rows
        out_shape = jax.ShapeDtypeStruct((tiles * tm, D_MODEL), F32)
    return pl.pallas_call(
        functools.partial(_ffn_kernel, nk=nk, tk=tk, tiles=tiles, tail=tail,
                          split_in=split_in, split_out=split_out),
        grid=(tiles, nk),
        in_specs=in_specs,
        out_specs=out_specs,
        out_shape=out_shape,
        scratch_shapes=[pltpu.VMEM((tm, D_MODEL), BF16), pltpu.VMEM((D_MODEL, 2 * tk), BF16)],
        compiler_params=_params("arbitrary", "arbitrary"),
        name="ffn_ln",
    )(*args, w_gu, w_gu, w_down, g, b)


def _proj_kernel(x_ref, w_ref, *rest, rot_heads, shifts):
    if rot_heads:
        c_ref, a_ref, b_ref, o_ref, xb_ref = rest
    else:
        o_ref, xb_ref = rest
    j = pl.program_id(1)

    @pl.when(j == 0)
    def _():
        xb_ref[...] = x_ref[...].astype(BF16)

    def plain():
        o_ref[...] = jnp.dot(xb_ref[...], w_ref[...].astype(BF16), preferred_element_type=F32)

    if not rot_heads:
        plain()
        return

    tm, tn = o_ref.shape
    heads = tn // LANES
    full_tiles, part_heads = divmod(rot_heads, heads)

    def rotated(n_heads):
        wb = w_ref[...].astype(BF16)
        for rows in _row_chunks(tm):
            y = jnp.dot(xb_ref[rows, :], wb, preferred_element_type=F32)
            c, a, b = c_ref[rows, :], a_ref[rows, :], b_ref[rows, :]
            for h in range(heads):
                yh = y[:, h * LANES:(h + 1) * LANES]
                if h < n_heads:
                    yh = yh * c + pltpu.roll(yh, shifts[0], 1) * a + pltpu.roll(yh, shifts[1], 1) * b
                o_ref[rows, h * LANES:(h + 1) * LANES] = yh

    pl.when(j < full_tiles)(lambda: rotated(heads))
    if part_heads:
        pl.when(j == full_tiles)(lambda: rotated(part_heads))
    pl.when(j >= full_tiles + (1 if part_heads else 0))(plain)


def _proj(x, w, layer, tm, rot_heads=0, tables=None, shifts=None):
    m = x.shape[0]
    n = w.shape[-1]
    tn = PROJ_TILE_N
    in_specs = [
        pl.BlockSpec((tm, D_MODEL), lambda i, j: (i, 0)),
        pl.BlockSpec((None, D_MODEL, tn), lambda i, j: (layer, 0, j)),
    ]
    args = [x, w]
    if rot_heads:
        tab = pl.BlockSpec((tm, LANES), lambda i, j: (i, 0))
        in_specs += [tab, tab, tab]
        args += list(tables)
    return pl.pallas_call(
        functools.partial(_proj_kernel, rot_heads=rot_heads, shifts=shifts),
        grid=(pl.cdiv(m, tm), n // tn),
        in_specs=in_specs,
        out_specs=pl.BlockSpec((tm, tn), lambda i, j: (i, j)),
        out_shape=jax.ShapeDtypeStruct((m, n), F32),
        scratch_shapes=[pltpu.VMEM((tm, D_MODEL), BF16)],
        compiler_params=_params("parallel", "arbitrary"),
        name="proj",
    )(*args)


def _mem_kv_kernel(x_ref, w_ref, o_ref, xb_ref):
    @pl.when(pl.program_id(0) == 0)
    def _():
        xb_ref[...] = x_ref[...].astype(BF16)

    o_ref[...] = jnp.dot(xb_ref[...], w_ref[...].astype(BF16), preferred_element_type=F32)


def _mem_kv(mem, w):
    rows = mem.shape[0]
    layers, _, n = w.shape
    return pl.pallas_call(
        _mem_kv_kernel,
        grid=(layers,),
        in_specs=[
            pl.BlockSpec((rows, D_MODEL), lambda l: (0, 0)),
            pl.BlockSpec((None, D_MODEL, n), lambda l: (l, 0, 0)),
        ],
        out_specs=pl.BlockSpec((None, rows, n), lambda l: (l, 0, 0)),
        out_shape=jax.ShapeDtypeStruct((layers, rows, n), F32),
        scratch_shapes=[pltpu.VMEM((rows, D_MODEL), BF16)],
        compiler_params=_params("arbitrary"),
        name="mem_kv",
    )(mem, w)


def _out_ln_kernel(att_ref, w_ref, x_ref, g_ref, b_ref, o_ref, y_ref, *, nn, tn):
    n = pl.program_id(1)
    y_ref[n] = jnp.dot(att_ref[...].astype(BF16), w_ref[...].astype(BF16),
                       preferred_element_type=F32)

    @pl.when(n == nn - 1)
    def _():
        cols = [slice(c * tn, (c + 1) * tn) for c in range(nn)]
        total = None
        for c in range(nn):
            z = ALPHA * x_ref[:, cols[c]] + y_ref[c]
            y_ref[c] = z
            part = jnp.sum(z, axis=-1, keepdims=True)
            total = part if total is None else total + part
        mu = total * (1.0 / D_MODEL)
        total = None
        for c in range(nn):
            zc = y_ref[c] - mu
            part = jnp.sum(zc * zc, axis=-1, keepdims=True)
            total = part if total is None else total + part
        rstd = lax.rsqrt(total * (1.0 / D_MODEL) + LN_EPS)
        for c in range(nn):
            o_ref[:, cols[c]] = (y_ref[c] - mu) * rstd * g_ref[:, cols[c]] + b_ref[:, cols[c]]


def _out_ln(att, w_out, layer, x, g, b, row0):
    tm = att.shape[0]
    assert row0 % tm == 0
    tn = OUT_TILE_N
    nn = D_MODEL // tn
    rows = pl.BlockSpec((tm, D_MODEL), lambda i, n: (row0 // tm, 0))
    return pl.pallas_call(
        functools.partial(_out_ln_kernel, nn=nn, tn=tn),
        grid=(1, nn),
        in_specs=[
            pl.BlockSpec((tm, D_MODEL), lambda i, n: (0, 0)),
            pl.BlockSpec((None, D_MODEL, tn), lambda i, n: (layer, 0, n)),
            rows,
            pl.BlockSpec((1, D_MODEL), lambda i, n: (0, 0)),
            pl.BlockSpec((1, D_MODEL), lambda i, n: (0, 0)),
        ],
        out_specs=rows,
        out_shape=jax.ShapeDtypeStruct(x.shape, F32),
        scratch_shapes=[pltpu.VMEM((nn, tm, tn), F32)],
        input_output_aliases={2: 0},
        compiler_params=_params("arbitrary", "arbitrary"),
        name="out_ln",
    )(att, w_out, x, g, b)


def _dot_nt(a, b):
    return lax.dot_general(a, b, (((1,), (1,)), ((), ())), preferred_element_type=F32)


def _out_proj_ln(first, att_ref, w_ref, wb_ref, x_ref, g_ref, b_ref, o_ref):
    @pl.when(first)
    def _():
        wb_ref[...] = w_ref[...].astype(BF16)

    y = jnp.dot(att_ref[...], wb_ref[...], preferred_element_type=F32)
    o_ref[...] = _layer_norm_rows(ALPHA * x_ref[...] + y, g_ref[...], b_ref[...])


def _normalised_pv(p, v, extra=None):
    den = jnp.dot(p, jnp.ones((p.shape[1], HEAD_DIM), p.dtype), preferred_element_type=F32)
    if extra is not None:
        den = den + extra
    return jnp.dot(p, v, preferred_element_type=F32) * (1.0 / den)


def _mem_attend_block(qm_ref, mk_ref, mv_ref, att_ref, col0):
    for h in range(MEM_HEADS):
        sl = slice(h * HEAD_DIM, (h + 1) * HEAD_DIM)
        q = qm_ref[:, sl].astype(BF16)
        s = _dot_nt(q, mk_ref[0, :, sl].astype(BF16)) * ATT_SCALE
        p = jnp.exp(s - jnp.max(s, axis=-1, keepdims=True))
        p = p * (1.0 / jnp.sum(p, axis=-1, keepdims=True))
        o = jnp.dot(p.astype(BF16), mv_ref[0, :, sl].astype(BF16), preferred_element_type=F32)
        att_ref[:, col0 + h * HEAD_DIM:col0 + (h + 1) * HEAD_DIM] = o.astype(att_ref.dtype)


def _swa_prompt_kernel(sink_ref, q_ref, kc_ref, kp_ref, vc_ref, vp_ref, qm_ref, mk_ref, mv_ref,
                       w_ref, x_ref, g_ref, b_ref, o_ref, kt_ref, vt_ref, s_ref, p_ref, att_ref, wb_ref,
                       *, nblk, steps):
    n = pl.program_id(1)
    nq = SWA_Q_HEADS
    w2 = 2 * WINDOW

    def prev_cur(cur_ref, prev_ref, i, sl):
        rows = slice(i * WINDOW, (i + 1) * WINDOW)
        before = prev_ref[:, sl] if i == 0 else cur_ref[(i - 1) * WINDOW:i * WINDOW, sl]
        return jnp.concatenate([before, cur_ref[rows, sl]], axis=0).astype(BF16)

    for i in range(nblk):
        rows = slice(i * WINDOW, (i + 1) * WINDOW)
        for h in range(SWA_KV_HEADS):
            sl = slice(h * HEAD_DIM, (h + 1) * HEAD_DIM)
            q3 = jnp.concatenate(
                [q_ref[rows, (h * SWA_GROUP + g) * HEAD_DIM:(h * SWA_GROUP + g + 1) * HEAD_DIM]
                 for g in range(SWA_GROUP)], axis=0).astype(BF16)
            s = _dot_nt(q3, prev_cur(kc_ref, kp_ref, i, sl)) * ATT_SCALE
            s_ref[i, h * SWA_GROUP:(h + 1) * SWA_GROUP] = s.reshape(SWA_GROUP, WINDOW, w2)
    for h in range(MEM_HEADS):
        sl = slice(h * HEAD_DIM, (h + 1) * HEAD_DIM)
        sm = _dot_nt(qm_ref[:, sl].astype(BF16), mk_ref[0, :, sl].astype(BF16)) * ATT_SCALE
        s_ref[:, nq + h] = sm.reshape(nblk, WINDOW, w2)

    qi = lax.broadcasted_iota(jnp.int32, (WINDOW, w2), 0)
    kj = lax.broadcasted_iota(jnp.int32, (WINDOW, w2), 1)
    sink = sink_ref[...]
    sink_terms = []
    for i in range(nblk):
        first_key = jnp.where(n > 0, qi, WINDOW) if i == 0 else qi
        ok = (kj >= first_key) & (kj <= qi + WINDOW)
        s = jnp.where(ok[None], s_ref[i, 0:nq], NEG_INF)
        m = jnp.maximum(jnp.max(s, axis=-1, keepdims=True), sink)
        p_ref[i, 0:nq] = jnp.exp(s - m).astype(BF16)
        sink_terms.append(jnp.exp(sink - m))
    s = s_ref[:, nq:nq + MEM_HEADS]
    p_ref[:, nq:nq + MEM_HEADS] = jnp.exp(s - jnp.max(s, axis=-1, keepdims=True)).astype(BF16)

    for i in range(nblk):
        rows = slice(i * WINDOW, (i + 1) * WINDOW)
        for h in range(SWA_KV_HEADS):
            sl = slice(h * HEAD_DIM, (h + 1) * HEAD_DIM)
            group = slice(h * SWA_GROUP, (h + 1) * SWA_GROUP)
            p3 = p_ref[i, group].reshape(SWA_GROUP * WINDOW, w2)
            o = _normalised_pv(p3, prev_cur(vc_ref, vp_ref, i, sl),
                               sink_terms[i][group].reshape(SWA_GROUP * WINDOW, 1))
            for g in range(SWA_GROUP):
                hq = h * SWA_GROUP + g
                att_ref[rows, hq * HEAD_DIM:(hq + 1) * HEAD_DIM] = (
                    o[g * WINDOW:(g + 1) * WINDOW].astype(BF16))
    for h in range(MEM_HEADS):
        sl = slice(h * HEAD_DIM, (h + 1) * HEAD_DIM)
        pm = p_ref[:, nq + h].reshape(nblk * WINDOW, w2)
        om = _normalised_pv(pm, mv_ref[0, :, sl].astype(BF16))
        att_ref[:, SWA_Q + h * HEAD_DIM:SWA_Q + (h + 1) * HEAD_DIM] = om.astype(BF16)

    @pl.when(n == steps - 1)
    def _():
        last = slice((nblk - 1) * WINDOW, nblk * WINDOW)
        kt_ref[0] = kc_ref[last, :]
        vt_ref[0] = vc_ref[last, :]

    first = jnp.logical_and(pl.program_id(0) == 0, n == 0)
    _out_proj_ln(first, att_ref, w_ref, wb_ref, x_ref, g_ref, b_ref, o_ref)


def _mixer_tail_specs(layer, row_of, rows):
    mem = (None, 1, MEM_LEN, MEM_DIM)
    vec = pl.BlockSpec((1, D_MODEL), lambda b, n: (0, 0))
    mixer = layer // 2
    return [
        pl.BlockSpec(mem, lambda b, n: (layer, b, 0, 0)),
        pl.BlockSpec(mem, lambda b, n: (layer, b, 0, 1)),
        pl.BlockSpec((None, D_MODEL, D_MODEL), lambda b, n: (mixer, 0, 0), pipeline_mode=pl.Buffered(1)),
        pl.BlockSpec((rows, D_MODEL), lambda b, n: (row_of(b, n), 0)),
        vec, vec,
    ]


def _swa_prompt(qkv, bsz, seq, mem_kv, layer, sinks, w_out, x, g, b):
    nblk = ATT_BLOCKS_PER_STEP
    rows = nblk * WINDOW
    steps = seq // rows
    assert seq % rows == 0 and seq >= WINDOW
    tail = pl.BlockSpec((1, WINDOW, SWA_KV), lambda b, n: (b, 0, 0))
    blocks_per_seq = seq // WINDOW
    kcol = SWA_Q // SWA_KV
    vcol = kcol + 1
    mcol = vcol + 1
    row_of = lambda b, n: b * steps + n
    prev_of = lambda b, n: b * blocks_per_seq + jnp.maximum(n * nblk - 1, 0)
    cur = lambda w, col: pl.BlockSpec((rows, w), lambda b, n: (row_of(b, n), col))
    prev = lambda col: pl.BlockSpec((WINDOW, SWA_KV), lambda b, n: (prev_of(b, n), col))
    nheads = SWA_Q_HEADS + MEM_HEADS
    assert MEM_LEN == 2 * WINDOW
    in_specs = [
        pl.BlockSpec((SWA_Q_HEADS, 1, 1), lambda b, n: (0, 0, 0)),
        cur(SWA_Q, 0), cur(SWA_KV, kcol), prev(kcol), cur(SWA_KV, vcol), prev(vcol), cur(MEM_DIM, mcol),
    ] + _mixer_tail_specs(layer, row_of, rows)
    return pl.pallas_call(
        functools.partial(_swa_prompt_kernel, nblk=nblk, steps=steps),
        grid=(bsz, steps),
        in_specs=in_specs,
        out_specs=[pl.BlockSpec((rows, D_MODEL), lambda b, n: (row_of(b, n), 0)), tail, tail],
        out_shape=[jax.ShapeDtypeStruct(x.shape, F32),
                   jax.ShapeDtypeStruct((bsz, WINDOW, SWA_KV), F32),
                   jax.ShapeDtypeStruct((bsz, WINDOW, SWA_KV), F32)],
        scratch_shapes=[pltpu.VMEM((nblk, nheads, WINDOW, 2 * WINDOW), F32),
                        pltpu.VMEM((nblk, nheads, WINDOW, 2 * WINDOW), BF16),
                        pltpu.VMEM((rows, D_MODEL), BF16),
                        pltpu.VMEM((D_MODEL, D_MODEL), BF16)],
        input_output_aliases={len(in_specs) - 3: 0},
        compiler_params=_params("arbitrary", "arbitrary"),
        name="swa_prompt",
    )(sinks.reshape(SWA_Q_HEADS, 1, 1), qkv, qkv, qkv, qkv, qkv, qkv, mem_kv, mem_kv, w_out, x, g, b)


def _head_norm_gate(o, gate):
    mu = jnp.mean(o, axis=-1, keepdims=True)
    oc = o - mu
    var = jnp.mean(oc * oc, axis=-1, keepdims=True)
    return jax.nn.silu(gate) * (oc * lax.rsqrt(var + HEAD_NORM_EPS))


def _ret_prompt_kernel(cdec_ref, decay_ref, qdec_ref, kdec_ref, q_ref, k_ref, v_ref, gate_ref, qm_ref,
                       mk_ref, mv_ref, w_ref, x_ref, g_ref, b_ref, o_ref, s_out_ref,
                       state_ref, att_ref, wb_ref, *, steps, nblk):
    c = pl.program_id(1)

    @pl.when(c == 0)
    def _():
        state_ref[...] = jnp.zeros_like(state_ref)

    for i in range(nblk):
        rows = slice(i * RET_CHUNK, (i + 1) * RET_CHUNK)
        for h in range(RET_HEADS):
            ksl = slice(h * RET_DK, (h + 1) * RET_DK)
            vsl = slice(h * RET_DV, (h + 1) * RET_DV)
            qc = q_ref[rows, ksl]
            kc = k_ref[rows, ksl] * RET_K_SCALE
            vb = v_ref[rows, vsl].astype(BF16)
            st = state_ref[h]
            inner = _dot_nt(qc.astype(BF16), kc.astype(BF16)) * decay_ref[h]
            o = (jnp.dot(inner.astype(BF16), vb, preferred_element_type=F32)
                 + jnp.dot((qc * qdec_ref[h]).astype(BF16), st.astype(BF16),
                           preferred_element_type=F32))
            kd = (kc * kdec_ref[h]).astype(BF16)
            state_ref[h] = cdec_ref[h] * st + lax.dot_general(
                kd, vb, (((0,), (0,)), ((), ())), preferred_element_type=F32)
            att_ref[rows, vsl] = _head_norm_gate(o, gate_ref[rows, vsl]).astype(BF16)
    _mem_attend_block(qm_ref, mk_ref, mv_ref, att_ref, RET_V)

    @pl.when(c == steps - 1)
    def _():
        s_out_ref[0] = state_ref[...]

    first = jnp.logical_and(pl.program_id(0) == 0, c == 0)
    _out_proj_ln(first, att_ref, w_ref, wb_ref, x_ref, g_ref, b_ref, o_ref)


def _drop_carried(kernel, first, count):
    def body(*refs):
        return kernel(*refs[:first], *refs[first + count:])
    return body


def _carry(prev_outputs):
    prev_outputs = list(prev_outputs)
    return [pl.BlockSpec(memory_space=pl.ANY)] * len(prev_outputs), prev_outputs


def _ret_prompt(qkvg, bsz, seq, mem_kv, layer, tables, w_out, x, g, b, carried):
    nblk = ATT_BLOCKS_PER_STEP
    rows = nblk * RET_CHUNK
    steps = seq // rows
    mixer = layer // 2
    decay, qdec, kdec, cdec = tables
    row_of = lambda b, n: b * steps + n
    cur = lambda w, col: pl.BlockSpec((rows, w), lambda b, n: (row_of(b, n), col))
    tab = pl.BlockSpec((RET_HEADS, RET_CHUNK, RET_CHUNK), lambda b, n: (0, 0, 0))
    in_specs = [
        pl.BlockSpec(memory_space=pltpu.SMEM), tab, tab, tab,
        cur(RET_QK, 0), cur(RET_QK, 1), cur(RET_V, 1), cur(RET_V, 2),
        cur(MEM_DIM, (2 * RET_QK + 2 * RET_V) // MEM_DIM),
    ] + _mixer_tail_specs(layer, row_of, rows)
    n_in = len(in_specs)
    carry_specs, carry_args = _carry(carried)
    aliases = {n_in - 3: 0}
    aliases.update({n_in + c: 1 + c for c in range(len(carry_args))})
    return pl.pallas_call(
        _drop_carried(functools.partial(_ret_prompt_kernel, steps=steps, nblk=nblk), n_in, len(carry_args)),
        grid=(bsz, steps),
        in_specs=in_specs + carry_specs,
        out_specs=[
            pl.BlockSpec((rows, D_MODEL), lambda b, n: (row_of(b, n), 0)),
            pl.BlockSpec((None, 1, RET_HEADS, RET_DK, RET_DV), lambda b, n: (mixer, b, 0, 0, 0)),
        ],
        out_shape=[
            jax.ShapeDtypeStruct(x.shape, F32),
            jax.ShapeDtypeStruct((DEPTH // 2, bsz, RET_HEADS, RET_DK, RET_DV), F32),
        ],
        scratch_shapes=[pltpu.VMEM((RET_HEADS, RET_DK, RET_DV), F32),
                        pltpu.VMEM((rows, D_MODEL), BF16),
                        pltpu.VMEM((D_MODEL, D_MODEL), BF16)],
        input_output_aliases=aliases,
        compiler_params=_params("arbitrary", "arbitrary"),
        name="ret_prompt",
    )(cdec, decay, qdec, kdec, qkvg, qkvg, qkvg, qkvg, qkvg, mem_kv, mem_kv, w_out, x, g, b, *carry_args)


def _pair_tokens(a):
    t, h, d = a.shape
    a = a.reshape(t // 2, 2, h, d)
    return jnp.concatenate([a[:, 0], a[:, 1]], axis=1)


def _twice(x):
    return jnp.concatenate([x, x], axis=0)


def _fold(x):
    half = x.shape[0] // 2
    return x[:half] + x[half:]


def _mem_attend_row(q, mk_ref, mv_ref):
    mk = _pair_tokens(mk_ref[0])
    mv = _pair_tokens(mv_ref[0])
    s = jnp.sum(mk * _twice(q)[None], axis=-1, keepdims=True) * ATT_SCALE
    m = jnp.max(s, axis=0)
    m = jnp.maximum(m[:MEM_HEADS], m[MEM_HEADS:])
    p = jnp.exp(s - _twice(m)[None])
    den = _fold(jnp.sum(p, axis=0))
    return _fold(jnp.sum(p * mv, axis=0)) * (1.0 / den)


def _swa_step_kernel(sink_ref, rows_ref, kbuf_ref, vbuf_ref, mk_ref, mv_ref, o_ref, nk_ref, nv_ref):
    krow0 = SWA_Q_HEADS
    vrow0 = krow0 + SWA_KV_HEADS
    mrow0 = vrow0 + SWA_KV_HEADS
    wb = kbuf_ref.shape[1]
    kb = _pair_tokens(kbuf_ref[0])
    vb = _pair_tokens(vbuf_ref[0])
    k_new = rows_ref[0, krow0:krow0 + SWA_KV_HEADS, :]
    v_new = rows_ref[0, vrow0:vrow0 + SWA_KV_HEADS, :]
    for g in range(SWA_GROUP):
        group_rows = pl.ds(g, SWA_KV_HEADS, stride=SWA_GROUP)
        q = rows_ref[0, group_rows, :]
        s_buf = jnp.sum(kb * _twice(q)[None], axis=-1, keepdims=True) * ATT_SCALE
        s_new = jnp.sum(k_new * q, axis=-1, keepdims=True) * ATT_SCALE
        sink = sink_ref[g]
        m = jnp.max(s_buf, axis=0)
        m = jnp.maximum(jnp.maximum(m[:SWA_KV_HEADS], m[SWA_KV_HEADS:]), jnp.maximum(s_new, sink))
        p_buf = jnp.exp(s_buf - _twice(m)[None])
        p_new = jnp.exp(s_new - m)
        den = _fold(jnp.sum(p_buf, axis=0)) + p_new + jnp.exp(sink - m)
        o_ref[0, group_rows, :] = (_fold(jnp.sum(p_buf * vb, axis=0)) + p_new * v_new) * (1.0 / den)
    nk_ref[0, 0:wb - 1] = kbuf_ref[0, 1:wb]
    nv_ref[0, 0:wb - 1] = vbuf_ref[0, 1:wb]
    nk_ref[0, wb - 1] = k_new
    nv_ref[0, wb - 1] = v_new
    o_ref[0, SWA_Q_HEADS:SWA_Q_HEADS + MEM_HEADS, :] = _mem_attend_row(
        rows_ref[0, mrow0:mrow0 + MEM_HEADS, :], mk_ref, mv_ref)


def _swa_step(rows, cache_k, cache_v, j, mem_k, mem_v, i, sinks, carried):
    bsz = rows.shape[0]
    wb = cache_k.shape[2]
    cache = pl.BlockSpec((None, 1, wb, SWA_KV_HEADS, HEAD_DIM), lambda b: (j, b, 0, 0, 0))
    mem = pl.BlockSpec((None, 1, MEM_LEN, MEM_HEADS, HEAD_DIM), lambda b: (i, b, 0, 0, 0))
    nrows = D_MODEL // HEAD_DIM
    sink_gk = sinks.reshape(SWA_KV_HEADS, SWA_GROUP).T.reshape(SWA_GROUP, SWA_KV_HEADS, 1)
    in_specs = [
        pl.BlockSpec(sink_gk.shape, lambda b: (0, 0, 0)),
        pl.BlockSpec((1,) + rows.shape[1:], lambda b: (b, 0, 0)),
        cache, cache, mem, mem,
    ]
    n_in = len(in_specs)
    carry_specs, carry_args = _carry(carried)
    return pl.pallas_call(
        _drop_carried(_swa_step_kernel, n_in, len(carry_args)),
        grid=(bsz,),
        in_specs=in_specs + carry_specs,
        out_specs=[pl.BlockSpec((1, nrows, HEAD_DIM), lambda b: (b, 0, 0)), cache, cache],
        out_shape=[
            jax.ShapeDtypeStruct((bsz, nrows, HEAD_DIM), F32),
            jax.ShapeDtypeStruct(cache_k.shape, F32),
            jax.ShapeDtypeStruct(cache_v.shape, F32),
        ],
        input_output_aliases={n_in + c: 1 + c for c in range(len(carry_args))},
        compiler_params=_params("parallel"),
        name="swa_step",
    )(sink_gk, rows, cache_k, cache_v, mem_k, mem_v, *carry_args)


def _ret_step_kernel(dec_ref, rows_ref, cols_ref, s_ref, mk_ref, mv_ref, o_ref, s_out_ref):
    krow0 = RET_HEADS
    vrow0 = 2 * RET_HEADS
    grow0 = vrow0 + 2 * RET_HEADS
    mrow0 = grow0 + 2 * RET_HEADS
    for h in range(RET_HEADS):
        q_row = rows_ref[0, h:h + 1, :]
        k_row = rows_ref[0, krow0 + h:krow0 + h + 1, :] * RET_K_SCALE
        q_col = cols_ref[0, :, h:h + 1]
        k_col = cols_ref[0, :, krow0 + h:krow0 + h + 1] * RET_K_SCALE
        inner = jnp.sum(q_row * k_row, axis=1, keepdims=True) * dec_ref[0, h]
        qd = q_col * dec_ref[1, h]
        kd = k_col * dec_ref[2, h]
        halves = []
        for t in range(2):
            lsl = slice(t * HEAD_DIM, (t + 1) * HEAD_DIM)
            v = rows_ref[0, vrow0 + 2 * h + t:vrow0 + 2 * h + t + 1, :]
            st = s_ref[0, h, :, lsl]
            halves.append(inner * v + jnp.sum(qd * st, axis=0, keepdims=True))
            s_out_ref[0, h, :, lsl] = dec_ref[3, h] * st + kd * v
        mu = (jnp.sum(halves[0], axis=1, keepdims=True)
              + jnp.sum(halves[1], axis=1, keepdims=True)) * (1.0 / RET_DV)
        cen = [o - mu for o in halves]
        var = (jnp.sum(cen[0] * cen[0], axis=1, keepdims=True)
               + jnp.sum(cen[1] * cen[1], axis=1, keepdims=True)) * (1.0 / RET_DV)
        rstd = lax.rsqrt(var + HEAD_NORM_EPS)
        for t in range(2):
            gate = rows_ref[0, grow0 + 2 * h + t:grow0 + 2 * h + t + 1, :]
            o_ref[0, 2 * h + t:2 * h + t + 1, :] = jax.nn.silu(gate) * (cen[t] * rstd)
    o_ref[0, 2 * RET_HEADS:2 * RET_HEADS + MEM_HEADS, :] = _mem_attend_row(
        rows_ref[0, mrow0:mrow0 + MEM_HEADS, :], mk_ref, mv_ref)


def _ret_step(rows, cols, state, j, mem_k, mem_v, i, dec, carried):
    bsz = rows.shape[0]
    st = pl.BlockSpec((None, 1, RET_HEADS, RET_DK, RET_DV), lambda b: (j, b, 0, 0, 0))
    mem = pl.BlockSpec((None, 1, MEM_LEN, MEM_HEADS, HEAD_DIM), lambda b: (i, b, 0, 0, 0))
    nrows = D_MODEL // HEAD_DIM
    in_specs = [
        pl.BlockSpec(memory_space=pltpu.SMEM),
        pl.BlockSpec((1,) + rows.shape[1:], lambda b: (b, 0, 0)),
        pl.BlockSpec((1,) + cols.shape[1:], lambda b: (b, 0, 0)),
        st, mem, mem,
    ]
    n_in = len(in_specs)
    carry_specs, carry_args = _carry(carried)
    return pl.pallas_call(
        _drop_carried(_ret_step_kernel, n_in, len(carry_args)),
        grid=(bsz,),
        in_specs=in_specs + carry_specs,
        out_specs=[pl.BlockSpec((1, nrows, HEAD_DIM), lambda b: (b, 0, 0)), st],
        out_shape=[
            jax.ShapeDtypeStruct((bsz, nrows, HEAD_DIM), F32),
            jax.ShapeDtypeStruct(state.shape, F32),
        ],
        input_output_aliases={n_in + c: 1 + c for c in range(len(carry_args))},
        compiler_params=_params("parallel"),
        name="ret_step",
    )(dec, rows, cols, state, mem_k, mem_v, *carry_args)


def _rope_tables(pos):
    half = ROPE_DIM // 2
    inv = ROPE_THETA ** (-jnp.arange(half, dtype=F32) / half)
    ang = pos.astype(F32)[:, None] * inv[None, :]
    cos, sin = jnp.cos(ang), jnp.sin(ang)
    n = pos.shape[0]
    rest = HEAD_DIM - ROPE_DIM
    c = jnp.concatenate([cos, cos, jnp.ones((n, rest), F32)], axis=-1)
    a = jnp.concatenate([-sin, jnp.zeros((n, HEAD_DIM - half), F32)], axis=-1)
    b = jnp.concatenate([jnp.zeros((n, half), F32), sin, jnp.zeros((n, rest), F32)], axis=-1)
    return (c, a, b), (HEAD_DIM - half, half)


def _ret_rot_tables(pos):
    half = RET_DK // 2
    angle = RET_ROT_BASE ** (-jnp.linspace(0.0, 1.0, half, dtype=F32))
    ang = pos.astype(F32)[:, None] * angle[None, :]
    cos, sin = jnp.cos(ang), jnp.sin(ang)
    n = pos.shape[0]
    zero = jnp.zeros_like(sin)
    c = jnp.stack([cos, cos], axis=-1).reshape(n, RET_DK)
    a = jnp.stack([-sin, zero], axis=-1).reshape(n, RET_DK)
    b = jnp.stack([zero, sin], axis=-1).reshape(n, RET_DK)
    return (c, a, b), (RET_DK - 1, 1)


def _ret_decay(chunk):
    log_g = jnp.log1p(-jnp.exp2(-5.0 - jnp.arange(RET_HEADS, dtype=F32)))
    n = jnp.arange(chunk, dtype=F32)
    rel = n[:, None] - n[None, :]
    decay = jnp.where(rel >= 0, jnp.exp(jnp.maximum(rel, 0.0) * log_g[:, None, None]), 0.0)
    q_dec = jnp.exp((n + 1.0) * log_g[:, None])
    k_dec = jnp.exp((chunk - 1.0 - n) * log_g[:, None])
    c_dec = jnp.exp(chunk * log_g)
    return decay, q_dec, k_dec, c_dec


def kernel(x_prompt, x_sample, cache_swa_k, cache_swa_v, state_ret, cache_mem_k, cache_mem_v,
           mem_prompt, ln_g, ln_b, ffn_w_gu, ffn_w_down, w_mem_kv, swa_w_in, swa_w_out,
           swa_sinks, ret_w_in, ret_w_out):
    bp, seq, _ = x_prompt.shape
    bs, sample_seq, _ = x_sample.shape
    assert sample_seq == 1
    mp, ms = bp * seq, bs
    assert mp % ms == 0

    mem2 = mem_prompt.reshape(bp * MEM_LEN, D_MODEL)
    mem_kv = _mem_kv(mem2, w_mem_kv).reshape(DEPTH, bp, MEM_LEN, 2 * MEM_DIM)
    mem_k_prompt = mem_kv[..., :MEM_DIM].reshape(DEPTH, bp, MEM_LEN, MEM_HEADS, HEAD_DIM)
    mem_v_prompt = mem_kv[..., MEM_DIM:].reshape(DEPTH, bp, MEM_LEN, MEM_HEADS, HEAD_DIM)

    tm, tiles = _row_tiles(mp + ms, FFN_MAX_ROWS)
    m_all = tm * tiles
    tm_proj, _ = _row_tiles(m_all, PROJ_MAX_ROWS)
    pad = m_all - mp - ms
    tail = (mp - (tiles - 1) * tm, ms)
    assert tail[0] >= 0 and tail[0] % 8 == 0 and ms % 8 == 0 and tail[0] + ms <= tm
    def per_row(table):
        return jnp.concatenate([jnp.tile(table[:seq], (bp, 1)),
                                jnp.broadcast_to(table[seq:], (ms, table.shape[1])),
                                jnp.zeros((pad, table.shape[1]), F32)], axis=0)

    pos = jnp.concatenate([jnp.arange(seq, dtype=jnp.int32), jnp.full((1,), PAST_LEN, jnp.int32)])
    rope_tabs, rope_shifts = _rope_tables(pos)
    rot_tabs, rot_shifts = _ret_rot_tables(pos)
    rope_tabs = tuple(per_row(t) for t in rope_tabs)
    rot_tabs = tuple(per_row(t) for t in rot_tabs)
    decay, q_dec, k_dec, c_dec = _ret_decay(RET_CHUNK)
    ret_tabs = (decay, jnp.broadcast_to(q_dec[:, :, None], decay.shape),
                jnp.broadcast_to(k_dec[:, :, None], decay.shape), c_dec)
    decay, q_dec, k_dec, c_dec = _ret_decay(sample_seq)
    step_dec = jnp.stack([decay[:, 0, 0], q_dec[:, 0], k_dec[:, 0], c_dec])

    def ln(i, s):
        return ln_g[i, s].reshape(1, D_MODEL), ln_b[i, s].reshape(1, D_MODEL)

    swa_k_prompt, swa_v_prompt = [], []
    swa_sample = [jnp.zeros_like(cache_swa_k), jnp.zeros_like(cache_swa_v)]
    ret_prompt = jnp.zeros((DEPTH // 2, bp, RET_HEADS, RET_DK, RET_DV), F32)
    ret_sample = jnp.zeros_like(state_ret)
    for i in range(DEPTH):
        j = i // 2
        if i == 0:
            x = _ffn(x_prompt.reshape(mp, D_MODEL), ffn_w_gu, ffn_w_down, i, 0, *ln(i, 0), tm, tiles, tail,
                     x_sample=x_sample.reshape(ms, D_MODEL))
        else:
            x = _ffn(x, ffn_w_gu, ffn_w_down, i, 0, *ln(i, 0), tm, tiles, tail)
        if i % 2 == 0:
            qkv = _proj(x, swa_w_in, j, tm_proj, rot_heads=SWA_Q_HEADS + SWA_KV_HEADS, tables=rope_tabs,
                        shifts=rope_shifts)
            x, k_tail, v_tail = _swa_prompt(qkv, bp, seq, mem_kv, i, swa_sinks[j], swa_w_out, x, *ln(i, 1))
            swa_k_prompt.append(k_tail.reshape(bp, WINDOW, SWA_KV_HEADS, HEAD_DIM))
            swa_v_prompt.append(v_tail.reshape(bp, WINDOW, SWA_KV_HEADS, HEAD_DIM))
            rows = qkv[mp:mp + ms].reshape(ms, SWA_IN_WIDTH // HEAD_DIM, HEAD_DIM)
            att, *swa_sample = _swa_step(rows, cache_swa_k, cache_swa_v, j, cache_mem_k, cache_mem_v, i,
                                         swa_sinks[j], swa_sample)
            x = _out_ln(att.reshape(ms, D_MODEL), swa_w_out, j, x, *ln(i, 1), mp)
        else:
            qkvg = _proj(x, ret_w_in, j, tm_proj, rot_heads=2 * RET_HEADS, tables=rot_tabs,
                         shifts=rot_shifts)
            x, ret_prompt = _ret_prompt(qkvg, bp, seq, mem_kv, i, ret_tabs, ret_w_out, x, *ln(i, 1),
                                        [ret_prompt])
            rows = qkvg[mp:mp + ms].reshape(ms, RET_IN_WIDTH // HEAD_DIM, HEAD_DIM)
            att, ret_sample = _ret_step(rows, jnp.swapaxes(rows, 1, 2), state_ret, j, cache_mem_k,
                                        cache_mem_v, i, step_dec,
                                        [ret_sample])
            x = _out_ln(att.reshape(ms, D_MODEL), ret_w_out, j, x, *ln(i, 1), mp)
        x = _ffn(x, ffn_w_gu, ffn_w_down, i, 1, *ln(i, 2), tm, tiles, tail, split_out=(i == DEPTH - 1))

    y_prompt = x[0].reshape(bp, seq, D_MODEL)
    y_sample = x[1].reshape(bs, sample_seq, D_MODEL)
    return (y_prompt, y_sample, jnp.stack(swa_k_prompt), jnp.stack(swa_v_prompt), swa_sample[0],
            swa_sample[1], ret_prompt, ret_sample, mem_k_prompt, mem_v_prompt)
```

```python
import functools

import jax
import jax.numpy as jnp
from jax import lax
from jax.experimental import pallas as pl
from jax.experimental.pallas import tpu as pltpu

F32 = jnp.float32
BF16 = jnp.bfloat16

D_MODEL = 2048
DEPTH = 4
PAST_LEN = 16384
HEAD_DIM = 128
MEM_LEN = 256
MEM_HEADS = 4
MEM_DIM = MEM_HEADS * HEAD_DIM
SELF_WIDTH = D_MODEL - MEM_DIM
SWA_Q_HEADS = SELF_WIDTH // HEAD_DIM
SWA_KV_HEADS = SWA_Q_HEADS // 3
SWA_GROUP = SWA_Q_HEADS // SWA_KV_HEADS
WINDOW = 128
ROPE_THETA = 500000.0
ROPE_DIM = HEAD_DIM // 4
RET_DK = 128
RET_DV = 2 * RET_DK
RET_HEADS = SELF_WIDTH // RET_DV
RET_CHUNK = 128
RET_ROT_BASE = 10000.0
D_FF = ((8 * D_MODEL // 3 + 255) // 256) * 256
LN_EPS = 1e-5
HEAD_NORM_EPS = 1e-6
ALPHA = (2.0 * DEPTH) ** 0.25
NEG_INF = -1e30
ATT_SCALE = HEAD_DIM ** -0.5
RET_K_SCALE = RET_DK ** -0.5

SWA_Q = SWA_Q_HEADS * HEAD_DIM
SWA_KV = SWA_KV_HEADS * HEAD_DIM
SWA_IN_WIDTH = SWA_Q + 2 * SWA_KV + MEM_DIM
RET_QK = RET_HEADS * RET_DK
RET_V = RET_HEADS * RET_DV
RET_IN_WIDTH = 2 * RET_QK + 2 * RET_V + MEM_DIM

VMEM_LIMIT_BYTES = 58 * 1024 * 1024
LANES = 128
FFN_TILE_K = 256
FFN_MAX_ROWS = 1040
PROJ_MAX_ROWS = 1040
ATT_BLOCKS_PER_STEP = 4
PROJ_TILE_N = 1024
OUT_TILE_N = 512
MAX_ROW_CHUNK = 256
BF16_SUBLANES = 16


def _row_tiles(rows, max_rows):
    tiles = -(-rows // max_rows)
    per_tile = -(-rows // tiles)
    return -(-per_tile // BF16_SUBLANES) * BF16_SUBLANES, tiles


def _row_chunks(tm):
    count = -(-tm // MAX_ROW_CHUNK)
    size = -(-(-(-tm // count)) // BF16_SUBLANES) * BF16_SUBLANES
    return [slice(r, min(r + size, tm)) for r in range(0, tm, size)]


def _params(*sem):
    return pltpu.CompilerParams(dimension_semantics=sem, vmem_limit_bytes=VMEM_LIMIT_BYTES)


def _layer_norm_rows(z, g, b):
    mu = jnp.mean(z, axis=-1, keepdims=True)
    zc = z - mu
    var = jnp.mean(zc * zc, axis=-1, keepdims=True)
    return zc * lax.rsqrt(var + LN_EPS) * g + b


def _ffn_kernel(*refs, nk, tk, tiles, tail, split_in, split_out):
    refs = list(refs)
    x_ref = refs.pop(0)
    xs_ref = refs.pop(0) if split_in else None
    wg_ref, wu_ref, wd_ref, g_ref, b_ref, o_ref = refs[:6]
    os_ref = refs[6] if split_out else None
    xb_ref, wgu_ref = refs[-2:]
    i = pl.program_id(0)
    k = pl.program_id(1)
    chunks = _row_chunks(o_ref.shape[0])

    def load_x(rows):
        xv = x_ref[rows, :]
        if not split_in or rows.stop <= tail[0]:
            return xv
        first, count = tail
        lo, hi = rows.start, rows.stop
        pieces = []
        if lo < first:
            pieces.append(xv[:first - lo])
        s0, s1 = max(lo, first), min(hi, first + count)
        if s0 < s1:
            pieces.append(xs_ref[s0 - first:s1 - first, :])
        if hi > first + count:
            pieces.append(jnp.zeros((hi - max(lo, first + count), xv.shape[1]), xv.dtype))
        return jnp.where(i == tiles - 1, jnp.concatenate(pieces, axis=0), xv)

    def cast_weights():
        wgu_ref[:, :tk] = wg_ref[...].astype(BF16)
        wgu_ref[:, tk:] = wu_ref[...].astype(BF16)
        return wd_ref[...].astype(BF16)

    def down(xb, wd):
        gu = jnp.dot(xb, wgu_ref[...], preferred_element_type=F32)
        h = (jax.nn.silu(gu[:, :tk]) * gu[:, tk:]).astype(BF16)
        return jnp.dot(h, wd, preferred_element_type=F32)

    @pl.when(k == 0)
    def _():
        wd = cast_weights()
        for rows in chunks:
            xb = load_x(rows).astype(BF16)
            xb_ref[rows, :] = xb
            o_ref[rows, :] = down(xb, wd)

    @pl.when(jnp.logical_and(k > 0, k < nk - 1))
    def _():
        wd = cast_weights()
        o_ref[...] += down(xb_ref[...], wd)

    @pl.when(k == nk - 1)
    def _():
        wd = cast_weights()
        for rows in chunks:
            acc = o_ref[rows, :] + down(xb_ref[rows, :], wd)
            z = ALPHA * load_x(rows) + 0.5 * acc
            o_ref[rows, :] = _layer_norm_rows(z, g_ref[...], b_ref[...])
        if split_out:
            @pl.when(i == tiles - 1)
            def _():
                os_ref[...] = o_ref[tail[0]:tail[0] + tail[1], :]


def _ffn(x, w_gu, w_down, layer, slot, g, b, tm, tiles, tail, x_sample=None, split_out=False):
    tk = FFN_TILE_K
    nk = D_FF // tk
    assert nk >= 2
    split_in = x_sample is not None
    rows = pl.BlockSpec((tm, D_MODEL), lambda i, k: (i, 0))
    small = pl.BlockSpec((tail[1], D_MODEL), lambda i, k: (0, 0))
    in_specs = [pl.BlockSpec((tm, D_MODEL), lambda i, k: (i, 0), pipeline_mode=pl.Buffered(1))]
    args = [x]
    if split_in:
        in_specs.append(small)
        args.append(x_sample)
    in_specs += [
        pl.BlockSpec((None, None, D_MODEL, tk), lambda i, k: (layer, slot, 0, k)),
        pl.BlockSpec((None, None, D_MODEL, tk), lambda i, k: (layer, slot, 0, nk + k)),
        pl.BlockSpec((None, None, tk, D_MODEL), lambda i, k: (layer, slot, k, 0)),
        pl.BlockSpec((1, D_MODEL), lambda i, k: (0, 0)),
        pl.BlockSpec((1, D_MODEL), lambda i, k: (0, 0)),
    ]
    if split_out:
        prompt_rows = (tiles - 1) * tm + tail[0]
        out_specs = [rows, small]
        out_shape = [jax.ShapeDtypeStruct((prompt_rows, D_MODEL), F32),
                     jax.ShapeDtypeStruct((tail[1], D_MODEL), F32)]
    else:
        out_specs = rows
        out_shape = jax.ShapeDtypeStruct((tiles * tm, D_MODEL), F32)
    return pl.pallas_call(
        functools.partial(_ffn_kernel, nk=nk, tk=tk, tiles=tiles, tail=tail,
                          split_in=split_in, split_out=split_out),
        grid=(tiles, nk),
        in_specs=in_specs,
        out_specs=out_specs,
        out_shape=out_shape,
        scratch_shapes=[pltpu.VMEM((tm, D_MODEL), BF16), pltpu.VMEM((D_MODEL, 2 * tk), BF16)],
        compiler_params=_params("arbitrary", "arbitrary"),
        name="ffn_ln",
    )(*args, w_gu, w_gu, w_down, g, b)


def _proj_kernel(x_ref, w_ref, *rest, rot_heads, shifts):
    if rot_heads:
        c_ref, a_ref, b_ref, o_ref, xb_ref = rest
    else:
        o_ref, xb_ref = rest
    j = pl.program_id(1)

    @pl.when(j == 0)
    def _():
        xb_ref[...] = x_ref[...].astype(BF16)

    def plain():
        o_ref[...] = jnp.dot(xb_ref[...], w_ref[...].astype(BF16), preferred_element_type=F32)

    if not rot_heads:
        plain()
        return

    tm, tn = o_ref.shape
    heads = tn // LANES
    full_tiles, part_heads = divmod(rot_heads, heads)

    def rotated(n_heads):
        wb = w_ref[...].astype(BF16)
        for rows in _row_chunks(tm):
            y = jnp.dot(xb_ref[rows, :], wb, preferred_element_type=F32)
            c, a, b = c_ref[rows, :], a_ref[rows, :], b_ref[rows, :]
            for h in range(heads):
                yh = y[:, h * LANES:(h + 1) * LANES]
                if h < n_heads:
                    yh = yh * c + pltpu.roll(yh, shifts[0], 1) * a + pltpu.roll(yh, shifts[1], 1) * b
                o_ref[rows, h * LANES:(h + 1) * LANES] = yh

    pl.when(j < full_tiles)(lambda: rotated(heads))
    if part_heads:
        pl.when(j == full_tiles)(lambda: rotated(part_heads))
    pl.when(j >= full_tiles + (1 if part_heads else 0))(plain)


def _proj(x, w, layer, tm, rot_heads=0, tables=None, shifts=None):
    m = x.shape[0]
    n = w.shape[-1]
    tn = PROJ_TILE_N
    in_specs = [
        pl.BlockSpec((tm, D_MODEL), lambda i, j: (i, 0)),
        pl.BlockSpec((None, D_MODEL, tn), lambda i, j: (layer, 0, j)),
    ]
    args = [x, w]
    if rot_heads:
        tab = pl.BlockSpec((tm, LANES), lambda i, j: (i, 0))
        in_specs += [tab, tab, tab]
        args += list(tables)
    return pl.pallas_call(
        functools.partial(_proj_kernel, rot_heads=rot_heads, shifts=shifts),
        grid=(pl.cdiv(m, tm), n // tn),
        in_specs=in_specs,
        out_specs=pl.BlockSpec((tm, tn), lambda i, j: (i, j)),
        out_shape=jax.ShapeDtypeStruct((m, n), F32),
        scratch_shapes=[pltpu.VMEM((tm, D_MODEL), BF16)],
        compiler_params=_params("parallel", "arbitrary"),
        name="proj",
    )(*args)


def _mem_kv_kernel(x_ref, w_ref, o_ref, xb_ref):
    @pl.when(pl.program_id(0) == 0)
    def _():
        xb_ref[...] = x_ref[...].astype(BF16)

    o_ref[...] = jnp.dot(xb_ref[...], w_ref[...].astype(BF16), preferred_element_type=F32)


def _mem_kv(mem, w):
    rows = mem.shape[0]
    layers, _, n = w.shape
    return pl.pallas_call(
        _mem_kv_kernel,
        grid=(layers,),
        in_specs=[
            pl.BlockSpec((rows, D_MODEL), lambda l: (0, 0)),
            pl.BlockSpec((None, D_MODEL, n), lambda l: (l, 0, 0)),
        ],
        out_specs=pl.BlockSpec((None, rows, n), lambda l: (l, 0, 0)),
        out_shape=jax.ShapeDtypeStruct((layers, rows, n), F32),
        scratch_shapes=[pltpu.VMEM((rows, D_MODEL), BF16)],
        compiler_params=_params("arbitrary"),
        name="mem_kv",
    )(mem, w)


def _out_ln_kernel(att_ref, w_ref, x_ref, g_ref, b_ref, o_ref, y_ref, *, nn, tn):
    n = pl.program_id(1)
    y_ref[n] = jnp.dot(att_ref[...].astype(BF16), w_ref[...].astype(BF16),
                       preferred_element_type=F32)

    @pl.when(n == nn - 1)
    def _():
        cols = [slice(c * tn, (c + 1) * tn) for c in range(nn)]
        total = None
        for c in range(nn):
            z = ALPHA * x_ref[:, cols[c]] + y_ref[c]
            y_ref[c] = z
            part = jnp.sum(z, axis=-1, keepdims=True)
            total = part if total is None else total + part
        mu = total * (1.0 / D_MODEL)
        total = None
        for c in range(nn):
            zc = y_ref[c] - mu
            part = jnp.sum(zc * zc, axis=-1, keepdims=True)
            total = part if total is None else total + part
        rstd = lax.rsqrt(total * (1.0 / D_MODEL) + LN_EPS)
        for c in range(nn):
            o_ref[:, cols[c]] = (y_ref[c] - mu) * rstd * g_ref[:, cols[c]] + b_ref[:, cols[c]]


def _out_ln(att, w_out, layer, x, g, b, row0):
    tm = att.shape[0]
    assert row0 % tm == 0
    tn = OUT_TILE_N
    nn = D_MODEL // tn
    rows = pl.BlockSpec((tm, D_MODEL), lambda i, n: (row0 // tm, 0))
    return pl.pallas_call(
        functools.partial(_out_ln_kernel, nn=nn, tn=tn),
        grid=(1, nn),
        in_specs=[
            pl.BlockSpec((tm, D_MODEL), lambda i, n: (0, 0)),
            pl.BlockSpec((None, D_MODEL, tn), lambda i, n: (layer, 0, n)),
            rows,
            pl.BlockSpec((1, D_MODEL), lambda i, n: (0, 0)),
            pl.BlockSpec((1, D_MODEL), lambda i, n: (0, 0)),
        ],
        out_specs=rows,
        out_shape=jax.ShapeDtypeStruct(x.shape, F32),
        scratch_shapes=[pltpu.VMEM((nn, tm, tn), F32)],
        input_output_aliases={2: 0},
        compiler_params=_params("arbitrary", "arbitrary"),
        name="out_ln",
    )(att, w_out, x, g, b)


def _dot_nt(a, b):
    return lax.dot_general(a, b, (((1,), (1,)), ((), ())), preferred_element_type=F32)


def _out_proj_ln(first, att_ref, w_ref, wb_ref, x_ref, g_ref, b_ref, o_ref):
    y = jnp.dot(att_ref[...], w_ref[...], preferred_element_type=F32)
    o_ref[...] = _layer_norm_rows(ALPHA * x_ref[...] + y, g_ref[...], b_ref[...])


def _normalised_pv(p, v, extra=None):
    den = jnp.dot(p, jnp.ones((p.shape[1], HEAD_DIM), p.dtype), preferred_element_type=F32)
    if extra is not None:
        den = den + extra
    return jnp.dot(p, v, preferred_element_type=F32) * (1.0 / den)


def _mem_attend_block(qm_ref, mk_ref, mv_ref, att_ref, col0):
    for h in range(MEM_HEADS):
        sl = slice(h * HEAD_DIM, (h + 1) * HEAD_DIM)
        q = qm_ref[:, sl].astype(BF16)
        s = _dot_nt(q, mk_ref[0, :, sl].astype(BF16)) * ATT_SCALE
        p = jnp.exp(s - jnp.max(s, axis=-1, keepdims=True))
        p = p * (1.0 / jnp.sum(p, axis=-1, keepdims=True))
        o = jnp.dot(p.astype(BF16), mv_ref[0, :, sl].astype(BF16), preferred_element_type=F32)
        att_ref[:, col0 + h * HEAD_DIM:col0 + (h + 1) * HEAD_DIM] = o.astype(att_ref.dtype)


def _swa_prompt_kernel(sink_ref, q_ref, kc_ref, kp_ref, vc_ref, vp_ref, qm_ref, mk_ref, mv_ref,
                       w_ref, x_ref, g_ref, b_ref, o_ref, kt_ref, vt_ref, s_ref, p_ref, att_ref, wb_ref,
                       *, nblk, steps):
    n = pl.program_id(1)
    nq = SWA_Q_HEADS
    w2 = 2 * WINDOW

    def prev_cur(cur_ref, prev_ref, i, sl):
        rows = slice(i * WINDOW, (i + 1) * WINDOW)
        before = prev_ref[:, sl] if i == 0 else cur_ref[(i - 1) * WINDOW:i * WINDOW, sl]
        return jnp.concatenate([before, cur_ref[rows, sl]], axis=0).astype(BF16)

    for i in range(nblk):
        rows = slice(i * WINDOW, (i + 1) * WINDOW)
        for h in range(SWA_KV_HEADS):
            sl = slice(h * HEAD_DIM, (h + 1) * HEAD_DIM)
            q3 = jnp.concatenate(
                [q_ref[rows, (h * SWA_GROUP + g) * HEAD_DIM:(h * SWA_GROUP + g + 1) * HEAD_DIM]
                 for g in range(SWA_GROUP)], axis=0).astype(BF16)
            s = _dot_nt(q3, prev_cur(kc_ref, kp_ref, i, sl)) * ATT_SCALE
            s_ref[i, h * SWA_GROUP:(h + 1) * SWA_GROUP] = s.reshape(SWA_GROUP, WINDOW, w2)
    for h in range(MEM_HEADS):
        sl = slice(h * HEAD_DIM, (h + 1) * HEAD_DIM)
        sm = _dot_nt(qm_ref[:, sl].astype(BF16), mk_ref[0, :, sl].astype(BF16)) * ATT_SCALE
        s_ref[:, nq + h] = sm.reshape(nblk, WINDOW, w2)

    qi = lax.broadcasted_iota(jnp.int32, (WINDOW, w2), 0)
    kj = lax.broadcasted_iota(jnp.int32, (WINDOW, w2), 1)
    sink = sink_ref[...]
    sink_terms = []
    for i in range(nblk):
        first_key = jnp.where(n > 0, qi, WINDOW) if i == 0 else qi
        ok = (kj >= first_key) & (kj <= qi + WINDOW)
        s = jnp.where(ok[None], s_ref[i, 0:nq], NEG_INF)
        m = jnp.maximum(jnp.max(s, axis=-1, keepdims=True), sink)
        p_ref[i, 0:nq] = jnp.exp(s - m).astype(BF16)
        sink_terms.append(jnp.exp(sink - m))
    s = s_ref[:, nq:nq + MEM_HEADS]
    p_ref[:, nq:nq + MEM_HEADS] = jnp.exp(s - jnp.max(s, axis=-1, keepdims=True)).astype(BF16)

    for i in range(nblk):
        rows = slice(i * WINDOW, (i + 1) * WINDOW)
        for h in range(SWA_KV_HEADS):
            sl = slice(h * HEAD_DIM, (h + 1) * HEAD_DIM)
            group = slice(h * SWA_GROUP, (h + 1) * SWA_GROUP)
            p3 = p_ref[i, group].reshape(SWA_GROUP * WINDOW, w2)
            o = _normalised_pv(p3, prev_cur(vc_ref, vp_ref, i, sl),
                               sink_terms[i][group].reshape(SWA_GROUP * WINDOW, 1))
            for g in range(SWA_GROUP):
                hq = h * SWA_GROUP + g
                att_ref[rows, hq * HEAD_DIM:(hq + 1) * HEAD_DIM] = (
                    o[g * WINDOW:(g + 1) * WINDOW].astype(BF16))
    for h in range(MEM_HEADS):
        sl = slice(h * HEAD_DIM, (h + 1) * HEAD_DIM)
        pm = p_ref[:, nq + h].reshape(nblk * WINDOW, w2)
        om = _normalised_pv(pm, mv_ref[0, :, sl].astype(BF16))
        att_ref[:, SWA_Q + h * HEAD_DIM:SWA_Q + (h + 1) * HEAD_DIM] = om.astype(BF16)

    @pl.when(n == steps - 1)
    def _():
        last = slice((nblk - 1) * WINDOW, nblk * WINDOW)
        kt_ref[0] = kc_ref[last, :]
        vt_ref[0] = vc_ref[last, :]

    first = jnp.logical_and(pl.program_id(0) == 0, n == 0)
    _out_proj_ln(first, att_ref, w_ref, wb_ref, x_ref, g_ref, b_ref, o_ref)


def _mixer_tail_specs(layer, row_of, rows):
    mem = (None, 1, MEM_LEN, MEM_DIM)
    vec = pl.BlockSpec((1, D_MODEL), lambda b, n: (0, 0))
    mixer = layer // 2
    return [
        pl.BlockSpec(mem, lambda b, n: (layer, b, 0, 0)),
        pl.BlockSpec(mem, lambda b, n: (layer, b, 0, 1)),
        pl.BlockSpec((None, D_MODEL, D_MODEL), lambda b, n: (mixer, 0, 0), pipeline_mode=pl.Buffered(1)),
        pl.BlockSpec((rows, D_MODEL), lambda b, n: (row_of(b, n), 0)),
        vec, vec,
    ]


def _swa_prompt(qkv, bsz, seq, mem_kv, layer, sinks, w_out, x, g, b):
    nblk = ATT_BLOCKS_PER_STEP
    rows = nblk * WINDOW
    steps = seq // rows
    assert seq % rows == 0 and seq >= WINDOW
    tail = pl.BlockSpec((1, WINDOW, SWA_KV), lambda b, n: (b, 0, 0))
    blocks_per_seq = seq // WINDOW
    kcol = SWA_Q // SWA_KV
    vcol = kcol + 1
    mcol = vcol + 1
    row_of = lambda b, n: b * steps + n
    prev_of = lambda b, n: b * blocks_per_seq + jnp.maximum(n * nblk - 1, 0)
    cur = lambda w, col: pl.BlockSpec((rows, w), lambda b, n: (row_of(b, n), col))
    prev = lambda col: pl.BlockSpec((WINDOW, SWA_KV), lambda b, n: (prev_of(b, n), col))
    nheads = SWA_Q_HEADS + MEM_HEADS
    assert MEM_LEN == 2 * WINDOW
    in_specs = [
        pl.BlockSpec((SWA_Q_HEADS, 1, 1), lambda b, n: (0, 0, 0)),
        cur(SWA_Q, 0), cur(SWA_KV, kcol), prev(kcol), cur(SWA_KV, vcol), prev(vcol), cur(MEM_DIM, mcol),
    ] + _mixer_tail_specs(layer, row_of, rows)
    return pl.pallas_call(
        functools.partial(_swa_prompt_kernel, nblk=nblk, steps=steps),
        grid=(bsz, steps),
        in_specs=in_specs,
        out_specs=[pl.BlockSpec((rows, D_MODEL), lambda b, n: (row_of(b, n), 0)), tail, tail],
        out_shape=[jax.ShapeDtypeStruct(x.shape, F32),
                   jax.ShapeDtypeStruct((bsz, WINDOW, SWA_KV), F32),
                   jax.ShapeDtypeStruct((bsz, WINDOW, SWA_KV), F32)],
        scratch_shapes=[pltpu.VMEM((nblk, nheads, WINDOW, 2 * WINDOW), F32),
                        pltpu.VMEM((nblk, nheads, WINDOW, 2 * WINDOW), BF16),
                        pltpu.VMEM((rows, D_MODEL), BF16),
                        pltpu.VMEM((BF16_SUBLANES, LANES), BF16)],
        input_output_aliases={len(in_specs) - 3: 0},
        compiler_params=_params("arbitrary", "arbitrary"),
        name="swa_prompt",
    )(sinks.reshape(SWA_Q_HEADS, 1, 1), qkv, qkv, qkv, qkv, qkv, qkv, mem_kv, mem_kv, w_out, x, g, b)


def _head_norm_gate(o, gate):
    mu = jnp.mean(o, axis=-1, keepdims=True)
    oc = o - mu
    var = jnp.mean(oc * oc, axis=-1, keepdims=True)
    return jax.nn.silu(gate) * (oc * lax.rsqrt(var + HEAD_NORM_EPS))


def _ret_prompt_kernel(cdec_ref, decay_ref, qdec_ref, kdec_ref, q_ref, k_ref, v_ref, gate_ref, qm_ref,
                       mk_ref, mv_ref, w_ref, x_ref, g_ref, b_ref, o_ref, s_out_ref,
                       state_ref, att_ref, wb_ref, *, steps, nblk):
    c = pl.program_id(1)

    @pl.when(c == 0)
    def _():
        state_ref[...] = jnp.zeros_like(state_ref)

    for i in range(nblk):
        rows = slice(i * RET_CHUNK, (i + 1) * RET_CHUNK)
        for h in range(RET_HEADS):
            ksl = slice(h * RET_DK, (h + 1) * RET_DK)
            vsl = slice(h * RET_DV, (h + 1) * RET_DV)
            qc = q_ref[rows, ksl]
            kc = k_ref[rows, ksl] * RET_K_SCALE
            vb = v_ref[rows, vsl].astype(BF16)
            st = state_ref[h]
            inner = _dot_nt(qc.astype(BF16), kc.astype(BF16)) * decay_ref[h]
            o = (jnp.dot(inner.astype(BF16), vb, preferred_element_type=F32)
                 + jnp.dot((qc * qdec_ref[h]).astype(BF16), st.astype(BF16),
                           preferred_element_type=F32))
            kd = (kc * kdec_ref[h]).astype(BF16)
            state_ref[h] = cdec_ref[h] * st + lax.dot_general(
                kd, vb, (((0,), (0,)), ((), ())), preferred_element_type=F32)
            att_ref[rows, vsl] = _head_norm_gate(o, gate_ref[rows, vsl]).astype(BF16)
    _mem_attend_block(qm_ref, mk_ref, mv_ref, att_ref, RET_V)

    @pl.when(c == steps - 1)
    def _():
        s_out_ref[0] = state_ref[...]

    first = jnp.logical_and(pl.program_id(0) == 0, c == 0)
    _out_proj_ln(first, att_ref, w_ref, wb_ref, x_ref, g_ref, b_ref, o_ref)


def _drop_carried(kernel, first, count):
    def body(*refs):
        return kernel(*refs[:first], *refs[first + count:])
    return body


def _carry(prev_outputs):
    prev_outputs = list(prev_outputs)
    return [pl.BlockSpec(memory_space=pl.ANY)] * len(prev_outputs), prev_outputs


def _ret_prompt(qkvg, bsz, seq, mem_kv, layer, tables, w_out, x, g, b, carried):
    nblk = ATT_BLOCKS_PER_STEP
    rows = nblk * RET_CHUNK
    steps = seq // rows
    mixer = layer // 2
    decay, qdec, kdec, cdec = tables
    row_of = lambda b, n: b * steps + n
    cur = lambda w, col: pl.BlockSpec((rows, w), lambda b, n: (row_of(b, n), col))
    tab = pl.BlockSpec((RET_HEADS, RET_CHUNK, RET_CHUNK), lambda b, n: (0, 0, 0))
    in_specs = [
        pl.BlockSpec(memory_space=pltpu.SMEM), tab, tab, tab,
        cur(RET_QK, 0), cur(RET_QK, 1), cur(RET_V, 1), cur(RET_V, 2),
        cur(MEM_DIM, (2 * RET_QK + 2 * RET_V) // MEM_DIM),
    ] + _mixer_tail_specs(layer, row_of, rows)
    n_in = len(in_specs)
    carry_specs, carry_args = _carry(carried)
    aliases = {n_in - 3: 0}
    aliases.update({n_in + c: 1 + c for c in range(len(carry_args))})
    return pl.pallas_call(
        _drop_carried(functools.partial(_ret_prompt_kernel, steps=steps, nblk=nblk), n_in, len(carry_args)),
        grid=(bsz, steps),
        in_specs=in_specs + carry_specs,
        out_specs=[
            pl.BlockSpec((rows, D_MODEL), lambda b, n: (row_of(b, n), 0)),
            pl.BlockSpec((None, 1, RET_HEADS, RET_DK, RET_DV), lambda b, n: (mixer, b, 0, 0, 0)),
        ],
        out_shape=[
            jax.ShapeDtypeStruct(x.shape, F32),
            jax.ShapeDtypeStruct((DEPTH // 2, bsz, RET_HEADS, RET_DK, RET_DV), F32),
        ],
        scratch_shapes=[pltpu.VMEM((RET_HEADS, RET_DK, RET_DV), F32),
                        pltpu.VMEM((rows, D_MODEL), BF16),
                        pltpu.VMEM((BF16_SUBLANES, LANES), BF16)],
        input_output_aliases=aliases,
        compiler_params=_params("arbitrary", "arbitrary"),
        name="ret_prompt",
    )(cdec, decay, qdec, kdec, qkvg, qkvg, qkvg, qkvg, qkvg, mem_kv, mem_kv, w_out, x, g, b, *carry_args)


def _pair_tokens(a):
    t, h, d = a.shape
    a = a.reshape(t // 2, 2, h, d)
    return jnp.concatenate([a[:, 0], a[:, 1]], axis=1)


def _twice(x):
    return jnp.concatenate([x, x], axis=0)


def _fold(x):
    half = x.shape[0] // 2
    return x[:half] + x[half:]


def _mem_attend_row(q, mk_ref, mv_ref):
    mk = _pair_tokens(mk_ref[0])
    mv = _pair_tokens(mv_ref[0])
    s = jnp.sum(mk * _twice(q)[None], axis=-1, keepdims=True) * ATT_SCALE
    m = jnp.max(s, axis=0)
    m = jnp.maximum(m[:MEM_HEADS], m[MEM_HEADS:])
    p = jnp.exp(s - _twice(m)[None])
    den = _fold(jnp.sum(p, axis=0))
    return _fold(jnp.sum(p * mv, axis=0)) * (1.0 / den)


def _swa_step_kernel(sink_ref, rows_ref, kbuf_ref, vbuf_ref, mk_ref, mv_ref, o_ref, nk_ref, nv_ref):
    krow0 = SWA_Q_HEADS
    vrow0 = krow0 + SWA_KV_HEADS
    mrow0 = vrow0 + SWA_KV_HEADS
    wb = kbuf_ref.shape[1]
    kb = _pair_tokens(kbuf_ref[0])
    vb = _pair_tokens(vbuf_ref[0])
    k_new = rows_ref[0, krow0:krow0 + SWA_KV_HEADS, :]
    v_new = rows_ref[0, vrow0:vrow0 + SWA_KV_HEADS, :]
    for g in range(SWA_GROUP):
        group_rows = pl.ds(g, SWA_KV_HEADS, stride=SWA_GROUP)
        q = rows_ref[0, group_rows, :]
        s_buf = jnp.sum(kb * _twice(q)[None], axis=-1, keepdims=True) * ATT_SCALE
        s_new = jnp.sum(k_new * q, axis=-1, keepdims=True) * ATT_SCALE
        sink = sink_ref[g]
        m = jnp.max(s_buf, axis=0)
        m = jnp.maximum(jnp.maximum(m[:SWA_KV_HEADS], m[SWA_KV_HEADS:]), jnp.maximum(s_new, sink))
        p_buf = jnp.exp(s_buf - _twice(m)[None])
        p_new = jnp.exp(s_new - m)
        den = _fold(jnp.sum(p_buf, axis=0)) + p_new + jnp.exp(sink - m)
        o_ref[0, group_rows, :] = (_fold(jnp.sum(p_buf * vb, axis=0)) + p_new * v_new) * (1.0 / den)
    nk_ref[0, 0:wb - 1] = kbuf_ref[0, 1:wb]
    nv_ref[0, 0:wb - 1] = vbuf_ref[0, 1:wb]
    nk_ref[0, wb - 1] = k_new
    nv_ref[0, wb - 1] = v_new
    o_ref[0, SWA_Q_HEADS:SWA_Q_HEADS + MEM_HEADS, :] = _mem_attend_row(
        rows_ref[0, mrow0:mrow0 + MEM_HEADS, :], mk_ref, mv_ref)


def _swa_step(rows, cache_k, cache_v, j, mem_k, mem_v, i, sinks, carried):
    bsz = rows.shape[0]
    wb = cache_k.shape[2]
    cache = pl.BlockSpec((None, 1, wb, SWA_KV_HEADS, HEAD_DIM), lambda b: (j, b, 0, 0, 0))
    mem = pl.BlockSpec((None, 1, MEM_LEN, MEM_HEADS, HEAD_DIM), lambda b: (i, b, 0, 0, 0))
    nrows = D_MODEL // HEAD_DIM
    sink_gk = sinks.reshape(SWA_KV_HEADS, SWA_GROUP).T.reshape(SWA_GROUP, SWA_KV_HEADS, 1)
    in_specs = [
        pl.BlockSpec(sink_gk.shape, lambda b: (0, 0, 0)),
        pl.BlockSpec((1,) + rows.shape[1:], lambda b: (b, 0, 0)),
        cache, cache, mem, mem,
    ]
    n_in = len(in_specs)
    carry_specs, carry_args = _carry(carried)
    return pl.pallas_call(
        _drop_carried(_swa_step_kernel, n_in, len(carry_args)),
        grid=(bsz,),
        in_specs=in_specs + carry_specs,
        out_specs=[pl.BlockSpec((1, nrows, HEAD_DIM), lambda b: (b, 0, 0)), cache, cache],
        out_shape=[
            jax.ShapeDtypeStruct((bsz, nrows, HEAD_DIM), F32),
            jax.ShapeDtypeStruct(cache_k.shape, F32),
            jax.ShapeDtypeStruct(cache_v.shape, F32),
        ],
        input_output_aliases={n_in + c: 1 + c for c in range(len(carry_args))},
        compiler_params=_params("parallel"),
        name="swa_step",
    )(sink_gk, rows, cache_k, cache_v, mem_k, mem_v, *carry_args)


def _ret_step_kernel(dec_ref, rows_ref, cols_ref, s_ref, mk_ref, mv_ref, o_ref, s_out_ref):
    krow0 = RET_HEADS
    vrow0 = 2 * RET_HEADS
    grow0 = vrow0 + 2 * RET_HEADS
    mrow0 = grow0 + 2 * RET_HEADS
    for h in range(RET_HEADS):
        q_row = rows_ref[0, h:h + 1, :]
        k_row = rows_ref[0, krow0 + h:krow0 + h + 1, :] * RET_K_SCALE
        q_col = cols_ref[0, :, h:h + 1]
        k_col = cols_ref[0, :, krow0 + h:krow0 + h + 1] * RET_K_SCALE
        inner = jnp.sum(q_row * k_row, axis=1, keepdims=True) * dec_ref[0, h]
        qd = q_col * dec_ref[1, h]
        kd = k_col * dec_ref[2, h]
        halves = []
        for t in range(2):
            lsl = slice(t * HEAD_DIM, (t + 1) * HEAD_DIM)
            v = rows_ref[0, vrow0 + 2 * h + t:vrow0 + 2 * h + t + 1, :]
            st = s_ref[0, h, :, lsl]
            halves.append(inner * v + jnp.sum(qd * st, axis=0, keepdims=True))
            s_out_ref[0, h, :, lsl] = dec_ref[3, h] * st + kd * v
        mu = (jnp.sum(halves[0], axis=1, keepdims=True)
              + jnp.sum(halves[1], axis=1, keepdims=True)) * (1.0 / RET_DV)
        cen = [o - mu for o in halves]
        var = (jnp.sum(cen[0] * cen[0], axis=1, keepdims=True)
               + jnp.sum(cen[1] * cen[1], axis=1, keepdims=True)) * (1.0 / RET_DV)
        rstd = lax.rsqrt(var + HEAD_NORM_EPS)
        for t in range(2):
            gate = rows_ref[0, grow0 + 2 * h + t:grow0 + 2 * h + t + 1, :]
            o_ref[0, 2 * h + t:2 * h + t + 1, :] = jax.nn.silu(gate) * (cen[t] * rstd)
    o_ref[0, 2 * RET_HEADS:2 * RET_HEADS + MEM_HEADS, :] = _mem_attend_row(
        rows_ref[0, mrow0:mrow0 + MEM_HEADS, :], mk_ref, mv_ref)


def _ret_step(rows, cols, state, j, mem_k, mem_v, i, dec, carried):
    bsz = rows.shape[0]
    st = pl.BlockSpec((None, 1, RET_HEADS, RET_DK, RET_DV), lambda b: (j, b, 0, 0, 0))
    mem = pl.BlockSpec((None, 1, MEM_LEN, MEM_HEADS, HEAD_DIM), lambda b: (i, b, 0, 0, 0))
    nrows = D_MODEL // HEAD_DIM
    in_specs = [
        pl.BlockSpec(memory_space=pltpu.SMEM),
        pl.BlockSpec((1,) + rows.shape[1:], lambda b: (b, 0, 0)),
        pl.BlockSpec((1,) + cols.shape[1:], lambda b: (b, 0, 0)),
        st, mem, mem,
    ]
    n_in = len(in_specs)
    carry_specs, carry_args = _carry(carried)
    return pl.pallas_call(
        _drop_carried(_ret_step_kernel, n_in, len(carry_args)),
        grid=(bsz,),
        in_specs=in_specs + carry_specs,
        out_specs=[pl.BlockSpec((1, nrows, HEAD_DIM), lambda b: (b, 0, 0)), st],
        out_shape=[
            jax.ShapeDtypeStruct((bsz, nrows, HEAD_DIM), F32),
            jax.ShapeDtypeStruct(state.shape, F32),
        ],
        input_output_aliases={n_in + c: 1 + c for c in range(len(carry_args))},
        compiler_params=_params("parallel"),
        name="ret_step",
    )(dec, rows, cols, state, mem_k, mem_v, *carry_args)


def _rope_tables(pos):
    half = ROPE_DIM // 2
    inv = ROPE_THETA ** (-jnp.arange(half, dtype=F32) / half)
    ang = pos.astype(F32)[:, None] * inv[None, :]
    cos, sin = jnp.cos(ang), jnp.sin(ang)
    n = pos.shape[0]
    rest = HEAD_DIM - ROPE_DIM
    c = jnp.concatenate([cos, cos, jnp.ones((n, rest), F32)], axis=-1)
    a = jnp.concatenate([-sin, jnp.zeros((n, HEAD_DIM - half), F32)], axis=-1)
    b = jnp.concatenate([jnp.zeros((n, half), F32), sin, jnp.zeros((n, rest), F32)], axis=-1)
    return (c, a, b), (HEAD_DIM - half, half)


def _ret_rot_tables(pos):
    half = RET_DK // 2
    angle = RET_ROT_BASE ** (-jnp.linspace(0.0, 1.0, half, dtype=F32))
    ang = pos.astype(F32)[:, None] * angle[None, :]
    cos, sin = jnp.cos(ang), jnp.sin(ang)
    n = pos.shape[0]
    zero = jnp.zeros_like(sin)
    c = jnp.stack([cos, cos], axis=-1).reshape(n, RET_DK)
    a = jnp.stack([-sin, zero], axis=-1).reshape(n, RET_DK)
    b = jnp.stack([zero, sin], axis=-1).reshape(n, RET_DK)
    return (c, a, b), (RET_DK - 1, 1)


def _ret_decay(chunk):
    log_g = jnp.log1p(-jnp.exp2(-5.0 - jnp.arange(RET_HEADS, dtype=F32)))
    n = jnp.arange(chunk, dtype=F32)
    rel = n[:, None] - n[None, :]
    decay = jnp.where(rel >= 0, jnp.exp(jnp.maximum(rel, 0.0) * log_g[:, None, None]), 0.0)
    q_dec = jnp.exp((n + 1.0) * log_g[:, None])
    k_dec = jnp.exp((chunk - 1.0 - n) * log_g[:, None])
    c_dec = jnp.exp(chunk * log_g)
    return decay, q_dec, k_dec, c_dec


def kernel(x_prompt, x_sample, cache_swa_k, cache_swa_v, state_ret, cache_mem_k, cache_mem_v,
           mem_prompt, ln_g, ln_b, ffn_w_gu, ffn_w_down, w_mem_kv, swa_w_in, swa_w_out,
           swa_sinks, ret_w_in, ret_w_out):
    bp, seq, _ = x_prompt.shape
    bs, sample_seq, _ = x_sample.shape
    assert sample_seq == 1
    mp, ms = bp * seq, bs
    assert mp % ms == 0

    mem2 = mem_prompt.reshape(bp * MEM_LEN, D_MODEL)
    mem_kv = _mem_kv(mem2, w_mem_kv).reshape(DEPTH, bp, MEM_LEN, 2 * MEM_DIM)
    mem_k_prompt = mem_kv[..., :MEM_DIM].reshape(DEPTH, bp, MEM_LEN, MEM_HEADS, HEAD_DIM)
    mem_v_prompt = mem_kv[..., MEM_DIM:].reshape(DEPTH, bp, MEM_LEN, MEM_HEADS, HEAD_DIM)

    tm, tiles = _row_tiles(mp + ms, FFN_MAX_ROWS)
    m_all = tm * tiles
    tm_proj, _ = _row_tiles(m_all, PROJ_MAX_ROWS)
    pad = m_all - mp - ms
    tail = (mp - (tiles - 1) * tm, ms)
    assert tail[0] >= 0 and tail[0] % 8 == 0 and ms % 8 == 0 and tail[0] + ms <= tm
    def per_row(table):
        return jnp.concatenate([jnp.tile(table[:seq], (bp, 1)),
                                jnp.broadcast_to(table[seq:], (ms, table.shape[1])),
                                jnp.zeros((pad, table.shape[1]), F32)], axis=0)

    pos = jnp.concatenate([jnp.arange(seq, dtype=jnp.int32), jnp.full((1,), PAST_LEN, jnp.int32)])
    rope_tabs, rope_shifts = _rope_tables(pos)
    rot_tabs, rot_shifts = _ret_rot_tables(pos)
    rope_tabs = tuple(per_row(t) for t in rope_tabs)
    rot_tabs = tuple(per_row(t) for t in rot_tabs)
    decay, q_dec, k_dec, c_dec = _ret_decay(RET_CHUNK)
    ret_tabs = (decay, jnp.broadcast_to(q_dec[:, :, None], decay.shape),
                jnp.broadcast_to(k_dec[:, :, None], decay.shape), c_dec)
    decay, q_dec, k_dec, c_dec = _ret_decay(sample_seq)
    step_dec = jnp.stack([decay[:, 0, 0], q_dec[:, 0], k_dec[:, 0], c_dec])

    def ln(i, s):
        return ln_g[i, s].reshape(1, D_MODEL), ln_b[i, s].reshape(1, D_MODEL)

    swa_w_out_b, ret_w_out_b = swa_w_out.astype(BF16), ret_w_out.astype(BF16)
    swa_k_prompt, swa_v_prompt = [], []
    swa_sample = [jnp.zeros_like(cache_swa_k), jnp.zeros_like(cache_swa_v)]
    ret_prompt = jnp.zeros((DEPTH // 2, bp, RET_HEADS, RET_DK, RET_DV), F32)
    ret_sample = jnp.zeros_like(state_ret)
    for i in range(DEPTH):
        j = i // 2
        if i == 0:
            x = _ffn(x_prompt.reshape(mp, D_MODEL), ffn_w_gu, ffn_w_down, i, 0, *ln(i, 0), tm, tiles, tail,
                     x_sample=x_sample.reshape(ms, D_MODEL))
        else:
            x = _ffn(x, ffn_w_gu, ffn_w_down, i, 0, *ln(i, 0), tm, tiles, tail)
        if i % 2 == 0:
            qkv = _proj(x, swa_w_in, j, tm_proj, rot_heads=SWA_Q_HEADS + SWA_KV_HEADS, tables=rope_tabs,
                        shifts=rope_shifts)
            x, k_tail, v_tail = _swa_prompt(qkv, bp, seq, mem_kv, i, swa_sinks[j], swa_w_out_b, x, *ln(i, 1))
            swa_k_prompt.append(k_tail.reshape(bp, WINDOW, SWA_KV_HEADS, HEAD_DIM))
            swa_v_prompt.append(v_tail.reshape(bp, WINDOW, SWA_KV_HEADS, HEAD_DIM))
            rows = qkv[mp:mp + ms].reshape(ms, SWA_IN_WIDTH // HEAD_DIM, HEAD_DIM)
            att, *swa_sample = _swa_step(rows, cache_swa_k, cache_swa_v, j, cache_mem_k, cache_mem_v, i,
                                         swa_sinks[j], swa_sample)
            x = _out_ln(att.reshape(ms, D_MODEL), swa_w_out, j, x, *ln(i, 1), mp)
        else:
            qkvg = _proj(x, ret_w_in, j, tm_proj, rot_heads=2 * RET_HEADS, tables=rot_tabs,
                         shifts=rot_shifts)
            x, ret_prompt = _ret_prompt(qkvg, bp, seq, mem_kv, i, ret_tabs, ret_w_out_b, x, *ln(i, 1),
                                        [ret_prompt])
            rows = qkvg[mp:mp + ms].reshape(ms, RET_IN_WIDTH // HEAD_DIM, HEAD_DIM)
            att, ret_sample = _ret_step(rows, jnp.swapaxes(rows, 1, 2), state_ret, j, cache_mem_k,
                                        cache_mem_v, i, step_dec,
                                        [ret_sample])
            x = _out_ln(att.reshape(ms, D_MODEL), ret_w_out, j, x, *ln(i, 1), mp)
        x = _ffn(x, ffn_w_gu, ffn_w_down, i, 1, *ln(i, 2), tm, tiles, tail, split_out=(i == DEPTH - 1))

    y_prompt = x[0].reshape(bp, seq, D_MODEL)
    y_sample = x[1].reshape(bs, sample_seq, D_MODEL)
    return (y_prompt, y_sample, jnp.stack(swa_k_prompt), jnp.stack(swa_v_prompt), swa_sample[0],
            swa_sample[1], ret_prompt, ret_sample, mem_k_prompt, mem_v_prompt)
```

```python
import functools

import jax
import jax.numpy as jnp
from jax import lax
from jax.experimental import pallas as pl
from jax.experimental.pallas import tpu as pltpu

F32 = jnp.float32
BF16 = jnp.bfloat16

D_MODEL = 2048
DEPTH = 4
PAST_LEN = 16384
HEAD_DIM = 128
MEM_LEN = 256
MEM_HEADS = 4
MEM_DIM = MEM_HEADS * HEAD_DIM
SELF_WIDTH = D_MODEL - MEM_DIM
SWA_Q_HEADS = SELF_WIDTH // HEAD_DIM
SWA_KV_HEADS = SWA_Q_HEADS // 3
SWA_GROUP = SWA_Q_HEADS // SWA_KV_HEADS
WINDOW = 128
ROPE_THETA = 500000.0
ROPE_DIM = HEAD_DIM // 4
RET_DK = 128
RET_DV = 2 * RET_DK
RET_HEADS = SELF_WIDTH // RET_DV
RET_CHUNK = 128
RET_ROT_BASE = 10000.0
D_FF = ((8 * D_MODEL // 3 + 255) // 256) * 256
LN_EPS = 1e-5
HEAD_NORM_EPS = 1e-6
ALPHA = (2.0 * DEPTH) ** 0.25
NEG_INF = -1e30
ATT_SCALE = HEAD_DIM ** -0.5
RET_K_SCALE = RET_DK ** -0.5

SWA_Q = SWA_Q_HEADS * HEAD_DIM
SWA_KV = SWA_KV_HEADS * HEAD_DIM
SWA_IN_WIDTH = SWA_Q + 2 * SWA_KV + MEM_DIM
RET_QK = RET_HEADS * RET_DK
RET_V = RET_HEADS * RET_DV
RET_IN_WIDTH = 2 * RET_QK + 2 * RET_V + MEM_DIM

VMEM_LIMIT_BYTES = 58 * 1024 * 1024
LANES = 128
FFN_TILE_K = 256
FFN_MAX_ROWS = 1040
PROJ_MAX_ROWS = 1040
ATT_BLOCKS_PER_STEP = 4
PROJ_TILE_N = 1024
OUT_TILE_N = 512
MAX_ROW_CHUNK = 256
BF16_SUBLANES = 16


def _row_tiles(rows, max_rows):
    tiles = -(-rows // max_rows)
    per_tile = -(-rows // tiles)
    return -(-per_tile // BF16_SUBLANES) * BF16_SUBLANES, tiles


def _row_chunks(tm):
    count = -(-tm // MAX_ROW_CHUNK)
    size = -(-(-(-tm // count)) // BF16_SUBLANES) * BF16_SUBLANES
    return [slice(r, min(r + size, tm)) for r in range(0, tm, size)]


def _params(*sem):
    return pltpu.CompilerParams(dimension_semantics=sem, vmem_limit_bytes=VMEM_LIMIT_BYTES)


def _layer_norm_rows(z, g, b):
    mu = jnp.mean(z, axis=-1, keepdims=True)
    zc = z - mu
    var = jnp.mean(zc * zc, axis=-1, keepdims=True)
    return zc * lax.rsqrt(var + LN_EPS) * g + b


def _ffn_kernel(*refs, nk, tk, tiles, tail, split_in, split_out):
    refs = list(refs)
    x_ref = refs.pop(0)
    xs_ref = refs.pop(0) if split_in else None
    wg_ref, wu_ref, wd_ref, g_ref, b_ref, o_ref = refs[:6]
    os_ref = refs[6] if split_out else None
    xb_ref, wgu_ref = refs[-2:]
    i = pl.program_id(0)
    k = pl.program_id(1)
    chunks = _row_chunks(o_ref.shape[0])

    def load_x(rows):
        xv = x_ref[rows, :]
        if not split_in or rows.stop <= tail[0]:
            return xv
        first, count = tail
        lo, hi = rows.start, rows.stop
        pieces = []
        if lo < first:
            pieces.append(xv[:first - lo])
        s0, s1 = max(lo, first), min(hi, first + count)
        if s0 < s1:
            pieces.append(xs_ref[s0 - first:s1 - first, :])
        if hi > first + count:
            pieces.append(jnp.zeros((hi - max(lo, first + count), xv.shape[1]), xv.dtype))
        return jnp.where(i == tiles - 1, jnp.concatenate(pieces, axis=0), xv)

    def cast_weights():
        wgu_ref[:, :tk] = wg_ref[...].astype(BF16)
        wgu_ref[:, tk:] = wu_ref[...].astype(BF16)
        return wd_ref[...].astype(BF16)

    def down(xb, wd):
        gu = jnp.dot(xb, wgu_ref[...], preferred_element_type=F32)
        h = (jax.nn.silu(gu[:, :tk]) * gu[:, tk:]).astype(BF16)
        return jnp.dot(h, wd, preferred_element_type=F32)

    @pl.when(k == 0)
    def _():
        wd = cast_weights()
        for rows in chunks:
            xb = load_x(rows).astype(BF16)
            xb_ref[rows, :] = xb
            o_ref[rows, :] = down(xb, wd)

    @pl.when(jnp.logical_and(k > 0, k < nk - 1))
    def _():
        wd = cast_weights()
        o_ref[...] += down(xb_ref[...], wd)

    @pl.when(k == nk - 1)
    def _():
        wd = cast_weights()
        for rows in chunks:
            acc = o_ref[rows, :] + down(xb_ref[rows, :], wd)
            z = ALPHA * load_x(rows) + 0.5 * acc
            o_ref[rows, :] = _layer_norm_rows(z, g_ref[...], b_ref[...])
        if split_out:
            @pl.when(i == tiles - 1)
            def _():
                os_ref[...] = o_ref[tail[0]:tail[0] + tail[1], :]


def _ffn(x, w_gu, w_down, layer, slot, g, b, tm, tiles, tail, x_sample=None, split_out=False):
    tk = FFN_TILE_K
    nk = D_FF // tk
    assert nk >= 2
    split_in = x_sample is not None
    rows = pl.BlockSpec((tm, D_MODEL), lambda i, k: (i, 0))
    small = pl.BlockSpec((tail[1], D_MODEL), lambda i, k: (0, 0))
    in_specs = [pl.BlockSpec((tm, D_MODEL), lambda i, k: (i, 0), pipeline_mode=pl.Buffered(1))]
    args = [x]
    if split_in:
        in_specs.append(small)
        args.append(x_sample)
    in_specs += [
        pl.BlockSpec((None, None, D_MODEL, tk), lambda i, k: (layer, slot, 0, k)),
        pl.BlockSpec((None, None, D_MODEL, tk), lambda i, k: (layer, slot, 0, nk + k)),
        pl.BlockSpec((None, None, tk, D_MODEL), lambda i, k: (layer, slot, k, 0)),
        pl.BlockSpec((1, D_MODEL), lambda i, k: (0, 0)),
        pl.BlockSpec((1, D_MODEL), lambda i, k: (0, 0)),
    ]
    if split_out:
        prompt_rows = (tiles - 1) * tm + tail[0]
        out_specs = [rows, small]
        out_shape = [jax.ShapeDtypeStruct((prompt_rows, D_MODEL), F32),
                     jax.ShapeDtypeStruct((tail[1], D_MODEL), F32)]
    else:
        out_specs = rows
        out_shape = jax.ShapeDtypeStruct((tiles * tm, D_MODEL), F32)
    return pl.pallas_call(
        functools.partial(_ffn_kernel, nk=nk, tk=tk, tiles=tiles, tail=tail,
                          split_in=split_in, split_out=split_out),
        grid=(tiles, nk),
        in_specs=in_specs,
        out_specs=out_specs,
        out_shape=out_shape,
        scratch_shapes=[pltpu.VMEM((tm, D_MODEL), BF16), pltpu.VMEM((D_MODEL, 2 * tk), BF16)],
        compiler_params=_params("arbitrary", "arbitrary"),
        name="ffn_ln",
    )(*args, w_gu, w_gu, w_down, g, b)


def _proj_kernel(x_ref, w_ref, *rest, rot_heads, shifts):
    if rot_heads:
        c_ref, a_ref, b_ref, o_ref, xb_ref = rest
    else:
        o_ref, xb_ref = rest
    j = pl.program_id(1)

    @pl.when(j == 0)
    def _():
        xb_ref[...] = x_ref[...].astype(BF16)

    def plain():
        o_ref[...] = jnp.dot(xb_ref[...], w_ref[...].astype(BF16), preferred_element_type=F32)

    if not rot_heads:
        plain()
        return

    tm, tn = o_ref.shape
    heads = tn // LANES
    full_tiles, part_heads = divmod(rot_heads, heads)

    def rotated(n_heads):
        wb = w_ref[...].astype(BF16)
        for rows in _row_chunks(tm):
            y = jnp.dot(xb_ref[rows, :], wb, preferred_element_type=F32)
            c, a, b = c_ref[rows, :], a_ref[rows, :], b_ref[rows, :]
            for h in range(heads):
                yh = y[:, h * LANES:(h + 1) * LANES]
                if h < n_heads:
                    yh = yh * c + pltpu.roll(yh, shifts[0], 1) * a + pltpu.roll(yh, shifts[1], 1) * b
                o_ref[rows, h * LANES:(h + 1) * LANES] = yh

    pl.when(j < full_tiles)(lambda: rotated(heads))
    if part_heads:
        pl.when(j == full_tiles)(lambda: rotated(part_heads))
    pl.when(j >= full_tiles + (1 if part_heads else 0))(plain)


def _proj(x, w, layer, tm, rot_heads=0, tables=None, shifts=None):
    m = x.shape[0]
    n = w.shape[-1]
    tn = PROJ_TILE_N
    in_specs = [
        pl.BlockSpec((tm, D_MODEL), lambda i, j: (i, 0)),
        pl.BlockSpec((None, D_MODEL, tn), lambda i, j: (layer, 0, j)),
    ]
    args = [x, w]
    if rot_heads:
        tab = pl.BlockSpec((tm, LANES), lambda i, j: (i, 0))
        in_specs += [tab, tab, tab]
        args += list(tables)
    return pl.pallas_call(
        functools.partial(_proj_kernel, rot_heads=rot_heads, shifts=shifts),
        grid=(pl.cdiv(m, tm), n // tn),
        in_specs=in_specs,
        out_specs=pl.BlockSpec((tm, tn), lambda i, j: (i, j)),
        out_shape=jax.ShapeDtypeStruct((m, n), F32),
        scratch_shapes=[pltpu.VMEM((tm, D_MODEL), BF16)],
        compiler_params=_params("parallel", "arbitrary"),
        name="proj",
    )(*args)


def _mem_kv_kernel(x_ref, w_ref, o_ref, xb_ref):
    @pl.when(pl.program_id(0) == 0)
    def _():
        xb_ref[...] = x_ref[...].astype(BF16)

    o_ref[...] = jnp.dot(xb_ref[...], w_ref[...].astype(BF16), preferred_element_type=F32)


def _mem_kv(mem, w):
    rows = mem.shape[0]
    layers, _, n = w.shape
    return pl.pallas_call(
        _mem_kv_kernel,
        grid=(layers,),
        in_specs=[
            pl.BlockSpec((rows, D_MODEL), lambda l: (0, 0)),
            pl.BlockSpec((None, D_MODEL, n), lambda l: (l, 0, 0)),
        ],
        out_specs=pl.BlockSpec((None, rows, n), lambda l: (l, 0, 0)),
        out_shape=jax.ShapeDtypeStruct((layers, rows, n), F32),
        scratch_shapes=[pltpu.VMEM((rows, D_MODEL), BF16)],
        compiler_params=_params("arbitrary"),
        name="mem_kv",
    )(mem, w)


def _out_ln_kernel(att_ref, w_ref, x_ref, g_ref, b_ref, o_ref, y_ref, *, nn, tn):
    n = pl.program_id(1)
    y_ref[n] = jnp.dot(att_ref[...].astype(BF16), w_ref[...].astype(BF16),
                       preferred_element_type=F32)

    @pl.when(n == nn - 1)
    def _():
        cols = [slice(c * tn, (c + 1) * tn) for c in range(nn)]
        total = None
        for c in range(nn):
            z = ALPHA * x_ref[:, cols[c]] + y_ref[c]
            y_ref[c] = z
            part = jnp.sum(z, axis=-1, keepdims=True)
            total = part if total is None else total + part
        mu = total * (1.0 / D_MODEL)
        total = None
        for c in range(nn):
            zc = y_ref[c] - mu
            part = jnp.sum(zc * zc, axis=-1, keepdims=True)
            total = part if total is None else total + part
        rstd = lax.rsqrt(total * (1.0 / D_MODEL) + LN_EPS)
        for c in range(nn):
            o_ref[:, cols[c]] = (y_ref[c] - mu) * rstd * g_ref[:, cols[c]] + b_ref[:, cols[c]]


def _out_ln(att, w_out, layer, x, g, b, row0):
    tm = att.shape[0]
    assert row0 % tm == 0
    tn = OUT_TILE_N
    nn = D_MODEL // tn
    rows = pl.BlockSpec((tm, D_MODEL), lambda i, n: (row0 // tm, 0))
    return pl.pallas_call(
        functools.partial(_out_ln_kernel, nn=nn, tn=tn),
        grid=(1, nn),
        in_specs=[
            pl.BlockSpec((tm, D_MODEL), lambda i, n: (0, 0)),
            pl.BlockSpec((None, D_MODEL, tn), lambda i, n: (layer, 0, n)),
            rows,
            pl.BlockSpec((1, D_MODEL), lambda i, n: (0, 0)),
            pl.BlockSpec((1, D_MODEL), lambda i, n: (0, 0)),
        ],
        out_specs=rows,
        out_shape=jax.ShapeDtypeStruct(x.shape, F32),
        scratch_shapes=[pltpu.VMEM((nn, tm, tn), F32)],
        input_output_aliases={2: 0},
        compiler_params=_params("arbitrary", "arbitrary"),
        name="out_ln",
    )(att, w_out, x, g, b)


def _dot_nt(a, b):
    return lax.dot_general(a, b, (((1,), (1,)), ((), ())), preferred_element_type=F32)


def _out_proj_ln(att_ref, w_ref, x_ref, g_ref, b_ref, o_ref):
    y = jnp.dot(att_ref[...], w_ref[...], preferred_element_type=F32)
    o_ref[...] = _layer_norm_rows(ALPHA * x_ref[...] + y, g_ref[...], b_ref[...])


def _normalised_pv(p, v, extra=None):
    width = v.shape[1]
    v_ones = jnp.concatenate([v, jnp.ones((p.shape[1], width), p.dtype)], axis=1)
    pv = jnp.dot(p, v_ones, preferred_element_type=F32)
    den = pv[:, width:]
    if extra is not None:
        den = den + extra
    return pv[:, :width] * (1.0 / den)


def _mem_attend_block(qm_ref, mk_ref, mv_ref, att_ref, col0):
    for h in range(MEM_HEADS):
        sl = slice(h * HEAD_DIM, (h + 1) * HEAD_DIM)
        q = qm_ref[:, sl].astype(BF16)
        s = _dot_nt(q, mk_ref[0, :, sl].astype(BF16)) * ATT_SCALE
        p = jnp.exp(s - jnp.max(s, axis=-1, keepdims=True))
        p = p * (1.0 / jnp.sum(p, axis=-1, keepdims=True))
        o = jnp.dot(p.astype(BF16), mv_ref[0, :, sl].astype(BF16), preferred_element_type=F32)
        att_ref[:, col0 + h * HEAD_DIM:col0 + (h + 1) * HEAD_DIM] = o.astype(att_ref.dtype)


def _swa_prompt_kernel(sink_ref, q_ref, kc_ref, kp_ref, vc_ref, vp_ref, qm_ref, mk_ref, mv_ref,
                       w_ref, x_ref, g_ref, b_ref, o_ref, kt_ref, vt_ref, s_ref, p_ref, att_ref,
                       *, nblk, steps):
    n = pl.program_id(1)
    nq = SWA_Q_HEADS
    w2 = 2 * WINDOW

    def prev_cur(cur_ref, prev_ref, i, sl):
        rows = slice(i * WINDOW, (i + 1) * WINDOW)
        before = prev_ref[:, sl] if i == 0 else cur_ref[(i - 1) * WINDOW:i * WINDOW, sl]
        return jnp.concatenate([before, cur_ref[rows, sl]], axis=0).astype(BF16)

    for i in range(nblk):
        rows = slice(i * WINDOW, (i + 1) * WINDOW)
        for h in range(SWA_KV_HEADS):
            sl = slice(h * HEAD_DIM, (h + 1) * HEAD_DIM)
            q3 = jnp.concatenate(
                [q_ref[rows, (h * SWA_GROUP + g) * HEAD_DIM:(h * SWA_GROUP + g + 1) * HEAD_DIM]
                 for g in range(SWA_GROUP)], axis=0).astype(BF16)
            s = _dot_nt(q3, prev_cur(kc_ref, kp_ref, i, sl)) * ATT_SCALE
            s_ref[i, h * SWA_GROUP:(h + 1) * SWA_GROUP] = s.reshape(SWA_GROUP, WINDOW, w2)
    for h in range(MEM_HEADS):
        sl = slice(h * HEAD_DIM, (h + 1) * HEAD_DIM)
        sm = _dot_nt(qm_ref[:, sl].astype(BF16), mk_ref[0, :, sl].astype(BF16)) * ATT_SCALE
        s_ref[:, nq + h] = sm.reshape(nblk, WINDOW, w2)

    qi = lax.broadcasted_iota(jnp.int32, (WINDOW, w2), 0)
    kj = lax.broadcasted_iota(jnp.int32, (WINDOW, w2), 1)
    sink = sink_ref[...]
    sink_terms = []
    for i in range(nblk):
        first_key = jnp.where(n > 0, qi, WINDOW) if i == 0 else qi
        ok = (kj >= first_key) & (kj <= qi + WINDOW)
        s = jnp.where(ok[None], s_ref[i, 0:nq], NEG_INF)
        m = jnp.maximum(jnp.max(s, axis=-1, keepdims=True), sink)
        p_ref[i, 0:nq] = jnp.exp(s - m).astype(BF16)
        sink_terms.append(jnp.exp(sink - m))
    s = s_ref[:, nq:nq + MEM_HEADS]
    p_ref[:, nq:nq + MEM_HEADS] = jnp.exp(s - jnp.max(s, axis=-1, keepdims=True)).astype(BF16)

    for i in range(nblk):
        rows = slice(i * WINDOW, (i + 1) * WINDOW)
        for h in range(SWA_KV_HEADS):
            sl = slice(h * HEAD_DIM, (h + 1) * HEAD_DIM)
            group = slice(h * SWA_GROUP, (h + 1) * SWA_GROUP)
            p3 = p_ref[i, group].reshape(SWA_GROUP * WINDOW, w2)
            o = _normalised_pv(p3, prev_cur(vc_ref, vp_ref, i, sl),
                               sink_terms[i][group].reshape(SWA_GROUP * WINDOW, 1))
            for g in range(SWA_GROUP):
                hq = h * SWA_GROUP + g
                att_ref[rows, hq * HEAD_DIM:(hq + 1) * HEAD_DIM] = (
                    o[g * WINDOW:(g + 1) * WINDOW].astype(BF16))
    for h in range(MEM_HEADS):
        sl = slice(h * HEAD_DIM, (h + 1) * HEAD_DIM)
        pm = p_ref[:, nq + h].reshape(nblk * WINDOW, w2)
        om = _normalised_pv(pm, mv_ref[0, :, sl].astype(BF16))
        att_ref[:, SWA_Q + h * HEAD_DIM:SWA_Q + (h + 1) * HEAD_DIM] = om.astype(BF16)

    @pl.when(n == steps - 1)
    def _():
        last = slice((nblk - 1) * WINDOW, nblk * WINDOW)
        kt_ref[0] = kc_ref[last, :]
        vt_ref[0] = vc_ref[last, :]

    _out_proj_ln(att_ref, w_ref, x_ref, g_ref, b_ref, o_ref)


def _mixer_tail_specs(layer, row_of, rows):
    mem = (None, 1, MEM_LEN, MEM_DIM)
    vec = pl.BlockSpec((1, D_MODEL), lambda b, n: (0, 0))
    mixer = layer // 2
    return [
        pl.BlockSpec(mem, lambda b, n: (layer, b, 0, 0)),
        pl.BlockSpec(mem, lambda b, n: (layer, b, 0, 1)),
        pl.BlockSpec((None, D_MODEL, D_MODEL), lambda b, n: (mixer, 0, 0), pipeline_mode=pl.Buffered(1)),
        pl.BlockSpec((rows, D_MODEL), lambda b, n: (row_of(b, n), 0)),
        vec, vec,
    ]


def _swa_prompt(qkv, bsz, seq, mem_kv, layer, sinks, w_out, x, g, b):
    nblk = ATT_BLOCKS_PER_STEP
    rows = nblk * WINDOW
    steps = seq // rows
    assert seq % rows == 0 and seq >= WINDOW
    tail = pl.BlockSpec((1, WINDOW, SWA_KV), lambda b, n: (b, 0, 0))
    blocks_per_seq = seq // WINDOW
    kcol = SWA_Q // SWA_KV
    vcol = kcol + 1
    mcol = vcol + 1
    row_of = lambda b, n: b * steps + n
    prev_of = lambda b, n: b * blocks_per_seq + jnp.maximum(n * nblk - 1, 0)
    cur = lambda w, col: pl.BlockSpec((rows, w), lambda b, n: (row_of(b, n), col))
    prev = lambda col: pl.BlockSpec((WINDOW, SWA_KV), lambda b, n: (prev_of(b, n), col))
    nheads = SWA_Q_HEADS + MEM_HEADS
    assert MEM_LEN == 2 * WINDOW
    in_specs = [
        pl.BlockSpec((SWA_Q_HEADS, 1, 1), lambda b, n: (0, 0, 0)),
        cur(SWA_Q, 0), cur(SWA_KV, kcol), prev(kcol), cur(SWA_KV, vcol), prev(vcol), cur(MEM_DIM, mcol),
    ] + _mixer_tail_specs(layer, row_of, rows)
    return pl.pallas_call(
        functools.partial(_swa_prompt_kernel, nblk=nblk, steps=steps),
        grid=(bsz, steps),
        in_specs=in_specs,
        out_specs=[pl.BlockSpec((rows, D_MODEL), lambda b, n: (row_of(b, n), 0)), tail, tail],
        out_shape=[jax.ShapeDtypeStruct(x.shape, F32),
                   jax.ShapeDtypeStruct((bsz, WINDOW, SWA_KV), F32),
                   jax.ShapeDtypeStruct((bsz, WINDOW, SWA_KV), F32)],
        scratch_shapes=[pltpu.VMEM((nblk, nheads, WINDOW, 2 * WINDOW), F32),
                        pltpu.VMEM((nblk, nheads, WINDOW, 2 * WINDOW), BF16),
                        pltpu.VMEM((rows, D_MODEL), BF16)],
        input_output_aliases={len(in_specs) - 3: 0},
        compiler_params=_params("arbitrary", "arbitrary"),
        name="swa_prompt",
    )(sinks.reshape(SWA_Q_HEADS, 1, 1), qkv, qkv, qkv, qkv, qkv, qkv, mem_kv, mem_kv, w_out, x, g, b)


def _head_norm_gate(o, gate):
    mu = jnp.mean(o, axis=-1, keepdims=True)
    oc = o - mu
    var = jnp.mean(oc * oc, axis=-1, keepdims=True)
    return jax.nn.silu(gate) * (oc * lax.rsqrt(var + HEAD_NORM_EPS))


def _ret_prompt_kernel(cdec_ref, decay_ref, qdec_ref, kdec_ref, q_ref, k_ref, v_ref, gate_ref, qm_ref,
                       mk_ref, mv_ref, w_ref, x_ref, g_ref, b_ref, o_ref, s_out_ref,
                       state_ref, att_ref, *, steps, nblk):
    c = pl.program_id(1)

    @pl.when(c == 0)
    def _():
        state_ref[...] = jnp.zeros_like(state_ref)

    for i in range(nblk):
        rows = slice(i * RET_CHUNK, (i + 1) * RET_CHUNK)
        for h in range(RET_HEADS):
            ksl = slice(h * RET_DK, (h + 1) * RET_DK)
            vsl = slice(h * RET_DV, (h + 1) * RET_DV)
            qc = q_ref[rows, ksl]
            kc = k_ref[rows, ksl] * RET_K_SCALE
            vb = v_ref[rows, vsl].astype(BF16)
            st = state_ref[h]
            inner = _dot_nt(qc.astype(BF16), kc.astype(BF16)) * decay_ref[h]
            o = (jnp.dot(inner.astype(BF16), vb, preferred_element_type=F32)
                 + jnp.dot((qc * qdec_ref[h]).astype(BF16), st.astype(BF16),
                           preferred_element_type=F32))
            kd = (kc * kdec_ref[h]).astype(BF16)
            state_ref[h] = cdec_ref[h] * st + lax.dot_general(
                kd, vb, (((0,), (0,)), ((), ())), preferred_element_type=F32)
            att_ref[rows, vsl] = _head_norm_gate(o, gate_ref[rows, vsl]).astype(BF16)
    _mem_attend_block(qm_ref, mk_ref, mv_ref, att_ref, RET_V)

    @pl.when(c == steps - 1)
    def _():
        s_out_ref[0] = state_ref[...]

    _out_proj_ln(att_ref, w_ref, x_ref, g_ref, b_ref, o_ref)


def _drop_carried(kernel, first, count):
    def body(*refs):
        return kernel(*refs[:first], *refs[first + count:])
    return body


def _carry(prev_outputs):
    prev_outputs = list(prev_outputs)
    return [pl.BlockSpec(memory_space=pl.ANY)] * len(prev_outputs), prev_outputs


def _ret_prompt(qkvg, bsz, seq, mem_kv, layer, tables, w_out, x, g, b, carried):
    nblk = ATT_BLOCKS_PER_STEP
    rows = nblk * RET_CHUNK
    steps = seq // rows
    mixer = layer // 2
    decay, qdec, kdec, cdec = tables
    row_of = lambda b, n: b * steps + n
    cur = lambda w, col: pl.BlockSpec((rows, w), lambda b, n: (row_of(b, n), col))
    tab = pl.BlockSpec((RET_HEADS, RET_CHUNK, RET_CHUNK), lambda b, n: (0, 0, 0))
    in_specs = [
        pl.BlockSpec(memory_space=pltpu.SMEM), tab, tab, tab,
        cur(RET_QK, 0), cur(RET_QK, 1), cur(RET_V, 1), cur(RET_V, 2),
        cur(MEM_DIM, (2 * RET_QK + 2 * RET_V) // MEM_DIM),
    ] + _mixer_tail_specs(layer, row_of, rows)
    n_in = len(in_specs)
    carry_specs, carry_args = _carry(carried)
    aliases = {n_in - 3: 0}
    aliases.update({n_in + c: 1 + c for c in range(len(carry_args))})
    return pl.pallas_call(
        _drop_carried(functools.partial(_ret_prompt_kernel, steps=steps, nblk=nblk), n_in, len(carry_args)),
        grid=(bsz, steps),
        in_specs=in_specs + carry_specs,
        out_specs=[
            pl.BlockSpec((rows, D_MODEL), lambda b, n: (row_of(b, n), 0)),
            pl.BlockSpec((None, 1, RET_HEADS, RET_DK, RET_DV), lambda b, n: (mixer, b, 0, 0, 0)),
        ],
        out_shape=[
            jax.ShapeDtypeStruct(x.shape, F32),
            jax.ShapeDtypeStruct((DEPTH // 2, bsz, RET_HEADS, RET_DK, RET_DV), F32),
        ],
        scratch_shapes=[pltpu.VMEM((RET_HEADS, RET_DK, RET_DV), F32),
                        pltpu.VMEM((rows, D_MODEL), BF16)],
        input_output_aliases=aliases,
        compiler_params=_params("arbitrary", "arbitrary"),
        name="ret_prompt",
    )(cdec, decay, qdec, kdec, qkvg, qkvg, qkvg, qkvg, qkvg, mem_kv, mem_kv, w_out, x, g, b, *carry_args)


def _pair_tokens(a):
    t, h, d = a.shape
    a = a.reshape(t // 2, 2, h, d)
    return jnp.concatenate([a[:, 0], a[:, 1]], axis=1)


def _twice(x):
    return jnp.concatenate([x, x], axis=0)


def _fold(x):
    half = x.shape[0] // 2
    return x[:half] + x[half:]


def _mem_attend_row(q, mk_ref, mv_ref):
    mk = _pair_tokens(mk_ref[0])
    mv = _pair_tokens(mv_ref[0])
    s = jnp.sum(mk * _twice(q)[None], axis=-1, keepdims=True) * ATT_SCALE
    m = jnp.max(s, axis=0)
    m = jnp.maximum(m[:MEM_HEADS], m[MEM_HEADS:])
    p = jnp.exp(s - _twice(m)[None])
    den = _fold(jnp.sum(p, axis=0))
    return _fold(jnp.sum(p * mv, axis=0)) * (1.0 / den)


def _swa_step_kernel(sink_ref, rows_ref, kbuf_ref, vbuf_ref, mk_ref, mv_ref, o_ref, nk_ref, nv_ref):
    krow0 = SWA_Q_HEADS
    vrow0 = krow0 + SWA_KV_HEADS
    mrow0 = vrow0 + SWA_KV_HEADS
    wb = kbuf_ref.shape[1]
    kb = _pair_tokens(kbuf_ref[0])
    vb = _pair_tokens(vbuf_ref[0])
    k_new = rows_ref[0, krow0:krow0 + SWA_KV_HEADS, :]
    v_new = rows_ref[0, vrow0:vrow0 + SWA_KV_HEADS, :]
    for g in range(SWA_GROUP):
        group_rows = pl.ds(g, SWA_KV_HEADS, stride=SWA_GROUP)
        q = rows_ref[0, group_rows, :]
        s_buf = jnp.sum(kb * _twice(q)[None], axis=-1, keepdims=True) * ATT_SCALE
        s_new = jnp.sum(k_new * q, axis=-1, keepdims=True) * ATT_SCALE
        sink = sink_ref[g]
        m = jnp.max(s_buf, axis=0)
        m = jnp.maximum(jnp.maximum(m[:SWA_KV_HEADS], m[SWA_KV_HEADS:]), jnp.maximum(s_new, sink))
        p_buf = jnp.exp(s_buf - _twice(m)[None])
        p_new = jnp.exp(s_new - m)
        den = _fold(jnp.sum(p_buf, axis=0)) + p_new + jnp.exp(sink - m)
        o_ref[0, group_rows, :] = (_fold(jnp.sum(p_buf * vb, axis=0)) + p_new * v_new) * (1.0 / den)
    nk_ref[0, 0:wb - 1] = kbuf_ref[0, 1:wb]
    nv_ref[0, 0:wb - 1] = vbuf_ref[0, 1:wb]
    nk_ref[0, wb - 1] = k_new
    nv_ref[0, wb - 1] = v_new
    o_ref[0, SWA_Q_HEADS:SWA_Q_HEADS + MEM_HEADS, :] = _mem_attend_row(
        rows_ref[0, mrow0:mrow0 + MEM_HEADS, :], mk_ref, mv_ref)


def _swa_step(rows, cache_k, cache_v, j, mem_k, mem_v, i, sinks, carried):
    bsz = rows.shape[0]
    wb = cache_k.shape[2]
    cache = pl.BlockSpec((None, 1, wb, SWA_KV_HEADS, HEAD_DIM), lambda b: (j, b, 0, 0, 0))
    mem = pl.BlockSpec((None, 1, MEM_LEN, MEM_HEADS, HEAD_DIM), lambda b: (i, b, 0, 0, 0))
    nrows = D_MODEL // HEAD_DIM
    sink_gk = sinks.reshape(SWA_KV_HEADS, SWA_GROUP).T.reshape(SWA_GROUP, SWA_KV_HEADS, 1)
    in_specs = [
        pl.BlockSpec(sink_gk.shape, lambda b: (0, 0, 0)),
        pl.BlockSpec((1,) + rows.shape[1:], lambda b: (b, 0, 0)),
        cache, cache, mem, mem,
    ]
    n_in = len(in_specs)
    carry_specs, carry_args = _carry(carried)
    return pl.pallas_call(
        _drop_carried(_swa_step_kernel, n_in, len(carry_args)),
        grid=(bsz,),
        in_specs=in_specs + carry_specs,
        out_specs=[pl.BlockSpec((1, nrows, HEAD_DIM), lambda b: (b, 0, 0)), cache, cache],
        out_shape=[
            jax.ShapeDtypeStruct((bsz, nrows, HEAD_DIM), F32),
            jax.ShapeDtypeStruct(cache_k.shape, F32),
            jax.ShapeDtypeStruct(cache_v.shape, F32),
        ],
        input_output_aliases={n_in + c: 1 + c for c in range(len(carry_args))},
        compiler_params=_params("parallel"),
        name="swa_step",
    )(sink_gk, rows, cache_k, cache_v, mem_k, mem_v, *carry_args)


def _ret_step_kernel(dec_ref, rows_ref, cols_ref, s_ref, mk_ref, mv_ref, o_ref, s_out_ref):
    krow0 = RET_HEADS
    vrow0 = 2 * RET_HEADS
    grow0 = vrow0 + 2 * RET_HEADS
    mrow0 = grow0 + 2 * RET_HEADS
    for h in range(RET_HEADS):
        q_row = rows_ref[0, h:h + 1, :]
        k_row = rows_ref[0, krow0 + h:krow0 + h + 1, :] * RET_K_SCALE
        q_col = cols_ref[0, :, h:h + 1]
        k_col = cols_ref[0, :, krow0 + h:krow0 + h + 1] * RET_K_SCALE
        inner = jnp.sum(q_row * k_row, axis=1, keepdims=True) * dec_ref[0, h]
        qd = q_col * dec_ref[1, h]
        kd = k_col * dec_ref[2, h]
        halves = []
        for t in range(2):
            lsl = slice(t * HEAD_DIM, (t + 1) * HEAD_DIM)
            v = rows_ref[0, vrow0 + 2 * h + t:vrow0 + 2 * h + t + 1, :]
            st = s_ref[0, h, :, lsl]
            halves.append(inner * v + jnp.sum(qd * st, axis=0, keepdims=True))
            s_out_ref[0, h, :, lsl] = dec_ref[3, h] * st + kd * v
        mu = (jnp.sum(halves[0], axis=1, keepdims=True)
              + jnp.sum(halves[1], axis=1, keepdims=True)) * (1.0 / RET_DV)
        cen = [o - mu for o in halves]
        var = (jnp.sum(cen[0] * cen[0], axis=1, keepdims=True)
               + jnp.sum(cen[1] * cen[1], axis=1, keepdims=True)) * (1.0 / RET_DV)
        rstd = lax.rsqrt(var + HEAD_NORM_EPS)
        for t in range(2):
            gate = rows_ref[0, grow0 + 2 * h + t:grow0 + 2 * h + t + 1, :]
            o_ref[0, 2 * h + t:2 * h + t + 1, :] = jax.nn.silu(gate) * (cen[t] * rstd)
    o_ref[0, 2 * RET_HEADS:2 * RET_HEADS + MEM_HEADS, :] = _mem_attend_row(
        rows_ref[0, mrow0:mrow0 + MEM_HEADS, :], mk_ref, mv_ref)


def _ret_step(rows, cols, state, j, mem_k, mem_v, i, dec, carried):
    bsz = rows.shape[0]
    st = pl.BlockSpec((None, 1, RET_HEADS, RET_DK, RET_DV), lambda b: (j, b, 0, 0, 0))
    mem = pl.BlockSpec((None, 1, MEM_LEN, MEM_HEADS, HEAD_DIM), lambda b: (i, b, 0, 0, 0))
    nrows = D_MODEL // HEAD_DIM
    in_specs = [
        pl.BlockSpec(memory_space=pltpu.SMEM),
        pl.BlockSpec((1,) + rows.shape[1:], lambda b: (b, 0, 0)),
        pl.BlockSpec((1,) + cols.shape[1:], lambda b: (b, 0, 0)),
        st, mem, mem,
    ]
    n_in = len(in_specs)
    carry_specs, carry_args = _carry(carried)
    return pl.pallas_call(
        _drop_carried(_ret_step_kernel, n_in, len(carry_args)),
        grid=(bsz,),
        in_specs=in_specs + carry_specs,
        out_specs=[pl.BlockSpec((1, nrows, HEAD_DIM), lambda b: (b, 0, 0)), st],
        out_shape=[
            jax.ShapeDtypeStruct((bsz, nrows, HEAD_DIM), F32),
            jax.ShapeDtypeStruct(state.shape, F32),
        ],
        input_output_aliases={n_in + c: 1 + c for c in range(len(carry_args))},
        compiler_params=_params("parallel"),
        name="ret_step",
    )(dec, rows, cols, state, mem_k, mem_v, *carry_args)


def _rope_tables(pos):
    half = ROPE_DIM // 2
    inv = ROPE_THETA ** (-jnp.arange(half, dtype=F32) / half)
    ang = pos.astype(F32)[:, None] * inv[None, :]
    cos, sin = jnp.cos(ang), jnp.sin(ang)
    n = pos.shape[0]
    rest = HEAD_DIM - ROPE_DIM
    c = jnp.concatenate([cos, cos, jnp.ones((n, rest), F32)], axis=-1)
    a = jnp.concatenate([-sin, jnp.zeros((n, HEAD_DIM - half), F32)], axis=-1)
    b = jnp.concatenate([jnp.zeros((n, half), F32), sin, jnp.zeros((n, rest), F32)], axis=-1)
    return (c, a, b), (HEAD_DIM - half, half)


def _ret_rot_tables(pos):
    half = RET_DK // 2
    angle = RET_ROT_BASE ** (-jnp.linspace(0.0, 1.0, half, dtype=F32))
    ang = pos.astype(F32)[:, None] * angle[None, :]
    cos, sin = jnp.cos(ang), jnp.sin(ang)
    n = pos.shape[0]
    zero = jnp.zeros_like(sin)
    c = jnp.stack([cos, cos], axis=-1).reshape(n, RET_DK)
    a = jnp.stack([-sin, zero], axis=-1).reshape(n, RET_DK)
    b = jnp.stack([zero, sin], axis=-1).reshape(n, RET_DK)
    return (c, a, b), (RET_DK - 1, 1)


def _ret_decay(chunk):
    log_g = jnp.log1p(-jnp.exp2(-5.0 - jnp.arange(RET_HEADS, dtype=F32)))
    n = jnp.arange(chunk, dtype=F32)
    rel = n[:, None] - n[None, :]
    decay = jnp.where(rel >= 0, jnp.exp(jnp.maximum(rel, 0.0) * log_g[:, None, None]), 0.0)
    q_dec = jnp.exp((n + 1.0) * log_g[:, None])
    k_dec = jnp.exp((chunk - 1.0 - n) * log_g[:, None])
    c_dec = jnp.exp(chunk * log_g)
    return decay, q_dec, k_dec, c_dec


def kernel(x_prompt, x_sample, cache_swa_k, cache_swa_v, state_ret, cache_mem_k, cache_mem_v,
           mem_prompt, ln_g, ln_b, ffn_w_gu, ffn_w_down, w_mem_kv, swa_w_in, swa_w_out,
           swa_sinks, ret_w_in, ret_w_out):
    bp, seq, _ = x_prompt.shape
    bs, sample_seq, _ = x_sample.shape
    assert sample_seq == 1
    mp, ms = bp * seq, bs
    assert mp % ms == 0

    mem2 = mem_prompt.reshape(bp * MEM_LEN, D_MODEL)
    mem_kv = _mem_kv(mem2, w_mem_kv).reshape(DEPTH, bp, MEM_LEN, 2 * MEM_DIM)
    mem_k_prompt = mem_kv[..., :MEM_DIM].reshape(DEPTH, bp, MEM_LEN, MEM_HEADS, HEAD_DIM)
    mem_v_prompt = mem_kv[..., MEM_DIM:].reshape(DEPTH, bp, MEM_LEN, MEM_HEADS, HEAD_DIM)

    tm, tiles = _row_tiles(mp + ms, FFN_MAX_ROWS)
    m_all = tm * tiles
    tm_proj, _ = _row_tiles(m_all, PROJ_MAX_ROWS)
    pad = m_all - mp - ms
    tail = (mp - (tiles - 1) * tm, ms)
    assert tail[0] >= 0 and tail[0] % 8 == 0 and ms % 8 == 0 and tail[0] + ms <= tm
    def per_row(table):
        return jnp.concatenate([jnp.tile(table[:seq], (bp, 1)),
                                jnp.broadcast_to(table[seq:], (ms, table.shape[1])),
                                jnp.zeros((pad, table.shape[1]), F32)], axis=0)

    pos = jnp.concatenate([jnp.arange(seq, dtype=jnp.int32), jnp.full((1,), PAST_LEN, jnp.int32)])
    rope_tabs, rope_shifts = _rope_tables(pos)
    rot_tabs, rot_shifts = _ret_rot_tables(pos)
    rope_tabs = tuple(per_row(t) for t in rope_tabs)
    rot_tabs = tuple(per_row(t) for t in rot_tabs)
    decay, q_dec, k_dec, c_dec = _ret_decay(RET_CHUNK)
    ret_tabs = (decay, jnp.broadcast_to(q_dec[:, :, None], decay.shape),
                jnp.broadcast_to(k_dec[:, :, None], decay.shape), c_dec)
    decay, q_dec, k_dec, c_dec = _ret_decay(sample_seq)
    step_dec = jnp.stack([decay[:, 0, 0], q_dec[:, 0], k_dec[:, 0], c_dec])

    def ln(i, s):
        return ln_g[i, s].reshape(1, D_MODEL), ln_b[i, s].reshape(1, D_MODEL)

    swa_w_out_b, ret_w_out_b = swa_w_out.astype(BF16), ret_w_out.astype(BF16)
    swa_k_prompt, swa_v_prompt = [], []
    swa_sample = [jnp.zeros_like(cache_swa_k), jnp.zeros_like(cache_swa_v)]
    ret_prompt = jnp.zeros((DEPTH // 2, bp, RET_HEADS, RET_DK, RET_DV), F32)
    ret_sample = jnp.zeros_like(state_ret)
    for i in range(DEPTH):
        j = i // 2
        if i == 0:
            x = _ffn(x_prompt.reshape(mp, D_MODEL), ffn_w_gu, ffn_w_down, i, 0, *ln(i, 0), tm, tiles, tail,
                     x_sample=x_sample.reshape(ms, D_MODEL))
        else:
            x = _ffn(x, ffn_w_gu, ffn_w_down, i, 0, *ln(i, 0), tm, tiles, tail)
        if i % 2 == 0:
            qkv = _proj(x, swa_w_in, j, tm_proj, rot_heads=SWA_Q_HEADS + SWA_KV_HEADS, tables=rope_tabs,
                        shifts=rope_shifts)
            x, k_tail, v_tail = _swa_prompt(qkv, bp, seq, mem_kv, i, swa_sinks[j], swa_w_out_b, x, *ln(i, 1))
            swa_k_prompt.append(k_tail.reshape(bp, WINDOW, SWA_KV_HEADS, HEAD_DIM))
            swa_v_prompt.append(v_tail.reshape(bp, WINDOW, SWA_KV_HEADS, HEAD_DIM))
            rows = qkv[mp:mp + ms].reshape(ms, SWA_IN_WIDTH // HEAD_DIM, HEAD_DIM)
            att, *swa_sample = _swa_step(rows, cache_swa_k, cache_swa_v, j, cache_mem_k, cache_mem_v, i,
                                         swa_sinks[j], swa_sample)
            x = _out_ln(att.reshape(ms, D_MODEL), swa_w_out, j, x, *ln(i, 1), mp)
        else:
            qkvg = _proj(x, ret_w_in, j, tm_proj, rot_heads=2 * RET_HEADS, tables=rot_tabs,
                         shifts=rot_shifts)
            x, ret_prompt = _ret_prompt(qkvg, bp, seq, mem_kv, i, ret_tabs, ret_w_out_b, x, *ln(i, 1),
                                        [ret_prompt])
            rows = qkvg[mp:mp + ms].reshape(ms, RET_IN_WIDTH // HEAD_DIM, HEAD_DIM)
            att, ret_sample = _ret_step(rows, jnp.swapaxes(rows, 1, 2), state_ret, j, cache_mem_k,
                                        cache_mem_v, i, step_dec,
                                        [ret_sample])
            x = _out_ln(att.reshape(ms, D_MODEL), ret_w_out, j, x, *ln(i, 1), mp)
        x = _ffn(x, ffn_w_gu, ffn_w_down, i, 1, *ln(i, 2), tm, tiles, tail, split_out=(i == DEPTH - 1))

    y_prompt = x[0].reshape(bp, seq, D_MODEL)
    y_sample = x[1].reshape(bs, sample_seq, D_MODEL)
    return (y_prompt, y_sample, jnp.stack(swa_k_prompt), jnp.stack(swa_v_prompt), swa_sample[0],
            swa_sample[1], ret_prompt, ret_sample, mem_k_prompt, mem_v_prompt)
```

```python
import functools

import jax
import jax.numpy as jnp
from jax import lax
from jax.experimental import pallas as pl
from jax.experimental.pallas import tpu as pltpu

F32 = jnp.float32
BF16 = jnp.bfloat16

D_MODEL = 2048
DEPTH = 4
PAST_LEN = 16384
HEAD_DIM = 128
MEM_LEN = 256
MEM_HEADS = 4
MEM_DIM = MEM_HEADS * HEAD_DIM
SELF_WIDTH = D_MODEL - MEM_DIM
SWA_Q_HEADS = SELF_WIDTH // HEAD_DIM
SWA_KV_HEADS = SWA_Q_HEADS // 3
SWA_GROUP = SWA_Q_HEADS // SWA_KV_HEADS
WINDOW = 128
ROPE_THETA = 500000.0
ROPE_DIM = HEAD_DIM // 4
RET_DK = 128
RET_DV = 2 * RET_DK
RET_HEADS = SELF_WIDTH // RET_DV
RET_CHUNK = 128
RET_ROT_BASE = 10000.0
D_FF = ((8 * D_MODEL // 3 + 255) // 256) * 256
LN_EPS = 1e-5
HEAD_NORM_EPS = 1e-6
ALPHA = (2.0 * DEPTH) ** 0.25
NEG_INF = -1e30
ATT_SCALE = HEAD_DIM ** -0.5
RET_K_SCALE = RET_DK ** -0.5

SWA_Q = SWA_Q_HEADS * HEAD_DIM
SWA_KV = SWA_KV_HEADS * HEAD_DIM
SWA_IN_WIDTH = SWA_Q + 2 * SWA_KV + MEM_DIM
RET_QK = RET_HEADS * RET_DK
RET_V = RET_HEADS * RET_DV
RET_IN_WIDTH = 2 * RET_QK + 2 * RET_V + MEM_DIM

VMEM_LIMIT_BYTES = 58 * 1024 * 1024
LANES = 128
FFN_TILE_K = 256
FFN_MAX_ROWS = 1040
PROJ_MAX_ROWS = 1040
ATT_BLOCKS_PER_STEP = 4
PROJ_TILE_N = 1024
OUT_TILE_N = 512
MAX_ROW_CHUNK = 256
BF16_SUBLANES = 16


def _row_tiles(rows, max_rows):
    tiles = -(-rows // max_rows)
    per_tile = -(-rows // tiles)
    return -(-per_tile // BF16_SUBLANES) * BF16_SUBLANES, tiles


def _row_chunks(tm):
    count = -(-tm // MAX_ROW_CHUNK)
    size = -(-(-(-tm // count)) // BF16_SUBLANES) * BF16_SUBLANES
    return [slice(r, min(r + size, tm)) for r in range(0, tm, size)]


def _params(*sem):
    return pltpu.CompilerParams(dimension_semantics=sem, vmem_limit_bytes=VMEM_LIMIT_BYTES)


def _layer_norm_rows(z, g, b):
    mu = jnp.mean(z, axis=-1, keepdims=True)
    zc = z - mu
    var = jnp.mean(zc * zc, axis=-1, keepdims=True)
    return zc * lax.rsqrt(var + LN_EPS) * g + b


def _ffn_kernel(*refs, nk, tk, tiles, tail, split_in, split_out):
    refs = list(refs)
    x_ref = refs.pop(0)
    xs_ref = refs.pop(0) if split_in else None
    wg_ref, wu_ref, wd_ref, g_ref, b_ref, o_ref = refs[:6]
    os_ref = refs[6] if split_out else None
    xb_ref, wgu_ref = refs[-2:]
    i = pl.program_id(0)
    k = pl.program_id(1)
    chunks = _row_chunks(o_ref.shape[0])

    def load_x(rows):
        xv = x_ref[rows, :]
        if not split_in or rows.stop <= tail[0]:
            return xv
        first, count = tail
        lo, hi = rows.start, rows.stop
        pieces = []
        if lo < first:
            pieces.append(xv[:first - lo])
        s0, s1 = max(lo, first), min(hi, first + count)
        if s0 < s1:
            pieces.append(xs_ref[s0 - first:s1 - first, :])
        if hi > first + count:
            pieces.append(jnp.zeros((hi - max(lo, first + count), xv.shape[1]), xv.dtype))
        return jnp.where(i == tiles - 1, jnp.concatenate(pieces, axis=0), xv)

    def cast_weights():
        wgu_ref[:, :tk] = wg_ref[...].astype(BF16)
        wgu_ref[:, tk:] = wu_ref[...].astype(BF16)
        return wd_ref[...].astype(BF16)

    def down(xb, wd):
        gu = jnp.dot(xb, wgu_ref[...], preferred_element_type=F32)
        h = (jax.nn.silu(gu[:, :tk]) * gu[:, tk:]).astype(BF16)
        return jnp.dot(h, wd, preferred_element_type=F32)

    @pl.when(k == 0)
    def _():
        wd = cast_weights()
        for rows in chunks:
            xb = load_x(rows).astype(BF16)
            xb_ref[rows, :] = xb
            o_ref[rows, :] = down(xb, wd)

    @pl.when(jnp.logical_and(k > 0, k < nk - 1))
    def _():
        wd = cast_weights()
        o_ref[...] += down(xb_ref[...], wd)

    @pl.when(k == nk - 1)
    def _():
        wd = cast_weights()
        for rows in chunks:
            acc = o_ref[rows, :] + down(xb_ref[rows, :], wd)
            z = ALPHA * load_x(rows) + 0.5 * acc
            o_ref[rows, :] = _layer_norm_rows(z, g_ref[...], b_ref[...])
        if split_out:
            @pl.when(i == tiles - 1)
            def _():
                os_ref[...] = o_ref[tail[0]:tail[0] + tail[1], :]


def _ffn(x, w_gu, w_down, layer, slot, g, b, tm, tiles, tail, x_sample=None, split_out=False):
    tk = FFN_TILE_K
    nk = D_FF // tk
    assert nk >= 2
    split_in = x_sample is not None
    rows = pl.BlockSpec((tm, D_MODEL), lambda i, k: (i, 0))
    small = pl.BlockSpec((tail[1], D_MODEL), lambda i, k: (0, 0))
    in_specs = [pl.BlockSpec((tm, D_MODEL), lambda i, k: (i, 0), pipeline_mode=pl.Buffered(1))]
    args = [x]
    if split_in:
        in_specs.append(small)
        args.append(x_sample)
    in_specs += [
        pl.BlockSpec((None, None, D_MODEL, tk), lambda i, k: (layer, slot, 0, k)),
        pl.BlockSpec((None, None, D_MODEL, tk), lambda i, k: (layer, slot, 0, nk + k)),
        pl.BlockSpec((None, None, tk, D_MODEL), lambda i, k: (layer, slot, k, 0)),
        pl.BlockSpec((1, D_MODEL), lambda i, k: (0, 0)),
        pl.BlockSpec((1, D_MODEL), lambda i, k: (0, 0)),
    ]
    if split_out:
        prompt_rows = (tiles - 1) * tm + tail[0]
        out_specs = [rows, small]
        out_shape = [jax.ShapeDtypeStruct((prompt_rows, D_MODEL), F32),
                     jax.ShapeDtypeStruct((tail[1], D_MODEL), F32)]
    else:
        out_specs = rows
        out_shape = jax.ShapeDtypeStruct((tiles * tm, D_MODEL), F32)
    return pl.pallas_call(
        functools.partial(_ffn_kernel, nk=nk, tk=tk, tiles=tiles, tail=tail,
                          split_in=split_in, split_out=split_out),
        grid=(tiles, nk),
        in_specs=in_specs,
        out_specs=out_specs,
        out_shape=out_shape,
        scratch_shapes=[pltpu.VMEM((tm, D_MODEL), BF16), pltpu.VMEM((D_MODEL, 2 * tk), BF16)],
        compiler_params=_params("arbitrary", "arbitrary"),
        name="ffn_ln",
    )(*args, w_gu, w_gu, w_down, g, b)


def _proj_kernel(x_ref, w_ref, *rest, rot_heads, shifts):
    if rot_heads:
        c_ref, a_ref, b_ref, o_ref, xb_ref = rest
    else:
        o_ref, xb_ref = rest
    j = pl.program_id(1)

    @pl.when(j == 0)
    def _():
        xb_ref[...] = x_ref[...].astype(BF16)

    def plain():
        o_ref[...] = jnp.dot(xb_ref[...], w_ref[...].astype(BF16), preferred_element_type=F32)

    if not rot_heads:
        plain()
        return

    tm, tn = o_ref.shape
    heads = tn // LANES
    full_tiles, part_heads = divmod(rot_heads, heads)

    def rotated(n_heads):
        wb = w_ref[...].astype(BF16)
        for rows in _row_chunks(tm):
            y = jnp.dot(xb_ref[rows, :], wb, preferred_element_type=F32)
            c, a, b = c_ref[rows, :], a_ref[rows, :], b_ref[rows, :]
            for h in range(heads):
                yh = y[:, h * LANES:(h + 1) * LANES]
                if h < n_heads:
                    yh = yh * c + pltpu.roll(yh, shifts[0], 1) * a + pltpu.roll(yh, shifts[1], 1) * b
                o_ref[rows, h * LANES:(h + 1) * LANES] = yh

    pl.when(j < full_tiles)(lambda: rotated(heads))
    if part_heads:
        pl.when(j == full_tiles)(lambda: rotated(part_heads))
    pl.when(j >= full_tiles + (1 if part_heads else 0))(plain)


def _proj(x, w, layer, tm, rot_heads=0, tables=None, shifts=None):
    m = x.shape[0]
    n = w.shape[-1]
    tn = PROJ_TILE_N
    in_specs = [
        pl.BlockSpec((tm, D_MODEL), lambda i, j: (i, 0)),
        pl.BlockSpec((None, D_MODEL, tn), lambda i, j: (layer, 0, j)),
    ]
    args = [x, w]
    if rot_heads:
        tab = pl.BlockSpec((tm, LANES), lambda i, j: (i, 0))
        in_specs += [tab, tab, tab]
        args += list(tables)
    return pl.pallas_call(
        functools.partial(_proj_kernel, rot_heads=rot_heads, shifts=shifts),
        grid=(pl.cdiv(m, tm), n // tn),
        in_specs=in_specs,
        out_specs=pl.BlockSpec((tm, tn), lambda i, j: (i, j)),
        out_shape=jax.ShapeDtypeStruct((m, n), F32),
        scratch_shapes=[pltpu.VMEM((tm, D_MODEL), BF16)],
        compiler_params=_params("parallel", "arbitrary"),
        name="proj",
    )(*args)


def _mem_kv_kernel(x_ref, w_ref, o_ref, xb_ref):
    @pl.when(pl.program_id(0) == 0)
    def _():
        xb_ref[...] = x_ref[...].astype(BF16)

    o_ref[...] = jnp.dot(xb_ref[...], w_ref[...].astype(BF16), preferred_element_type=F32)


def _mem_kv(mem, w):
    rows = mem.shape[0]
    layers, _, n = w.shape
    return pl.pallas_call(
        _mem_kv_kernel,
        grid=(layers,),
        in_specs=[
            pl.BlockSpec((rows, D_MODEL), lambda l: (0, 0)),
            pl.BlockSpec((None, D_MODEL, n), lambda l: (l, 0, 0)),
        ],
        out_specs=pl.BlockSpec((None, rows, n), lambda l: (l, 0, 0)),
        out_shape=jax.ShapeDtypeStruct((layers, rows, n), F32),
        scratch_shapes=[pltpu.VMEM((rows, D_MODEL), BF16)],
        compiler_params=_params("arbitrary"),
        name="mem_kv",
    )(mem, w)


def _out_ln_kernel(att_ref, w_ref, x_ref, g_ref, b_ref, o_ref, y_ref, *, nn, tn):
    n = pl.program_id(1)
    y_ref[n] = jnp.dot(att_ref[...].astype(BF16), w_ref[...].astype(BF16),
                       preferred_element_type=F32)

    @pl.when(n == nn - 1)
    def _():
        cols = [slice(c * tn, (c + 1) * tn) for c in range(nn)]
        total = None
        for c in range(nn):
            z = ALPHA * x_ref[:, cols[c]] + y_ref[c]
            y_ref[c] = z
            part = jnp.sum(z, axis=-1, keepdims=True)
            total = part if total is None else total + part
        mu = total * (1.0 / D_MODEL)
        total = None
        for c in range(nn):
            zc = y_ref[c] - mu
            part = jnp.sum(zc * zc, axis=-1, keepdims=True)
            total = part if total is None else total + part
        rstd = lax.rsqrt(total * (1.0 / D_MODEL) + LN_EPS)
        for c in range(nn):
            o_ref[:, cols[c]] = (y_ref[c] - mu) * rstd * g_ref[:, cols[c]] + b_ref[:, cols[c]]


def _out_ln(att, w_out, layer, x, g, b, row0):
    tm = att.shape[0]
    assert row0 % tm == 0
    tn = OUT_TILE_N
    nn = D_MODEL // tn
    rows = pl.BlockSpec((tm, D_MODEL), lambda i, n: (row0 // tm, 0))
    return pl.pallas_call(
        functools.partial(_out_ln_kernel, nn=nn, tn=tn),
        grid=(1, nn),
        in_specs=[
            pl.BlockSpec((tm, D_MODEL), lambda i, n: (0, 0)),
            pl.BlockSpec((None, D_MODEL, tn), lambda i, n: (layer, 0, n)),
            rows,
            pl.BlockSpec((1, D_MODEL), lambda i, n: (0, 0)),
            pl.BlockSpec((1, D_MODEL), lambda i, n: (0, 0)),
        ],
        out_specs=rows,
        out_shape=jax.ShapeDtypeStruct(x.shape, F32),
        scratch_shapes=[pltpu.VMEM((nn, tm, tn), F32)],
        input_output_aliases={2: 0},
        compiler_params=_params("arbitrary", "arbitrary"),
        name="out_ln",
    )(att, w_out, x, g, b)


def _dot_nt(a, b):
    return lax.dot_general(a, b, (((1,), (1,)), ((), ())), preferred_element_type=F32)


def _out_proj_ln(att_ref, w_ref, x_ref, g_ref, b_ref, o_ref):
    y = jnp.dot(att_ref[...], w_ref[...], preferred_element_type=F32)
    o_ref[...] = _layer_norm_rows(ALPHA * x_ref[...] + y, g_ref[...], b_ref[...])


def _normalised_pv(p, v, extra=None):
    width = v.shape[1]
    v_ones = jnp.concatenate([v, jnp.ones((p.shape[1], width), p.dtype)], axis=1)
    pv = jnp.dot(p, v_ones, preferred_element_type=F32)
    den = pv[:, width:]
    if extra is not None:
        den = den + extra
    return pv[:, :width] * (1.0 / den)


def _mem_attend_block(qm_ref, mk_ref, mv_ref, att_ref, col0):
    for h in range(MEM_HEADS):
        sl = slice(h * HEAD_DIM, (h + 1) * HEAD_DIM)
        q = qm_ref[:, sl].astype(BF16)
        s = _dot_nt(q, mk_ref[0, :, sl].astype(BF16)) * ATT_SCALE
        p = jnp.exp(s - jnp.max(s, axis=-1, keepdims=True))
        p = p * (1.0 / jnp.sum(p, axis=-1, keepdims=True))
        o = jnp.dot(p.astype(BF16), mv_ref[0, :, sl].astype(BF16), preferred_element_type=F32)
        att_ref[:, col0 + h * HEAD_DIM:col0 + (h + 1) * HEAD_DIM] = o.astype(att_ref.dtype)


def _swa_prompt_kernel(sink_ref, q_ref, kc_ref, kp_ref, vc_ref, vp_ref, qm_ref, mk_ref, mv_ref,
                       w_ref, x_ref, g_ref, b_ref, o_ref, kt_ref, vt_ref, s_ref, p_ref, att_ref,
                       *, nblk, steps):
    n = pl.program_id(1)
    nq = SWA_Q_HEADS
    w2 = 2 * WINDOW

    def prev_cur(cur_ref, prev_ref, i, sl):
        rows = slice(i * WINDOW, (i + 1) * WINDOW)
        before = prev_ref[:, sl] if i == 0 else cur_ref[(i - 1) * WINDOW:i * WINDOW, sl]
        return jnp.concatenate([before, cur_ref[rows, sl]], axis=0).astype(BF16)

    for i in range(nblk):
        rows = slice(i * WINDOW, (i + 1) * WINDOW)
        for h in range(SWA_KV_HEADS):
            sl = slice(h * HEAD_DIM, (h + 1) * HEAD_DIM)
            q3 = jnp.concatenate(
                [q_ref[rows, (h * SWA_GROUP + g) * HEAD_DIM:(h * SWA_GROUP + g + 1) * HEAD_DIM]
                 for g in range(SWA_GROUP)], axis=0).astype(BF16)
            s = _dot_nt(q3, prev_cur(kc_ref, kp_ref, i, sl)) * ATT_SCALE
            s_ref[i, h * SWA_GROUP:(h + 1) * SWA_GROUP] = s.reshape(SWA_GROUP, WINDOW, w2)
    for h in range(MEM_HEADS):
        sl = slice(h * HEAD_DIM, (h + 1) * HEAD_DIM)
        sm = _dot_nt(qm_ref[:, sl].astype(BF16), mk_ref[0, :, sl].astype(BF16)) * ATT_SCALE
        s_ref[:, nq + h] = sm.reshape(nblk, WINDOW, w2)

    qi = lax.broadcasted_iota(jnp.int32, (WINDOW, w2), 0)
    kj = lax.broadcasted_iota(jnp.int32, (WINDOW, w2), 1)
    sink = sink_ref[...]
    sink_terms = []
    for i in range(nblk):
        first_key = jnp.where(n > 0, qi, WINDOW) if i == 0 else qi
        ok = (kj >= first_key) & (kj <= qi + WINDOW)
        s = jnp.where(ok[None], s_ref[i, 0:nq], NEG_INF)
        m = jnp.maximum(jnp.max(s, axis=-1, keepdims=True), sink)
        p_ref[i, 0:nq] = jnp.exp(s - m).astype(BF16)
        sink_terms.append(jnp.exp(sink - m))
    s = s_ref[:, nq:nq + MEM_HEADS]
    p_ref[:, nq:nq + MEM_HEADS] = jnp.exp(s - jnp.max(s, axis=-1, keepdims=True)).astype(BF16)

    for i in range(nblk):
        rows = slice(i * WINDOW, (i + 1) * WINDOW)
        for h in range(SWA_KV_HEADS):
            sl = slice(h * HEAD_DIM, (h + 1) * HEAD_DIM)
            group = slice(h * SWA_GROUP, (h + 1) * SWA_GROUP)
            p3 = p_ref[i, group].reshape(SWA_GROUP * WINDOW, w2)
            o = _normalised_pv(p3, prev_cur(vc_ref, vp_ref, i, sl),
                               sink_terms[i][group].reshape(SWA_GROUP * WINDOW, 1))
            for g in range(SWA_GROUP):
                hq = h * SWA_GROUP + g
                att_ref[rows, hq * HEAD_DIM:(hq + 1) * HEAD_DIM] = (
                    o[g * WINDOW:(g + 1) * WINDOW].astype(BF16))
    for h in range(MEM_HEADS):
        sl = slice(h * HEAD_DIM, (h + 1) * HEAD_DIM)
        pm = p_ref[:, nq + h].reshape(nblk * WINDOW, w2)
        om = _normalised_pv(pm, mv_ref[0, :, sl].astype(BF16))
        att_ref[:, SWA_Q + h * HEAD_DIM:SWA_Q + (h + 1) * HEAD_DIM] = om.astype(BF16)

    @pl.when(n == steps - 1)
    def _():
        last = slice((nblk - 1) * WINDOW, nblk * WINDOW)
        kt_ref[0] = kc_ref[last, :]
        vt_ref[0] = vc_ref[last, :]

    _out_proj_ln(att_ref, w_ref, x_ref, g_ref, b_ref, o_ref)


def _mixer_tail_specs(layer, row_of, rows):
    mem = (None, 1, MEM_LEN, MEM_DIM)
    vec = pl.BlockSpec((1, D_MODEL), lambda b, n: (0, 0))
    mixer = layer // 2
    return [
        pl.BlockSpec(mem, lambda b, n: (layer, b, 0, 0)),
        pl.BlockSpec(mem, lambda b, n: (layer, b, 0, 1)),
        pl.BlockSpec((None, D_MODEL, D_MODEL), lambda b, n: (mixer, 0, 0), pipeline_mode=pl.Buffered(1)),
        pl.BlockSpec((rows, D_MODEL), lambda b, n: (row_of(b, n), 0)),
        vec, vec,
    ]


def _swa_prompt(qkv, bsz, seq, mem_kv, layer, sinks, w_out, x, g, b):
    nblk = ATT_BLOCKS_PER_STEP
    rows = nblk * WINDOW
    steps = seq // rows
    assert seq % rows == 0 and seq >= WINDOW
    tail = pl.BlockSpec((1, WINDOW, SWA_KV), lambda b, n: (b, 0, 0))
    blocks_per_seq = seq // WINDOW
    kcol = SWA_Q // SWA_KV
    vcol = kcol + 1
    mcol = vcol + 1
    row_of = lambda b, n: b * steps + n
    prev_of = lambda b, n: b * blocks_per_seq + jnp.maximum(n * nblk - 1, 0)
    cur = lambda w, col: pl.BlockSpec((rows, w), lambda b, n: (row_of(b, n), col))
    prev = lambda col: pl.BlockSpec((WINDOW, SWA_KV), lambda b, n: (prev_of(b, n), col))
    nheads = SWA_Q_HEADS + MEM_HEADS
    assert MEM_LEN == 2 * WINDOW
    in_specs = [
        pl.BlockSpec((SWA_Q_HEADS, 1, 1), lambda b, n: (0, 0, 0)),
        cur(SWA_Q, 0), cur(SWA_KV, kcol), prev(kcol), cur(SWA_KV, vcol), prev(vcol), cur(MEM_DIM, mcol),
    ] + _mixer_tail_specs(layer, row_of, rows)
    return pl.pallas_call(
        functools.partial(_swa_prompt_kernel, nblk=nblk, steps=steps),
        grid=(bsz, steps),
        in_specs=in_specs,
        out_specs=[pl.BlockSpec((rows, D_MODEL), lambda b, n: (row_of(b, n), 0)), tail, tail],
        out_shape=[jax.ShapeDtypeStruct(x.shape, F32),
                   jax.ShapeDtypeStruct((bsz, WINDOW, SWA_KV), F32),
                   jax.ShapeDtypeStruct((bsz, WINDOW, SWA_KV), F32)],
        scratch_shapes=[pltpu.VMEM((nblk, nheads, WINDOW, 2 * WINDOW), F32),
                        pltpu.VMEM((nblk, nheads, WINDOW, 2 * WINDOW), BF16),
                        pltpu.VMEM((rows, D_MODEL), BF16)],
        input_output_aliases={len(in_specs) - 3: 0},
        compiler_params=_params("arbitrary", "arbitrary"),
        name="swa_prompt",
    )(sinks.reshape(SWA_Q_HEADS, 1, 1), qkv, qkv, qkv, qkv, qkv, qkv, mem_kv, mem_kv, w_out, x, g, b)


def _head_norm_gate(o, gate):
    mu = jnp.mean(o, axis=-1, keepdims=True)
    oc = o - mu
    var = jnp.mean(oc * oc, axis=-1, keepdims=True)
    return jax.nn.silu(gate) * (oc * lax.rsqrt(var + HEAD_NORM_EPS))


def _ret_prompt_kernel(cdec_ref, decay_ref, qdec_ref, kdec_ref, q_ref, k_ref, v_ref, gate_ref, qm_ref,
                       mk_ref, mv_ref, w_ref, x_ref, g_ref, b_ref, o_ref, s_out_ref,
                       state_ref, att_ref, *, steps, nblk):
    c = pl.program_id(1)

    @pl.when(c == 0)
    def _():
        state_ref[...] = jnp.zeros_like(state_ref)

    for i in range(nblk):
        rows = slice(i * RET_CHUNK, (i + 1) * RET_CHUNK)
        for h in range(RET_HEADS):
            ksl = slice(h * RET_DK, (h + 1) * RET_DK)
            vsl = slice(h * RET_DV, (h + 1) * RET_DV)
            qc = q_ref[rows, ksl]
            kc = k_ref[rows, ksl] * RET_K_SCALE
            vb = v_ref[rows, vsl].astype(BF16)
            st = state_ref[h]
            inner = _dot_nt(qc.astype(BF16), kc.astype(BF16)) * decay_ref[h]
            o = (jnp.dot(inner.astype(BF16), vb, preferred_element_type=F32)
                 + jnp.dot((qc * qdec_ref[h]).astype(BF16), st.astype(BF16),
                           preferred_element_type=F32))
            kd = (kc * kdec_ref[h]).astype(BF16)
            state_ref[h] = cdec_ref[h] * st + lax.dot_general(
                kd, vb, (((0,), (0,)), ((), ())), preferred_element_type=F32)
            att_ref[rows, vsl] = _head_norm_gate(o, gate_ref[rows, vsl]).astype(BF16)
    _mem_attend_block(qm_ref, mk_ref, mv_ref, att_ref, RET_V)

    @pl.when(c == steps - 1)
    def _():
        s_out_ref[0] = state_ref[...]

    _out_proj_ln(att_ref, w_ref, x_ref, g_ref, b_ref, o_ref)


def _drop_carried(kernel, first, count):
    def body(*refs):
        return kernel(*refs[:first], *refs[first + count:])
    return body


def _carry(prev_outputs):
    prev_outputs = list(prev_outputs)
    return [pl.BlockSpec(memory_space=pl.ANY)] * len(prev_outputs), prev_outputs


def _ret_prompt(qkvg, bsz, seq, mem_kv, layer, tables, w_out, x, g, b, carried):
    nblk = ATT_BLOCKS_PER_STEP
    rows = nblk * RET_CHUNK
    steps = seq // rows
    mixer = layer // 2
    decay, qdec, kdec, cdec = tables
    row_of = lambda b, n: b * steps + n
    cur = lambda w, col: pl.BlockSpec((rows, w), lambda b, n: (row_of(b, n), col))
    tab = pl.BlockSpec((RET_HEADS, RET_CHUNK, RET_CHUNK), lambda b, n: (0, 0, 0))
    in_specs = [
        pl.BlockSpec(memory_space=pltpu.SMEM), tab, tab, tab,
        cur(RET_QK, 0), cur(RET_QK, 1), cur(RET_V, 1), cur(RET_V, 2),
        cur(MEM_DIM, (2 * RET_QK + 2 * RET_V) // MEM_DIM),
    ] + _mixer_tail_specs(layer, row_of, rows)
    n_in = len(in_specs)
    carry_specs, carry_args = _carry(carried)
    aliases = {n_in - 3: 0}
    aliases.update({n_in + c: 1 + c for c in range(len(carry_args))})
    return pl.pallas_call(
        _drop_carried(functools.partial(_ret_prompt_kernel, steps=steps, nblk=nblk), n_in, len(carry_args)),
        grid=(bsz, steps),
        in_specs=in_specs + carry_specs,
        out_specs=[
            pl.BlockSpec((rows, D_MODEL), lambda b, n: (row_of(b, n), 0)),
            pl.BlockSpec((None, 1, RET_HEADS, RET_DK, RET_DV), lambda b, n: (mixer, b, 0, 0, 0)),
        ],
        out_shape=[
            jax.ShapeDtypeStruct(x.shape, F32),
            jax.ShapeDtypeStruct((DEPTH // 2, bsz, RET_HEADS, RET_DK, RET_DV), F32),
        ],
        scratch_shapes=[pltpu.VMEM((RET_HEADS, RET_DK, RET_DV), F32),
                        pltpu.VMEM((rows, D_MODEL), BF16)],
        input_output_aliases=aliases,
        compiler_params=_params("arbitrary", "arbitrary"),
        name="ret_prompt",
    )(cdec, decay, qdec, kdec, qkvg, qkvg, qkvg, qkvg, qkvg, mem_kv, mem_kv, w_out, x, g, b, *carry_args)


def _pair_tokens(a):
    t, h, d = a.shape
    a = a.reshape(t // 2, 2, h, d)
    return jnp.concatenate([a[:, 0], a[:, 1]], axis=1)


def _twice(x):
    return jnp.concatenate([x, x], axis=0)


def _fold(x):
    half = x.shape[0] // 2
    return x[:half] + x[half:]


def _mem_attend_row(q, mk_ref, mv_ref):
    mk = _pair_tokens(mk_ref[0])
    mv = _pair_tokens(mv_ref[0])
    s = jnp.sum(mk * _twice(q)[None], axis=-1, keepdims=True) * ATT_SCALE
    m = jnp.max(s, axis=0)
    m = jnp.maximum(m[:MEM_HEADS], m[MEM_HEADS:])
    p = jnp.exp(s - _twice(m)[None])
    den = _fold(jnp.sum(p, axis=0))
    return _fold(jnp.sum(p * mv, axis=0)) * (1.0 / den)


def _swa_step_kernel(sink_ref, rows_ref, kbuf_ref, vbuf_ref, mk_ref, mv_ref, o_ref, nk_ref, nv_ref):
    krow0 = SWA_Q_HEADS
    vrow0 = krow0 + SWA_KV_HEADS
    mrow0 = vrow0 + SWA_KV_HEADS
    wb = kbuf_ref.shape[1]
    kb = _pair_tokens(kbuf_ref[0])
    vb = _pair_tokens(vbuf_ref[0])
    k_new = rows_ref[0, krow0:krow0 + SWA_KV_HEADS, :]
    v_new = rows_ref[0, vrow0:vrow0 + SWA_KV_HEADS, :]
    for g in range(SWA_GROUP):
        group_rows = pl.ds(g, SWA_KV_HEADS, stride=SWA_GROUP)
        q = rows_ref[0, group_rows, :]
        s_buf = jnp.sum(kb * _twice(q)[None], axis=-1, keepdims=True) * ATT_SCALE
        s_new = jnp.sum(k_new * q, axis=-1, keepdims=True) * ATT_SCALE
        sink = sink_ref[g]
        m = jnp.max(s_buf, axis=0)
        m = jnp.maximum(jnp.maximum(m[:SWA_KV_HEADS], m[SWA_KV_HEADS:]), jnp.maximum(s_new, sink))
        p_buf = jnp.exp(s_buf - _twice(m)[None])
        p_new = jnp.exp(s_new - m)
        den = _fold(jnp.sum(p_buf, axis=0)) + p_new + jnp.exp(sink - m)
        o_ref[0, group_rows, :] = (_fold(jnp.sum(p_buf * vb, axis=0)) + p_new * v_new) * (1.0 / den)
    nk_ref[0, 0:wb - 1] = kbuf_ref[0, 1:wb]
    nv_ref[0, 0:wb - 1] = vbuf_ref[0, 1:wb]
    nk_ref[0, wb - 1] = k_new
    nv_ref[0, wb - 1] = v_new
    o_ref[0, SWA_Q_HEADS:SWA_Q_HEADS + MEM_HEADS, :] = _mem_attend_row(
        rows_ref[0, mrow0:mrow0 + MEM_HEADS, :], mk_ref, mv_ref)


def _swa_step(rows, cache_k, cache_v, j, mem_k, mem_v, i, sinks, carried):
    bsz = rows.shape[0]
    wb = cache_k.shape[2]
    cache = pl.BlockSpec((None, 1, wb, SWA_KV_HEADS, HEAD_DIM), lambda b: (j, b, 0, 0, 0))
    mem = pl.BlockSpec((None, 1, MEM_LEN, MEM_HEADS, HEAD_DIM), lambda b: (i, b, 0, 0, 0))
    nrows = D_MODEL // HEAD_DIM
    sink_gk = sinks.reshape(SWA_KV_HEADS, SWA_GROUP).T.reshape(SWA_GROUP, SWA_KV_HEADS, 1)
    in_specs = [
        pl.BlockSpec(sink_gk.shape, lambda b: (0, 0, 0)),
        pl.BlockSpec((1,) + rows.shape[1:], lambda b: (b, 0, 0)),
        cache, cache, mem, mem,
    ]
    n_in = len(in_specs)
    carry_specs, carry_args = _carry(carried)
    return pl.pallas_call(
        _drop_carried(_swa_step_kernel, n_in, len(carry_args)),
        grid=(bsz,),
        in_specs=in_specs + carry_specs,
        out_specs=[pl.BlockSpec((1, nrows, HEAD_DIM), lambda b: (b, 0, 0)), cache, cache],
        out_shape=[
            jax.ShapeDtypeStruct((bsz, nrows, HEAD_DIM), F32),
            jax.ShapeDtypeStruct(cache_k.shape, F32),
            jax.ShapeDtypeStruct(cache_v.shape, F32),
        ],
        input_output_aliases={n_in + c: 1 + c for c in range(len(carry_args))},
        compiler_params=_params("parallel"),
        name="swa_step",
    )(sink_gk, rows, cache_k, cache_v, mem_k, mem_v, *carry_args)


def _ret_step_kernel(dec_ref, rows_ref, s_ref, mk_ref, mv_ref, o_ref, s_out_ref):
    krow0 = RET_HEADS
    vrow0 = 2 * RET_HEADS
    grow0 = vrow0 + 2 * RET_HEADS
    mrow0 = grow0 + 2 * RET_HEADS
    first_row = lax.broadcasted_iota(jnp.int32, (BF16_SUBLANES, 1), 0) == 0

    def lift(row):
        return jnp.where(first_row, row, 0.0).astype(BF16)

    def wide(row0):
        return jnp.concatenate([rows_ref[0, row0:row0 + 1, :], rows_ref[0, row0 + 1:row0 + 2, :]], axis=1)

    for h in range(RET_HEADS):
        q_row = rows_ref[0, h:h + 1, :]
        k_row = rows_ref[0, krow0 + h:krow0 + h + 1, :] * RET_K_SCALE
        v_row = wide(vrow0 + 2 * h)
        st = s_ref[0, h]
        inner = jnp.sum(q_row * k_row, axis=1, keepdims=True) * dec_ref[0, h]
        cross = jnp.dot(lift(q_row * dec_ref[1, h]), st.astype(BF16), preferred_element_type=F32)[0:1]
        outer = lax.dot_general(lift(k_row * dec_ref[2, h]), lift(v_row), (((0,), (0,)), ((), ())),
                                preferred_element_type=F32)
        s_out_ref[0, h] = dec_ref[3, h] * st + outer
        out = _head_norm_gate(inner * v_row + cross, wide(grow0 + 2 * h))
        o_ref[0, 2 * h:2 * h + 1, :] = out[:, :HEAD_DIM]
        o_ref[0, 2 * h + 1:2 * h + 2, :] = out[:, HEAD_DIM:]
    o_ref[0, 2 * RET_HEADS:2 * RET_HEADS + MEM_HEADS, :] = _mem_attend_row(
        rows_ref[0, mrow0:mrow0 + MEM_HEADS, :], mk_ref, mv_ref)


def _ret_step(rows, state, j, mem_k, mem_v, i, dec, carried):
    bsz = rows.shape[0]
    st = pl.BlockSpec((None, 1, RET_HEADS, RET_DK, RET_DV), lambda b: (j, b, 0, 0, 0))
    mem = pl.BlockSpec((None, 1, MEM_LEN, MEM_HEADS, HEAD_DIM), lambda b: (i, b, 0, 0, 0))
    nrows = D_MODEL // HEAD_DIM
    in_specs = [
        pl.BlockSpec(memory_space=pltpu.SMEM),
        pl.BlockSpec((1,) + rows.shape[1:], lambda b: (b, 0, 0)),
        st, mem, mem,
    ]
    n_in = len(in_specs)
    carry_specs, carry_args = _carry(carried)
    return pl.pallas_call(
        _drop_carried(_ret_step_kernel, n_in, len(carry_args)),
        grid=(bsz,),
        in_specs=in_specs + carry_specs,
        out_specs=[pl.BlockSpec((1, nrows, HEAD_DIM), lambda b: (b, 0, 0)), st],
        out_shape=[
            jax.ShapeDtypeStruct((bsz, nrows, HEAD_DIM), F32),
            jax.ShapeDtypeStruct(state.shape, F32),
        ],
        input_output_aliases={n_in + c: 1 + c for c in range(len(carry_args))},
        compiler_params=_params("parallel"),
        name="ret_step",
    )(dec, rows, state, mem_k, mem_v, *carry_args)


def _rope_tables(pos):
    half = ROPE_DIM // 2
    inv = ROPE_THETA ** (-jnp.arange(half, dtype=F32) / half)
    ang = pos.astype(F32)[:, None] * inv[None, :]
    cos, sin = jnp.cos(ang), jnp.sin(ang)
    n = pos.shape[0]
    rest = HEAD_DIM - ROPE_DIM
    c = jnp.concatenate([cos, cos, jnp.ones((n, rest), F32)], axis=-1)
    a = jnp.concatenate([-sin, jnp.zeros((n, HEAD_DIM - half), F32)], axis=-1)
    b = jnp.concatenate([jnp.zeros((n, half), F32), sin, jnp.zeros((n, rest), F32)], axis=-1)
    return (c, a, b), (HEAD_DIM - half, half)


def _ret_rot_tables(pos):
    half = RET_DK // 2
    angle = RET_ROT_BASE ** (-jnp.linspace(0.0, 1.0, half, dtype=F32))
    ang = pos.astype(F32)[:, None] * angle[None, :]
    cos, sin = jnp.cos(ang), jnp.sin(ang)
    n = pos.shape[0]
    zero = jnp.zeros_like(sin)
    c = jnp.stack([cos, cos], axis=-1).reshape(n, RET_DK)
    a = jnp.stack([-sin, zero], axis=-1).reshape(n, RET_DK)
    b = jnp.stack([zero, sin], axis=-1).reshape(n, RET_DK)
    return (c, a, b), (RET_DK - 1, 1)


def _ret_decay(chunk):
    log_g = jnp.log1p(-jnp.exp2(-5.0 - jnp.arange(RET_HEADS, dtype=F32)))
    n = jnp.arange(chunk, dtype=F32)
    rel = n[:, None] - n[None, :]
    decay = jnp.where(rel >= 0, jnp.exp(jnp.maximum(rel, 0.0) * log_g[:, None, None]), 0.0)
    q_dec = jnp.exp((n + 1.0) * log_g[:, None])
    k_dec = jnp.exp((chunk - 1.0 - n) * log_g[:, None])
    c_dec = jnp.exp(chunk * log_g)
    return decay, q_dec, k_dec, c_dec


def kernel(x_prompt, x_sample, cache_swa_k, cache_swa_v, state_ret, cache_mem_k, cache_mem_v,
           mem_prompt, ln_g, ln_b, ffn_w_gu, ffn_w_down, w_mem_kv, swa_w_in, swa_w_out,
           swa_sinks, ret_w_in, ret_w_out):
    bp, seq, _ = x_prompt.shape
    bs, sample_seq, _ = x_sample.shape
    assert sample_seq == 1
    mp, ms = bp * seq, bs
    assert mp % ms == 0

    mem2 = mem_prompt.reshape(bp * MEM_LEN, D_MODEL)
    mem_kv = _mem_kv(mem2, w_mem_kv).reshape(DEPTH, bp, MEM_LEN, 2 * MEM_DIM)
    mem_k_prompt = mem_kv[..., :MEM_DIM].reshape(DEPTH, bp, MEM_LEN, MEM_HEADS, HEAD_DIM)
    mem_v_prompt = mem_kv[..., MEM_DIM:].reshape(DEPTH, bp, MEM_LEN, MEM_HEADS, HEAD_DIM)

    tm, tiles = _row_tiles(mp + ms, FFN_MAX_ROWS)
    m_all = tm * tiles
    tm_proj, _ = _row_tiles(m_all, PROJ_MAX_ROWS)
    pad = m_all - mp - ms
    tail = (mp - (tiles - 1) * tm, ms)
    assert tail[0] >= 0 and tail[0] % 8 == 0 and ms % 8 == 0 and tail[0] + ms <= tm
    def per_row(table):
        return jnp.concatenate([jnp.tile(table[:seq], (bp, 1)),
                                jnp.broadcast_to(table[seq:], (ms, table.shape[1])),
                                jnp.zeros((pad, table.shape[1]), F32)], axis=0)

    pos = jnp.concatenate([jnp.arange(seq, dtype=jnp.int32), jnp.full((1,), PAST_LEN, jnp.int32)])
    rope_tabs, rope_shifts = _rope_tables(pos)
    rot_tabs, rot_shifts = _ret_rot_tables(pos)
    rope_tabs = tuple(per_row(t) for t in rope_tabs)
    rot_tabs = tuple(per_row(t) for t in rot_tabs)
    decay, q_dec, k_dec, c_dec = _ret_decay(RET_CHUNK)
    ret_tabs = (decay, jnp.broadcast_to(q_dec[:, :, None], decay.shape),
                jnp.broadcast_to(k_dec[:, :, None], decay.shape), c_dec)
    decay, q_dec, k_dec, c_dec = _ret_decay(sample_seq)
    step_dec = jnp.stack([decay[:, 0, 0], q_dec[:, 0], k_dec[:, 0], c_dec])

    def ln(i, s):
        return ln_g[i, s].reshape(1, D_MODEL), ln_b[i, s].reshape(1, D_MODEL)

    swa_w_out_b, ret_w_out_b = swa_w_out.astype(BF16), ret_w_out.astype(BF16)
    swa_k_prompt, swa_v_prompt = [], []
    swa_sample = [jnp.zeros_like(cache_swa_k), jnp.zeros_like(cache_swa_v)]
    ret_prompt = jnp.zeros((DEPTH // 2, bp, RET_HEADS, RET_DK, RET_DV), F32)
    ret_sample = jnp.zeros_like(state_ret)
    for i in range(DEPTH):
        j = i // 2
        if i == 0:
            x = _ffn(x_prompt.reshape(mp, D_MODEL), ffn_w_gu, ffn_w_down, i, 0, *ln(i, 0), tm, tiles, tail,
                     x_sample=x_sample.reshape(ms, D_MODEL))
        else:
            x = _ffn(x, ffn_w_gu, ffn_w_down, i, 0, *ln(i, 0), tm, tiles, tail)
        if i % 2 == 0:
            qkv = _proj(x, swa_w_in, j, tm_proj, rot_heads=SWA_Q_HEADS + SWA_KV_HEADS, tables=rope_tabs,
                        shifts=rope_shifts)
            x, k_tail, v_tail = _swa_prompt(qkv, bp, seq, mem_kv, i, swa_sinks[j], swa_w_out_b, x, *ln(i, 1))
            swa_k_prompt.append(k_tail.reshape(bp, WINDOW, SWA_KV_HEADS, HEAD_DIM))
            swa_v_prompt.append(v_tail.reshape(bp, WINDOW, SWA_KV_HEADS, HEAD_DIM))
            rows = qkv[mp:mp + ms].reshape(ms, SWA_IN_WIDTH // HEAD_DIM, HEAD_DIM)
            att, *swa_sample = _swa_step(rows, cache_swa_k, cache_swa_v, j, cache_mem_k, cache_mem_v, i,
                                         swa_sinks[j], swa_sample)
            x = _out_ln(att.reshape(ms, D_MODEL), swa_w_out, j, x, *ln(i, 1), mp)
        else:
            qkvg = _proj(x, ret_w_in, j, tm_proj, rot_heads=2 * RET_HEADS, tables=rot_tabs,
                         shifts=rot_shifts)
            x, ret_prompt = _ret_prompt(qkvg, bp, seq, mem_kv, i, ret_tabs, ret_w_out_b, x, *ln(i, 1),
                                        [ret_prompt])
            rows = qkvg[mp:mp + ms].reshape(ms, RET_IN_WIDTH // HEAD_DIM, HEAD_DIM)
            att, ret_sample = _ret_step(rows, state_ret, j, cache_mem_k,
                                        cache_mem_v, i, step_dec,
                                        [ret_sample])
            x = _out_ln(att.reshape(ms, D_MODEL), ret_w_out, j, x, *ln(i, 1), mp)
        x = _ffn(x, ffn_w_gu, ffn_w_down, i, 1, *ln(i, 2), tm, tiles, tail, split_out=(i == DEPTH - 1))

    y_prompt = x[0].reshape(bp, seq, D_MODEL)
    y_sample = x[1].reshape(bs, sample_seq, D_MODEL)
    return (y_prompt, y_sample, jnp.stack(swa_k_prompt), jnp.stack(swa_v_prompt), swa_sample[0],
            swa_sample[1], ret_prompt, ret_sample, mem_k_prompt, mem_v_prompt)
```

```python
import functools

import jax
import jax.numpy as jnp
from jax import lax
from jax.experimental import pallas as pl
from jax.experimental.pallas import tpu as pltpu

F32 = jnp.float32
BF16 = jnp.bfloat16

D_MODEL = 2048
DEPTH = 4
PAST_LEN = 16384
HEAD_DIM = 128
MEM_LEN = 256
MEM_HEADS = 4
MEM_DIM = MEM_HEADS * HEAD_DIM
SELF_WIDTH = D_MODEL - MEM_DIM
SWA_Q_HEADS = SELF_WIDTH // HEAD_DIM
SWA_KV_HEADS = SWA_Q_HEADS // 3
SWA_GROUP = SWA_Q_HEADS // SWA_KV_HEADS
WINDOW = 128
ROPE_THETA = 500000.0
ROPE_DIM = HEAD_DIM // 4
RET_DK = 128
RET_DV = 2 * RET_DK
RET_HEADS = SELF_WIDTH // RET_DV
RET_CHUNK = 128
RET_ROT_BASE = 10000.0
D_FF = ((8 * D_MODEL // 3 + 255) // 256) * 256
LN_EPS = 1e-5
HEAD_NORM_EPS = 1e-6
ALPHA = (2.0 * DEPTH) ** 0.25
NEG_INF = -1e30
ATT_SCALE = HEAD_DIM ** -0.5
RET_K_SCALE = RET_DK ** -0.5

SWA_Q = SWA_Q_HEADS * HEAD_DIM
SWA_KV = SWA_KV_HEADS * HEAD_DIM
SWA_IN_WIDTH = SWA_Q + 2 * SWA_KV + MEM_DIM
RET_QK = RET_HEADS * RET_DK
RET_V = RET_HEADS * RET_DV
RET_IN_WIDTH = 2 * RET_QK + 2 * RET_V + MEM_DIM

VMEM_LIMIT_BYTES = 58 * 1024 * 1024
LANES = 128
FFN_TILE_K = 256
FFN_MAX_ROWS = 1040
PROJ_MAX_ROWS = 1040
ATT_BLOCKS_PER_STEP = 4
PROJ_TILE_N = 1024
OUT_TILE_N = 512
MAX_ROW_CHUNK = 256
BF16_SUBLANES = 16


def _row_tiles(rows, max_rows):
    tiles = -(-rows // max_rows)
    per_tile = -(-rows // tiles)
    return -(-per_tile // BF16_SUBLANES) * BF16_SUBLANES, tiles


def _row_chunks(tm):
    count = -(-tm // MAX_ROW_CHUNK)
    size = -(-(-(-tm // count)) // BF16_SUBLANES) * BF16_SUBLANES
    return [slice(r, min(r + size, tm)) for r in range(0, tm, size)]


def _params(*sem):
    return pltpu.CompilerParams(dimension_semantics=sem, vmem_limit_bytes=VMEM_LIMIT_BYTES)


def _layer_norm_rows(z, g, b):
    mu = jnp.mean(z, axis=-1, keepdims=True)
    zc = z - mu
    var = jnp.mean(zc * zc, axis=-1, keepdims=True)
    return zc * lax.rsqrt(var + LN_EPS) * g + b


def _ffn_kernel(*refs, nk, tk, tiles, tail, split_in, split_out):
    refs = list(refs)
    x_ref = refs.pop(0)
    xs_ref = refs.pop(0) if split_in else None
    wg_ref, wu_ref, wd_ref, g_ref, b_ref, o_ref = refs[:6]
    os_ref = refs[6] if split_out else None
    xb_ref, wgu_ref = refs[-2:]
    i = pl.program_id(0)
    k = pl.program_id(1)
    chunks = _row_chunks(o_ref.shape[0])

    def load_x(rows):
        xv = x_ref[rows, :]
        if not split_in or rows.stop <= tail[0]:
            return xv
        first, count = tail
        lo, hi = rows.start, rows.stop
        pieces = []
        if lo < first:
            pieces.append(xv[:first - lo])
        s0, s1 = max(lo, first), min(hi, first + count)
        if s0 < s1:
            pieces.append(xs_ref[s0 - first:s1 - first, :])
        if hi > first + count:
            pieces.append(jnp.zeros((hi - max(lo, first + count), xv.shape[1]), xv.dtype))
        return jnp.where(i == tiles - 1, jnp.concatenate(pieces, axis=0), xv)

    def cast_weights():
        wgu_ref[:, :tk] = wg_ref[...].astype(BF16)
        wgu_ref[:, tk:] = wu_ref[...].astype(BF16)
        return wd_ref[...].astype(BF16)

    def down(xb, wd):
        gu = jnp.dot(xb, wgu_ref[...], preferred_element_type=F32)
        h = (jax.nn.silu(gu[:, :tk]) * gu[:, tk:]).astype(BF16)
        return jnp.dot(h, wd, preferred_element_type=F32)

    @pl.when(k == 0)
    def _():
        wd = cast_weights()
        for rows in chunks:
            xb = load_x(rows).astype(BF16)
            xb_ref[rows, :] = xb
            o_ref[rows, :] = down(xb, wd)

    @pl.when(jnp.logical_and(k > 0, k < nk - 1))
    def _():
        wd = cast_weights()
        o_ref[...] += down(xb_ref[...], wd)

    @pl.when(k == nk - 1)
    def _():
        wd = cast_weights()
        for rows in chunks:
            acc = o_ref[rows, :] + down(xb_ref[rows, :], wd)
            z = ALPHA * load_x(rows) + 0.5 * acc
            o_ref[rows, :] = _layer_norm_rows(z, g_ref[...], b_ref[...])
        if split_out:
            @pl.when(i == tiles - 1)
            def _():
                os_ref[...] = o_ref[tail[0]:tail[0] + tail[1], :]


def _ffn(x, w_gu, w_down, layer, slot, g, b, tm, tiles, tail, x_sample=None, split_out=False):
    tk = FFN_TILE_K
    nk = D_FF // tk
    assert nk >= 2
    split_in = x_sample is not None
    rows = pl.BlockSpec((tm, D_MODEL), lambda i, k: (i, 0))
    small = pl.BlockSpec((tail[1], D_MODEL), lambda i, k: (0, 0))
    in_specs = [pl.BlockSpec((tm, D_MODEL), lambda i, k: (i, 0), pipeline_mode=pl.Buffered(1))]
    args = [x]
    if split_in:
        in_specs.append(small)
        args.append(x_sample)
    in_specs += [
        pl.BlockSpec((None, None, D_MODEL, tk), lambda i, k: (layer, slot, 0, k)),
        pl.BlockSpec((None, None, D_MODEL, tk), lambda i, k: (layer, slot, 0, nk + k)),
        pl.BlockSpec((None, None, tk, D_MODEL), lambda i, k: (layer, slot, k, 0)),
        pl.BlockSpec((1, D_MODEL), lambda i, k: (0, 0)),
        pl.BlockSpec((1, D_MODEL), lambda i, k: (0, 0)),
    ]
    if split_out:
        prompt_rows = (tiles - 1) * tm + tail[0]
        out_specs = [rows, small]
        out_shape = [jax.ShapeDtypeStruct((prompt_rows, D_MODEL), F32),
                     jax.ShapeDtypeStruct((tail[1], D_MODEL), F32)]
    else:
        out_specs = rows
        out_shape = jax.ShapeDtypeStruct((tiles * tm, D_MODEL), F32)
    return pl.pallas_call(
        functools.partial(_ffn_kernel, nk=nk, tk=tk, tiles=tiles, tail=tail,
                          split_in=split_in, split_out=split_out),
        grid=(tiles, nk),
        in_specs=in_specs,
        out_specs=out_specs,
        out_shape=out_shape,
        scratch_shapes=[pltpu.VMEM((tm, D_MODEL), BF16), pltpu.VMEM((D_MODEL, 2 * tk), BF16)],
        compiler_params=_params("arbitrary", "arbitrary"),
        name="ffn_ln",
    )(*args, w_gu, w_gu, w_down, g, b)


def _proj_kernel(x_ref, w_ref, *rest, rot_heads, shifts):
    if rot_heads:
        c_ref, a_ref, b_ref, o_ref, xb_ref = rest
    else:
        o_ref, xb_ref = rest
    j = pl.program_id(1)

    @pl.when(j == 0)
    def _():
        xb_ref[...] = x_ref[...].astype(BF16)

    def plain():
        o_ref[...] = jnp.dot(xb_ref[...], w_ref[...].astype(BF16), preferred_element_type=F32)

    if not rot_heads:
        plain()
        return

    tm, tn = o_ref.shape
    heads = tn // LANES
    full_tiles, part_heads = divmod(rot_heads, heads)

    def rotated(n_heads):
        wb = w_ref[...].astype(BF16)
        for rows in _row_chunks(tm):
            y = jnp.dot(xb_ref[rows, :], wb, preferred_element_type=F32)
            c, a, b = c_ref[rows, :], a_ref[rows, :], b_ref[rows, :]
            for h in range(heads):
                yh = y[:, h * LANES:(h + 1) * LANES]
                if h < n_heads:
                    yh = yh * c + pltpu.roll(yh, shifts[0], 1) * a + pltpu.roll(yh, shifts[1], 1) * b
                o_ref[rows, h * LANES:(h + 1) * LANES] = yh

    pl.when(j < full_tiles)(lambda: rotated(heads))
    if part_heads:
        pl.when(j == full_tiles)(lambda: rotated(part_heads))
    pl.when(j >= full_tiles + (1 if part_heads else 0))(plain)


def _proj(x, w, layer, tm, rot_heads=0, tables=None, shifts=None):
    m = x.shape[0]
    n = w.shape[-1]
    tn = PROJ_TILE_N
    in_specs = [
        pl.BlockSpec((tm, D_MODEL), lambda i, j: (i, 0)),
        pl.BlockSpec((None, D_MODEL, tn), lambda i, j: (layer, 0, j)),
    ]
    args = [x, w]
    if rot_heads:
        tab = pl.BlockSpec((tm, LANES), lambda i, j: (i, 0))
        in_specs += [tab, tab, tab]
        args += list(tables)
    return pl.pallas_call(
        functools.partial(_proj_kernel, rot_heads=rot_heads, shifts=shifts),
        grid=(pl.cdiv(m, tm), n // tn),
        in_specs=in_specs,
        out_specs=pl.BlockSpec((tm, tn), lambda i, j: (i, j)),
        out_shape=jax.ShapeDtypeStruct((m, n), F32),
        scratch_shapes=[pltpu.VMEM((tm, D_MODEL), BF16)],
        compiler_params=_params("parallel", "arbitrary"),
        name="proj",
    )(*args)


def _mem_kv_kernel(x_ref, w_ref, o_ref, xb_ref):
    @pl.when(pl.program_id(0) == 0)
    def _():
        xb_ref[...] = x_ref[...].astype(BF16)

    o_ref[...] = jnp.dot(xb_ref[...], w_ref[...].astype(BF16), preferred_element_type=F32)


def _mem_kv(mem, w):
    rows = mem.shape[0]
    layers, _, n = w.shape
    return pl.pallas_call(
        _mem_kv_kernel,
        grid=(layers,),
        in_specs=[
            pl.BlockSpec((rows, D_MODEL), lambda l: (0, 0)),
            pl.BlockSpec((None, D_MODEL, n), lambda l: (l, 0, 0)),
        ],
        out_specs=pl.BlockSpec((None, rows, n), lambda l: (l, 0, 0)),
        out_shape=jax.ShapeDtypeStruct((layers, rows, n), F32),
        scratch_shapes=[pltpu.VMEM((rows, D_MODEL), BF16)],
        compiler_params=_params("arbitrary"),
        name="mem_kv",
    )(mem, w)


def _out_ln_kernel(att_ref, w_ref, x_ref, g_ref, b_ref, o_ref, y_ref, *, nn, tn):
    n = pl.program_id(1)
    y_ref[n] = jnp.dot(att_ref[...].astype(BF16), w_ref[...], preferred_element_type=F32)

    @pl.when(n == nn - 1)
    def _():
        cols = [slice(c * tn, (c + 1) * tn) for c in range(nn)]
        total = None
        for c in range(nn):
            z = ALPHA * x_ref[:, cols[c]] + y_ref[c]
            y_ref[c] = z
            part = jnp.sum(z, axis=-1, keepdims=True)
            total = part if total is None else total + part
        mu = total * (1.0 / D_MODEL)
        total = None
        for c in range(nn):
            zc = y_ref[c] - mu
            part = jnp.sum(zc * zc, axis=-1, keepdims=True)
            total = part if total is None else total + part
        rstd = lax.rsqrt(total * (1.0 / D_MODEL) + LN_EPS)
        for c in range(nn):
            o_ref[:, cols[c]] = (y_ref[c] - mu) * rstd * g_ref[:, cols[c]] + b_ref[:, cols[c]]


def _out_ln(att, w_out, layer, x, g, b, row0):
    tm = att.shape[0]
    assert row0 % tm == 0
    tn = OUT_TILE_N
    nn = D_MODEL // tn
    rows = pl.BlockSpec((tm, D_MODEL), lambda i, n: (row0 // tm, 0))
    return pl.pallas_call(
        functools.partial(_out_ln_kernel, nn=nn, tn=tn),
        grid=(1, nn),
        in_specs=[
            pl.BlockSpec((tm, D_MODEL), lambda i, n: (0, 0)),
            pl.BlockSpec((None, D_MODEL, tn), lambda i, n: (layer, 0, n)),
            rows,
            pl.BlockSpec((1, D_MODEL), lambda i, n: (0, 0)),
            pl.BlockSpec((1, D_MODEL), lambda i, n: (0, 0)),
        ],
        out_specs=rows,
        out_shape=jax.ShapeDtypeStruct(x.shape, F32),
        scratch_shapes=[pltpu.VMEM((nn, tm, tn), F32)],
        input_output_aliases={2: 0},
        compiler_params=_params("arbitrary", "arbitrary"),
        name="out_ln",
    )(att, w_out, x, g, b)


def _dot_nt(a, b):
    return lax.dot_general(a, b, (((1,), (1,)), ((), ())), preferred_element_type=F32)


def _out_proj_ln(att_ref, w_ref, x_ref, g_ref, b_ref, o_ref):
    y = jnp.dot(att_ref[...], w_ref[...], preferred_element_type=F32)
    o_ref[...] = _layer_norm_rows(ALPHA * x_ref[...] + y, g_ref[...], b_ref[...])


def _normalised_pv(p, v, extra=None):
    width = v.shape[1]
    v_ones = jnp.concatenate([v, jnp.ones((p.shape[1], width), p.dtype)], axis=1)
    pv = jnp.dot(p, v_ones, preferred_element_type=F32)
    den = pv[:, width:]
    if extra is not None:
        den = den + extra
    return pv[:, :width] * (1.0 / den)


def _mem_attend_block(qm_ref, mk_ref, mv_ref, att_ref, col0):
    for h in range(MEM_HEADS):
        sl = slice(h * HEAD_DIM, (h + 1) * HEAD_DIM)
        q = qm_ref[:, sl].astype(BF16)
        s = _dot_nt(q, mk_ref[0, :, sl].astype(BF16)) * ATT_SCALE
        p = jnp.exp(s - jnp.max(s, axis=-1, keepdims=True))
        p = p * (1.0 / jnp.sum(p, axis=-1, keepdims=True))
        o = jnp.dot(p.astype(BF16), mv_ref[0, :, sl].astype(BF16), preferred_element_type=F32)
        att_ref[:, col0 + h * HEAD_DIM:col0 + (h + 1) * HEAD_DIM] = o.astype(att_ref.dtype)


def _swa_prompt_kernel(sink_ref, q_ref, kc_ref, kp_ref, vc_ref, vp_ref, qm_ref, mk_ref, mv_ref,
                       w_ref, x_ref, g_ref, b_ref, o_ref, kt_ref, vt_ref, s_ref, p_ref, att_ref,
                       *, nblk, steps):
    n = pl.program_id(1)
    nq = SWA_Q_HEADS
    w2 = 2 * WINDOW

    def prev_cur(cur_ref, prev_ref, i, sl):
        rows = slice(i * WINDOW, (i + 1) * WINDOW)
        before = prev_ref[:, sl] if i == 0 else cur_ref[(i - 1) * WINDOW:i * WINDOW, sl]
        return jnp.concatenate([before, cur_ref[rows, sl]], axis=0).astype(BF16)

    for i in range(nblk):
        rows = slice(i * WINDOW, (i + 1) * WINDOW)
        for h in range(SWA_KV_HEADS):
            sl = slice(h * HEAD_DIM, (h + 1) * HEAD_DIM)
            q3 = jnp.concatenate(
                [q_ref[rows, (h * SWA_GROUP + g) * HEAD_DIM:(h * SWA_GROUP + g + 1) * HEAD_DIM]
                 for g in range(SWA_GROUP)], axis=0).astype(BF16)
            s = _dot_nt(q3, prev_cur(kc_ref, kp_ref, i, sl)) * ATT_SCALE
            s_ref[i, h * SWA_GROUP:(h + 1) * SWA_GROUP] = s.reshape(SWA_GROUP, WINDOW, w2)
    for h in range(MEM_HEADS):
        sl = slice(h * HEAD_DIM, (h + 1) * HEAD_DIM)
        sm = _dot_nt(qm_ref[:, sl].astype(BF16), mk_ref[0, :, sl].astype(BF16)) * ATT_SCALE
        s_ref[:, nq + h] = sm.reshape(nblk, WINDOW, w2)

    qi = lax.broadcasted_iota(jnp.int32, (WINDOW, w2), 0)
    kj = lax.broadcasted_iota(jnp.int32, (WINDOW, w2), 1)
    sink = sink_ref[...]
    sink_terms = []
    for i in range(nblk):
        first_key = jnp.where(n > 0, qi, WINDOW) if i == 0 else qi
        ok = (kj >= first_key) & (kj <= qi + WINDOW)
        s = jnp.where(ok[None], s_ref[i, 0:nq], NEG_INF)
        m = jnp.maximum(jnp.max(s, axis=-1, keepdims=True), sink)
        p_ref[i, 0:nq] = jnp.exp(s - m).astype(BF16)
        sink_terms.append(jnp.exp(sink - m))
    s = s_ref[:, nq:nq + MEM_HEADS]
    p_ref[:, nq:nq + MEM_HEADS] = jnp.exp(s - jnp.max(s, axis=-1, keepdims=True)).astype(BF16)

    for i in range(nblk):
        rows = slice(i * WINDOW, (i + 1) * WINDOW)
        for h in range(SWA_KV_HEADS):
            sl = slice(h * HEAD_DIM, (h + 1) * HEAD_DIM)
            group = slice(h * SWA_GROUP, (h + 1) * SWA_GROUP)
            p3 = p_ref[i, group].reshape(SWA_GROUP * WINDOW, w2)
            o = _normalised_pv(p3, prev_cur(vc_ref, vp_ref, i, sl),
                               sink_terms[i][group].reshape(SWA_GROUP * WINDOW, 1))
            for g in range(SWA_GROUP):
                hq = h * SWA_GROUP + g
                att_ref[rows, hq * HEAD_DIM:(hq + 1) * HEAD_DIM] = (
                    o[g * WINDOW:(g + 1) * WINDOW].astype(BF16))
    for h in range(MEM_HEADS):
        sl = slice(h * HEAD_DIM, (h + 1) * HEAD_DIM)
        pm = p_ref[:, nq + h].reshape(nblk * WINDOW, w2)
        om = _normalised_pv(pm, mv_ref[0, :, sl].astype(BF16))
        att_ref[:, SWA_Q + h * HEAD_DIM:SWA_Q + (h + 1) * HEAD_DIM] = om.astype(BF16)

    @pl.when(n == steps - 1)
    def _():
        last = slice((nblk - 1) * WINDOW, nblk * WINDOW)
        kt_ref[0] = kc_ref[last, :]
        vt_ref[0] = vc_ref[last, :]

    _out_proj_ln(att_ref, w_ref, x_ref, g_ref, b_ref, o_ref)


def _mixer_tail_specs(layer, row_of, rows):
    mem = (None, 1, MEM_LEN, MEM_DIM)
    vec = pl.BlockSpec((1, D_MODEL), lambda b, n: (0, 0))
    mixer = layer // 2
    return [
        pl.BlockSpec(mem, lambda b, n: (layer, b, 0, 0)),
        pl.BlockSpec(mem, lambda b, n: (layer, b, 0, 1)),
        pl.BlockSpec((None, D_MODEL, D_MODEL), lambda b, n: (mixer, 0, 0), pipeline_mode=pl.Buffered(1)),
        pl.BlockSpec((rows, D_MODEL), lambda b, n: (row_of(b, n), 0)),
        vec, vec,
    ]


def _swa_prompt(qkv, bsz, seq, mem_kv, layer, sinks, w_out, x, g, b):
    nblk = ATT_BLOCKS_PER_STEP
    rows = nblk * WINDOW
    steps = seq // rows
    assert seq % rows == 0 and seq >= WINDOW
    tail = pl.BlockSpec((1, WINDOW, SWA_KV), lambda b, n: (b, 0, 0))
    blocks_per_seq = seq // WINDOW
    kcol = SWA_Q // SWA_KV
    vcol = kcol + 1
    mcol = vcol + 1
    row_of = lambda b, n: b * steps + n
    prev_of = lambda b, n: b * blocks_per_seq + jnp.maximum(n * nblk - 1, 0)
    cur = lambda w, col: pl.BlockSpec((rows, w), lambda b, n: (row_of(b, n), col))
    prev = lambda col: pl.BlockSpec((WINDOW, SWA_KV), lambda b, n: (prev_of(b, n), col))
    nheads = SWA_Q_HEADS + MEM_HEADS
    assert MEM_LEN == 2 * WINDOW
    in_specs = [
        pl.BlockSpec((SWA_Q_HEADS, 1, 1), lambda b, n: (0, 0, 0)),
        cur(SWA_Q, 0), cur(SWA_KV, kcol), prev(kcol), cur(SWA_KV, vcol), prev(vcol), cur(MEM_DIM, mcol),
    ] + _mixer_tail_specs(layer, row_of, rows)
    return pl.pallas_call(
        functools.partial(_swa_prompt_kernel, nblk=nblk, steps=steps),
        grid=(bsz, steps),
        in_specs=in_specs,
        out_specs=[pl.BlockSpec((rows, D_MODEL), lambda b, n: (row_of(b, n), 0)), tail, tail],
        out_shape=[jax.ShapeDtypeStruct(x.shape, F32),
                   jax.ShapeDtypeStruct((bsz, WINDOW, SWA_KV), F32),
                   jax.ShapeDtypeStruct((bsz, WINDOW, SWA_KV), F32)],
        scratch_shapes=[pltpu.VMEM((nblk, nheads, WINDOW, 2 * WINDOW), F32),
                        pltpu.VMEM((nblk, nheads, WINDOW, 2 * WINDOW), BF16),
                        pltpu.VMEM((rows, D_MODEL), BF16)],
        input_output_aliases={len(in_specs) - 3: 0},
        compiler_params=_params("arbitrary", "arbitrary"),
        name="swa_prompt",
    )(sinks.reshape(SWA_Q_HEADS, 1, 1), qkv, qkv, qkv, qkv, qkv, qkv, mem_kv, mem_kv, w_out, x, g, b)


def _head_norm_gate(o, gate):
    mu = jnp.mean(o, axis=-1, keepdims=True)
    oc = o - mu
    var = jnp.mean(oc * oc, axis=-1, keepdims=True)
    return jax.nn.silu(gate) * (oc * lax.rsqrt(var + HEAD_NORM_EPS))


def _ret_prompt_kernel(cdec_ref, decay_ref, qdec_ref, kdec_ref, q_ref, k_ref, v_ref, gate_ref, qm_ref,
                       mk_ref, mv_ref, w_ref, x_ref, g_ref, b_ref, o_ref, s_out_ref,
                       state_ref, att_ref, *, steps, nblk):
    c = pl.program_id(1)

    @pl.when(c == 0)
    def _():
        state_ref[...] = jnp.zeros_like(state_ref)

    for i in range(nblk):
        rows = slice(i * RET_CHUNK, (i + 1) * RET_CHUNK)
        for h in range(RET_HEADS):
            ksl = slice(h * RET_DK, (h + 1) * RET_DK)
            vsl = slice(h * RET_DV, (h + 1) * RET_DV)
            qc = q_ref[rows, ksl]
            kc = k_ref[rows, ksl] * RET_K_SCALE
            vb = v_ref[rows, vsl].astype(BF16)
            st = state_ref[h]
            inner = _dot_nt(qc.astype(BF16), kc.astype(BF16)) * decay_ref[h]
            o = (jnp.dot(inner.astype(BF16), vb, preferred_element_type=F32)
                 + jnp.dot((qc * qdec_ref[h]).astype(BF16), st.astype(BF16),
                           preferred_element_type=F32))
            kd = (kc * kdec_ref[h]).astype(BF16)
            state_ref[h] = cdec_ref[h] * st + lax.dot_general(
                kd, vb, (((0,), (0,)), ((), ())), preferred_element_type=F32)
            att_ref[rows, vsl] = _head_norm_gate(o, gate_ref[rows, vsl]).astype(BF16)
    _mem_attend_block(qm_ref, mk_ref, mv_ref, att_ref, RET_V)

    @pl.when(c == steps - 1)
    def _():
        s_out_ref[0] = state_ref[...]

    _out_proj_ln(att_ref, w_ref, x_ref, g_ref, b_ref, o_ref)


def _drop_carried(kernel, first, count):
    def body(*refs):
        return kernel(*refs[:first], *refs[first + count:])
    return body


def _carry(prev_outputs):
    prev_outputs = list(prev_outputs)
    return [pl.BlockSpec(memory_space=pl.ANY)] * len(prev_outputs), prev_outputs


def _ret_prompt(qkvg, bsz, seq, mem_kv, layer, tables, w_out, x, g, b, carried):
    nblk = ATT_BLOCKS_PER_STEP
    rows = nblk * RET_CHUNK
    steps = seq // rows
    mixer = layer // 2
    decay, qdec, kdec, cdec = tables
    row_of = lambda b, n: b * steps + n
    cur = lambda w, col: pl.BlockSpec((rows, w), lambda b, n: (row_of(b, n), col))
    tab = pl.BlockSpec((RET_HEADS, RET_CHUNK, RET_CHUNK), lambda b, n: (0, 0, 0))
    in_specs = [
        pl.BlockSpec(memory_space=pltpu.SMEM), tab, tab, tab,
        cur(RET_QK, 0), cur(RET_QK, 1), cur(RET_V, 1), cur(RET_V, 2),
        cur(MEM_DIM, (2 * RET_QK + 2 * RET_V) // MEM_DIM),
    ] + _mixer_tail_specs(layer, row_of, rows)
    n_in = len(in_specs)
    carry_specs, carry_args = _carry(carried)
    aliases = {n_in - 3: 0}
    aliases.update({n_in + c: 1 + c for c in range(len(carry_args))})
    return pl.pallas_call(
        _drop_carried(functools.partial(_ret_prompt_kernel, steps=steps, nblk=nblk), n_in, len(carry_args)),
        grid=(bsz, steps),
        in_specs=in_specs + carry_specs,
        out_specs=[
            pl.BlockSpec((rows, D_MODEL), lambda b, n: (row_of(b, n), 0)),
            pl.BlockSpec((None, 1, RET_HEADS, RET_DK, RET_DV), lambda b, n: (mixer, b, 0, 0, 0)),
        ],
        out_shape=[
            jax.ShapeDtypeStruct(x.shape, F32),
            jax.ShapeDtypeStruct((DEPTH // 2, bsz, RET_HEADS, RET_DK, RET_DV), F32),
        ],
        scratch_shapes=[pltpu.VMEM((RET_HEADS, RET_DK, RET_DV), F32),
                        pltpu.VMEM((rows, D_MODEL), BF16)],
        input_output_aliases=aliases,
        compiler_params=_params("arbitrary", "arbitrary"),
        name="ret_prompt",
    )(cdec, decay, qdec, kdec, qkvg, qkvg, qkvg, qkvg, qkvg, mem_kv, mem_kv, w_out, x, g, b, *carry_args)


def _pair_tokens(a):
    t, h, d = a.shape
    a = a.reshape(t // 2, 2, h, d)
    return jnp.concatenate([a[:, 0], a[:, 1]], axis=1)


def _twice(x):
    return jnp.concatenate([x, x], axis=0)


def _fold(x):
    half = x.shape[0] // 2
    return x[:half] + x[half:]


def _mem_attend_row(q, mk_ref, mv_ref):
    mk = _pair_tokens(mk_ref[0])
    mv = _pair_tokens(mv_ref[0])
    s = jnp.sum(mk * _twice(q)[None], axis=-1, keepdims=True) * ATT_SCALE
    m = jnp.max(s, axis=0)
    m = jnp.maximum(m[:MEM_HEADS], m[MEM_HEADS:])
    p = jnp.exp(s - _twice(m)[None])
    den = _fold(jnp.sum(p, axis=0))
    return _fold(jnp.sum(p * mv, axis=0)) * (1.0 / den)


def _swa_step_kernel(sink_ref, rows_ref, kbuf_ref, vbuf_ref, mk_ref, mv_ref, o_ref, nk_ref, nv_ref):
    krow0 = SWA_Q_HEADS
    vrow0 = krow0 + SWA_KV_HEADS
    mrow0 = vrow0 + SWA_KV_HEADS
    wb = kbuf_ref.shape[1]
    kb = _pair_tokens(kbuf_ref[0])
    vb = _pair_tokens(vbuf_ref[0])
    k_new = rows_ref[0, krow0:krow0 + SWA_KV_HEADS, :]
    v_new = rows_ref[0, vrow0:vrow0 + SWA_KV_HEADS, :]
    for g in range(SWA_GROUP):
        group_rows = pl.ds(g, SWA_KV_HEADS, stride=SWA_GROUP)
        q = rows_ref[0, group_rows, :]
        s_buf = jnp.sum(kb * _twice(q)[None], axis=-1, keepdims=True) * ATT_SCALE
        s_new = jnp.sum(k_new * q, axis=-1, keepdims=True) * ATT_SCALE
        sink = sink_ref[g]
        m = jnp.max(s_buf, axis=0)
        m = jnp.maximum(jnp.maximum(m[:SWA_KV_HEADS], m[SWA_KV_HEADS:]), jnp.maximum(s_new, sink))
        p_buf = jnp.exp(s_buf - _twice(m)[None])
        p_new = jnp.exp(s_new - m)
        den = _fold(jnp.sum(p_buf, axis=0)) + p_new + jnp.exp(sink - m)
        o_ref[0, group_rows, :] = (_fold(jnp.sum(p_buf * vb, axis=0)) + p_new * v_new) * (1.0 / den)
    nk_ref[0, 0:wb - 1] = kbuf_ref[0, 1:wb]
    nv_ref[0, 0:wb - 1] = vbuf_ref[0, 1:wb]
    nk_ref[0, wb - 1] = k_new
    nv_ref[0, wb - 1] = v_new
    o_ref[0, SWA_Q_HEADS:SWA_Q_HEADS + MEM_HEADS, :] = _mem_attend_row(
        rows_ref[0, mrow0:mrow0 + MEM_HEADS, :], mk_ref, mv_ref)


def _swa_step(rows, cache_k, cache_v, j, mem_k, mem_v, i, sinks, carried):
    bsz = rows.shape[0]
    wb = cache_k.shape[2]
    cache = pl.BlockSpec((None, 1, wb, SWA_KV_HEADS, HEAD_DIM), lambda b: (j, b, 0, 0, 0))
    mem = pl.BlockSpec((None, 1, MEM_LEN, MEM_HEADS, HEAD_DIM), lambda b: (i, b, 0, 0, 0))
    nrows = D_MODEL // HEAD_DIM
    sink_gk = sinks.reshape(SWA_KV_HEADS, SWA_GROUP).T.reshape(SWA_GROUP, SWA_KV_HEADS, 1)
    in_specs = [
        pl.BlockSpec(sink_gk.shape, lambda b: (0, 0, 0)),
        pl.BlockSpec((1,) + rows.shape[1:], lambda b: (b, 0, 0)),
        cache, cache, mem, mem,
    ]
    n_in = len(in_specs)
    carry_specs, carry_args = _carry(carried)
    return pl.pallas_call(
        _drop_carried(_swa_step_kernel, n_in, len(carry_args)),
        grid=(bsz,),
        in_specs=in_specs + carry_specs,
        out_specs=[pl.BlockSpec((1, nrows, HEAD_DIM), lambda b: (b, 0, 0)), cache, cache],
        out_shape=[
            jax.ShapeDtypeStruct((bsz, nrows, HEAD_DIM), F32),
            jax.ShapeDtypeStruct(cache_k.shape, F32),
            jax.ShapeDtypeStruct(cache_v.shape, F32),
        ],
        input_output_aliases={n_in + c: 1 + c for c in range(len(carry_args))},
        compiler_params=_params("parallel"),
        name="swa_step",
    )(sink_gk, rows, cache_k, cache_v, mem_k, mem_v, *carry_args)


def _ret_step_kernel(dec_ref, rows_ref, s_ref, mk_ref, mv_ref, o_ref, s_out_ref):
    krow0 = RET_HEADS
    vrow0 = 2 * RET_HEADS
    grow0 = vrow0 + 2 * RET_HEADS
    mrow0 = grow0 + 2 * RET_HEADS
    first_row = lax.broadcasted_iota(jnp.int32, (BF16_SUBLANES, 1), 0) == 0

    def lift(row):
        return jnp.where(first_row, row, 0.0).astype(BF16)

    def wide(row0):
        return jnp.concatenate([rows_ref[0, row0:row0 + 1, :], rows_ref[0, row0 + 1:row0 + 2, :]], axis=1)

    for h in range(RET_HEADS):
        q_row = rows_ref[0, h:h + 1, :]
        k_row = rows_ref[0, krow0 + h:krow0 + h + 1, :] * RET_K_SCALE
        v_row = wide(vrow0 + 2 * h)
        st = s_ref[0, h]
        inner = jnp.sum(q_row * k_row, axis=1, keepdims=True) * dec_ref[0, h]
        cross = jnp.dot(lift(q_row * dec_ref[1, h]), st.astype(BF16), preferred_element_type=F32)[0:1]
        outer = lax.dot_general(lift(k_row * dec_ref[2, h]), lift(v_row), (((0,), (0,)), ((), ())),
                                preferred_element_type=F32)
        s_out_ref[0, h] = dec_ref[3, h] * st + outer
        out = _head_norm_gate(inner * v_row + cross, wide(grow0 + 2 * h))
        o_ref[0, 2 * h:2 * h + 1, :] = out[:, :HEAD_DIM]
        o_ref[0, 2 * h + 1:2 * h + 2, :] = out[:, HEAD_DIM:]
    o_ref[0, 2 * RET_HEADS:2 * RET_HEADS + MEM_HEADS, :] = _mem_attend_row(
        rows_ref[0, mrow0:mrow0 + MEM_HEADS, :], mk_ref, mv_ref)


def _ret_step(rows, state, j, mem_k, mem_v, i, dec, carried):
    bsz = rows.shape[0]
    st = pl.BlockSpec((None, 1, RET_HEADS, RET_DK, RET_DV), lambda b: (j, b, 0, 0, 0))
    mem = pl.BlockSpec((None, 1, MEM_LEN, MEM_HEADS, HEAD_DIM), lambda b: (i, b, 0, 0, 0))
    nrows = D_MODEL // HEAD_DIM
    in_specs = [
        pl.BlockSpec(memory_space=pltpu.SMEM),
        pl.BlockSpec((1,) + rows.shape[1:], lambda b: (b, 0, 0)),
        st, mem, mem,
    ]
    n_in = len(in_specs)
    carry_specs, carry_args = _carry(carried)
    return pl.pallas_call(
        _drop_carried(_ret_step_kernel, n_in, len(carry_args)),
        grid=(bsz,),
        in_specs=in_specs + carry_specs,
        out_specs=[pl.BlockSpec((1, nrows, HEAD_DIM), lambda b: (b, 0, 0)), st],
        out_shape=[
            jax.ShapeDtypeStruct((bsz, nrows, HEAD_DIM), F32),
            jax.ShapeDtypeStruct(state.shape, F32),
        ],
        input_output_aliases={n_in + c: 1 + c for c in range(len(carry_args))},
        compiler_params=_params("parallel"),
        name="ret_step",
    )(dec, rows, state, mem_k, mem_v, *carry_args)


def _rope_tables(pos):
    half = ROPE_DIM // 2
    inv = ROPE_THETA ** (-jnp.arange(half, dtype=F32) / half)
    ang = pos.astype(F32)[:, None] * inv[None, :]
    cos, sin = jnp.cos(ang), jnp.sin(ang)
    n = pos.shape[0]
    rest = HEAD_DIM - ROPE_DIM
    c = jnp.concatenate([cos, cos, jnp.ones((n, rest), F32)], axis=-1)
    a = jnp.concatenate([-sin, jnp.zeros((n, HEAD_DIM - half), F32)], axis=-1)
    b = jnp.concatenate([jnp.zeros((n, half), F32), sin, jnp.zeros((n, rest), F32)], axis=-1)
    return (c, a, b), (HEAD_DIM - half, half)


def _ret_rot_tables(pos):
    half = RET_DK // 2
    angle = RET_ROT_BASE ** (-jnp.linspace(0.0, 1.0, half, dtype=F32))
    ang = pos.astype(F32)[:, None] * angle[None, :]
    cos, sin = jnp.cos(ang), jnp.sin(ang)
    n = pos.shape[0]
    zero = jnp.zeros_like(sin)
    c = jnp.stack([cos, cos], axis=-1).reshape(n, RET_DK)
    a = jnp.stack([-sin, zero], axis=-1).reshape(n, RET_DK)
    b = jnp.stack([zero, sin], axis=-1).reshape(n, RET_DK)
    return (c, a, b), (RET_DK - 1, 1)


def _ret_decay(chunk):
    log_g = jnp.log1p(-jnp.exp2(-5.0 - jnp.arange(RET_HEADS, dtype=F32)))
    n = jnp.arange(chunk, dtype=F32)
    rel = n[:, None] - n[None, :]
    decay = jnp.where(rel >= 0, jnp.exp(jnp.maximum(rel, 0.0) * log_g[:, None, None]), 0.0)
    q_dec = jnp.exp((n + 1.0) * log_g[:, None])
    k_dec = jnp.exp((chunk - 1.0 - n) * log_g[:, None])
    c_dec = jnp.exp(chunk * log_g)
    return decay, q_dec, k_dec, c_dec


def kernel(x_prompt, x_sample, cache_swa_k, cache_swa_v, state_ret, cache_mem_k, cache_mem_v,
           mem_prompt, ln_g, ln_b, ffn_w_gu, ffn_w_down, w_mem_kv, swa_w_in, swa_w_out,
           swa_sinks, ret_w_in, ret_w_out):
    bp, seq, _ = x_prompt.shape
    bs, sample_seq, _ = x_sample.shape
    assert sample_seq == 1
    mp, ms = bp * seq, bs
    assert mp % ms == 0

    mem2 = mem_prompt.reshape(bp * MEM_LEN, D_MODEL)
    mem_kv = _mem_kv(mem2, w_mem_kv).reshape(DEPTH, bp, MEM_LEN, 2 * MEM_DIM)
    mem_k_prompt = mem_kv[..., :MEM_DIM].reshape(DEPTH, bp, MEM_LEN, MEM_HEADS, HEAD_DIM)
    mem_v_prompt = mem_kv[..., MEM_DIM:].reshape(DEPTH, bp, MEM_LEN, MEM_HEADS, HEAD_DIM)

    tm, tiles = _row_tiles(mp + ms, FFN_MAX_ROWS)
    m_all = tm * tiles
    tm_proj, _ = _row_tiles(m_all, PROJ_MAX_ROWS)
    pad = m_all - mp - ms
    tail = (mp - (tiles - 1) * tm, ms)
    assert tail[0] >= 0 and tail[0] % 8 == 0 and ms % 8 == 0 and tail[0] + ms <= tm
    def per_row(table):
        return jnp.concatenate([jnp.tile(table[:seq], (bp, 1)),
                                jnp.broadcast_to(table[seq:], (ms, table.shape[1])),
                                jnp.zeros((pad, table.shape[1]), F32)], axis=0)

    pos = jnp.concatenate([jnp.arange(seq, dtype=jnp.int32), jnp.full((1,), PAST_LEN, jnp.int32)])
    rope_tabs, rope_shifts = _rope_tables(pos)
    rot_tabs, rot_shifts = _ret_rot_tables(pos)
    rope_tabs = tuple(per_row(t) for t in rope_tabs)
    rot_tabs = tuple(per_row(t) for t in rot_tabs)
    decay, q_dec, k_dec, c_dec = _ret_decay(RET_CHUNK)
    ret_tabs = (decay, jnp.broadcast_to(q_dec[:, :, None], decay.shape),
                jnp.broadcast_to(k_dec[:, :, None], decay.shape), c_dec)
    decay, q_dec, k_dec, c_dec = _ret_decay(sample_seq)
    step_dec = jnp.stack([decay[:, 0, 0], q_dec[:, 0], k_dec[:, 0], c_dec])

    def ln(i, s):
        return ln_g[i, s].reshape(1, D_MODEL), ln_b[i, s].reshape(1, D_MODEL)

    swa_w_out_b, ret_w_out_b = swa_w_out.astype(BF16), ret_w_out.astype(BF16)
    swa_k_prompt, swa_v_prompt = [], []
    swa_sample = [jnp.zeros_like(cache_swa_k), jnp.zeros_like(cache_swa_v)]
    ret_prompt = jnp.zeros((DEPTH // 2, bp, RET_HEADS, RET_DK, RET_DV), F32)
    ret_sample = jnp.zeros_like(state_ret)
    for i in range(DEPTH):
        j = i // 2
        if i == 0:
            x = _ffn(x_prompt.reshape(mp, D_MODEL), ffn_w_gu, ffn_w_down, i, 0, *ln(i, 0), tm, tiles, tail,
                     x_sample=x_sample.reshape(ms, D_MODEL))
        else:
            x = _ffn(x, ffn_w_gu, ffn_w_down, i, 0, *ln(i, 0), tm, tiles, tail)
        if i % 2 == 0:
            qkv = _proj(x, swa_w_in, j, tm_proj, rot_heads=SWA_Q_HEADS + SWA_KV_HEADS, tables=rope_tabs,
                        shifts=rope_shifts)
            x, k_tail, v_tail = _swa_prompt(qkv, bp, seq, mem_kv, i, swa_sinks[j], swa_w_out_b, x, *ln(i, 1))
            swa_k_prompt.append(k_tail.reshape(bp, WINDOW, SWA_KV_HEADS, HEAD_DIM))
            swa_v_prompt.append(v_tail.reshape(bp, WINDOW, SWA_KV_HEADS, HEAD_DIM))
            rows = qkv[mp:mp + ms].reshape(ms, SWA_IN_WIDTH // HEAD_DIM, HEAD_DIM)
            att, *swa_sample = _swa_step(rows, cache_swa_k, cache_swa_v, j, cache_mem_k, cache_mem_v, i,
                                         swa_sinks[j], swa_sample)
            x = _out_ln(att.reshape(ms, D_MODEL), swa_w_out_b, j, x, *ln(i, 1), mp)
        else:
            qkvg = _proj(x, ret_w_in, j, tm_proj, rot_heads=2 * RET_HEADS, tables=rot_tabs,
                         shifts=rot_shifts)
            x, ret_prompt = _ret_prompt(qkvg, bp, seq, mem_kv, i, ret_tabs, ret_w_out_b, x, *ln(i, 1),
                                        [ret_prompt])
            rows = qkvg[mp:mp + ms].reshape(ms, RET_IN_WIDTH // HEAD_DIM, HEAD_DIM)
            att, ret_sample = _ret_step(rows, state_ret, j, cache_mem_k,
                                        cache_mem_v, i, step_dec,
                                        [ret_sample])
            x = _out_ln(att.reshape(ms, D_MODEL), ret_w_out_b, j, x, *ln(i, 1), mp)
        x = _ffn(x, ffn_w_gu, ffn_w_down, i, 1, *ln(i, 2), tm, tiles, tail, split_out=(i == DEPTH - 1))

    y_prompt = x[0].reshape(bp, seq, D_MODEL)
    y_sample = x[1].reshape(bs, sample_seq, D_MODEL)
    return (y_prompt, y_sample, jnp.stack(swa_k_prompt), jnp.stack(swa_v_prompt), swa_sample[0],
            swa_sample[1], ret_prompt, ret_sample, mem_k_prompt, mem_v_prompt)
```
